```python
import jax, jax.numpy as jnp
from jax import lax
import numpy as np

D_MODEL = 1024
BATCH = 2
SEQ = 8192
DEPTH = 2
DEC_BATCH = 16
DEC_SEQ = 2048
PAST_LEN = 128

GRID_W = 64
HEAD_DIM = 64
Q_BLOCK = 128
N_FOURIER_GROUPS = 4
FOURIER_GROUP_DIM = D_MODEL // 16
D_FOURIER = N_FOURIER_GROUPS * FOURIER_GROUP_DIM
N_HEADS_B = (D_MODEL - D_FOURIER) // HEAD_DIM
N_KV_B = N_HEADS_B // 3
ROPE_THETA = 10000.0
D_CONV = D_MODEL // 2
N_HEADS_D = (D_MODEL - D_CONV) // HEAD_DIM
N_KV_D = N_HEADS_D // 4
WINDOW = 128
N_EXPERTS = 16
N_GROUPS = 4
EXPERTS_PER_GROUP = N_EXPERTS // N_GROUPS
TOP_K = 2
D_FF_EXPERT = D_MODEL // 2

LN_EPS = 1e-5
RMS_EPS = 1e-6
NEG_INF = -1e30
ALPHA = float((2 * DEPTH) ** 0.25)
BETA = float((8 * DEPTH) ** -0.25)
N_EVEN = (DEPTH + 1) // 2
N_ODD = DEPTH // 2
D_Q_B = N_HEADS_B * HEAD_DIM
D_KV_B = N_KV_B * HEAD_DIM
D_IN_EVEN = D_FOURIER + D_Q_B + 2 * D_KV_B
D_CAT_EVEN = D_FOURIER + D_Q_B
D_Q_D = N_HEADS_D * HEAD_DIM
D_KV_D = N_KV_D * HEAD_DIM
D_IN_ODD = 3 * D_CONV + D_Q_D + 2 * D_KV_D
D_CAT_ODD = D_CONV + D_Q_D

kernel_name = 'hybrid_fourier_gqa_conv_window_moe_encoder'


def layer_norm(x, g, b):
    xf = x.astype(jnp.float32)
    mu = jnp.mean(xf, axis=-1, keepdims=True)
    xc = xf - mu
    var = jnp.mean(xc * xc, axis=-1, keepdims=True)
    return (xc * lax.rsqrt(var + LN_EPS) * g + b).astype(x.dtype)


def rms_norm(x, g):
    xf = x.astype(jnp.float32)
    ms = jnp.mean(xf * xf, axis=-1, keepdims=True)
    return (xf * lax.rsqrt(ms + RMS_EPS) * g).astype(x.dtype)


def alibi_slopes(n):
    return jnp.asarray(np.array([2.0 ** (-8.0 * (i + 1) / n) for i in range(n)], dtype=np.float32))


def axial_rope_angles(S):
    rows = S // GRID_W
    row = jnp.repeat(jnp.arange(rows), GRID_W).astype(jnp.float32)
    col = jnp.tile(jnp.arange(GRID_W), rows).astype(jnp.float32)
    n_freq = HEAD_DIM // 4
    inv_freq = ROPE_THETA ** (-jnp.arange(n_freq, dtype=jnp.float32) / n_freq)
    return row[:, None] * inv_freq, col[:, None] * inv_freq


def _rotate(xh, ang):
    n = ang.shape[-1]
    c = jnp.cos(ang)[None, :, None, :]
    s = jnp.sin(ang)[None, :, None, :]
    x1, x2 = xh[..., :n], xh[..., n:]
    return jnp.concatenate([x1 * c - x2 * s, x1 * s + x2 * c], axis=-1)


def apply_axial_rope(x, ang_r, ang_c):
    half = HEAD_DIM // 2
    xf = x.astype(jnp.float32)
    out = jnp.concatenate([_rotate(xf[..., :half], ang_r), _rotate(xf[..., half:], ang_c)], axis=-1)
    return out.astype(x.dtype)


def fourier_mix(u, g):
    Bn, S, _ = u.shape
    ug = u.reshape(Bn, S, N_FOURIER_GROUPS, FOURIER_GROUP_DIM)
    ug = rms_norm(ug, g.reshape(N_FOURIER_GROUPS, FOURIER_GROUP_DIM)).astype(jnp.float32)
    f = jnp.fft.fftn(ug, axes=(1, 3), norm='ortho').real
    return f.reshape(Bn, S, D_FOURIER).astype(u.dtype)


def global_axial_attention(q, k, v, qg, kg):
    Bn, S = q.shape[:2]
    ang_r, ang_c = axial_rope_angles(S)
    q = apply_axial_rope(rms_norm(q, qg), ang_r, ang_c) * (HEAD_DIM ** -0.5)
    k = apply_axial_rope(rms_norm(k, kg), ang_r, ang_c)
    grp = N_HEADS_B // N_KV_B
    nb = S // Q_BLOCK
    qb = jnp.moveaxis(q.reshape(Bn, nb, Q_BLOCK, N_KV_B, grp, HEAD_DIM), 1, 0)

    def block(qi):
        s = jnp.einsum('bqkgd,bskd->bkgqs', qi, k, preferred_element_type=jnp.float32)
        p = jax.nn.softmax(s, axis=-1)
        return jnp.einsum('bkgqs,bskd->bqkgd', p.astype(v.dtype), v)

    o = lax.map(block, qb)
    return jnp.moveaxis(o, 0, 1).reshape(Bn, S, D_Q_B)


def window_sink_attention(q, k, v, sink):
    Bn, S = q.shape[:2]
    nb = S // Q_BLOCK
    grp = N_HEADS_D // N_KV_D
    qb = (q * (HEAD_DIM ** -0.5)).reshape(Bn, nb, Q_BLOCK, N_KV_D, grp, HEAD_DIM)
    pad = ((0, 0), (Q_BLOCK, Q_BLOCK), (0, 0), (0, 0))
    kp = jnp.pad(k, pad).reshape(Bn, nb + 2, Q_BLOCK, N_KV_D, HEAD_DIM)
    vp = jnp.pad(v, pad).reshape(Bn, nb + 2, Q_BLOCK, N_KV_D, HEAD_DIM)
    kb = jnp.concatenate([kp[:, :-2], kp[:, 1:-1], kp[:, 2:]], axis=2)
    vb = jnp.concatenate([vp[:, :-2], vp[:, 1:-1], vp[:, 2:]], axis=2)
    s = jnp.einsum('bnqkgd,bnskd->bnkgqs', qb, kb, preferred_element_type=jnp.float32)
    qpos = jnp.arange(nb)[:, None] * Q_BLOCK + jnp.arange(Q_BLOCK)[None, :]
    kpos = jnp.arange(nb)[:, None] * Q_BLOCK - Q_BLOCK + jnp.arange(3 * Q_BLOCK)[None, :]
    rel = kpos[:, None, :] - qpos[:, :, None]
    valid = (jnp.abs(rel) <= WINDOW) & (kpos[:, None, :] >= 0) & (kpos[:, None, :] < S)
    slopes = alibi_slopes(N_HEADS_D).reshape(N_KV_D, grp)
    bias = -slopes[None, :, :, None, None] * jnp.abs(rel).astype(jnp.float32)[:, None, None]
    s = jnp.where(valid[:, None, None], s + bias, NEG_INF)
    sink_col = jnp.broadcast_to(sink.astype(jnp.float32).reshape(N_KV_D, grp, 1, 1), s.shape[:-1] + (1,))
    p = jax.nn.softmax(jnp.concatenate([s, sink_col], axis=-1), axis=-1)[..., :-1]
    o = jnp.einsum('bnkgqs,bnskd->bnqkgd', p.astype(vb.dtype), vb)
    return o.reshape(Bn, S, D_Q_D)


def short_conv_mix(bg, cg, h, conv_w, conv_b):
    S = h.shape[1]
    u = cg * h
    up = jnp.pad(u, ((0, 0), (1, 1), (0, 0)))
    y = up[:, :S] * conv_w[0] + up[:, 1:S + 1] * conv_w[1] + up[:, 2:] * conv_w[2] + conv_b
    return bg * y


def even_mixer(x, w_in, f_g, q_g, k_g, w_out):
    Bn, S, _ = x.shape
    proj = x @ w_in
    u_f, q, k, v = jnp.split(proj, [D_FOURIER, D_FOURIER + D_Q_B, D_FOURIER + D_Q_B + D_KV_B], axis=-1)
    f = fourier_mix(u_f, f_g)
    a = global_axial_attention(q.reshape(Bn, S, N_HEADS_B, HEAD_DIM),
                               k.reshape(Bn, S, N_KV_B, HEAD_DIM),
                               v.reshape(Bn, S, N_KV_B, HEAD_DIM), q_g, k_g)
    return jnp.concatenate([f, a], axis=-1) @ w_out


def odd_mixer(x, w_in, conv_w, conv_b, sink, w_out):
    Bn, S, _ = x.shape
    proj = x @ w_in
    c3 = 3 * D_CONV
    bg, cg, h, q, k, v = jnp.split(proj, [D_CONV, 2 * D_CONV, c3, c3 + D_Q_D, c3 + D_Q_D + D_KV_D], axis=-1)
    c = short_conv_mix(bg, cg, h, conv_w, conv_b)
    a = window_sink_attention(q.reshape(Bn, S, N_HEADS_D, HEAD_DIM),
                              k.reshape(Bn, S, N_KV_D, HEAD_DIM),
                              v.reshape(Bn, S, N_KV_D, HEAD_DIM), sink)
    return jnp.concatenate([c, a], axis=-1) @ w_out


def grouped_moe(x, router_w, router_b, w_gate, w_up, w_down):
    Bn, S, D = x.shape
    xt = x.reshape(-1, D)
    scores = jax.nn.sigmoid(jnp.dot(xt, router_w, preferred_element_type=jnp.float32))
    biased = scores + router_b.astype(jnp.float32)
    group_score = lax.top_k(biased.reshape(-1, N_GROUPS, EXPERTS_PER_GROUP), TOP_K)[0].sum(-1)
    gmask = jax.nn.one_hot(jnp.argmax(group_score, axis=-1), N_GROUPS, dtype=jnp.bool_)
    masked = jnp.where(jnp.repeat(gmask, EXPERTS_PER_GROUP, axis=1), biased, NEG_INF)
    _, idx = lax.top_k(masked, TOP_K)
    w_sel = jnp.take_along_axis(scores, idx, axis=-1)
    w_sel = w_sel / jnp.sum(w_sel, axis=-1, keepdims=True)
    gates = jnp.sum(jax.nn.one_hot(idx, N_EXPERTS, dtype=jnp.float32) * w_sel[..., None], axis=1)
    out = jnp.zeros((xt.shape[0], D), jnp.float32)
    for e in range(N_EXPERTS):
        hdn = jax.nn.silu(xt @ w_gate[e]) * (xt @ w_up[e])
        out = out + gates[:, e:e + 1] * jnp.dot(hdn, w_down[e], preferred_element_type=jnp.float32)
    return out.astype(x.dtype).reshape(Bn, S, D)


def trunk(x, w_in_even, fourier_norm_g, q_norm_g, k_norm_g, w_out_even,
          w_in_odd, conv_w, conv_b, sink_logits, w_out_odd,
          ln_mix_g, ln_mix_b, ln_ffn_g, ln_ffn_b,
          router_w, router_b, w_gate, w_up, w_down):
    for l in range(DEPTH):
        i = l // 2
        if l % 2 == 0:
            m = even_mixer(x, w_in_even[i], fourier_norm_g[i], q_norm_g[i], k_norm_g[i], w_out_even[i])
        else:
            m = odd_mixer(x, w_in_odd[i], conv_w[i], conv_b[i], sink_logits[i], w_out_odd[i])
        x = layer_norm(ALPHA * x + m, ln_mix_g[l], ln_mix_b[l])
        f = grouped_moe(x, router_w, router_b, w_gate[l], w_up[l], w_down[l])
        x = layer_norm(ALPHA * x + f, ln_ffn_g[l], ln_ffn_b[l])
    return x


def setup_inputs(seed: int = 0) -> dict:
    key = jax.random.key(seed)
    ks = jax.random.split(key, 24)
    nrm = lambda k, shape, scale: jax.random.normal(k, shape, jnp.float32) * scale
    return {
        'x_prompt': nrm(ks[0], (BATCH, SEQ, D_MODEL), 1.0),
        'x_sample': nrm(ks[1], (DEC_BATCH, DEC_SEQ, D_MODEL), 1.0),
        'w_in_even': nrm(ks[2], (N_EVEN, D_MODEL, D_IN_EVEN), D_MODEL ** -0.5),
        'fourier_norm_g': 1.0 + nrm(ks[3], (N_EVEN, D_FOURIER), 0.02),
        'q_norm_g': 1.0 + nrm(ks[4], (N_EVEN, HEAD_DIM), 0.02),
        'k_norm_g': 1.0 + nrm(ks[5], (N_EVEN, HEAD_DIM), 0.02),
        'w_out_even': nrm(ks[6], (N_EVEN, D_CAT_EVEN, D_MODEL), BETA * D_CAT_EVEN ** -0.5),
        'w_in_odd': nrm(ks[7], (N_ODD, D_MODEL, D_IN_ODD), D_MODEL ** -0.5),
        'conv_w': nrm(ks[8], (N_ODD, 3, D_CONV), 3 ** -0.5),
        'conv_b': nrm(ks[9], (N_ODD, D_CONV), 0.02),
        'sink_logits': nrm(ks[10], (N_ODD, N_HEADS_D), 0.5),
        'w_out_odd': nrm(ks[11], (N_ODD, D_CAT_ODD, D_MODEL), BETA * D_CAT_ODD ** -0.5),
        'ln_mix_g': 1.0 + nrm(ks[12], (DEPTH, D_MODEL), 0.02),
        'ln_mix_b': nrm(ks[13], (DEPTH, D_MODEL), 0.02),
        'ln_ffn_g': 1.0 + nrm(ks[14], (DEPTH, D_MODEL), 0.02),
        'ln_ffn_b': nrm(ks[15], (DEPTH, D_MODEL), 0.02),
        'router_w': nrm(ks[16], (D_MODEL, N_EXPERTS), D_MODEL ** -0.5),
        'router_b': nrm(ks[17], (N_EXPERTS,), 0.01),
        'w_gate': nrm(ks[18], (DEPTH, N_EXPERTS, D_MODEL, D_FF_EXPERT), D_MODEL ** -0.5),
        'w_up': nrm(ks[19], (DEPTH, N_EXPERTS, D_MODEL, D_FF_EXPERT), D_MODEL ** -0.5),
        'w_down': nrm(ks[20], (DEPTH, N_EXPERTS, D_FF_EXPERT, D_MODEL), BETA * D_FF_EXPERT ** -0.5),
    }


def reference(x_prompt, x_sample, w_in_even, fourier_norm_g, q_norm_g, k_norm_g, w_out_even,
              w_in_odd, conv_w, conv_b, sink_logits, w_out_odd,
              ln_mix_g, ln_mix_b, ln_ffn_g, ln_ffn_b,
              router_w, router_b, w_gate, w_up, w_down):
    y_prompt = trunk(x_prompt, w_in_even, fourier_norm_g, q_norm_g, k_norm_g, w_out_even,
                     w_in_odd, conv_w, conv_b, sink_logits, w_out_odd,
                     ln_mix_g, ln_mix_b, ln_ffn_g, ln_ffn_b,
                     router_w, router_b, w_gate, w_up, w_down)
    y_sample = trunk(x_sample, w_in_even, fourier_norm_g, q_norm_g, k_norm_g, w_out_even,
                     w_in_odd, conv_w, conv_b, sink_logits, w_out_odd,
                     ln_mix_g, ln_mix_b, ln_ffn_g, ln_ffn_b,
                     router_w, router_b, w_gate, w_up, w_down)
    return (y_prompt, y_sample)
```

```python
import functools
import math

import numpy as np
import jax
import jax.numpy as jnp
from jax import lax
from jax.experimental import pallas as pl
from jax.experimental.pallas import tpu as pltpu

F32 = jnp.float32
BF16 = jnp.bfloat16
I32 = jnp.int32

D_MODEL = 1024
HEAD_DIM = 64
GRID_W = 64
Q_BLOCK = 128
WINDOW = 128
ROPE_THETA = 10000.0
N_FOURIER_GROUPS = 4
D_FOURIER = 256
N_HEADS_B = 12
N_KV_B = 4
GRP_B = 3
D_CONV = 512
N_HEADS_D = 8
N_KV_D = 2
GRP_D = 4
N_EXPERTS = 16
N_GROUPS = 4
EXPERTS_PER_GROUP = 4
D_FF = 512
LN_EPS = 1e-5
RMS_EPS = 1e-6
NEG_INF = -1e30
LOG2E = 1.4426950408889634
FFT_S2 = 128
V7X_VMEM_LIMIT = 48 * 1024 * 1024


def _cparams(sem):
    return pltpu.CompilerParams(dimension_semantics=sem, vmem_limit_bytes=V7X_VMEM_LIMIT)


def _layer_norm(y, g, b):
    mu = jnp.mean(y, axis=-1, keepdims=True)
    yc = y - mu
    var = jnp.mean(yc * yc, axis=-1, keepdims=True)
    return yc * lax.rsqrt(var + LN_EPS) * g + b


N_NORM_COLS = D_FOURIER + 1024


def _inproj_even_kernel(x_ref, w_ref, gain_ref, seg_ref, segt_ref, cos_ref, sin_ref, vone_ref,
                        uf_ref, qk_ref, v_ref):
    x = x_ref[...].astype(BF16)
    proj = jnp.dot(x, w_ref[...], preferred_element_type=F32)
    nrm = proj[:, :N_NORM_COLS]
    sq = (nrm * nrm).astype(BF16)
    ssum = jnp.dot(sq, seg_ref[...], preferred_element_type=F32)
    r = lax.rsqrt(ssum * (1.0 / HEAD_DIM) + RMS_EPS)
    rh = r.astype(BF16)
    rl = (r - rh.astype(F32)).astype(BF16)
    rex = jnp.dot(jnp.concatenate([rh, rl], axis=1), segt_ref[...], preferred_element_type=F32)
    y = nrm * rex * gain_ref[...]
    uf_ref[...] = y[:, :D_FOURIER].astype(BF16)
    yq = y[:, D_FOURIER:]
    c = jnp.concatenate([cos_ref[...]] * 8, axis=1)
    s = jnp.concatenate([sin_ref[...]] * 8, axis=1)
    lane = lax.broadcasted_iota(I32, yq.shape, 1)
    first = (lane & 31) < 16
    sw = jnp.where(first, pltpu.roll(yq, 1024 - 16, 1), pltpu.roll(yq, 16, 1))
    qk_ref[...] = (yq * c + sw * s).astype(BF16)
    vv = proj[:, N_NORM_COLS:] + vone_ref[...]
    for h in range(N_KV_B):
        v_ref[h] = vv[:, 128 * h:128 * (h + 1)].astype(BF16)


def _rope_tables(n_pos):
    t = jnp.arange(n_pos)
    row = (t // GRID_W).astype(F32)
    col = (t % GRID_W).astype(F32)
    n_freq = HEAD_DIM // 4
    inv_freq = ROPE_THETA ** (-jnp.arange(n_freq, dtype=F32) / n_freq)
    ar = row[:, None] * inv_freq
    ac = col[:, None] * inv_freq
    ang = jnp.concatenate([ar, ar, ac, ac], axis=1)
    sign = jnp.asarray(np.tile(np.repeat(np.array([-1.0, 1.0], np.float32), 16), 2))
    cos = jnp.cos(ang)
    sin = jnp.sin(ang) * sign
    return jnp.concatenate([cos, cos], axis=1), jnp.concatenate([sin, sin], axis=1)


def _pos_block_map(classes, ts):
    bounds = []
    tile0 = 0
    for nseq, S, _ in classes:
        n_tiles = nseq * S // ts
        bounds.append((tile0, tile0 + n_tiles, S // ts))
        tile0 += n_tiles

    def fn(i):
        out = (i - bounds[-1][0]) % bounds[-1][2]
        for lo, hi, per in reversed(bounds[:-1]):
            out = jnp.where(i < hi, (i - lo) % per, out)
        return out

    return fn


def _inproj_even(x, w_in, f_g, q_g, k_g, classes, ts):
    T = x.shape[0]
    wf = w_in[:, :D_FOURIER]
    wq = w_in[:, D_FOURIER:D_FOURIER + 768].reshape(D_MODEL, N_KV_B, GRP_B * HEAD_DIM)
    wk = w_in[:, D_FOURIER + 768:D_FOURIER + 1024].reshape(D_MODEL, N_KV_B, HEAD_DIM)
    wv = w_in[:, D_FOURIER + 1024:].reshape(D_MODEL, N_KV_B, HEAD_DIM)
    wqk = jnp.concatenate([wk, wq], axis=2).reshape(D_MODEL, 1024)
    wvp = jnp.concatenate([wv, jnp.zeros_like(wv)], axis=2).reshape(D_MODEL, 512)
    w = jnp.concatenate([wf, wqk, wvp], axis=1).astype(BF16)
    qscale = HEAD_DIM ** -0.5 * LOG2E
    gqk = jnp.tile(jnp.concatenate([k_g, q_g * qscale, q_g * qscale, q_g * qscale]), N_KV_B)
    gain = jnp.concatenate([f_g, gqk])[None, :].astype(F32)
    seg_np = np.zeros((N_NORM_COLS, 128), np.float32)
    seg_np[np.arange(N_NORM_COLS), np.arange(N_NORM_COLS) // HEAD_DIM] = 1.0
    seg = jnp.asarray(seg_np, BF16)
    segt = jnp.asarray(np.concatenate([seg_np.T, seg_np.T], axis=0), BF16)
    max_s = max(S for _, S, _ in classes)
    cos, sin = _rope_tables(max_s)
    vone_np = np.zeros((1, 512), np.float32)
    for h in range(N_KV_B):
        vone_np[0, 128 * h + 64:128 * (h + 1)] = 1.0
    vone = jnp.asarray(vone_np)
    posmap = _pos_block_map(classes, ts)
    n_w = w.shape[1]
    return pl.pallas_call(
        _inproj_even_kernel,
        grid=(T // ts,),
        in_specs=[
            pl.BlockSpec((ts, D_MODEL), lambda i: (i, 0)),
            pl.BlockSpec((D_MODEL, n_w), lambda i: (0, 0)),
            pl.BlockSpec((1, N_NORM_COLS), lambda i: (0, 0)),
            pl.BlockSpec((N_NORM_COLS, 128), lambda i: (0, 0)),
            pl.BlockSpec((256, N_NORM_COLS), lambda i: (0, 0)),
            pl.BlockSpec((ts, 128), lambda i: (posmap(i), 0)),
            pl.BlockSpec((ts, 128), lambda i: (posmap(i), 0)),
            pl.BlockSpec((1, 512), lambda i: (0, 0)),
        ],
        out_specs=[
            pl.BlockSpec((ts, D_FOURIER), lambda i: (i, 0)),
            pl.BlockSpec((ts, 1024), lambda i: (i, 0)),
            pl.BlockSpec((N_KV_B, ts, 128), lambda i: (0, i, 0)),
        ],
        out_shape=[
            jax.ShapeDtypeStruct((T, D_FOURIER), BF16),
            jax.ShapeDtypeStruct((T, 1024), BF16),
            jax.ShapeDtypeStruct((N_KV_B, T, 128), BF16),
        ],
        compiler_params=_cparams(("parallel",)),
        name="inproj_even",
    )(x, w, gain, seg, segt, cos, sin, vone)


def _fft1_kernel(a_ref, w1_ref, twr_ref, twi_ref, zr_ref, zi_ref, *, s1):
    z = jnp.dot(w1_ref[...], a_ref[0], preferred_element_type=F32)
    zr = z[:s1]
    zi = z[s1:]
    twr = twr_ref[...]
    twi = twi_ref[...]
    zr_ref[0] = (zr * twr - zi * twi).astype(BF16)
    zi_ref[0] = (zr * twi + zi * twr).astype(BF16)


def _fft2_kernel(zr_ref, zi_ref, w2a_ref, w2b_ref, mix_ref, o_ref, *, cb):
    s2 = FFT_S2
    for c in range(cb):
        pp = (jnp.dot(w2a_ref[...], zr_ref[0, c], preferred_element_type=F32)
              + jnp.dot(w2b_ref[...], zi_ref[0, c], preferred_element_type=F32))
        f = (jnp.dot(pp[:s2].astype(BF16), mix_ref[:D_FOURIER], preferred_element_type=F32)
             + jnp.dot(pp[s2:].astype(BF16), mix_ref[D_FOURIER:], preferred_element_type=F32))
        o_ref[0, :, D_FOURIER * c:D_FOURIER * (c + 1)] = f.astype(BF16)


def _dft_mats(n):
    k = np.arange(n)
    ang = 2.0 * np.pi * ((k[:, None] * k[None, :]) % n) / n
    return np.cos(ang), np.sin(ang)


def _fourier_mix(uf_part, nseq, S):
    s2 = FFT_S2
    s1 = S // s2
    ncol = s2 * D_FOURIER
    a = uf_part.reshape(nseq, s1, ncol)
    c1, sn1 = _dft_mats(s1)
    w1 = jnp.asarray(np.concatenate([c1, -sn1], axis=0), BF16)
    cc = jnp.arange(s1, dtype=I32)[:, None]
    bb = jnp.arange(s2, dtype=I32)[None, :]
    ang = (2.0 * math.pi / S) * ((cc * bb) % S).astype(F32)
    twr = jnp.repeat(jnp.cos(ang), D_FOURIER, axis=1)
    twi = jnp.repeat(-jnp.sin(ang), D_FOURIER, axis=1)
    tn = min(ncol, 4096)
    zr, zi = pl.pallas_call(
        functools.partial(_fft1_kernel, s1=s1),
        grid=(ncol // tn, nseq),
        in_specs=[
            pl.BlockSpec((1, s1, tn), lambda j, b: (b, 0, j)),
            pl.BlockSpec((2 * s1, s1), lambda j, b: (0, 0)),
            pl.BlockSpec((s1, tn), lambda j, b: (0, j)),
            pl.BlockSpec((s1, tn), lambda j, b: (0, j)),
        ],
        out_specs=[
            pl.BlockSpec((1, s1, tn), lambda j, b: (b, 0, j)),
            pl.BlockSpec((1, s1, tn), lambda j, b: (b, 0, j)),
        ],
        out_shape=[jax.ShapeDtypeStruct((nseq, s1, ncol), BF16)] * 2,
        compiler_params=_cparams(("parallel", "parallel")),
        name="fourier_stage1",
    )(a, w1, twr, twi)
    zr = zr.reshape(nseq, s1, s2, D_FOURIER)
    zi = zi.reshape(nseq, s1, s2, D_FOURIER)
    c2, sn2 = _dft_mats(s2)
    w2a = jnp.asarray(np.concatenate([c2, -sn2], axis=0), BF16)
    w2b = jnp.asarray(np.concatenate([sn2, c2], axis=0), BF16)
    gc, gs = _dft_mats(HEAD_DIM)
    scale = 1.0 / math.sqrt(S * HEAD_DIM)
    eye = np.eye(N_FOURIER_GROUPS)
    mix = jnp.asarray(np.concatenate([np.kron(eye, gc), np.kron(eye, gs)], axis=0) * scale, BF16)
    cb = min(8, s1)
    out = pl.pallas_call(
        functools.partial(_fft2_kernel, cb=cb),
        grid=(nseq, s1 // cb),
        in_specs=[
            pl.BlockSpec((1, cb, s2, D_FOURIER), lambda b, j: (b, j, 0, 0)),
            pl.BlockSpec((1, cb, s2, D_FOURIER), lambda b, j: (b, j, 0, 0)),
            pl.BlockSpec((2 * s2, s2), lambda b, j: (0, 0)),
            pl.BlockSpec((2 * s2, s2), lambda b, j: (0, 0)),
            pl.BlockSpec((2 * D_FOURIER, D_FOURIER), lambda b, j: (0, 0)),
        ],
        out_specs=pl.BlockSpec((1, s2, cb * D_FOURIER), lambda b, j: (b, 0, j)),
        out_shape=jax.ShapeDtypeStruct((nseq, s2, s1 * D_FOURIER), BF16),
        compiler_params=_cparams(("parallel", "parallel")),
        name="fourier_stage2",
    )(zr, zi, w2a, w2b, mix)
    return out.reshape(nseq * S, D_FOURIER)


def _gattn_kernel(q_ref, k_ref, v_ref, place_ref, o_ref, *, tq, tk, S):
    q3 = q_ref[...]
    qs = [q3[:, HEAD_DIM * (g + 1):HEAD_DIM * (g + 2)] for g in range(GRP_B)]

    def body(j, carry):
        off = pl.multiple_of(j * tk, tk)
        kc = k_ref[pl.ds(off, tk), 0:HEAD_DIM]
        vc = v_ref[0, pl.ds(off, tk), :]
        new = []
        for g in range(GRP_B):
            m_prev, acc = carry[g]
            s = lax.dot_general(qs[g], kc, (((1,), (1,)), ((), ())), preferred_element_type=F32)
            m_new = jnp.maximum(m_prev, jnp.max(s, axis=1, keepdims=True))
            alpha = jnp.exp2(m_prev - m_new)
            p = jnp.exp2(s - m_new).astype(BF16)
            acc = alpha * acc + jnp.dot(p, vc, preferred_element_type=F32)
            new.append((m_new, acc))
        return tuple(new)

    init = tuple((jnp.full((tq, 1), NEG_INF, F32), jnp.zeros((tq, 128), F32)) for _ in range(GRP_B))
    fin = lax.fori_loop(0, S // tk, body, init)
    out = jnp.zeros((tq, 256), F32)
    for g in range(GRP_B):
        acc = fin[g][1]
        o = (acc / acc[:, HEAD_DIM:HEAD_DIM + 1]).astype(BF16)
        out = out + jnp.dot(o, place_ref[g], preferred_element_type=F32)
    o_ref[...] = out.astype(BF16)


def _lane_place(n_src, n_dst, groups):
    pm = np.zeros((len(groups), n_src, n_dst), np.float32)
    for gi, slot in enumerate(groups):
        pm[gi, np.arange(HEAD_DIM), HEAD_DIM * slot + np.arange(HEAD_DIM)] = 1.0
    return jnp.asarray(pm, BF16)


def _global_attention(qk, v_aug, nseq, S, tok0, tq, tk):
    T = qk.shape[0]
    n_qt = S // tq
    qt0 = tok0 // tq
    s0 = tok0 // S
    place = _lane_place(128, 256, [1, 2, 3])
    return pl.pallas_call(
        functools.partial(_gattn_kernel, tq=tq, tk=tk, S=S),
        grid=(nseq, N_KV_B, n_qt),
        in_specs=[
            pl.BlockSpec((tq, 256), lambda b, h, i: (qt0 + b * n_qt + i, h)),
            pl.BlockSpec((S, 256), lambda b, h, i: (s0 + b, h)),
            pl.BlockSpec((1, S, 128), lambda b, h, i: (h, s0 + b, 0)),
            pl.BlockSpec((GRP_B, 128, 256), lambda b, h, i: (0, 0, 0)),
        ],
        out_specs=pl.BlockSpec((tq, 256), lambda b, h, i: (b * n_qt + i, h)),
        out_shape=jax.ShapeDtypeStruct((nseq * S, 1024), BF16),
        compiler_params=_cparams(("parallel", "parallel", "arbitrary")),
        name="global_attention",
    )(qk, qk, v_aug, place)


def _route(x1, rwh_ref, rwl_ref, rb_ref):
    xh = x1.astype(BF16)
    xl = (x1 - xh.astype(F32)).astype(BF16)
    lg = (jnp.dot(xh, rwh_ref[...], preferred_element_type=F32)
          + jnp.dot(xl, rwh_ref[...], preferred_element_type=F32)
          + jnp.dot(xh, rwl_ref[...], preferred_element_type=F32))
    lgt = lg.T[:N_EXPERTS]
    sc = 1.0 / (1.0 + jnp.exp(-lgt))
    bi = sc + rb_ref[...]
    srow = [sc[e:e + 1] for e in range(N_EXPERTS)]
    brow = [bi[e:e + 1] for e in range(N_EXPERTS)]
    gsel = None
    best = None
    for g in range(N_GROUPS):
        a, b, c, d = brow[4 * g:4 * g + 4]
        m1, n1 = jnp.maximum(a, b), jnp.minimum(a, b)
        m2, n2 = jnp.maximum(c, d), jnp.minimum(c, d)
        gs = jnp.maximum(m1, m2) + jnp.maximum(jnp.minimum(m1, m2), jnp.maximum(n1, n2))
        if g == 0:
            best, gsel = gs, jnp.zeros(gs.shape, I32)
        else:
            better = gs > best
            gsel = jnp.where(better, g, gsel)
            best = jnp.where(better, gs, best)
    masked = [jnp.where(gsel == (e // EXPERTS_PER_GROUP), brow[e], NEG_INF) for e in range(N_EXPERTS)]
    i1 = jnp.zeros(gsel.shape, I32)
    b1 = masked[0]
    s1 = srow[0]
    for e in range(1, N_EXPERTS):
        better = masked[e] > b1
        i1 = jnp.where(better, e, i1)
        b1 = jnp.where(better, masked[e], b1)
        s1 = jnp.where(better, srow[e], s1)
    i2 = jnp.full(gsel.shape, -1, I32)
    b2 = jnp.full(b1.shape, -jnp.inf, F32)
    s2 = jnp.zeros(b1.shape, F32)
    for e in range(N_EXPERTS):
        better = (masked[e] > b2) & (i1 != e)
        i2 = jnp.where(better, e, i2)
        b2 = jnp.where(better, masked[e], b2)
        s2 = jnp.where(better, srow[e], s2)
    den = s1 + s2
    return i1, i2, s1 / den, s2 / den


def _store_route(idx_ref, wt_ref, route):
    i1, i2, w1, w2 = route
    rid = lax.broadcasted_iota(I32, idx_ref.shape, 0)
    idx_ref[...] = jnp.where(rid == 0, i1, jnp.where(rid == 1, i2, 0))
    wt_ref[...] = jnp.where(rid == 0, w1, jnp.where(rid == 1, w2, 0.0))


def _router_operands(router_w, router_b):
    rw = jnp.zeros((D_MODEL, 128), F32).at[:, :N_EXPERTS].set(router_w.astype(F32))
    rwh = rw.astype(BF16)
    rwl = (rw - rwh.astype(F32)).astype(BF16)
    rb = router_b.astype(F32)[:, None]
    return rwh, rwl, rb


_ROUTER_SPECS = [
    pl.BlockSpec((D_MODEL, 128), lambda i: (0, 0)),
    pl.BlockSpec((D_MODEL, 128), lambda i: (0, 0)),
    pl.BlockSpec((N_EXPERTS, 1), lambda i: (0, 0)),
]


def _mix_out_specs(ts, T):
    specs = [
        pl.BlockSpec((ts, D_MODEL), lambda i: (i, 0)),
        pl.BlockSpec((8, ts), lambda i: (0, i)),
        pl.BlockSpec((8, ts), lambda i: (0, i)),
    ]
    shapes = [
        jax.ShapeDtypeStruct((T, D_MODEL), F32),
        jax.ShapeDtypeStruct((8, T), I32),
        jax.ShapeDtypeStruct((8, T), F32),
    ]
    return specs, shapes


def _outproj_even_kernel(f_ref, a_ref, x_ref, wf_ref, wa_ref, g_ref, b_ref, rwh_ref, rwl_ref, rb_ref,
                         x1_ref, idx_ref, wt_ref, *, alpha):
    m = (jnp.dot(f_ref[...], wf_ref[...], preferred_element_type=F32)
         + jnp.dot(a_ref[...], wa_ref[...], preferred_element_type=F32))
    x1 = _layer_norm(alpha * x_ref[...] + m, g_ref[...], b_ref[...])
    x1_ref[...] = x1
    _store_route(idx_ref, wt_ref, _route(x1, rwh_ref, rwl_ref, rb_ref))


def _outproj_even(f, a, x, w_out, ln_g, ln_b, router, alpha, ts):
    T = x.shape[0]
    wf = w_out[:D_FOURIER].astype(BF16)
    wa = w_out[D_FOURIER:].reshape(N_KV_B, GRP_B * HEAD_DIM, D_MODEL)
    wa = jnp.concatenate([jnp.zeros((N_KV_B, HEAD_DIM, D_MODEL), w_out.dtype), wa], axis=1)
    wa = wa.reshape(1024, D_MODEL).astype(BF16)
    out_specs, out_shapes = _mix_out_specs(ts, T)
    return pl.pallas_call(
        functools.partial(_outproj_even_kernel, alpha=alpha),
        grid=(T // ts,),
        in_specs=[
            pl.BlockSpec((ts, D_FOURIER), lambda i: (i, 0)),
            pl.BlockSpec((ts, 1024), lambda i: (i, 0)),
            pl.BlockSpec((ts, D_MODEL), lambda i: (i, 0)),
            pl.BlockSpec((D_FOURIER, D_MODEL), lambda i: (0, 0)),
            pl.BlockSpec((1024, D_MODEL), lambda i: (0, 0)),
            pl.BlockSpec((1, D_MODEL), lambda i: (0, 0)),
            pl.BlockSpec((1, D_MODEL), lambda i: (0, 0)),
        ] + _ROUTER_SPECS,
        out_specs=out_specs,
        out_shape=out_shapes,
        compiler_params=_cparams(("parallel",)),
        name="outproj_even",
    )(f, a, x, wf, wa, ln_g[None, :], ln_b[None, :], *router)


def _moe_kernel(te_ref, nu_ref, tok_ref, tokn_ref, x_hbm, wg_ref, wu_ref, wd_ref, y_ref, xbuf, sem, *, tm):
    i = pl.program_id(0)
    n = pl.num_programs(0)
    slot = i % 2

    def gather_copy(row_tok, row, s):
        return pltpu.make_async_copy(x_hbm.at[pl.ds(row_tok, 1)], xbuf.at[s, pl.ds(row, 1)], sem.at[s])

    def issue(ref, s):
        def body(r, c):
            gather_copy(ref[0, 0, r], r, s).start()
            return c
        lax.fori_loop(0, tm, body, 0, unroll=8)

    @pl.when(i == 0)
    def _():
        issue(tok_ref, 0)

    @pl.when(i + 1 < n)
    def _():
        issue(tokn_ref, 1 - slot)

    def drain(r, c):
        gather_copy(0, r, slot).wait()
        return c
    lax.fori_loop(0, tm, drain, 0, unroll=8)

    @pl.when(i < nu_ref[0])
    def _():
        xg = xbuf[slot].astype(BF16)
        hg = jnp.dot(xg, wg_ref[0], preferred_element_type=F32)
        hu = jnp.dot(xg, wu_ref[0], preferred_element_type=F32)
        hdn = (hg / (1.0 + jnp.exp(-hg)) * hu).astype(BF16)
        y_ref[...] = jnp.dot(hdn, wd_ref[0], preferred_element_type=F32)

    @pl.when(i >= nu_ref[0])
    def _():
        y_ref[...] = jnp.zeros(y_ref.shape, F32)


def _moe_plan(idx, tm):
    T = idx.shape[1]
    e = idx.T.reshape(-1)
    oh = (e[:, None] == jnp.arange(N_EXPERTS, dtype=I32)[None, :]).astype(I32)
    csum = jnp.cumsum(oh, axis=0)
    counts = csum[-1]
    rank = jnp.take_along_axis(csum, e[:, None], axis=1)[:, 0] - 1
    padded = ((counts + tm - 1) // tm) * tm
    ends = jnp.cumsum(padded)
    starts = ends - padded
    pos = starts[e] + rank
    n_rows = 2 * T + N_EXPERTS * tm
    row_tok = jnp.zeros((n_rows,), I32).at[pos].set(jnp.arange(2 * T, dtype=I32) // 2)
    tile_start = jnp.arange(n_rows // tm, dtype=I32) * tm
    tile_e = jnp.minimum(jnp.searchsorted(ends, tile_start, side="right"), N_EXPERTS - 1).astype(I32)
    n_used = (ends[-1] // tm).astype(I32)[None]
    return row_tok, tile_e, n_used, pos.astype(I32)


def _moe_experts(x1, row_tok, tile_e, n_used, wg, wu, wd, tm):
    n_rows = row_tok.shape[0]
    n_tiles = n_rows // tm
    tok3 = row_tok.reshape(n_tiles, 1, tm)
    grid_spec = pltpu.PrefetchScalarGridSpec(
        num_scalar_prefetch=2,
        grid=(n_tiles,),
        in_specs=[
            pl.BlockSpec((1, 1, tm), lambda i, te, nu: (i, 0, 0), memory_space=pltpu.SMEM),
            pl.BlockSpec((1, 1, tm), lambda i, te, nu: (jnp.minimum(i + 1, n_tiles - 1), 0, 0),
                         memory_space=pltpu.SMEM),
            pl.BlockSpec(memory_space=pl.ANY),
            pl.BlockSpec((1, D_MODEL, D_FF), lambda i, te, nu: (te[i], 0, 0)),
            pl.BlockSpec((1, D_MODEL, D_FF), lambda i, te, nu: (te[i], 0, 0)),
            pl.BlockSpec((1, D_FF, D_MODEL), lambda i, te, nu: (te[i], 0, 0)),
        ],
        out_specs=pl.BlockSpec((tm, D_MODEL), lambda i, te, nu: (i, 0)),
        scratch_shapes=[pltpu.VMEM((2, tm, D_MODEL), F32), pltpu.SemaphoreType.DMA((2,))],
    )
    return pl.pallas_call(
        functools.partial(_moe_kernel, tm=tm),
        grid_spec=grid_spec,
        out_shape=jax.ShapeDtypeStruct((n_rows, D_MODEL), F32),
        compiler_params=_cparams(("arbitrary",)),
        name="moe_experts",
    )(tile_e, n_used, tok3, tok3, x1, wg, wu, wd)


def _combine_kernel(pos_ref, posn_ref, y_hbm, x_ref, wt_ref, g_ref, b_ref, o_ref, ybuf, sem, *, tc, alpha):
    i = pl.program_id(0)
    n = pl.num_programs(0)
    slot = i % 2

    def gather_copy(src_row, k, row, s):
        return pltpu.make_async_copy(y_hbm.at[pl.ds(src_row, 1)], ybuf.at[s, k, pl.ds(row, 1)], sem.at[s])

    def issue(ref, s):
        def body(r, c):
            gather_copy(ref[0, 0, 2 * r], 0, r, s).start()
            gather_copy(ref[0, 0, 2 * r + 1], 1, r, s).start()
            return c
        lax.fori_loop(0, tc, body, 0, unroll=8)

    @pl.when(i == 0)
    def _():
        issue(pos_ref, 0)

    @pl.when(i + 1 < n)
    def _():
        issue(posn_ref, 1 - slot)

    def drain(r, c):
        gather_copy(0, 0, r, slot).wait()
        gather_copy(0, 1, r, slot).wait()
        return c
    lax.fori_loop(0, tc, drain, 0, unroll=8)

    w = wt_ref[...]
    f = w[:, 0:1] * ybuf[slot, 0] + w[:, 1:2] * ybuf[slot, 1]
    o_ref[...] = _layer_norm(alpha * x_ref[...] + f, g_ref[...], b_ref[...])


def _moe_combine(y, pos, wts, x1, ln_g, ln_b, alpha, tc):
    T = x1.shape[0]
    n_tiles = T // tc
    pos3 = pos.reshape(n_tiles, 1, 2 * tc)
    return pl.pallas_call(
        functools.partial(_combine_kernel, tc=tc, alpha=alpha),
        grid=(n_tiles,),
        in_specs=[
            pl.BlockSpec((1, 1, 2 * tc), lambda i: (i, 0, 0), memory_space=pltpu.SMEM),
            pl.BlockSpec((1, 1, 2 * tc), lambda i: (jnp.minimum(i + 1, n_tiles - 1), 0, 0),
                         memory_space=pltpu.SMEM),
            pl.BlockSpec(memory_space=pl.ANY),
            pl.BlockSpec((tc, D_MODEL), lambda i: (i, 0)),
            pl.BlockSpec((tc, 2), lambda i: (i, 0)),
            pl.BlockSpec((1, D_MODEL), lambda i: (0, 0)),
            pl.BlockSpec((1, D_MODEL), lambda i: (0, 0)),
        ],
        out_specs=pl.BlockSpec((tc, D_MODEL), lambda i: (i, 0)),
        out_shape=jax.ShapeDtypeStruct((T, D_MODEL), F32),
        scratch_shapes=[pltpu.VMEM((2, 2, tc, D_MODEL), F32), pltpu.SemaphoreType.DMA((2,))],
        compiler_params=_cparams(("arbitrary",)),
        name="moe_combine",
    )(pos3, pos3, y, x1, wts, ln_g[None, :], ln_b[None, :])


def _moe_layer(x1, idx, wt, wg, wu, wd, ln_g, ln_b, alpha, tm, tc):
    row_tok, tile_e, n_used, pos = _moe_plan(idx[:2], tm)
    y = _moe_experts(x1, row_tok, tile_e, n_used, wg.astype(BF16), wu.astype(BF16), wd.astype(BF16), tm)
    return _moe_combine(y, pos, wt[:2].T, x1, ln_g, ln_b, alpha, tc)


def _inproj_odd_kernel(x_ref, w_ref, vone_ref, bg_ref, u_ref, q_ref, k_ref, v_ref):
    x = x_ref[...].astype(BF16)
    proj = jnp.dot(x, w_ref[...], preferred_element_type=F32)
    bg_ref[...] = proj[:, :512].astype(BF16)
    u_ref[...] = (proj[:, 512:1024] * proj[:, 1024:1536]).astype(BF16)
    q_ref[...] = (proj[:, 1536:2048] * (HEAD_DIM ** -0.5 * LOG2E)).astype(BF16)
    k_ref[...] = proj[:, 2048:2304].astype(BF16)
    v_ref[...] = (proj[:, 2304:2560] + vone_ref[...]).astype(BF16)


def _inproj_odd(x, w_in, ts):
    T = x.shape[0]
    c3 = 3 * D_CONV
    wk = w_in[:, c3 + 512:c3 + 640].reshape(D_MODEL, N_KV_D, HEAD_DIM)
    wv = w_in[:, c3 + 640:c3 + 768].reshape(D_MODEL, N_KV_D, HEAD_DIM)
    wkp = jnp.concatenate([wk, jnp.zeros_like(wk)], axis=2).reshape(D_MODEL, 256)
    wvp = jnp.concatenate([wv, jnp.zeros_like(wv)], axis=2).reshape(D_MODEL, 256)
    w = jnp.concatenate([w_in[:, :c3 + 512], wkp, wvp], axis=1).astype(BF16)
    vone_np = np.zeros((1, 256), np.float32)
    for h in range(N_KV_D):
        vone_np[0, 128 * h + 64:128 * (h + 1)] = 1.0
    n_w = w.shape[1]
    widths = [512, 512, 512, 256, 256]
    return pl.pallas_call(
        _inproj_odd_kernel,
        grid=(T // ts,),
        in_specs=[
            pl.BlockSpec((ts, D_MODEL), lambda i: (i, 0)),
            pl.BlockSpec((D_MODEL, n_w), lambda i: (0, 0)),
            pl.BlockSpec((1, 256), lambda i: (0, 0)),
        ],
        out_specs=[pl.BlockSpec((ts, n), lambda i: (i, 0)) for n in widths],
        out_shape=[jax.ShapeDtypeStruct((T, n), BF16) for n in widths],
        compiler_params=_cparams(("parallel",)),
        name="inproj_odd",
    )(x, w, jnp.asarray(vone_np))


def _wattn_kernel(q_ref, kp_ref, kc_ref, kn_ref, vp_ref, vc_ref, vn_ref, bias_ref, sink_ref, place_ref,
                  o_ref, *, tq, seq_tiles):
    i = pl.program_id(0)
    first = i < 0
    last = i < 0
    for lo, hi, per in seq_tiles:
        inside = (i >= lo) & (i < hi)
        first = first | (inside & ((i - lo) % per == 0))
        last = last | (inside & ((i - lo) % per == per - 1))
    kfull = jnp.concatenate([kp_ref[...], kc_ref[...], kn_ref[...]], axis=0)[:, :HEAD_DIM]
    vfull = jnp.concatenate([vp_ref[...], vc_ref[...], vn_ref[...]], axis=0)
    q4 = q_ref[...]
    bias = bias_ref[0]
    sink = sink_ref[0]
    nb = tq // Q_BLOCK
    col = lax.broadcasted_iota(I32, (GRP_D * Q_BLOCK, 3 * Q_BLOCK), 1)
    for n in range(nb):
        qs = jnp.concatenate([q4[Q_BLOCK * n:Q_BLOCK * (n + 1), HEAD_DIM * g:HEAD_DIM * (g + 1)]
                              for g in range(GRP_D)], axis=0)
        keys = kfull[Q_BLOCK * n:Q_BLOCK * (n + 3)]
        vals = vfull[Q_BLOCK * n:Q_BLOCK * (n + 3)]
        s = lax.dot_general(qs, keys, (((1,), (1,)), ((), ())), preferred_element_type=F32) + bias
        if n == 0:
            s = jnp.where(first & (col < Q_BLOCK), NEG_INF, s)
        if n == nb - 1:
            s = jnp.where(last & (col >= 2 * Q_BLOCK), NEG_INF, s)
        m = jnp.maximum(jnp.max(s, axis=1, keepdims=True), sink)
        p = jnp.exp2(s - m).astype(BF16)
        acc = jnp.dot(p, vals, preferred_element_type=F32)
        den = acc[:, HEAD_DIM:HEAD_DIM + 1] + jnp.exp2(sink - m)
        o = (acc / den).astype(BF16)
        out = jnp.zeros((Q_BLOCK, 256), F32)
        for g in range(GRP_D):
            out = out + jnp.dot(o[Q_BLOCK * g:Q_BLOCK * (g + 1)], place_ref[g], preferred_element_type=F32)
        o_ref[Q_BLOCK * n:Q_BLOCK * (n + 1), :] = out.astype(BF16)


def _window_attention(q, kw, vw, sink_logits, classes, tq):
    T = q.shape[0]
    n_tiles = T // tq
    hb = tq // Q_BLOCK
    n_hblk = T // Q_BLOCK
    r = np.arange(Q_BLOCK)[:, None]
    j = np.arange(3 * Q_BLOCK)[None, :]
    rel = np.abs(j - Q_BLOCK - r).astype(np.float32)
    slopes = np.array([2.0 ** (-8.0 * (h + 1) / N_HEADS_D) for h in range(N_HEADS_D)], np.float32)
    bias_np = np.where(rel[None] <= WINDOW, -slopes[:, None, None] * rel[None] * LOG2E, NEG_INF)
    bias = jnp.asarray(bias_np.reshape(N_KV_D, GRP_D * Q_BLOCK, 3 * Q_BLOCK), F32)
    sink = jnp.repeat(sink_logits.astype(F32) * LOG2E, Q_BLOCK).reshape(N_KV_D, GRP_D * Q_BLOCK, 1)
    place = _lane_place(128, 256, [0, 1, 2, 3])
    seq_tiles = []
    t0 = 0
    for nseq, S, _ in classes:
        cnt = nseq * S // tq
        seq_tiles.append((t0, t0 + cnt, S // tq))
        t0 += cnt
    prev_map = lambda i, h: (jnp.maximum(i * hb - 1, 0), h)
    next_map = lambda i, h: (jnp.minimum((i + 1) * hb, n_hblk - 1), h)
    return pl.pallas_call(
        functools.partial(_wattn_kernel, tq=tq, seq_tiles=tuple(seq_tiles)),
        grid=(n_tiles, N_KV_D),
        in_specs=[
            pl.BlockSpec((tq, 256), lambda i, h: (i, h)),
            pl.BlockSpec((Q_BLOCK, 128), prev_map),
            pl.BlockSpec((tq, 128), lambda i, h: (i, h)),
            pl.BlockSpec((Q_BLOCK, 128), next_map),
            pl.BlockSpec((Q_BLOCK, 128), prev_map),
            pl.BlockSpec((tq, 128), lambda i, h: (i, h)),
            pl.BlockSpec((Q_BLOCK, 128), next_map),
            pl.BlockSpec((1, GRP_D * Q_BLOCK, 3 * Q_BLOCK), lambda i, h: (h, 0, 0)),
            pl.BlockSpec((1, GRP_D * Q_BLOCK, 1), lambda i, h: (h, 0, 0)),
            pl.BlockSpec((GRP_D, 128, 256), lambda i, h: (0, 0, 0)),
        ],
        out_specs=pl.BlockSpec((tq, 256), lambda i, h: (i, h)),
        out_shape=jax.ShapeDtypeStruct((T, 512), BF16),
        compiler_params=_cparams(("parallel", "parallel")),
        name="window_attention",
    )(q, kw, kw, kw, vw, vw, vw, bias, sink, place)


HALO = 16


def _outproj_odd_kernel(bg_ref, u_ref, up_ref, un_ref, a_ref, x_ref, cw_ref, cb_ref, wc_ref, wa_ref,
                        g_ref, b_ref, rwh_ref, rwl_ref, rb_ref, x1_ref, idx_ref, wt_ref,
                        *, alpha, ts, seq_tiles):
    i = pl.program_id(0)
    first = i < 0
    last = i < 0
    for lo, hi, per in seq_tiles:
        inside = (i >= lo) & (i < hi)
        first = first | (inside & ((i - lo) % per == 0))
        last = last | (inside & ((i - lo) % per == per - 1))
    u = u_ref[...].astype(F32)
    prev_row = jnp.where(first, 0.0, up_ref[HALO - 1:HALO, :].astype(F32))
    next_row = jnp.where(last, 0.0, un_ref[0:1, :].astype(F32))
    rid = lax.broadcasted_iota(I32, u.shape, 0)
    ud = jnp.where(rid == 0, prev_row, pltpu.roll(u, 1, 0))
    uu = jnp.where(rid == ts - 1, next_row, pltpu.roll(u, ts - 1, 0))
    cw = cw_ref[...]
    y = ud * cw[0:1] + u * cw[1:2] + uu * cw[2:3] + cb_ref[...]
    c = (bg_ref[...].astype(F32) * y).astype(BF16)
    m = (jnp.dot(c, wc_ref[...], preferred_element_type=F32)
         + jnp.dot(a_ref[...], wa_ref[...], preferred_element_type=F32))
    x1 = _layer_norm(alpha * x_ref[...] + m, g_ref[...], b_ref[...])
    x1_ref[...] = x1
    _store_route(idx_ref, wt_ref, _route(x1, rwh_ref, rwl_ref, rb_ref))


def _outproj_odd(bg, u, a, x, conv_w, conv_b, w_out, ln_g, ln_b, router, alpha, classes, ts):
    T = x.shape[0]
    hb = ts // HALO
    n_h = T // HALO
    seq_tiles = []
    t0 = 0
    for nseq, S, _ in classes:
        cnt = nseq * S // ts
        seq_tiles.append((t0, t0 + cnt, S // ts))
        t0 += cnt
    out_specs, out_shapes = _mix_out_specs(ts, T)
    return pl.pallas_call(
        functools.partial(_outproj_odd_kernel, alpha=alpha, ts=ts, seq_tiles=tuple(seq_tiles)),
        grid=(T // ts,),
        in_specs=[
            pl.BlockSpec((ts, D_CONV), lambda i: (i, 0)),
            pl.BlockSpec((ts, D_CONV), lambda i: (i, 0)),
            pl.BlockSpec((HALO, D_CONV), lambda i: (jnp.maximum(i * hb - 1, 0), 0)),
            pl.BlockSpec((HALO, D_CONV), lambda i: (jnp.minimum((i + 1) * hb, n_h - 1), 0)),
            pl.BlockSpec((ts, 512), lambda i: (i, 0)),
            pl.BlockSpec((ts, D_MODEL), lambda i: (i, 0)),
            pl.BlockSpec((3, D_CONV), lambda i: (0, 0)),
            pl.BlockSpec((1, D_CONV), lambda i: (0, 0)),
            pl.BlockSpec((D_CONV, D_MODEL), lambda i: (0, 0)),
            pl.BlockSpec((512, D_MODEL), lambda i: (0, 0)),
            pl.BlockSpec((1, D_MODEL), lambda i: (0, 0)),
            pl.BlockSpec((1, D_MODEL), lambda i: (0, 0)),
        ] + _ROUTER_SPECS,
        out_specs=out_specs,
        out_shape=out_shapes,
        compiler_params=_cparams(("parallel",)),
        name="outproj_odd",
    )(bg, u, u, u, a, x, conv_w.astype(F32), conv_b.astype(F32)[None, :],
      w_out[:D_CONV].astype(BF16), w_out[D_CONV:].astype(BF16), ln_g[None, :], ln_b[None, :], *router)


def _tile(n, cap):
    t = cap
    while n % t:
        t //= 2
    return t


def kernel(x_prompt, x_sample, w_in_even, fourier_norm_g, q_norm_g, k_norm_g, w_out_even, w_in_odd, conv_w,
           conv_b, sink_logits, w_out_odd, ln_mix_g, ln_mix_b, ln_ffn_g, ln_ffn_b, router_w, router_b,
           w_gate, w_up, w_down):
    depth = ln_mix_g.shape[0]
    alpha = float((2 * depth) ** 0.25)
    bp, sp, _ = x_prompt.shape
    bs, ss, _ = x_sample.shape
    classes = ((bp, sp, 0), (bs, ss, bp * sp))
    T = bp * sp + bs * ss
    min_s = min(sp, ss)
    ts = _tile(min_s, 512)
    tq_w = _tile(min_s, 256)
    tm = 256
    tc = _tile(min_s, 256)
    x = jnp.concatenate([x_prompt.reshape(bp * sp, D_MODEL), x_sample.reshape(bs * ss, D_MODEL)], axis=0)
    router = _router_operands(router_w, router_b)
    for l in range(depth):
        i = l // 2
        if l % 2 == 0:
            uf, qk, v_aug = _inproj_even(x, w_in_even[i], fourier_norm_g[i], q_norm_g[i], k_norm_g[i],
                                         classes, ts)
            f_parts, a_parts = [], []
            for nseq, S, tok0 in classes:
                f_parts.append(_fourier_mix(uf[tok0:tok0 + nseq * S], nseq, S))
                a_parts.append(_global_attention(qk, v_aug, nseq, S, tok0, _tile(S, 512), _tile(S, 512)))
            f = jnp.concatenate(f_parts, axis=0)
            a = jnp.concatenate(a_parts, axis=0)
            x1, idx, wt = _outproj_even(f, a, x, w_out_even[i], ln_mix_g[l], ln_mix_b[l], router, alpha, ts)
        else:
            bg, u, q, kw, vw = _inproj_odd(x, w_in_odd[i], ts)
            a = _window_attention(q, kw, vw, sink_logits[i], classes, tq_w)
            x1, idx, wt = _outproj_odd(bg, u, a, x, conv_w[i], conv_b[i], w_out_odd[i], ln_mix_g[l],
                                       ln_mix_b[l], router, alpha, classes, ts)
        x = _moe_layer(x1, idx, wt, w_gate[l], w_up[l], w_down[l], ln_ffn_g[l], ln_ffn_b[l], alpha, tm, tc)
    y_prompt = x[:bp * sp].reshape(bp, sp, D_MODEL)
    y_sample = x[bp * sp:].reshape(bs, ss, D_MODEL)
    return (y_prompt, y_sample)
```

```python
import functools
import math

import numpy as np
import jax
import jax.numpy as jnp
from jax import lax
from jax.experimental import pallas as pl
from jax.experimental.pallas import tpu as pltpu

F32 = jnp.float32
BF16 = jnp.bfloat16
I32 = jnp.int32

D_MODEL = 1024
HEAD_DIM = 64
GRID_W = 64
Q_BLOCK = 128
WINDOW = 128
ROPE_THETA = 10000.0
N_FOURIER_GROUPS = 4
D_FOURIER = 256
N_HEADS_B = 12
N_KV_B = 4
GRP_B = 3
D_CONV = 512
N_HEADS_D = 8
N_KV_D = 2
GRP_D = 4
N_EXPERTS = 16
N_GROUPS = 4
EXPERTS_PER_GROUP = 4
D_FF = 512
LN_EPS = 1e-5
RMS_EPS = 1e-6
NEG_INF = -1e30
LOG2E = 1.4426950408889634
FFT_S2 = 128
V7X_VMEM_LIMIT = 48 * 1024 * 1024


def _cparams(sem):
    return pltpu.CompilerParams(dimension_semantics=sem, vmem_limit_bytes=V7X_VMEM_LIMIT)


def _layer_norm(y, g, b):
    mu = jnp.mean(y, axis=-1, keepdims=True)
    yc = y - mu
    var = jnp.mean(yc * yc, axis=-1, keepdims=True)
    return yc * lax.rsqrt(var + LN_EPS) * g + b


N_NORM_COLS = D_FOURIER + 1024


def _inproj_even_kernel(x_ref, w_ref, gain_ref, seg_ref, segt_ref, cos_ref, sin_ref, vone_ref,
                        uf_ref, qk_ref, v_ref):
    x = x_ref[...].astype(BF16)
    proj = jnp.dot(x, w_ref[...], preferred_element_type=F32)
    nrm = proj[:, :N_NORM_COLS]
    sq = (nrm * nrm).astype(BF16)
    ssum = jnp.dot(sq, seg_ref[...], preferred_element_type=F32)
    r = lax.rsqrt(ssum * (1.0 / HEAD_DIM) + RMS_EPS)
    rh = r.astype(BF16)
    rl = (r - rh.astype(F32)).astype(BF16)
    rex = jnp.dot(jnp.concatenate([rh, rl], axis=1), segt_ref[...], preferred_element_type=F32)
    y = nrm * rex * gain_ref[...]
    uf_ref[...] = y[:, :D_FOURIER].astype(BF16)
    yq = y[:, D_FOURIER:]
    c = jnp.concatenate([cos_ref[...]] * 8, axis=1)
    s = jnp.concatenate([sin_ref[...]] * 8, axis=1)
    lane = lax.broadcasted_iota(I32, yq.shape, 1)
    first = (lane & 31) < 16
    sw = jnp.where(first, pltpu.roll(yq, 1024 - 16, 1), pltpu.roll(yq, 16, 1))
    qk_ref[...] = (yq * c + sw * s).astype(BF16)
    vv = proj[:, N_NORM_COLS:] + vone_ref[...]
    for h in range(N_KV_B):
        v_ref[h] = vv[:, 128 * h:128 * (h + 1)].astype(BF16)


def _rope_tables(n_pos):
    t = jnp.arange(n_pos)
    row = (t // GRID_W).astype(F32)
    col = (t % GRID_W).astype(F32)
    n_freq = HEAD_DIM // 4
    inv_freq = ROPE_THETA ** (-jnp.arange(n_freq, dtype=F32) / n_freq)
    ar = row[:, None] * inv_freq
    ac = col[:, None] * inv_freq
    ang = jnp.concatenate([ar, ar, ac, ac], axis=1)
    sign = jnp.asarray(np.tile(np.repeat(np.array([-1.0, 1.0], np.float32), 16), 2))
    cos = jnp.cos(ang)
    sin = jnp.sin(ang) * sign
    return jnp.concatenate([cos, cos], axis=1), jnp.concatenate([sin, sin], axis=1)


def _pos_block_map(classes, ts):
    bounds = []
    tile0 = 0
    for nseq, S, _ in classes:
        n_tiles = nseq * S // ts
        bounds.append((tile0, tile0 + n_tiles, S // ts))
        tile0 += n_tiles

    def fn(i):
        out = (i - bounds[-1][0]) % bounds[-1][2]
        for lo, hi, per in reversed(bounds[:-1]):
            out = jnp.where(i < hi, (i - lo) % per, out)
        return out

    return fn


def _inproj_even(x, w_in, f_g, q_g, k_g, classes, ts):
    T = x.shape[0]
    wf = w_in[:, :D_FOURIER]
    wq = w_in[:, D_FOURIER:D_FOURIER + 768].reshape(D_MODEL, N_KV_B, GRP_B * HEAD_DIM)
    wk = w_in[:, D_FOURIER + 768:D_FOURIER + 1024].reshape(D_MODEL, N_KV_B, HEAD_DIM)
    wv = w_in[:, D_FOURIER + 1024:].reshape(D_MODEL, N_KV_B, HEAD_DIM)
    wqk = jnp.concatenate([wk, wq], axis=2).reshape(D_MODEL, 1024)
    wvp = jnp.concatenate([wv, jnp.zeros_like(wv)], axis=2).reshape(D_MODEL, 512)
    w = jnp.concatenate([wf, wqk, wvp], axis=1).astype(BF16)
    qscale = HEAD_DIM ** -0.5 * LOG2E
    gqk = jnp.tile(jnp.concatenate([k_g, q_g * qscale, q_g * qscale, q_g * qscale]), N_KV_B)
    gain = jnp.concatenate([f_g, gqk])[None, :].astype(F32)
    seg_np = np.zeros((N_NORM_COLS, 128), np.float32)
    seg_np[np.arange(N_NORM_COLS), np.arange(N_NORM_COLS) // HEAD_DIM] = 1.0
    seg = jnp.asarray(seg_np, BF16)
    segt = jnp.asarray(np.concatenate([seg_np.T, seg_np.T], axis=0), BF16)
    max_s = max(S for _, S, _ in classes)
    cos, sin = _rope_tables(max_s)
    vone_np = np.zeros((1, 512), np.float32)
    for h in range(N_KV_B):
        vone_np[0, 128 * h + 64:128 * (h + 1)] = 1.0
    vone = jnp.asarray(vone_np)
    posmap = _pos_block_map(classes, ts)
    n_w = w.shape[1]
    return pl.pallas_call(
        _inproj_even_kernel,
        grid=(T // ts,),
        in_specs=[
            pl.BlockSpec((ts, D_MODEL), lambda i: (i, 0)),
            pl.BlockSpec((D_MODEL, n_w), lambda i: (0, 0)),
            pl.BlockSpec((1, N_NORM_COLS), lambda i: (0, 0)),
            pl.BlockSpec((N_NORM_COLS, 128), lambda i: (0, 0)),
            pl.BlockSpec((256, N_NORM_COLS), lambda i: (0, 0)),
            pl.BlockSpec((ts, 128), lambda i: (posmap(i), 0)),
            pl.BlockSpec((ts, 128), lambda i: (posmap(i), 0)),
            pl.BlockSpec((1, 512), lambda i: (0, 0)),
        ],
        out_specs=[
            pl.BlockSpec((ts, D_FOURIER), lambda i: (i, 0)),
            pl.BlockSpec((ts, 1024), lambda i: (i, 0)),
            pl.BlockSpec((N_KV_B, ts, 128), lambda i: (0, i, 0)),
        ],
        out_shape=[
            jax.ShapeDtypeStruct((T, D_FOURIER), BF16),
            jax.ShapeDtypeStruct((T, 1024), BF16),
            jax.ShapeDtypeStruct((N_KV_B, T, 128), BF16),
        ],
        compiler_params=_cparams(("parallel",)),
        name="inproj_even",
    )(x, w, gain, seg, segt, cos, sin, vone)


def _fft1_kernel(a_ref, w1_ref, twr_ref, twi_ref, zr_ref, zi_ref, *, s1):
    z = jnp.dot(w1_ref[...], a_ref[0], preferred_element_type=F32)
    zr = z[:s1]
    zi = z[s1:]
    twr = twr_ref[...]
    twi = twi_ref[...]
    zr_ref[0] = (zr * twr - zi * twi).astype(BF16)
    zi_ref[0] = (zr * twi + zi * twr).astype(BF16)


def _fft2_kernel(zr_ref, zi_ref, w2a_ref, w2b_ref, mix_ref, o_ref, *, cb):
    s2 = FFT_S2
    for c in range(cb):
        pp = (jnp.dot(w2a_ref[...], zr_ref[0, c], preferred_element_type=F32)
              + jnp.dot(w2b_ref[...], zi_ref[0, c], preferred_element_type=F32))
        f = (jnp.dot(pp[:s2].astype(BF16), mix_ref[:D_FOURIER], preferred_element_type=F32)
             + jnp.dot(pp[s2:].astype(BF16), mix_ref[D_FOURIER:], preferred_element_type=F32))
        o_ref[0, :, D_FOURIER * c:D_FOURIER * (c + 1)] = f.astype(BF16)


def _dft_mats(n):
    k = np.arange(n)
    ang = 2.0 * np.pi * ((k[:, None] * k[None, :]) % n) / n
    return np.cos(ang), np.sin(ang)


def _fourier_mix(uf_part, nseq, S):
    s2 = FFT_S2
    s1 = S // s2
    ncol = s2 * D_FOURIER
    a = uf_part.reshape(nseq, s1, ncol)
    c1, sn1 = _dft_mats(s1)
    w1 = jnp.asarray(np.concatenate([c1, -sn1], axis=0), BF16)
    cc = jnp.arange(s1, dtype=I32)[:, None]
    bb = jnp.arange(s2, dtype=I32)[None, :]
    ang = (2.0 * math.pi / S) * ((cc * bb) % S).astype(F32)
    twr = jnp.repeat(jnp.cos(ang), D_FOURIER, axis=1)
    twi = jnp.repeat(-jnp.sin(ang), D_FOURIER, axis=1)
    tn = min(ncol, 4096)
    zr, zi = pl.pallas_call(
        functools.partial(_fft1_kernel, s1=s1),
        grid=(ncol // tn, nseq),
        in_specs=[
            pl.BlockSpec((1, s1, tn), lambda j, b: (b, 0, j)),
            pl.BlockSpec((2 * s1, s1), lambda j, b: (0, 0)),
            pl.BlockSpec((s1, tn), lambda j, b: (0, j)),
            pl.BlockSpec((s1, tn), lambda j, b: (0, j)),
        ],
        out_specs=[
            pl.BlockSpec((1, s1, tn), lambda j, b: (b, 0, j)),
            pl.BlockSpec((1, s1, tn), lambda j, b: (b, 0, j)),
        ],
        out_shape=[jax.ShapeDtypeStruct((nseq, s1, ncol), BF16)] * 2,
        compiler_params=_cparams(("parallel", "parallel")),
        name="fourier_stage1",
    )(a, w1, twr, twi)
    zr = zr.reshape(nseq, s1, s2, D_FOURIER)
    zi = zi.reshape(nseq, s1, s2, D_FOURIER)
    c2, sn2 = _dft_mats(s2)
    w2a = jnp.asarray(np.concatenate([c2, -sn2], axis=0), BF16)
    w2b = jnp.asarray(np.concatenate([sn2, c2], axis=0), BF16)
    gc, gs = _dft_mats(HEAD_DIM)
    scale = 1.0 / math.sqrt(S * HEAD_DIM)
    eye = np.eye(N_FOURIER_GROUPS)
    mix = jnp.asarray(np.concatenate([np.kron(eye, gc), np.kron(eye, gs)], axis=0) * scale, BF16)
    cb = min(8, s1)
    out = pl.pallas_call(
        functools.partial(_fft2_kernel, cb=cb),
        grid=(nseq, s1 // cb),
        in_specs=[
            pl.BlockSpec((1, cb, s2, D_FOURIER), lambda b, j: (b, j, 0, 0)),
            pl.BlockSpec((1, cb, s2, D_FOURIER), lambda b, j: (b, j, 0, 0)),
            pl.BlockSpec((2 * s2, s2), lambda b, j: (0, 0)),
            pl.BlockSpec((2 * s2, s2), lambda b, j: (0, 0)),
            pl.BlockSpec((2 * D_FOURIER, D_FOURIER), lambda b, j: (0, 0)),
        ],
        out_specs=pl.BlockSpec((1, s2, cb * D_FOURIER), lambda b, j: (b, 0, j)),
        out_shape=jax.ShapeDtypeStruct((nseq, s2, s1 * D_FOURIER), BF16),
        compiler_params=_cparams(("parallel", "parallel")),
        name="fourier_stage2",
    )(zr, zi, w2a, w2b, mix)
    return out.reshape(nseq * S, D_FOURIER)


SCORE_BOUND_MAX = 100.0


def _gattn_kernel(bounded_ref, q_ref, k_ref, v_ref, place_ref, o_ref, *, tq, tk, S):
    q3 = q_ref[...]
    qs = [q3[:, HEAD_DIM * (g + 1):HEAD_DIM * (g + 2)] for g in range(GRP_B)]

    def chunk(j):
        off = pl.multiple_of(j * tk, tk)
        return k_ref[pl.ds(off, tk), 0:HEAD_DIM], v_ref[0, pl.ds(off, tk), :]

    def scores(g, kc):
        return lax.dot_general(qs[g], kc, (((1,), (1,)), ((), ())), preferred_element_type=F32)

    def finish(accs):
        out = jnp.zeros((tq, 256), F32)
        for g in range(GRP_B):
            o = (accs[g] / accs[g][:, HEAD_DIM:HEAD_DIM + 1]).astype(BF16)
            out = out + jnp.dot(o, place_ref[g], preferred_element_type=F32)
        o_ref[...] = out.astype(BF16)

    @pl.when(bounded_ref[0] == 1)
    def _():
        def body(j, accs):
            kc, vc = chunk(j)
            return tuple(accs[g] + jnp.dot(jnp.exp2(scores(g, kc)).astype(BF16), vc,
                                           preferred_element_type=F32) for g in range(GRP_B))
        finish(lax.fori_loop(0, S // tk, body, tuple(jnp.zeros((tq, 128), F32) for _ in range(GRP_B))))

    @pl.when(bounded_ref[0] != 1)
    def _():
        def body(j, carry):
            kc, vc = chunk(j)
            new = []
            for g in range(GRP_B):
                m_prev, acc = carry[g]
                s = scores(g, kc)
                m_new = jnp.maximum(m_prev, jnp.max(s, axis=1, keepdims=True))
                p = jnp.exp2(s - m_new).astype(BF16)
                acc = jnp.exp2(m_prev - m_new) * acc + jnp.dot(p, vc, preferred_element_type=F32)
                new.append((m_new, acc))
            return tuple(new)
        init = tuple((jnp.full((tq, 1), NEG_INF, F32), jnp.zeros((tq, 128), F32)) for _ in range(GRP_B))
        fin = lax.fori_loop(0, S // tk, body, init)
        finish([fin[g][1] for g in range(GRP_B)])


def _score_bounded(q_g, k_g):
    bound = 1.05 * HEAD_DIM * (HEAD_DIM ** -0.5 * LOG2E) * jnp.max(jnp.abs(q_g)) * jnp.max(jnp.abs(k_g))
    return (bound <= SCORE_BOUND_MAX).astype(I32)[None]


def _lane_place(n_src, n_dst, groups):
    pm = np.zeros((len(groups), n_src, n_dst), np.float32)
    for gi, slot in enumerate(groups):
        pm[gi, np.arange(HEAD_DIM), HEAD_DIM * slot + np.arange(HEAD_DIM)] = 1.0
    return jnp.asarray(pm, BF16)


def _global_attention(qk, v_aug, bounded, nseq, S, tok0, tq, tk):
    n_qt = S // tq
    qt0 = tok0 // tq
    s0 = tok0 // S
    place = _lane_place(128, 256, [1, 2, 3])
    grid_spec = pltpu.PrefetchScalarGridSpec(
        num_scalar_prefetch=1,
        grid=(nseq, N_KV_B, n_qt),
        in_specs=[
            pl.BlockSpec((tq, 256), lambda b, h, i, fl: (qt0 + b * n_qt + i, h)),
            pl.BlockSpec((S, 256), lambda b, h, i, fl: (s0 + b, h)),
            pl.BlockSpec((1, S, 128), lambda b, h, i, fl: (h, s0 + b, 0)),
            pl.BlockSpec((GRP_B, 128, 256), lambda b, h, i, fl: (0, 0, 0)),
        ],
        out_specs=pl.BlockSpec((tq, 256), lambda b, h, i, fl: (b * n_qt + i, h)),
    )
    return pl.pallas_call(
        functools.partial(_gattn_kernel, tq=tq, tk=tk, S=S),
        grid_spec=grid_spec,
        out_shape=jax.ShapeDtypeStruct((nseq * S, 1024), BF16),
        compiler_params=_cparams(("parallel", "parallel", "arbitrary")),
        name="global_attention",
    )(bounded, qk, qk, v_aug, place)


def _route(x1, rwh_ref, rwl_ref, rb_ref):
    xh = x1.astype(BF16)
    xl = (x1 - xh.astype(F32)).astype(BF16)
    lg = (jnp.dot(xh, rwh_ref[...], preferred_element_type=F32)
          + jnp.dot(xl, rwh_ref[...], preferred_element_type=F32)
          + jnp.dot(xh, rwl_ref[...], preferred_element_type=F32))
    lgt = lg.T[:N_EXPERTS]
    sc = 1.0 / (1.0 + jnp.exp(-lgt))
    bi = sc + rb_ref[...]
    srow = [sc[e:e + 1] for e in range(N_EXPERTS)]
    brow = [bi[e:e + 1] for e in range(N_EXPERTS)]
    gsel = None
    best = None
    for g in range(N_GROUPS):
        a, b, c, d = brow[4 * g:4 * g + 4]
        m1, n1 = jnp.maximum(a, b), jnp.minimum(a, b)
        m2, n2 = jnp.maximum(c, d), jnp.minimum(c, d)
        gs = jnp.maximum(m1, m2) + jnp.maximum(jnp.minimum(m1, m2), jnp.maximum(n1, n2))
        if g == 0:
            best, gsel = gs, jnp.zeros(gs.shape, I32)
        else:
            better = gs > best
            gsel = jnp.where(better, g, gsel)
            best = jnp.where(better, gs, best)
    masked = [jnp.where(gsel == (e // EXPERTS_PER_GROUP), brow[e], NEG_INF) for e in range(N_EXPERTS)]
    i1 = jnp.zeros(gsel.shape, I32)
    b1 = masked[0]
    s1 = srow[0]
    for e in range(1, N_EXPERTS):
        better = masked[e] > b1
        i1 = jnp.where(better, e, i1)
        b1 = jnp.where(better, masked[e], b1)
        s1 = jnp.where(better, srow[e], s1)
    i2 = jnp.full(gsel.shape, -1, I32)
    b2 = jnp.full(b1.shape, -jnp.inf, F32)
    s2 = jnp.zeros(b1.shape, F32)
    for e in range(N_EXPERTS):
        better = (masked[e] > b2) & (i1 != e)
        i2 = jnp.where(better, e, i2)
        b2 = jnp.where(better, masked[e], b2)
        s2 = jnp.where(better, srow[e], s2)
    den = s1 + s2
    return i1, i2, s1 / den, s2 / den


def _store_route(idx_ref, wt_ref, route):
    i1, i2, w1, w2 = route
    rid = lax.broadcasted_iota(I32, idx_ref.shape, 0)
    idx_ref[...] = jnp.where(rid == 0, i1, jnp.where(rid == 1, i2, 0))
    wt_ref[...] = jnp.where(rid == 0, w1, jnp.where(rid == 1, w2, 0.0))


def _router_operands(router_w, router_b):
    rw = jnp.zeros((D_MODEL, 128), F32).at[:, :N_EXPERTS].set(router_w.astype(F32))
    rwh = rw.astype(BF16)
    rwl = (rw - rwh.astype(F32)).astype(BF16)
    rb = router_b.astype(F32)[:, None]
    return rwh, rwl, rb


_ROUTER_SPECS = [
    pl.BlockSpec((D_MODEL, 128), lambda i: (0, 0)),
    pl.BlockSpec((D_MODEL, 128), lambda i: (0, 0)),
    pl.BlockSpec((N_EXPERTS, 1), lambda i: (0, 0)),
]


def _mix_out_specs(ts, T):
    specs = [
        pl.BlockSpec((ts, D_MODEL), lambda i: (i, 0)),
        pl.BlockSpec((8, ts), lambda i: (0, i)),
        pl.BlockSpec((8, ts), lambda i: (0, i)),
    ]
    shapes = [
        jax.ShapeDtypeStruct((T, D_MODEL), F32),
        jax.ShapeDtypeStruct((8, T), I32),
        jax.ShapeDtypeStruct((8, T), F32),
    ]
    return specs, shapes


def _outproj_even_kernel(f_ref, a_ref, x_ref, wf_ref, wa_ref, g_ref, b_ref, rwh_ref, rwl_ref, rb_ref,
                         x1_ref, idx_ref, wt_ref, *, alpha):
    m = (jnp.dot(f_ref[...], wf_ref[...], preferred_element_type=F32)
         + jnp.dot(a_ref[...], wa_ref[...], preferred_element_type=F32))
    x1 = _layer_norm(alpha * x_ref[...] + m, g_ref[...], b_ref[...])
    x1_ref[...] = x1
    _store_route(idx_ref, wt_ref, _route(x1, rwh_ref, rwl_ref, rb_ref))


def _outproj_even(f, a, x, w_out, ln_g, ln_b, router, alpha, ts):
    T = x.shape[0]
    wf = w_out[:D_FOURIER].astype(BF16)
    wa = w_out[D_FOURIER:].reshape(N_KV_B, GRP_B * HEAD_DIM, D_MODEL)
    wa = jnp.concatenate([jnp.zeros((N_KV_B, HEAD_DIM, D_MODEL), w_out.dtype), wa], axis=1)
    wa = wa.reshape(1024, D_MODEL).astype(BF16)
    out_specs, out_shapes = _mix_out_specs(ts, T)
    return pl.pallas_call(
        functools.partial(_outproj_even_kernel, alpha=alpha),
        grid=(T // ts,),
        in_specs=[
            pl.BlockSpec((ts, D_FOURIER), lambda i: (i, 0)),
            pl.BlockSpec((ts, 1024), lambda i: (i, 0)),
            pl.BlockSpec((ts, D_MODEL), lambda i: (i, 0)),
            pl.BlockSpec((D_FOURIER, D_MODEL), lambda i: (0, 0)),
            pl.BlockSpec((1024, D_MODEL), lambda i: (0, 0)),
            pl.BlockSpec((1, D_MODEL), lambda i: (0, 0)),
            pl.BlockSpec((1, D_MODEL), lambda i: (0, 0)),
        ] + _ROUTER_SPECS,
        out_specs=out_specs,
        out_shape=out_shapes,
        compiler_params=_cparams(("parallel",)),
        name="outproj_even",
    )(f, a, x, wf, wa, ln_g[None, :], ln_b[None, :], *router)


def _moe_plan(idx, tm):
    T = idx.shape[1]
    e = idx.T.reshape(-1)
    oh = (e[:, None] == jnp.arange(N_EXPERTS, dtype=I32)[None, :]).astype(I32)
    csum = jnp.cumsum(oh, axis=0)
    counts = csum[-1]
    rank = jnp.sum(csum * oh, axis=1) - 1
    padded = ((counts + tm - 1) // tm) * tm
    ends = jnp.cumsum(padded)
    starts = ends - padded
    pos = (jnp.sum(starts[None, :] * oh, axis=1) + rank).astype(I32)
    n_rows = 2 * T + N_EXPERTS * tm
    tile_start = jnp.arange(n_rows // tm, dtype=I32) * tm
    tile_e = jnp.minimum(jnp.sum((ends[None, :] <= tile_start[:, None]).astype(I32), axis=1), N_EXPERTS - 1)
    n_used = (ends[-1] // tm).astype(I32)[None]
    pad_lo = jnp.concatenate([starts + counts, ends[-1:]]).astype(I32)
    pad_hi = jnp.concatenate([ends, jnp.full((1,), n_rows, I32)]).astype(I32)
    return pos, tile_e.astype(I32), n_used, pad_lo, pad_hi, n_rows


def _dispatch_kernel(lo_ref, hi_ref, pos_ref, x_ref, xs_hbm, zrow, sem, zsem, *, td):
    i = pl.program_id(0)

    def row_copy(r, dst):
        return pltpu.make_async_copy(x_ref.at[pl.ds(r, 1)], xs_hbm.at[pl.ds(dst, 1)], sem)

    def issue(r, c):
        row_copy(r, pos_ref[0, 0, 2 * r]).start()
        row_copy(r, pos_ref[0, 0, 2 * r + 1]).start()
        return c
    lax.fori_loop(0, td, issue, 0, unroll=8)

    def zero_copy(dst):
        return pltpu.make_async_copy(zrow, xs_hbm.at[pl.ds(dst, 1)], zsem)

    @pl.when(i == pl.num_programs(0) - 1)
    def _():
        zrow[...] = jnp.zeros(zrow.shape, F32)
        for e in range(N_EXPERTS + 1):
            lax.fori_loop(lo_ref[e], hi_ref[e], lambda r, c: (zero_copy(r).start(), c)[1], 0)
        for e in range(N_EXPERTS + 1):
            lax.fori_loop(lo_ref[e], hi_ref[e], lambda r, c: (zero_copy(0).wait(), c)[1], 0)

    def drain(r, c):
        row_copy(0, 0).wait()
        row_copy(0, 0).wait()
        return c
    lax.fori_loop(0, td, drain, 0, unroll=8)


def _moe_dispatch(x1, pos, pad_lo, pad_hi, n_rows, td):
    T = x1.shape[0]
    n_tiles = T // td
    pos3 = pos.reshape(n_tiles, 1, 2 * td)
    grid_spec = pltpu.PrefetchScalarGridSpec(
        num_scalar_prefetch=2,
        grid=(n_tiles,),
        in_specs=[
            pl.BlockSpec((1, 1, 2 * td), lambda i, lo, hi: (i, 0, 0), memory_space=pltpu.SMEM),
            pl.BlockSpec((td, D_MODEL), lambda i, lo, hi: (i, 0)),
        ],
        out_specs=pl.BlockSpec(memory_space=pl.ANY),
        scratch_shapes=[pltpu.VMEM((1, D_MODEL), F32), pltpu.SemaphoreType.DMA(()),
                        pltpu.SemaphoreType.DMA(())],
    )
    return pl.pallas_call(
        functools.partial(_dispatch_kernel, td=td),
        grid_spec=grid_spec,
        out_shape=jax.ShapeDtypeStruct((n_rows, D_MODEL), F32),
        compiler_params=_cparams(("arbitrary",)),
        name="moe_dispatch",
    )(pad_lo, pad_hi, pos3, x1)


def _moe_kernel(te_ref, nu_ref, x_ref, wg_ref, wu_ref, wd_ref, y_ref):
    i = pl.program_id(0)

    @pl.when(i < nu_ref[0])
    def _():
        xg = x_ref[...].astype(BF16)
        hg = jnp.dot(xg, wg_ref[0], preferred_element_type=F32)
        hu = jnp.dot(xg, wu_ref[0], preferred_element_type=F32)
        hdn = (hg / (1.0 + jnp.exp(-hg)) * hu).astype(BF16)
        y_ref[...] = jnp.dot(hdn, wd_ref[0], preferred_element_type=F32)

    @pl.when(i >= nu_ref[0])
    def _():
        y_ref[...] = jnp.zeros(y_ref.shape, F32)


def _moe_experts(xs, tile_e, n_used, wg, wu, wd, tm):
    n_rows = xs.shape[0]
    grid_spec = pltpu.PrefetchScalarGridSpec(
        num_scalar_prefetch=2,
        grid=(n_rows // tm,),
        in_specs=[
            pl.BlockSpec((tm, D_MODEL), lambda i, te, nu: (i, 0)),
            pl.BlockSpec((1, D_MODEL, D_FF), lambda i, te, nu: (te[i], 0, 0)),
            pl.BlockSpec((1, D_MODEL, D_FF), lambda i, te, nu: (te[i], 0, 0)),
            pl.BlockSpec((1, D_FF, D_MODEL), lambda i, te, nu: (te[i], 0, 0)),
        ],
        out_specs=pl.BlockSpec((tm, D_MODEL), lambda i, te, nu: (i, 0)),
    )
    return pl.pallas_call(
        _moe_kernel,
        grid_spec=grid_spec,
        out_shape=jax.ShapeDtypeStruct((n_rows, D_MODEL), F32),
        compiler_params=_cparams(("arbitrary",)),
        name="moe_experts",
    )(tile_e, n_used, xs, wg, wu, wd)


def _combine_kernel(pos_ref, posn_ref, y_hbm, x_ref, wt_ref, g_ref, b_ref, o_ref, ybuf, sem, *, tc, alpha):
    i = pl.program_id(0)
    n = pl.num_programs(0)
    slot = i % 2

    def gather_copy(src_row, k, row, s):
        return pltpu.make_async_copy(y_hbm.at[pl.ds(src_row, 1)], ybuf.at[s, k, pl.ds(row, 1)], sem.at[s])

    def issue(ref, s):
        def body(r, c):
            gather_copy(ref[0, 0, 2 * r], 0, r, s).start()
            gather_copy(ref[0, 0, 2 * r + 1], 1, r, s).start()
            return c
        lax.fori_loop(0, tc, body, 0, unroll=8)

    @pl.when(i == 0)
    def _():
        issue(pos_ref, 0)

    @pl.when(i + 1 < n)
    def _():
        issue(posn_ref, 1 - slot)

    def drain(r, c):
        gather_copy(0, 0, r, slot).wait()
        gather_copy(0, 1, r, slot).wait()
        return c
    lax.fori_loop(0, tc, drain, 0, unroll=8)

    w = wt_ref[...]
    f = w[:, 0:1] * ybuf[slot, 0] + w[:, 1:2] * ybuf[slot, 1]
    o_ref[...] = _layer_norm(alpha * x_ref[...] + f, g_ref[...], b_ref[...])


def _moe_combine(y, pos, wts, x1, ln_g, ln_b, alpha, tc):
    T = x1.shape[0]
    n_tiles = T // tc
    pos3 = pos.reshape(n_tiles, 1, 2 * tc)
    return pl.pallas_call(
        functools.partial(_combine_kernel, tc=tc, alpha=alpha),
        grid=(n_tiles,),
        in_specs=[
            pl.BlockSpec((1, 1, 2 * tc), lambda i: (i, 0, 0), memory_space=pltpu.SMEM),
            pl.BlockSpec((1, 1, 2 * tc), lambda i: (jnp.minimum(i + 1, n_tiles - 1), 0, 0),
                         memory_space=pltpu.SMEM),
            pl.BlockSpec(memory_space=pl.ANY),
            pl.BlockSpec((tc, D_MODEL), lambda i: (i, 0)),
            pl.BlockSpec((tc, 2), lambda i: (i, 0)),
            pl.BlockSpec((1, D_MODEL), lambda i: (0, 0)),
            pl.BlockSpec((1, D_MODEL), lambda i: (0, 0)),
        ],
        out_specs=pl.BlockSpec((tc, D_MODEL), lambda i: (i, 0)),
        out_shape=jax.ShapeDtypeStruct((T, D_MODEL), F32),
        scratch_shapes=[pltpu.VMEM((2, 2, tc, D_MODEL), F32), pltpu.SemaphoreType.DMA((2,))],
        compiler_params=_cparams(("arbitrary",)),
        name="moe_combine",
    )(pos3, pos3, y, x1, wts, ln_g[None, :], ln_b[None, :])


def _moe_layer(x1, idx, wt, wg, wu, wd, ln_g, ln_b, alpha, tm, tc):
    pos, tile_e, n_used, pad_lo, pad_hi, n_rows = _moe_plan(idx[:2], tm)
    xs = _moe_dispatch(x1, pos, pad_lo, pad_hi, n_rows, tc)
    y = _moe_experts(xs, tile_e, n_used, wg.astype(BF16), wu.astype(BF16), wd.astype(BF16), tm)
    return _moe_combine(y, pos, wt[:2].T, x1, ln_g, ln_b, alpha, tc)


def _inproj_odd_kernel(x_ref, w_ref, vone_ref, bg_ref, u_ref, q_ref, k_ref, v_ref):
    x = x_ref[...].astype(BF16)
    proj = jnp.dot(x, w_ref[...], preferred_element_type=F32)
    bg_ref[...] = proj[:, :512].astype(BF16)
    u_ref[...] = (proj[:, 512:1024] * proj[:, 1024:1536]).astype(BF16)
    q_ref[...] = (proj[:, 1536:2048] * (HEAD_DIM ** -0.5 * LOG2E)).astype(BF16)
    k_ref[...] = proj[:, 2048:2304].astype(BF16)
    v_ref[...] = (proj[:, 2304:2560] + vone_ref[...]).astype(BF16)


def _inproj_odd(x, w_in, ts):
    T = x.shape[0]
    c3 = 3 * D_CONV
    wk = w_in[:, c3 + 512:c3 + 640].reshape(D_MODEL, N_KV_D, HEAD_DIM)
    wv = w_in[:, c3 + 640:c3 + 768].reshape(D_MODEL, N_KV_D, HEAD_DIM)
    wkp = jnp.concatenate([wk, jnp.zeros_like(wk)], axis=2).reshape(D_MODEL, 256)
    wvp = jnp.concatenate([wv, jnp.zeros_like(wv)], axis=2).reshape(D_MODEL, 256)
    w = jnp.concatenate([w_in[:, :c3 + 512], wkp, wvp], axis=1).astype(BF16)
    vone_np = np.zeros((1, 256), np.float32)
    for h in range(N_KV_D):
        vone_np[0, 128 * h + 64:128 * (h + 1)] = 1.0
    n_w = w.shape[1]
    widths = [512, 512, 512, 256, 256]
    return pl.pallas_call(
        _inproj_odd_kernel,
        grid=(T // ts,),
        in_specs=[
            pl.BlockSpec((ts, D_MODEL), lambda i: (i, 0)),
            pl.BlockSpec((D_MODEL, n_w), lambda i: (0, 0)),
            pl.BlockSpec((1, 256), lambda i: (0, 0)),
        ],
        out_specs=[pl.BlockSpec((ts, n), lambda i: (i, 0)) for n in widths],
        out_shape=[jax.ShapeDtypeStruct((T, n), BF16) for n in widths],
        compiler_params=_cparams(("parallel",)),
        name="inproj_odd",
    )(x, w, jnp.asarray(vone_np))


def _wattn_kernel(q_ref, kp_ref, kc_ref, kn_ref, vp_ref, vc_ref, vn_ref, bias_ref, sink_ref, place_ref,
                  o_ref, *, tq, seq_tiles):
    i = pl.program_id(0)
    first = i < 0
    last = i < 0
    for lo, hi, per in seq_tiles:
        inside = (i >= lo) & (i < hi)
        first = first | (inside & ((i - lo) % per == 0))
        last = last | (inside & ((i - lo) % per == per - 1))
    kfull = jnp.concatenate([kp_ref[...], kc_ref[...], kn_ref[...]], axis=0)[:, :HEAD_DIM]
    vfull = jnp.concatenate([vp_ref[...], vc_ref[...], vn_ref[...]], axis=0)
    q4 = q_ref[...]
    bias = bias_ref[0]
    sink = sink_ref[0]
    nb = tq // Q_BLOCK
    col = lax.broadcasted_iota(I32, (GRP_D * Q_BLOCK, 3 * Q_BLOCK), 1)
    for n in range(nb):
        qs = jnp.concatenate([q4[Q_BLOCK * n:Q_BLOCK * (n + 1), HEAD_DIM * g:HEAD_DIM * (g + 1)]
                              for g in range(GRP_D)], axis=0)
        keys = kfull[Q_BLOCK * n:Q_BLOCK * (n + 3)]
        vals = vfull[Q_BLOCK * n:Q_BLOCK * (n + 3)]
        s = lax.dot_general(qs, keys, (((1,), (1,)), ((), ())), preferred_element_type=F32) + bias
        if n == 0:
            s = jnp.where(first & (col < Q_BLOCK), NEG_INF, s)
        if n == nb - 1:
            s = jnp.where(last & (col >= 2 * Q_BLOCK), NEG_INF, s)
        m = jnp.maximum(jnp.max(s, axis=1, keepdims=True), sink)
        p = jnp.exp2(s - m).astype(BF16)
        acc = jnp.dot(p, vals, preferred_element_type=F32)
        den = acc[:, HEAD_DIM:HEAD_DIM + 1] + jnp.exp2(sink - m)
        o = (acc / den).astype(BF16)
        out = jnp.zeros((Q_BLOCK, 256), F32)
        for g in range(GRP_D):
            out = out + jnp.dot(o[Q_BLOCK * g:Q_BLOCK * (g + 1)], place_ref[g], preferred_element_type=F32)
        o_ref[Q_BLOCK * n:Q_BLOCK * (n + 1), :] = out.astype(BF16)


def _window_attention(q, kw, vw, sink_logits, classes, tq):
    T = q.shape[0]
    n_tiles = T // tq
    hb = tq // Q_BLOCK
    n_hblk = T // Q_BLOCK
    r = np.arange(Q_BLOCK)[:, None]
    j = np.arange(3 * Q_BLOCK)[None, :]
    rel = np.abs(j - Q_BLOCK - r).astype(np.float32)
    slopes = np.array([2.0 ** (-8.0 * (h + 1) / N_HEADS_D) for h in range(N_HEADS_D)], np.float32)
    bias_np = np.where(rel[None] <= WINDOW, -slopes[:, None, None] * rel[None] * LOG2E, NEG_INF)
    bias = jnp.asarray(bias_np.reshape(N_KV_D, GRP_D * Q_BLOCK, 3 * Q_BLOCK), F32)
    sink = jnp.repeat(sink_logits.astype(F32) * LOG2E, Q_BLOCK).reshape(N_KV_D, GRP_D * Q_BLOCK, 1)
    place = _lane_place(128, 256, [0, 1, 2, 3])
    seq_tiles = []
    t0 = 0
    for nseq, S, _ in classes:
        cnt = nseq * S // tq
        seq_tiles.append((t0, t0 + cnt, S // tq))
        t0 += cnt
    prev_map = lambda i, h: (jnp.maximum(i * hb - 1, 0), h)
    next_map = lambda i, h: (jnp.minimum((i + 1) * hb, n_hblk - 1), h)
    return pl.pallas_call(
        functools.partial(_wattn_kernel, tq=tq, seq_tiles=tuple(seq_tiles)),
        grid=(n_tiles, N_KV_D),
        in_specs=[
            pl.BlockSpec((tq, 256), lambda i, h: (i, h)),
            pl.BlockSpec((Q_BLOCK, 128), prev_map),
            pl.BlockSpec((tq, 128), lambda i, h: (i, h)),
            pl.BlockSpec((Q_BLOCK, 128), next_map),
            pl.BlockSpec((Q_BLOCK, 128), prev_map),
            pl.BlockSpec((tq, 128), lambda i, h: (i, h)),
            pl.BlockSpec((Q_BLOCK, 128), next_map),
            pl.BlockSpec((1, GRP_D * Q_BLOCK, 3 * Q_BLOCK), lambda i, h: (h, 0, 0)),
            pl.BlockSpec((1, GRP_D * Q_BLOCK, 1), lambda i, h: (h, 0, 0)),
            pl.BlockSpec((GRP_D, 128, 256), lambda i, h: (0, 0, 0)),
        ],
        out_specs=pl.BlockSpec((tq, 256), lambda i, h: (i, h)),
        out_shape=jax.ShapeDtypeStruct((T, 512), BF16),
        compiler_params=_cparams(("parallel", "parallel")),
        name="window_attention",
    )(q, kw, kw, kw, vw, vw, vw, bias, sink, place)


HALO = 16


def _outproj_odd_kernel(bg_ref, u_ref, up_ref, un_ref, a_ref, x_ref, cw_ref, cb_ref, wc_ref, wa_ref,
                        g_ref, b_ref, rwh_ref, rwl_ref, rb_ref, x1_ref, idx_ref, wt_ref,
                        *, alpha, ts, seq_tiles):
    i = pl.program_id(0)
    first = i < 0
    last = i < 0
    for lo, hi, per in seq_tiles:
        inside = (i >= lo) & (i < hi)
        first = first | (inside & ((i - lo) % per == 0))
        last = last | (inside & ((i - lo) % per == per - 1))
    u = u_ref[...].astype(F32)
    prev_row = jnp.where(first, 0.0, up_ref[HALO - 1:HALO, :].astype(F32))
    next_row = jnp.where(last, 0.0, un_ref[0:1, :].astype(F32))
    rid = lax.broadcasted_iota(I32, u.shape, 0)
    ud = jnp.where(rid == 0, prev_row, pltpu.roll(u, 1, 0))
    uu = jnp.where(rid == ts - 1, next_row, pltpu.roll(u, ts - 1, 0))
    cw = cw_ref[...]
    y = ud * cw[0:1] + u * cw[1:2] + uu * cw[2:3] + cb_ref[...]
    c = (bg_ref[...].astype(F32) * y).astype(BF16)
    m = (jnp.dot(c, wc_ref[...], preferred_element_type=F32)
         + jnp.dot(a_ref[...], wa_ref[...], preferred_element_type=F32))
    x1 = _layer_norm(alpha * x_ref[...] + m, g_ref[...], b_ref[...])
    x1_ref[...] = x1
    _store_route(idx_ref, wt_ref, _route(x1, rwh_ref, rwl_ref, rb_ref))


def _outproj_odd(bg, u, a, x, conv_w, conv_b, w_out, ln_g, ln_b, router, alpha, classes, ts):
    T = x.shape[0]
    hb = ts // HALO
    n_h = T // HALO
    seq_tiles = []
    t0 = 0
    for nseq, S, _ in classes:
        cnt = nseq * S // ts
        seq_tiles.append((t0, t0 + cnt, S // ts))
        t0 += cnt
    out_specs, out_shapes = _mix_out_specs(ts, T)
    return pl.pallas_call(
        functools.partial(_outproj_odd_kernel, alpha=alpha, ts=ts, seq_tiles=tuple(seq_tiles)),
        grid=(T // ts,),
        in_specs=[
            pl.BlockSpec((ts, D_CONV), lambda i: (i, 0)),
            pl.BlockSpec((ts, D_CONV), lambda i: (i, 0)),
            pl.BlockSpec((HALO, D_CONV), lambda i: (jnp.maximum(i * hb - 1, 0), 0)),
            pl.BlockSpec((HALO, D_CONV), lambda i: (jnp.minimum((i + 1) * hb, n_h - 1), 0)),
            pl.BlockSpec((ts, 512), lambda i: (i, 0)),
            pl.BlockSpec((ts, D_MODEL), lambda i: (i, 0)),
            pl.BlockSpec((3, D_CONV), lambda i: (0, 0)),
            pl.BlockSpec((1, D_CONV), lambda i: (0, 0)),
            pl.BlockSpec((D_CONV, D_MODEL), lambda i: (0, 0)),
            pl.BlockSpec((512, D_MODEL), lambda i: (0, 0)),
            pl.BlockSpec((1, D_MODEL), lambda i: (0, 0)),
            pl.BlockSpec((1, D_MODEL), lambda i: (0, 0)),
        ] + _ROUTER_SPECS,
        out_specs=out_specs,
        out_shape=out_shapes,
        compiler_params=_cparams(("parallel",)),
        name="outproj_odd",
    )(bg, u, u, u, a, x, conv_w.astype(F32), conv_b.astype(F32)[None, :],
      w_out[:D_CONV].astype(BF16), w_out[D_CONV:].astype(BF16), ln_g[None, :], ln_b[None, :], *router)


def _tile(n, cap):
    t = cap
    while n % t:
        t //= 2
    return t


def kernel(x_prompt, x_sample, w_in_even, fourier_norm_g, q_norm_g, k_norm_g, w_out_even, w_in_odd, conv_w,
           conv_b, sink_logits, w_out_odd, ln_mix_g, ln_mix_b, ln_ffn_g, ln_ffn_b, router_w, router_b,
           w_gate, w_up, w_down):
    depth = ln_mix_g.shape[0]
    alpha = float((2 * depth) ** 0.25)
    bp, sp, _ = x_prompt.shape
    bs, ss, _ = x_sample.shape
    classes = ((bp, sp, 0), (bs, ss, bp * sp))
    T = bp * sp + bs * ss
    min_s = min(sp, ss)
    ts = _tile(min_s, 512)
    tq_w = _tile(min_s, 256)
    tm = 256
    tc = _tile(min_s, 256)
    x = jnp.concatenate([x_prompt.reshape(bp * sp, D_MODEL), x_sample.reshape(bs * ss, D_MODEL)], axis=0)
    router = _router_operands(router_w, router_b)
    for l in range(depth):
        i = l // 2
        if l % 2 == 0:
            uf, qk, v_aug = _inproj_even(x, w_in_even[i], fourier_norm_g[i], q_norm_g[i], k_norm_g[i],
                                         classes, ts)
            bounded = _score_bounded(q_norm_g[i], k_norm_g[i])
            f_parts, a_parts = [], []
            for nseq, S, tok0 in classes:
                f_parts.append(_fourier_mix(uf[tok0:tok0 + nseq * S], nseq, S))
                a_parts.append(_global_attention(qk, v_aug, bounded, nseq, S, tok0, _tile(S, 512),
                                                 _tile(S, 2048)))
            f = jnp.concatenate(f_parts, axis=0)
            a = jnp.concatenate(a_parts, axis=0)
            x1, idx, wt = _outproj_even(f, a, x, w_out_even[i], ln_mix_g[l], ln_mix_b[l], router, alpha, ts)
        else:
            bg, u, q, kw, vw = _inproj_odd(x, w_in_odd[i], ts)
            a = _window_attention(q, kw, vw, sink_logits[i], classes, tq_w)
            x1, idx, wt = _outproj_odd(bg, u, a, x, conv_w[i], conv_b[i], w_out_odd[i], ln_mix_g[l],
                                       ln_mix_b[l], router, alpha, classes, ts)
        x = _moe_layer(x1, idx, wt, w_gate[l], w_up[l], w_down[l], ln_ffn_g[l], ln_ffn_b[l], alpha, tm, tc)
    y_prompt = x[:bp * sp].reshape(bp, sp, D_MODEL)
    y_sample = x[bp * sp:].reshape(bs, ss, D_MODEL)
    return (y_prompt, y_sample)
```

```python
import functools
import math

import numpy as np
import jax
import jax.numpy as jnp
from jax import lax
from jax.experimental import pallas as pl
from jax.experimental.pallas import tpu as pltpu

F32 = jnp.float32
BF16 = jnp.bfloat16
I32 = jnp.int32

D_MODEL = 1024
HEAD_DIM = 64
GRID_W = 64
Q_BLOCK = 128
WINDOW = 128
ROPE_THETA = 10000.0
N_FOURIER_GROUPS = 4
D_FOURIER = 256
N_HEADS_B = 12
N_KV_B = 4
GRP_B = 3
D_CONV = 512
N_HEADS_D = 8
N_KV_D = 2
GRP_D = 4
N_EXPERTS = 16
N_GROUPS = 4
EXPERTS_PER_GROUP = 4
D_FF = 512
LN_EPS = 1e-5
RMS_EPS = 1e-6
NEG_INF = -1e30
LOG2E = 1.4426950408889634
FFT_S2 = 128
V7X_VMEM_LIMIT = 48 * 1024 * 1024


def _cparams(sem):
    return pltpu.CompilerParams(dimension_semantics=sem, vmem_limit_bytes=V7X_VMEM_LIMIT)


def _layer_norm(y, g, b):
    mu = jnp.mean(y, axis=-1, keepdims=True)
    yc = y - mu
    var = jnp.mean(yc * yc, axis=-1, keepdims=True)
    return yc * lax.rsqrt(var + LN_EPS) * g + b


N_NORM_COLS = D_FOURIER + 1024


def _parts_specs(parts, ts):
    specs, bounds, lo = [], [], 0
    for p in parts:
        n = p.shape[0] // ts
        specs.append(pl.BlockSpec((ts, p.shape[1]), lambda i, lo=lo, n=n: (jnp.clip(i - lo, 0, n - 1), 0)))
        lo += n
        bounds.append(lo)
    return specs, tuple(bounds)


def _pick_rows(refs, bounds, r0, nr):
    i = pl.program_id(0)
    out = refs[-1][r0:r0 + nr, :]
    for ref, hi in reversed(list(zip(refs[:-1], bounds[:-1]))):
        out = jnp.where(i < hi, ref[r0:r0 + nr, :], out)
    return out


def _inproj_even_kernel(*refs, nx, bounds):
    x_refs = refs[:nx]
    (w_ref, gain_ref, seg_ref, segt_ref, cos_ref, sin_ref, vone_ref, uf_ref, qk_ref, v_ref) = refs[nx:]
    x = _pick_rows(x_refs, bounds, 0, x_refs[0].shape[0]).astype(BF16)
    proj = jnp.dot(x, w_ref[...], preferred_element_type=F32)
    nrm = proj[:, :N_NORM_COLS]
    sq = (nrm * nrm).astype(BF16)
    ssum = jnp.dot(sq, seg_ref[...], preferred_element_type=F32)
    r = lax.rsqrt(ssum * (1.0 / HEAD_DIM) + RMS_EPS)
    rh = r.astype(BF16)
    rl = (r - rh.astype(F32)).astype(BF16)
    rex = jnp.dot(jnp.concatenate([rh, rl], axis=1), segt_ref[...], preferred_element_type=F32)
    y = nrm * rex * gain_ref[...]
    uf_ref[...] = y[:, :D_FOURIER].astype(BF16)
    yq = y[:, D_FOURIER:]
    c = jnp.concatenate([cos_ref[...]] * 8, axis=1)
    s = jnp.concatenate([sin_ref[...]] * 8, axis=1)
    lane = lax.broadcasted_iota(I32, yq.shape, 1)
    first = (lane & 31) < 16
    sw = jnp.where(first, pltpu.roll(yq, 1024 - 16, 1), pltpu.roll(yq, 16, 1))
    qk_ref[...] = (yq * c + sw * s).astype(BF16)
    vv = proj[:, N_NORM_COLS:] + vone_ref[...]
    for h in range(N_KV_B):
        v_ref[h] = vv[:, 128 * h:128 * (h + 1)].astype(BF16)


def _rope_tables(n_pos):
    t = jnp.arange(n_pos)
    row = (t // GRID_W).astype(F32)
    col = (t % GRID_W).astype(F32)
    n_freq = HEAD_DIM // 4
    inv_freq = ROPE_THETA ** (-jnp.arange(n_freq, dtype=F32) / n_freq)
    ar = row[:, None] * inv_freq
    ac = col[:, None] * inv_freq
    ang = jnp.concatenate([ar, ar, ac, ac], axis=1)
    sign = jnp.asarray(np.tile(np.repeat(np.array([-1.0, 1.0], np.float32), 16), 2))
    cos = jnp.cos(ang)
    sin = jnp.sin(ang) * sign
    return jnp.concatenate([cos, cos], axis=1), jnp.concatenate([sin, sin], axis=1)


def _pos_block_map(classes, ts):
    bounds = []
    tile0 = 0
    for nseq, S, _ in classes:
        n_tiles = nseq * S // ts
        bounds.append((tile0, tile0 + n_tiles, S // ts))
        tile0 += n_tiles

    def fn(i):
        out = (i - bounds[-1][0]) % bounds[-1][2]
        for lo, hi, per in reversed(bounds[:-1]):
            out = jnp.where(i < hi, (i - lo) % per, out)
        return out

    return fn


def _inproj_even(x_parts, w_in, f_g, q_g, k_g, classes, ts):
    T = sum(p.shape[0] for p in x_parts)
    x_specs, bounds = _parts_specs(x_parts, ts)
    wf = w_in[:, :D_FOURIER]
    wq = w_in[:, D_FOURIER:D_FOURIER + 768].reshape(D_MODEL, N_KV_B, GRP_B * HEAD_DIM)
    wk = w_in[:, D_FOURIER + 768:D_FOURIER + 1024].reshape(D_MODEL, N_KV_B, HEAD_DIM)
    wv = w_in[:, D_FOURIER + 1024:].reshape(D_MODEL, N_KV_B, HEAD_DIM)
    wqk = jnp.concatenate([wk, wq], axis=2).reshape(D_MODEL, 1024)
    wvp = jnp.concatenate([wv, jnp.zeros_like(wv)], axis=2).reshape(D_MODEL, 512)
    w = jnp.concatenate([wf, wqk, wvp], axis=1).astype(BF16)
    qscale = HEAD_DIM ** -0.5 * LOG2E
    gqk = jnp.tile(jnp.concatenate([k_g, q_g * qscale, q_g * qscale, q_g * qscale]), N_KV_B)
    gain = jnp.concatenate([f_g, gqk])[None, :].astype(F32)
    seg_np = np.zeros((N_NORM_COLS, 128), np.float32)
    seg_np[np.arange(N_NORM_COLS), np.arange(N_NORM_COLS) // HEAD_DIM] = 1.0
    seg = jnp.asarray(seg_np, BF16)
    segt = jnp.asarray(np.concatenate([seg_np.T, seg_np.T], axis=0), BF16)
    max_s = max(S for _, S, _ in classes)
    cos, sin = _rope_tables(max_s)
    vone_np = np.zeros((1, 512), np.float32)
    for h in range(N_KV_B):
        vone_np[0, 128 * h + 64:128 * (h + 1)] = 1.0
    vone = jnp.asarray(vone_np)
    posmap = _pos_block_map(classes, ts)
    n_w = w.shape[1]
    return pl.pallas_call(
        functools.partial(_inproj_even_kernel, nx=len(x_parts), bounds=bounds),
        grid=(T // ts,),
        in_specs=x_specs + [
            pl.BlockSpec((D_MODEL, n_w), lambda i: (0, 0)),
            pl.BlockSpec((1, N_NORM_COLS), lambda i: (0, 0)),
            pl.BlockSpec((N_NORM_COLS, 128), lambda i: (0, 0)),
            pl.BlockSpec((256, N_NORM_COLS), lambda i: (0, 0)),
            pl.BlockSpec((ts, 128), lambda i: (posmap(i), 0)),
            pl.BlockSpec((ts, 128), lambda i: (posmap(i), 0)),
            pl.BlockSpec((1, 512), lambda i: (0, 0)),
        ],
        out_specs=[
            pl.BlockSpec((ts, D_FOURIER), lambda i: (i, 0)),
            pl.BlockSpec((ts, 1024), lambda i: (i, 0)),
            pl.BlockSpec((N_KV_B, ts, 128), lambda i: (0, i, 0)),
        ],
        out_shape=[
            jax.ShapeDtypeStruct((T, D_FOURIER), BF16),
            jax.ShapeDtypeStruct((T, 1024), BF16),
            jax.ShapeDtypeStruct((N_KV_B, T, 128), BF16),
        ],
        compiler_params=_cparams(("parallel",)),
        name="inproj_even",
    )(*x_parts, w, gain, seg, segt, cos, sin, vone)


def _fft1_kernel(a_ref, w1_ref, twr_ref, twi_ref, zr_ref, zi_ref, *, s1):
    z = jnp.dot(w1_ref[...], a_ref[0], preferred_element_type=F32)
    zr = z[:s1]
    zi = z[s1:]
    twr = twr_ref[...]
    twi = twi_ref[...]
    zr_ref[0] = (zr * twr - zi * twi).astype(BF16)
    zi_ref[0] = (zr * twi + zi * twr).astype(BF16)


def _fft2_kernel(zr_ref, zi_ref, w2a_ref, w2b_ref, mix_ref, o_ref, *, cb):
    s2 = FFT_S2
    for c in range(cb):
        pp = (jnp.dot(w2a_ref[...], zr_ref[0, c], preferred_element_type=F32)
              + jnp.dot(w2b_ref[...], zi_ref[0, c], preferred_element_type=F32))
        f = (jnp.dot(pp[:s2].astype(BF16), mix_ref[:D_FOURIER], preferred_element_type=F32)
             + jnp.dot(pp[s2:].astype(BF16), mix_ref[D_FOURIER:], preferred_element_type=F32))
        o_ref[0, :, D_FOURIER * c:D_FOURIER * (c + 1)] = f.astype(BF16)


def _dft_mats(n):
    k = jnp.arange(n, dtype=I32)
    ang = (2.0 * math.pi / n) * ((k[:, None] * k[None, :]) % n).astype(F32)
    return jnp.cos(ang), jnp.sin(ang)


def _fourier_mix(uf_part, nseq, S):
    s2 = FFT_S2
    s1 = S // s2
    ncol = s2 * D_FOURIER
    a = uf_part.reshape(nseq, s1, ncol)
    c1, sn1 = _dft_mats(s1)
    w1 = jnp.concatenate([c1, -sn1], axis=0).astype(BF16)
    cc = jnp.arange(s1, dtype=I32)[:, None]
    bb = jnp.arange(s2, dtype=I32)[None, :]
    ang = (2.0 * math.pi / S) * ((cc * bb) % S).astype(F32)
    twr = jnp.repeat(jnp.cos(ang), D_FOURIER, axis=1)
    twi = jnp.repeat(-jnp.sin(ang), D_FOURIER, axis=1)
    tn = min(ncol, 4096)
    zr, zi = pl.pallas_call(
        functools.partial(_fft1_kernel, s1=s1),
        grid=(ncol // tn, nseq),
        in_specs=[
            pl.BlockSpec((1, s1, tn), lambda j, b: (b, 0, j)),
            pl.BlockSpec((2 * s1, s1), lambda j, b: (0, 0)),
            pl.BlockSpec((s1, tn), lambda j, b: (0, j)),
            pl.BlockSpec((s1, tn), lambda j, b: (0, j)),
        ],
        out_specs=[
            pl.BlockSpec((1, s1, tn), lambda j, b: (b, 0, j)),
            pl.BlockSpec((1, s1, tn), lambda j, b: (b, 0, j)),
        ],
        out_shape=[jax.ShapeDtypeStruct((nseq, s1, ncol), BF16)] * 2,
        compiler_params=_cparams(("parallel", "parallel")),
        name="fourier_stage1",
    )(a, w1, twr, twi)
    zr = zr.reshape(nseq, s1, s2, D_FOURIER)
    zi = zi.reshape(nseq, s1, s2, D_FOURIER)
    c2, sn2 = _dft_mats(s2)
    w2a = jnp.concatenate([c2, -sn2], axis=0).astype(BF16)
    w2b = jnp.concatenate([sn2, c2], axis=0).astype(BF16)
    gc, gs = _dft_mats(HEAD_DIM)
    scale = 1.0 / math.sqrt(S * HEAD_DIM)
    eye = jnp.eye(N_FOURIER_GROUPS, dtype=F32)
    mix = (jnp.concatenate([jnp.kron(eye, gc), jnp.kron(eye, gs)], axis=0) * scale).astype(BF16)
    cb = min(8, s1)
    out = pl.pallas_call(
        functools.partial(_fft2_kernel, cb=cb),
        grid=(nseq, s1 // cb),
        in_specs=[
            pl.BlockSpec((1, cb, s2, D_FOURIER), lambda b, j: (b, j, 0, 0)),
            pl.BlockSpec((1, cb, s2, D_FOURIER), lambda b, j: (b, j, 0, 0)),
            pl.BlockSpec((2 * s2, s2), lambda b, j: (0, 0)),
            pl.BlockSpec((2 * s2, s2), lambda b, j: (0, 0)),
            pl.BlockSpec((2 * D_FOURIER, D_FOURIER), lambda b, j: (0, 0)),
        ],
        out_specs=pl.BlockSpec((1, s2, cb * D_FOURIER), lambda b, j: (b, 0, j)),
        out_shape=jax.ShapeDtypeStruct((nseq, s2, s1 * D_FOURIER), BF16),
        compiler_params=_cparams(("parallel", "parallel")),
        name="fourier_stage2",
    )(zr, zi, w2a, w2b, mix)
    return out.reshape(nseq * S, D_FOURIER)


SCORE_BOUND_MAX = 100.0


def _gattn_kernel(bounded_ref, q_ref, k_ref, v_ref, place_ref, o_ref, *, tq, tk, S):
    q3 = q_ref[...]
    qs = [q3[:, HEAD_DIM * (g + 1):HEAD_DIM * (g + 2)] for g in range(GRP_B)]

    def chunk(j):
        off = pl.multiple_of(j * tk, tk)
        return k_ref[pl.ds(off, tk), 0:HEAD_DIM], v_ref[0, pl.ds(off, tk), :]

    def scores(g, kc):
        return lax.dot_general(qs[g], kc, (((1,), (1,)), ((), ())), preferred_element_type=F32)

    def finish(accs):
        out = jnp.zeros((tq, 256), F32)
        for g in range(GRP_B):
            o = (accs[g] / accs[g][:, HEAD_DIM:HEAD_DIM + 1]).astype(BF16)
            out = out + jnp.dot(o, place_ref[g], preferred_element_type=F32)
        o_ref[...] = out.astype(BF16)

    @pl.when(bounded_ref[0] == 1)
    def _():
        def body(j, accs):
            kc, vc = chunk(j)
            return tuple(accs[g] + jnp.dot(jnp.exp2(scores(g, kc)).astype(BF16), vc,
                                           preferred_element_type=F32) for g in range(GRP_B))
        finish(lax.fori_loop(0, S // tk, body, tuple(jnp.zeros((tq, 128), F32) for _ in range(GRP_B))))

    @pl.when(bounded_ref[0] != 1)
    def _():
        def body(j, carry):
            kc, vc = chunk(j)
            new = []
            for g in range(GRP_B):
                m_prev, acc = carry[g]
                s = scores(g, kc)
                m_new = jnp.maximum(m_prev, jnp.max(s, axis=1, keepdims=True))
                p = jnp.exp2(s - m_new).astype(BF16)
                acc = jnp.exp2(m_prev - m_new) * acc + jnp.dot(p, vc, preferred_element_type=F32)
                new.append((m_new, acc))
            return tuple(new)
        init = tuple((jnp.full((tq, 1), NEG_INF, F32), jnp.zeros((tq, 128), F32)) for _ in range(GRP_B))
        fin = lax.fori_loop(0, S // tk, body, init)
        finish([fin[g][1] for g in range(GRP_B)])


def _score_bounded(q_g, k_g):
    bound = 1.05 * HEAD_DIM * (HEAD_DIM ** -0.5 * LOG2E) * jnp.max(jnp.abs(q_g)) * jnp.max(jnp.abs(k_g))
    return (bound <= SCORE_BOUND_MAX).astype(I32)[None]


def _lane_place(n_src, n_dst, groups):
    pm = np.zeros((len(groups), n_src, n_dst), np.float32)
    for gi, slot in enumerate(groups):
        pm[gi, np.arange(HEAD_DIM), HEAD_DIM * slot + np.arange(HEAD_DIM)] = 1.0
    return jnp.asarray(pm, BF16)


def _global_attention(qk, v_aug, bounded, nseq, S, tok0, tq, tk):
    n_qt = S // tq
    qt0 = tok0 // tq
    s0 = tok0 // S
    place = _lane_place(128, 256, [1, 2, 3])
    grid_spec = pltpu.PrefetchScalarGridSpec(
        num_scalar_prefetch=1,
        grid=(nseq, N_KV_B, n_qt),
        in_specs=[
            pl.BlockSpec((tq, 256), lambda b, h, i, fl: (qt0 + b * n_qt + i, h)),
            pl.BlockSpec((S, 256), lambda b, h, i, fl: (s0 + b, h)),
            pl.BlockSpec((1, S, 128), lambda b, h, i, fl: (h, s0 + b, 0)),
            pl.BlockSpec((GRP_B, 128, 256), lambda b, h, i, fl: (0, 0, 0)),
        ],
        out_specs=pl.BlockSpec((tq, 256), lambda b, h, i, fl: (b * n_qt + i, h)),
    )
    return pl.pallas_call(
        functools.partial(_gattn_kernel, tq=tq, tk=tk, S=S),
        grid_spec=grid_spec,
        out_shape=jax.ShapeDtypeStruct((nseq * S, 1024), BF16),
        compiler_params=_cparams(("parallel", "parallel", "arbitrary")),
        name="global_attention",
    )(bounded, qk, qk, v_aug, place)


def _route(x1, rw_ref, rb_ref):
    n = x1.shape[0]
    xh = x1.astype(BF16)
    xl = (x1 - xh.astype(F32)).astype(BF16)
    o = jnp.dot(jnp.concatenate([xh, xl], axis=0), rw_ref[...], preferred_element_type=F32)
    lg = o[:n, :128] + o[:n, 128:] + o[n:, :128]
    lgt = lg.T[:N_EXPERTS]
    sc = 1.0 / (1.0 + jnp.exp(-lgt))
    bi = sc + rb_ref[...]
    srow = [sc[e:e + 1] for e in range(N_EXPERTS)]
    brow = [bi[e:e + 1] for e in range(N_EXPERTS)]
    gsel = None
    best = None
    for g in range(N_GROUPS):
        a, b, c, d = brow[4 * g:4 * g + 4]
        m1, n1 = jnp.maximum(a, b), jnp.minimum(a, b)
        m2, n2 = jnp.maximum(c, d), jnp.minimum(c, d)
        gs = jnp.maximum(m1, m2) + jnp.maximum(jnp.minimum(m1, m2), jnp.maximum(n1, n2))
        if g == 0:
            best, gsel = gs, jnp.zeros(gs.shape, I32)
        else:
            better = gs > best
            gsel = jnp.where(better, g, gsel)
            best = jnp.where(better, gs, best)
    masked = [jnp.where(gsel == (e // EXPERTS_PER_GROUP), brow[e], NEG_INF) for e in range(N_EXPERTS)]
    i1 = jnp.zeros(gsel.shape, I32)
    b1 = masked[0]
    s1 = srow[0]
    for e in range(1, N_EXPERTS):
        better = masked[e] > b1
        i1 = jnp.where(better, e, i1)
        b1 = jnp.where(better, masked[e], b1)
        s1 = jnp.where(better, srow[e], s1)
    i2 = jnp.full(gsel.shape, -1, I32)
    b2 = jnp.full(b1.shape, -jnp.inf, F32)
    s2 = jnp.zeros(b1.shape, F32)
    for e in range(N_EXPERTS):
        better = (masked[e] > b2) & (i1 != e)
        i2 = jnp.where(better, e, i2)
        b2 = jnp.where(better, masked[e], b2)
        s2 = jnp.where(better, srow[e], s2)
    den = s1 + s2
    return i1, i2, s1 / den, s2 / den


ROW_SPLITS = 2


def _norm_route_store(r0, nr, m, x, g_ref, b_ref, rw_ref, rb_ref, x1_ref, idx_ref, wt_ref, alpha):
    x1 = _layer_norm(alpha * x + m, g_ref[...], b_ref[...])
    x1_ref[r0:r0 + nr, :] = x1
    i1, i2, w1, w2 = _route(x1, rw_ref, rb_ref)
    rid = lax.broadcasted_iota(I32, (8, nr), 0)
    idx_ref[:, r0:r0 + nr] = jnp.where(rid == 0, i1, jnp.where(rid == 1, i2, 0))
    wt_ref[:, r0:r0 + nr] = jnp.where(rid == 0, w1, jnp.where(rid == 1, w2, 0.0))


def _router_operands(router_w, router_b):
    rw = jnp.zeros((D_MODEL, 128), F32).at[:, :N_EXPERTS].set(router_w.astype(F32))
    rwh = rw.astype(BF16)
    rwl = (rw - rwh.astype(F32)).astype(BF16)
    rb = router_b.astype(F32)[:, None]
    return jnp.concatenate([rwh, rwl], axis=1), rb


_ROUTER_SPECS = [
    pl.BlockSpec((D_MODEL, 256), lambda i: (0, 0)),
    pl.BlockSpec((N_EXPERTS, 1), lambda i: (0, 0)),
]


def _mix_out_specs(ts, T):
    specs = [
        pl.BlockSpec((ts, D_MODEL), lambda i: (i, 0)),
        pl.BlockSpec((8, ts), lambda i: (0, i)),
        pl.BlockSpec((8, ts), lambda i: (0, i)),
    ]
    shapes = [
        jax.ShapeDtypeStruct((T, D_MODEL), F32),
        jax.ShapeDtypeStruct((8, T), I32),
        jax.ShapeDtypeStruct((8, T), F32),
    ]
    return specs, shapes


def _outproj_even_kernel(*refs, nf, na, nx, fb, ab, xb, alpha):
    f_refs, a_refs, x_refs = refs[:nf], refs[nf:nf + na], refs[nf + na:nf + na + nx]
    wf_ref, wa_ref, g_ref, b_ref, rw_ref, rb_ref, x1_ref, idx_ref, wt_ref = refs[nf + na + nx:]
    nr = x1_ref.shape[0] // ROW_SPLITS
    for h in range(ROW_SPLITS):
        r0 = h * nr
        m = (jnp.dot(_pick_rows(f_refs, fb, r0, nr), wf_ref[...], preferred_element_type=F32)
             + jnp.dot(_pick_rows(a_refs, ab, r0, nr), wa_ref[...], preferred_element_type=F32))
        xr = _pick_rows(x_refs, xb, r0, nr)
        _norm_route_store(r0, nr, m, xr, g_ref, b_ref, rw_ref, rb_ref, x1_ref, idx_ref, wt_ref, alpha)


def _outproj_even(f_parts, a_parts, x_parts, w_out, ln_g, ln_b, router, alpha, ts):
    T = sum(p.shape[0] for p in x_parts)
    f_specs, fb = _parts_specs(f_parts, ts)
    a_specs, ab = _parts_specs(a_parts, ts)
    x_specs, xb = _parts_specs(x_parts, ts)
    wf = w_out[:D_FOURIER].astype(BF16)
    wa = w_out[D_FOURIER:].reshape(N_KV_B, GRP_B * HEAD_DIM, D_MODEL)
    wa = jnp.concatenate([jnp.zeros((N_KV_B, HEAD_DIM, D_MODEL), w_out.dtype), wa], axis=1)
    wa = wa.reshape(1024, D_MODEL).astype(BF16)
    out_specs, out_shapes = _mix_out_specs(ts, T)
    return pl.pallas_call(
        functools.partial(_outproj_even_kernel, nf=len(f_parts), na=len(a_parts), nx=len(x_parts),
                          fb=fb, ab=ab, xb=xb, alpha=alpha),
        grid=(T // ts,),
        in_specs=f_specs + a_specs + x_specs + [
            pl.BlockSpec((D_FOURIER, D_MODEL), lambda i: (0, 0)),
            pl.BlockSpec((1024, D_MODEL), lambda i: (0, 0)),
            pl.BlockSpec((1, D_MODEL), lambda i: (0, 0)),
            pl.BlockSpec((1, D_MODEL), lambda i: (0, 0)),
        ] + _ROUTER_SPECS,
        out_specs=out_specs,
        out_shape=out_shapes,
        compiler_params=_cparams(("parallel",)),
        name="outproj_even",
    )(*f_parts, *a_parts, *x_parts, wf, wa, ln_g[None, :], ln_b[None, :], *router)


CHUNK = 8


def _moe_plan(idx, td, tm):
    T = idx.shape[1]
    nt = T // td
    e = idx.T.reshape(-1)
    oh = (e[:, None] == jnp.arange(N_EXPERTS, dtype=I32)[None, :]).astype(I32)
    csum = jnp.cumsum(oh, axis=0)
    tile_end = csum.reshape(nt, 2 * td, N_EXPERTS)[:, -1, :]
    base = jnp.concatenate([jnp.zeros((1, N_EXPERTS), I32), tile_end[:-1]], axis=0)
    cnt = tile_end - base
    c8 = ((cnt + CHUNK - 1) // CHUNK) * CHUNK
    off8 = jnp.cumsum(c8, axis=1) - c8
    base8 = jnp.cumsum(c8, axis=0) - c8
    seg = jnp.sum(c8, axis=0)
    padded = ((seg + tm - 1) // tm) * tm
    ends = jnp.cumsum(padded)
    starts = ends - padded
    dst = starts[None, :] + base8
    shift = jnp.repeat(off8 - base, 2 * td, axis=0)
    lpos = (jnp.sum((csum + shift) * oh, axis=1) - 1).astype(I32)
    n_rows = -(-(2 * T + nt * N_EXPERTS * (CHUNK - 1)) // tm) * tm + N_EXPERTS * tm
    tile_start = jnp.arange(n_rows // tm, dtype=I32) * tm
    tile_e = jnp.minimum(jnp.sum((ends[None, :] <= tile_start[:, None]).astype(I32), axis=1), N_EXPERTS - 1)
    n_used = (ends[-1] // tm).astype(I32)[None]
    tab = jnp.concatenate([cnt, dst], axis=1).astype(I32).reshape(nt, 1, 2 * N_EXPERTS)
    pad_lo = jnp.concatenate([starts + seg, ends[-1:]]).astype(I32)
    pad_hi = jnp.concatenate([ends, jnp.full((1,), n_rows, I32)]).astype(I32)
    lpos_tk = lpos.reshape(T, 2)
    lpos_rows = jnp.zeros((8, T), I32).at[:2].set(lpos_tk.T)
    return tab, lpos_rows, lpos_tk, tile_e.astype(I32), n_used, pad_lo, pad_hi, n_rows


def _chunk_loops(tab_ref, fn):
    off = jnp.int32(0)
    total = jnp.int32(0)
    for e in range(N_EXPERTS):
        nc = (tab_ref[0, 0, e] + (CHUNK - 1)) // CHUNK
        dst = tab_ref[0, 0, N_EXPERTS + e]

        def body(j, c, off=off, dst=dst):
            fn(pl.multiple_of(off + CHUNK * j, CHUNK), pl.multiple_of(dst + CHUNK * j, CHUNK))
            return c
        lax.fori_loop(0, nc, body, 0)
        off = off + CHUNK * nc
        total = total + nc
    return total


def _dispatch_kernel(lo_ref, hi_ref, tab_ref, lp_ref, x_ref, xs_hbm, stage, zbuf, nwait, sem, zsem, *, ns):
    i = pl.program_id(0)
    slot = i % 2

    def zero_copy(dst):
        return pltpu.make_async_copy(zbuf, xs_hbm.at[pl.ds(pl.multiple_of(dst, CHUNK), CHUNK)], zsem)

    @pl.when(i == 0)
    def _():
        nwait[0] = 0
        zbuf[...] = jnp.zeros(zbuf.shape, F32)
        for e in range(N_EXPERTS + 1):
            nz = (hi_ref[e] - lo_ref[e]) // CHUNK
            lax.fori_loop(0, nz, lambda j, c, e=e: (zero_copy(lo_ref[e] + CHUNK * j).start(), c)[1], 0)
        for e in range(N_EXPERTS + 1):
            nz = (hi_ref[e] - lo_ref[e]) // CHUNK
            lax.fori_loop(0, nz, lambda j, c: (zero_copy(0).wait(), c)[1], 0)

    lp = lp_ref[...]
    rows = lax.broadcasted_iota(I32, (ns, lp.shape[1]), 0)
    perm = jnp.where((rows == lp[0:1]) | (rows == lp[1:2]), 1.0, 0.0).astype(BF16)
    stage[slot] = jnp.dot(perm, x_ref[...].astype(BF16), preferred_element_type=F32)

    def piece(local_row, sorted_row):
        return pltpu.make_async_copy(stage.at[slot, pl.ds(local_row, CHUNK)],
                                     xs_hbm.at[pl.ds(sorted_row, CHUNK)], sem)

    lax.fori_loop(0, nwait[0], lambda j, c: (piece(0, 0).wait(), c)[1], 0)
    nwait[0] = _chunk_loops(tab_ref, lambda lr, sr: piece(lr, sr).start())

    @pl.when(i == pl.num_programs(0) - 1)
    def _():
        lax.fori_loop(0, nwait[0], lambda j, c: (piece(0, 0).wait(), c)[1], 0)


def _moe_dispatch(x1, tab, lpos_rows, pad_lo, pad_hi, n_rows, td):
    T = x1.shape[0]
    ns = 2 * td + 128
    grid_spec = pltpu.PrefetchScalarGridSpec(
        num_scalar_prefetch=2,
        grid=(T // td,),
        in_specs=[
            pl.BlockSpec((1, 1, 2 * N_EXPERTS), lambda i, lo, hi: (i, 0, 0), memory_space=pltpu.SMEM),
            pl.BlockSpec((8, td), lambda i, lo, hi: (0, i)),
            pl.BlockSpec((td, D_MODEL), lambda i, lo, hi: (i, 0)),
        ],
        out_specs=pl.BlockSpec(memory_space=pl.ANY),
        scratch_shapes=[pltpu.VMEM((2, ns, D_MODEL), F32), pltpu.VMEM((CHUNK, D_MODEL), F32),
                        pltpu.SMEM((1,), I32), pltpu.SemaphoreType.DMA(()), pltpu.SemaphoreType.DMA(())],
    )
    return pl.pallas_call(
        functools.partial(_dispatch_kernel, ns=ns),
        grid_spec=grid_spec,
        out_shape=jax.ShapeDtypeStruct((n_rows, D_MODEL), F32),
        compiler_params=_cparams(("arbitrary",)),
        name="moe_dispatch",
    )(pad_lo, pad_hi, tab, lpos_rows, x1)


def _moe_kernel(te_ref, nu_ref, x_ref, wg_ref, wu_ref, wd_ref, y_ref):
    i = pl.program_id(0)

    @pl.when(i < nu_ref[0])
    def _():
        xg = x_ref[...].astype(BF16)
        hg = jnp.dot(xg, wg_ref[0], preferred_element_type=F32)
        hu = jnp.dot(xg, wu_ref[0], preferred_element_type=F32)
        hdn = (hg / (1.0 + jnp.exp(-hg)) * hu).astype(BF16)
        y_ref[...] = jnp.dot(hdn, wd_ref[0], preferred_element_type=F32)

    @pl.when(i >= nu_ref[0])
    def _():
        y_ref[...] = jnp.zeros(y_ref.shape, F32)


def _moe_experts(xs, tile_e, n_used, wg, wu, wd, tm):
    n_rows = xs.shape[0]
    grid_spec = pltpu.PrefetchScalarGridSpec(
        num_scalar_prefetch=2,
        grid=(n_rows // tm,),
        in_specs=[
            pl.BlockSpec((tm, D_MODEL), lambda i, te, nu: (i, 0)),
            pl.BlockSpec((1, D_MODEL, D_FF), lambda i, te, nu: (te[i], 0, 0)),
            pl.BlockSpec((1, D_MODEL, D_FF), lambda i, te, nu: (te[i], 0, 0)),
            pl.BlockSpec((1, D_FF, D_MODEL), lambda i, te, nu: (te[i], 0, 0)),
        ],
        out_specs=pl.BlockSpec((tm, D_MODEL), lambda i, te, nu: (i, 0)),
    )
    return pl.pallas_call(
        _moe_kernel,
        grid_spec=grid_spec,
        out_shape=jax.ShapeDtypeStruct((n_rows, D_MODEL), F32),
        compiler_params=_cparams(("arbitrary",)),
        name="moe_experts",
    )(tile_e, n_used, xs, wg, wu, wd)


def _combine_kernel(tab_ref, tabn_ref, lp_ref, y_hbm, x_ref, wt_ref, g_ref, b_ref, *rest, ns, alpha, bounds):
    o_refs, (ystage, sem) = rest[:len(bounds)], rest[len(bounds):]
    i = pl.program_id(0)
    n = pl.num_programs(0)
    slot = i % 2

    def piece(local_row, sorted_row, s):
        return pltpu.make_async_copy(y_hbm.at[pl.ds(sorted_row, CHUNK)],
                                     ystage.at[s, pl.ds(local_row, CHUNK)], sem.at[s])

    @pl.when(i == 0)
    def _():
        ystage[...] = jnp.zeros(ystage.shape, F32)
        _chunk_loops(tab_ref, lambda lr, sr: piece(lr, sr, 0).start())

    @pl.when(i + 1 < n)
    def _():
        _chunk_loops(tabn_ref, lambda lr, sr: piece(lr, sr, 1 - slot).start())

    _chunk_loops(tab_ref, lambda lr, sr: piece(0, 0, slot).wait())

    ys = ystage[slot].astype(BF16)
    lp = lp_ref[...]
    cols = lax.broadcasted_iota(I32, (lp.shape[0], ns), 1)
    w = wt_ref[...]
    f = jnp.zeros(x_ref.shape, F32)
    for k in range(2):
        sel = jnp.where(cols == lp[:, k:k + 1], 1.0, 0.0).astype(BF16)
        f = f + w[:, k:k + 1] * jnp.dot(sel, ys, preferred_element_type=F32)
    o = _layer_norm(alpha * x_ref[...] + f, g_ref[...], b_ref[...])
    if len(bounds) == 1:
        o_refs[0][...] = o
    else:
        lo = 0
        for o_ref, hi in zip(o_refs, bounds):
            @pl.when((i >= lo) & (i < hi))
            def _(o_ref=o_ref):
                o_ref[...] = o
            lo = hi


def _moe_combine(y, tab, lpos_tk, wts, x1, ln_g, ln_b, alpha, tc, out_rows):
    T = x1.shape[0]
    n_tiles = T // tc
    ns = 2 * tc + 128
    out_specs, out_shapes, bounds, lo = [], [], [], 0
    for rows in out_rows:
        n = rows // tc
        out_specs.append(pl.BlockSpec((tc, D_MODEL), lambda i, lo=lo, n=n: (jnp.clip(i - lo, 0, n - 1), 0)))
        out_shapes.append(jax.ShapeDtypeStruct((rows, D_MODEL), F32))
        lo += n
        bounds.append(lo)
    return pl.pallas_call(
        functools.partial(_combine_kernel, ns=ns, alpha=alpha, bounds=tuple(bounds)),
        grid=(n_tiles,),
        in_specs=[
            pl.BlockSpec((1, 1, 2 * N_EXPERTS), lambda i: (i, 0, 0), memory_space=pltpu.SMEM),
            pl.BlockSpec((1, 1, 2 * N_EXPERTS), lambda i: (jnp.minimum(i + 1, n_tiles - 1), 0, 0),
                         memory_space=pltpu.SMEM),
            pl.BlockSpec((tc, 2), lambda i: (i, 0)),
            pl.BlockSpec(memory_space=pl.ANY),
            pl.BlockSpec((tc, D_MODEL), lambda i: (i, 0)),
            pl.BlockSpec((tc, 2), lambda i: (i, 0)),
            pl.BlockSpec((1, D_MODEL), lambda i: (0, 0)),
            pl.BlockSpec((1, D_MODEL), lambda i: (0, 0)),
        ],
        out_specs=out_specs,
        out_shape=out_shapes,
        scratch_shapes=[pltpu.VMEM((2, ns, D_MODEL), F32), pltpu.SemaphoreType.DMA((2,))],
        compiler_params=_cparams(("arbitrary",)),
        name="moe_combine",
    )(tab, tab, lpos_tk, y, x1, wts, ln_g[None, :], ln_b[None, :])


def _moe_layer(x1, idx, wt, wg, wu, wd, ln_g, ln_b, alpha, tm, tc, out_rows):
    tab, lpos_rows, lpos_tk, tile_e, n_used, pad_lo, pad_hi, n_rows = _moe_plan(idx[:2], tc, tm)
    xs = _moe_dispatch(x1, tab, lpos_rows, pad_lo, pad_hi, n_rows, tc)
    y = _moe_experts(xs, tile_e, n_used, wg.astype(BF16), wu.astype(BF16), wd.astype(BF16), tm)
    return _moe_combine(y, tab, lpos_tk, wt[:2].T, x1, ln_g, ln_b, alpha, tc, out_rows)


def _inproj_odd_kernel(x_ref, w_ref, vone_ref, bg_ref, u_ref, q_ref, k_ref, v_ref):
    x = x_ref[...].astype(BF16)
    proj = jnp.dot(x, w_ref[...], preferred_element_type=F32)
    bg_ref[...] = proj[:, :512].astype(BF16)
    u_ref[...] = (proj[:, 512:1024] * proj[:, 1024:1536]).astype(BF16)
    q_ref[...] = (proj[:, 1536:2048] * (HEAD_DIM ** -0.5 * LOG2E)).astype(BF16)
    k_ref[...] = proj[:, 2048:2304].astype(BF16)
    v_ref[...] = (proj[:, 2304:2560] + vone_ref[...]).astype(BF16)


def _inproj_odd(x, w_in, ts):
    T = x.shape[0]
    c3 = 3 * D_CONV
    wk = w_in[:, c3 + 512:c3 + 640].reshape(D_MODEL, N_KV_D, HEAD_DIM)
    wv = w_in[:, c3 + 640:c3 + 768].reshape(D_MODEL, N_KV_D, HEAD_DIM)
    wkp = jnp.concatenate([wk, jnp.zeros_like(wk)], axis=2).reshape(D_MODEL, 256)
    wvp = jnp.concatenate([wv, jnp.zeros_like(wv)], axis=2).reshape(D_MODEL, 256)
    w = jnp.concatenate([w_in[:, :c3 + 512], wkp, wvp], axis=1).astype(BF16)
    vone_np = np.zeros((1, 256), np.float32)
    for h in range(N_KV_D):
        vone_np[0, 128 * h + 64:128 * (h + 1)] = 1.0
    n_w = w.shape[1]
    widths = [512, 512, 512, 256, 256]
    return pl.pallas_call(
        _inproj_odd_kernel,
        grid=(T // ts,),
        in_specs=[
            pl.BlockSpec((ts, D_MODEL), lambda i: (i, 0)),
            pl.BlockSpec((D_MODEL, n_w), lambda i: (0, 0)),
            pl.BlockSpec((1, 256), lambda i: (0, 0)),
        ],
        out_specs=[pl.BlockSpec((ts, n), lambda i: (i, 0)) for n in widths],
        out_shape=[jax.ShapeDtypeStruct((T, n), BF16) for n in widths],
        compiler_params=_cparams(("parallel",)),
        name="inproj_odd",
    )(x, w, jnp.asarray(vone_np))


def _wattn_kernel(q_ref, kp_ref, kc_ref, kn_ref, vp_ref, vc_ref, vn_ref, bias_ref, sink_ref, place_ref,
                  o_ref, *, tq, seq_tiles):
    i = pl.program_id(0)
    first = i < 0
    last = i < 0
    for lo, hi, per in seq_tiles:
        inside = (i >= lo) & (i < hi)
        first = first | (inside & ((i - lo) % per == 0))
        last = last | (inside & ((i - lo) % per == per - 1))
    kfull = jnp.concatenate([kp_ref[...], kc_ref[...], kn_ref[...]], axis=0)[:, :HEAD_DIM]
    vfull = jnp.concatenate([vp_ref[...], vc_ref[...], vn_ref[...]], axis=0)
    q4 = q_ref[...]
    bias = bias_ref[0]
    sink = sink_ref[0]
    nb = tq // Q_BLOCK
    col = lax.broadcasted_iota(I32, (GRP_D * Q_BLOCK, 3 * Q_BLOCK), 1)
    for n in range(nb):
        qs = jnp.concatenate([q4[Q_BLOCK * n:Q_BLOCK * (n + 1), HEAD_DIM * g:HEAD_DIM * (g + 1)]
                              for g in range(GRP_D)], axis=0)
        keys = kfull[Q_BLOCK * n:Q_BLOCK * (n + 3)]
        vals = vfull[Q_BLOCK * n:Q_BLOCK * (n + 3)]
        s = lax.dot_general(qs, keys, (((1,), (1,)), ((), ())), preferred_element_type=F32) + bias
        if n == 0:
            s = jnp.where(first & (col < Q_BLOCK), NEG_INF, s)
        if n == nb - 1:
            s = jnp.where(last & (col >= 2 * Q_BLOCK), NEG_INF, s)
        m = jnp.maximum(jnp.max(s, axis=1, keepdims=True), sink)
        p = jnp.exp2(s - m).astype(BF16)
        acc = jnp.dot(p, vals, preferred_element_type=F32)
        den = acc[:, HEAD_DIM:HEAD_DIM + 1] + jnp.exp2(sink - m)
        o = (acc / den).astype(BF16)
        out = jnp.zeros((Q_BLOCK, 256), F32)
        for g in range(GRP_D):
            out = out + jnp.dot(o[Q_BLOCK * g:Q_BLOCK * (g + 1)], place_ref[g], preferred_element_type=F32)
        o_ref[Q_BLOCK * n:Q_BLOCK * (n + 1), :] = out.astype(BF16)


def _window_attention(q, kw, vw, sink_logits, classes, tq):
    T = q.shape[0]
    n_tiles = T // tq
    hb = tq // Q_BLOCK
    n_hblk = T // Q_BLOCK
    r = jnp.arange(Q_BLOCK, dtype=I32)[:, None]
    j = jnp.arange(3 * Q_BLOCK, dtype=I32)[None, :]
    rel = jnp.abs(j - Q_BLOCK - r).astype(F32)
    slopes = jnp.asarray(np.array([2.0 ** (-8.0 * (h + 1) / N_HEADS_D) for h in range(N_HEADS_D)], np.float32))
    bias = jnp.where(rel[None] <= WINDOW, -slopes[:, None, None] * rel[None] * LOG2E, NEG_INF)
    bias = bias.reshape(N_KV_D, GRP_D * Q_BLOCK, 3 * Q_BLOCK)
    sink = jnp.repeat(sink_logits.astype(F32) * LOG2E, Q_BLOCK).reshape(N_KV_D, GRP_D * Q_BLOCK, 1)
    place = _lane_place(128, 256, [0, 1, 2, 3])
    seq_tiles = []
    t0 = 0
    for nseq, S, _ in classes:
        cnt = nseq * S // tq
        seq_tiles.append((t0, t0 + cnt, S // tq))
        t0 += cnt
    prev_map = lambda i, h: (jnp.maximum(i * hb - 1, 0), h)
    next_map = lambda i, h: (jnp.minimum((i + 1) * hb, n_hblk - 1), h)
    return pl.pallas_call(
        functools.partial(_wattn_kernel, tq=tq, seq_tiles=tuple(seq_tiles)),
        grid=(n_tiles, N_KV_D),
        in_specs=[
            pl.BlockSpec((tq, 256), lambda i, h: (i, h)),
            pl.BlockSpec((Q_BLOCK, 128), prev_map),
            pl.BlockSpec((tq, 128), lambda i, h: (i, h)),
            pl.BlockSpec((Q_BLOCK, 128), next_map),
            pl.BlockSpec((Q_BLOCK, 128), prev_map),
            pl.BlockSpec((tq, 128), lambda i, h: (i, h)),
            pl.BlockSpec((Q_BLOCK, 128), next_map),
            pl.BlockSpec((1, GRP_D * Q_BLOCK, 3 * Q_BLOCK), lambda i, h: (h, 0, 0)),
            pl.BlockSpec((1, GRP_D * Q_BLOCK, 1), lambda i, h: (h, 0, 0)),
            pl.BlockSpec((GRP_D, 128, 256), lambda i, h: (0, 0, 0)),
        ],
        out_specs=pl.BlockSpec((tq, 256), lambda i, h: (i, h)),
        out_shape=jax.ShapeDtypeStruct((T, 512), BF16),
        compiler_params=_cparams(("parallel", "parallel")),
        name="window_attention",
    )(q, kw, kw, kw, vw, vw, vw, bias, sink, place)


HALO = 16


def _outproj_odd_kernel(bg_ref, u_ref, up_ref, un_ref, a_ref, x_ref, cw_ref, cb_ref, wc_ref, wa_ref,
                        g_ref, b_ref, rw_ref, rb_ref, x1_ref, idx_ref, wt_ref,
                        *, alpha, ts, seq_tiles):
    i = pl.program_id(0)
    first = i < 0
    last = i < 0
    for lo, hi, per in seq_tiles:
        inside = (i >= lo) & (i < hi)
        first = first | (inside & ((i - lo) % per == 0))
        last = last | (inside & ((i - lo) % per == per - 1))
    u = u_ref[...].astype(F32)
    prev_row = jnp.where(first, 0.0, up_ref[HALO - 1:HALO, :].astype(F32))
    next_row = jnp.where(last, 0.0, un_ref[0:1, :].astype(F32))
    rid = lax.broadcasted_iota(I32, u.shape, 0)
    ud = jnp.where(rid == 0, prev_row, pltpu.roll(u, 1, 0))
    uu = jnp.where(rid == ts - 1, next_row, pltpu.roll(u, ts - 1, 0))
    cw = cw_ref[...]
    y = ud * cw[0:1] + u * cw[1:2] + uu * cw[2:3] + cb_ref[...]
    c = (bg_ref[...].astype(F32) * y).astype(BF16)
    nr = ts // ROW_SPLITS
    for h in range(ROW_SPLITS):
        r0 = h * nr
        m = (jnp.dot(c[r0:r0 + nr], wc_ref[...], preferred_element_type=F32)
             + jnp.dot(a_ref[r0:r0 + nr, :], wa_ref[...], preferred_element_type=F32))
        _norm_route_store(r0, nr, m, x_ref[r0:r0 + nr, :], g_ref, b_ref, rw_ref, rb_ref, x1_ref, idx_ref,
                          wt_ref, alpha)


def _outproj_odd(bg, u, a, x, conv_w, conv_b, w_out, ln_g, ln_b, router, alpha, classes, ts):
    T = x.shape[0]
    hb = ts // HALO
    n_h = T // HALO
    seq_tiles = []
    t0 = 0
    for nseq, S, _ in classes:
        cnt = nseq * S // ts
        seq_tiles.append((t0, t0 + cnt, S // ts))
        t0 += cnt
    out_specs, out_shapes = _mix_out_specs(ts, T)
    return pl.pallas_call(
        functools.partial(_outproj_odd_kernel, alpha=alpha, ts=ts, seq_tiles=tuple(seq_tiles)),
        grid=(T // ts,),
        in_specs=[
            pl.BlockSpec((ts, D_CONV), lambda i: (i, 0)),
            pl.BlockSpec((ts, D_CONV), lambda i: (i, 0)),
            pl.BlockSpec((HALO, D_CONV), lambda i: (jnp.maximum(i * hb - 1, 0), 0)),
            pl.BlockSpec((HALO, D_CONV), lambda i: (jnp.minimum((i + 1) * hb, n_h - 1), 0)),
            pl.BlockSpec((ts, 512), lambda i: (i, 0)),
            pl.BlockSpec((ts, D_MODEL), lambda i: (i, 0)),
            pl.BlockSpec((3, D_CONV), lambda i: (0, 0)),
            pl.BlockSpec((1, D_CONV), lambda i: (0, 0)),
            pl.BlockSpec((D_CONV, D_MODEL), lambda i: (0, 0)),
            pl.BlockSpec((512, D_MODEL), lambda i: (0, 0)),
            pl.BlockSpec((1, D_MODEL), lambda i: (0, 0)),
            pl.BlockSpec((1, D_MODEL), lambda i: (0, 0)),
        ] + _ROUTER_SPECS,
        out_specs=out_specs,
        out_shape=out_shapes,
        compiler_params=_cparams(("parallel",)),
        name="outproj_odd",
    )(bg, u, u, u, a, x, conv_w.astype(F32), conv_b.astype(F32)[None, :],
      w_out[:D_CONV].astype(BF16), w_out[D_CONV:].astype(BF16), ln_g[None, :], ln_b[None, :], *router)


def _tile(n, cap):
    t = cap
    while n % t:
        t //= 2
    return t


def kernel(x_prompt, x_sample, w_in_even, fourier_norm_g, q_norm_g, k_norm_g, w_out_even, w_in_odd, conv_w,
           conv_b, sink_logits, w_out_odd, ln_mix_g, ln_mix_b, ln_ffn_g, ln_ffn_b, router_w, router_b,
           w_gate, w_up, w_down):
    depth = ln_mix_g.shape[0]
    alpha = float((2 * depth) ** 0.25)
    bp, sp, _ = x_prompt.shape
    bs, ss, _ = x_sample.shape
    classes = ((bp, sp, 0), (bs, ss, bp * sp))
    T = bp * sp + bs * ss
    min_s = min(sp, ss)
    ts = _tile(min_s, 512)
    tq_w = _tile(min_s, 512)
    tm = 512
    tc = _tile(min_s, 256)
    x_parts = [x_prompt.reshape(bp * sp, D_MODEL), x_sample.reshape(bs * ss, D_MODEL)]
    router = _router_operands(router_w, router_b)
    for l in range(depth):
        i = l // 2
        if l % 2 == 0:
            uf, qk, v_aug = _inproj_even(x_parts, w_in_even[i], fourier_norm_g[i], q_norm_g[i], k_norm_g[i],
                                         classes, ts)
            bounded = _score_bounded(q_norm_g[i], k_norm_g[i])
            f_parts, a_parts = [], []
            for nseq, S, tok0 in classes:
                f_parts.append(_fourier_mix(uf[tok0:tok0 + nseq * S], nseq, S))
                a_parts.append(_global_attention(qk, v_aug, bounded, nseq, S, tok0, _tile(S, 512),
                                                 _tile(S, 2048)))
            x1, idx, wt = _outproj_even(f_parts, a_parts, x_parts, w_out_even[i], ln_mix_g[l], ln_mix_b[l],
                                        router, alpha, ts)
        else:
            x = x_parts[0]
            bg, u, q, kw, vw = _inproj_odd(x, w_in_odd[i], ts)
            a = _window_attention(q, kw, vw, sink_logits[i], classes, tq_w)
            x1, idx, wt = _outproj_odd(bg, u, a, x, conv_w[i], conv_b[i], w_out_odd[i], ln_mix_g[l],
                                       ln_mix_b[l], router, alpha, classes, ts)
        out_rows = [bp * sp, bs * ss] if l == depth - 1 else [T]
        x_parts = _moe_layer(x1, idx, wt, w_gate[l], w_up[l], w_down[l], ln_ffn_g[l], ln_ffn_b[l], alpha, tm,
                             tc, out_rows)
    if len(x_parts) == 1:
        x_parts = [x_parts[0][:bp * sp], x_parts[0][bp * sp:]]
    return (x_parts[0].reshape(bp, sp, D_MODEL), x_parts[1].reshape(bs, ss, D_MODEL))
```

```python
import functools
import math

import numpy as np
import jax
import jax.numpy as jnp
from jax import lax
from jax.experimental import pallas as pl
from jax.experimental.pallas import tpu as pltpu

F32 = jnp.float32
BF16 = jnp.bfloat16
I32 = jnp.int32

D_MODEL = 1024
HEAD_DIM = 64
GRID_W = 64
Q_BLOCK = 128
WINDOW = 128
ROPE_THETA = 10000.0
N_FOURIER_GROUPS = 4
D_FOURIER = 256
N_HEADS_B = 12
N_KV_B = 4
GRP_B = 3
D_CONV = 512
N_HEADS_D = 8
N_KV_D = 2
GRP_D = 4
N_EXPERTS = 16
N_GROUPS = 4
EXPERTS_PER_GROUP = 4
D_FF = 512
LN_EPS = 1e-5
RMS_EPS = 1e-6
NEG_INF = -1e30
LOG2E = 1.4426950408889634
FFT_S2 = 128
V7X_VMEM_LIMIT = 48 * 1024 * 1024


def _cparams(sem):
    return pltpu.CompilerParams(dimension_semantics=sem, vmem_limit_bytes=V7X_VMEM_LIMIT)


def _layer_norm(y, g, b):
    mu = jnp.mean(y, axis=-1, keepdims=True)
    yc = y - mu
    var = jnp.mean(yc * yc, axis=-1, keepdims=True)
    return yc * lax.rsqrt(var + LN_EPS) * g + b


N_NORM_COLS = D_FOURIER + 1024


def _parts_specs(parts, ts):
    specs, bounds, lo = [], [], 0
    for p in parts:
        n = p.shape[0] // ts
        specs.append(pl.BlockSpec((ts, p.shape[1]), lambda i, lo=lo, n=n: (jnp.clip(i - lo, 0, n - 1), 0)))
        lo += n
        bounds.append(lo)
    return specs, tuple(bounds)


def _pick_rows(refs, bounds, r0, nr):
    i = pl.program_id(0)
    out = refs[-1][r0:r0 + nr, :]
    for ref, hi in reversed(list(zip(refs[:-1], bounds[:-1]))):
        out = jnp.where(i < hi, ref[r0:r0 + nr, :], out)
    return out


def _inproj_even_kernel(*refs, nx, bounds):
    x_refs = refs[:nx]
    (w_ref, gain_ref, seg_ref, segt_ref, cos_ref, sin_ref, vone_ref, uf_ref, qk_ref, v_ref) = refs[nx:]
    x = _pick_rows(x_refs, bounds, 0, x_refs[0].shape[0]).astype(BF16)
    proj = jnp.dot(x, w_ref[...], preferred_element_type=F32)
    nrm = proj[:, :N_NORM_COLS]
    sq = (nrm * nrm).astype(BF16)
    ssum = jnp.dot(sq, seg_ref[...], preferred_element_type=F32)
    r = lax.rsqrt(ssum * (1.0 / HEAD_DIM) + RMS_EPS)
    rh = r.astype(BF16)
    rl = (r - rh.astype(F32)).astype(BF16)
    rex = jnp.dot(jnp.concatenate([rh, rl], axis=1), segt_ref[...], preferred_element_type=F32)
    y = nrm * rex * gain_ref[...]
    uf_ref[...] = y[:, :D_FOURIER].astype(BF16)
    yq = y[:, D_FOURIER:]
    c = jnp.concatenate([cos_ref[...]] * 8, axis=1)
    s = jnp.concatenate([sin_ref[...]] * 8, axis=1)
    lane = lax.broadcasted_iota(I32, yq.shape, 1)
    first = (lane & 31) < 16
    sw = jnp.where(first, pltpu.roll(yq, 1024 - 16, 1), pltpu.roll(yq, 16, 1))
    qk_ref[...] = (yq * c + sw * s).astype(BF16)
    vv = proj[:, N_NORM_COLS:] + vone_ref[...]
    for h in range(N_KV_B):
        v_ref[h] = vv[:, 128 * h:128 * (h + 1)].astype(BF16)


def _rope_tables(n_pos):
    t = jnp.arange(n_pos)
    row = (t // GRID_W).astype(F32)
    col = (t % GRID_W).astype(F32)
    n_freq = HEAD_DIM // 4
    inv_freq = ROPE_THETA ** (-jnp.arange(n_freq, dtype=F32) / n_freq)
    ar = row[:, None] * inv_freq
    ac = col[:, None] * inv_freq
    ang = jnp.concatenate([ar, ar, ac, ac], axis=1)
    sign = jnp.asarray(np.tile(np.repeat(np.array([-1.0, 1.0], np.float32), 16), 2))
    cos = jnp.cos(ang)
    sin = jnp.sin(ang) * sign
    return jnp.concatenate([cos, cos], axis=1), jnp.concatenate([sin, sin], axis=1)


def _pos_block_map(classes, ts):
    bounds = []
    tile0 = 0
    for nseq, S, _ in classes:
        n_tiles = nseq * S // ts
        bounds.append((tile0, tile0 + n_tiles, S // ts))
        tile0 += n_tiles

    def fn(i):
        out = (i - bounds[-1][0]) % bounds[-1][2]
        for lo, hi, per in reversed(bounds[:-1]):
            out = jnp.where(i < hi, (i - lo) % per, out)
        return out

    return fn


def _inproj_even(x_parts, w_in, f_g, q_g, k_g, classes, ts):
    T = sum(p.shape[0] for p in x_parts)
    x_specs, bounds = _parts_specs(x_parts, ts)
    wf = w_in[:, :D_FOURIER]
    wq = w_in[:, D_FOURIER:D_FOURIER + 768].reshape(D_MODEL, N_KV_B, GRP_B * HEAD_DIM)
    wk = w_in[:, D_FOURIER + 768:D_FOURIER + 1024].reshape(D_MODEL, N_KV_B, HEAD_DIM)
    wv = w_in[:, D_FOURIER + 1024:].reshape(D_MODEL, N_KV_B, HEAD_DIM)
    wqk = jnp.concatenate([wk, wq], axis=2).reshape(D_MODEL, 1024)
    wvp = jnp.concatenate([wv, jnp.zeros_like(wv)], axis=2).reshape(D_MODEL, 512)
    w = jnp.concatenate([wf, wqk, wvp], axis=1).astype(BF16)
    qscale = HEAD_DIM ** -0.5 * LOG2E
    gqk = jnp.tile(jnp.concatenate([k_g, q_g * qscale, q_g * qscale, q_g * qscale]), N_KV_B)
    gain = jnp.concatenate([f_g, gqk])[None, :].astype(F32)
    seg_np = np.zeros((N_NORM_COLS, 128), np.float32)
    seg_np[np.arange(N_NORM_COLS), np.arange(N_NORM_COLS) // HEAD_DIM] = 1.0
    seg = jnp.asarray(seg_np, BF16)
    segt = jnp.asarray(np.concatenate([seg_np.T, seg_np.T], axis=0), BF16)
    max_s = max(S for _, S, _ in classes)
    cos, sin = _rope_tables(max_s)
    vone_np = np.zeros((1, 512), np.float32)
    for h in range(N_KV_B):
        vone_np[0, 128 * h + 64:128 * (h + 1)] = 1.0
    vone = jnp.asarray(vone_np)
    posmap = _pos_block_map(classes, ts)
    n_w = w.shape[1]
    return pl.pallas_call(
        functools.partial(_inproj_even_kernel, nx=len(x_parts), bounds=bounds),
        grid=(T // ts,),
        in_specs=x_specs + [
            pl.BlockSpec((D_MODEL, n_w), lambda i: (0, 0)),
            pl.BlockSpec((1, N_NORM_COLS), lambda i: (0, 0)),
            pl.BlockSpec((N_NORM_COLS, 128), lambda i: (0, 0)),
            pl.BlockSpec((256, N_NORM_COLS), lambda i: (0, 0)),
            pl.BlockSpec((ts, 128), lambda i: (posmap(i), 0)),
            pl.BlockSpec((ts, 128), lambda i: (posmap(i), 0)),
            pl.BlockSpec((1, 512), lambda i: (0, 0)),
        ],
        out_specs=[
            pl.BlockSpec((ts, D_FOURIER), lambda i: (i, 0)),
            pl.BlockSpec((ts, 1024), lambda i: (i, 0)),
            pl.BlockSpec((N_KV_B, ts, 128), lambda i: (0, i, 0)),
        ],
        out_shape=[
            jax.ShapeDtypeStruct((T, D_FOURIER), BF16),
            jax.ShapeDtypeStruct((T, 1024), BF16),
            jax.ShapeDtypeStruct((N_KV_B, T, 128), BF16),
        ],
        compiler_params=_cparams(("parallel",)),
        name="inproj_even",
    )(*x_parts, w, gain, seg, segt, cos, sin, vone)


def _fft1_kernel(a_ref, w1_ref, twr_ref, twi_ref, zr_ref, zi_ref, *, s1):
    z = jnp.dot(w1_ref[...], a_ref[0], preferred_element_type=F32)
    zr = z[:s1]
    zi = z[s1:]
    twr = twr_ref[...]
    twi = twi_ref[...]
    zr_ref[0] = (zr * twr - zi * twi).astype(BF16)
    zi_ref[0] = (zr * twi + zi * twr).astype(BF16)


def _fft2_kernel(zr_ref, zi_ref, w2a_ref, w2b_ref, mix_ref, o_ref, *, cb):
    s2 = FFT_S2
    for c in range(cb):
        pp = (jnp.dot(w2a_ref[...], zr_ref[0, c], preferred_element_type=F32)
              + jnp.dot(w2b_ref[...], zi_ref[0, c], preferred_element_type=F32))
        f = (jnp.dot(pp[:s2].astype(BF16), mix_ref[:D_FOURIER], preferred_element_type=F32)
             + jnp.dot(pp[s2:].astype(BF16), mix_ref[D_FOURIER:], preferred_element_type=F32))
        o_ref[0, :, D_FOURIER * c:D_FOURIER * (c + 1)] = f.astype(BF16)


def _dft_mats(n):
    k = jnp.arange(n, dtype=I32)
    ang = (2.0 * math.pi / n) * ((k[:, None] * k[None, :]) % n).astype(F32)
    return jnp.cos(ang), jnp.sin(ang)


def _fourier_mix(uf_part, nseq, S):
    s2 = FFT_S2
    s1 = S // s2
    ncol = s2 * D_FOURIER
    a = uf_part.reshape(nseq, s1, ncol)
    c1, sn1 = _dft_mats(s1)
    w1 = jnp.concatenate([c1, -sn1], axis=0).astype(BF16)
    cc = jnp.arange(s1, dtype=I32)[:, None]
    bb = jnp.arange(s2, dtype=I32)[None, :]
    ang = (2.0 * math.pi / S) * ((cc * bb) % S).astype(F32)
    twr = jnp.repeat(jnp.cos(ang), D_FOURIER, axis=1)
    twi = jnp.repeat(-jnp.sin(ang), D_FOURIER, axis=1)
    tn = min(ncol, 4096)
    zr, zi = pl.pallas_call(
        functools.partial(_fft1_kernel, s1=s1),
        grid=(ncol // tn, nseq),
        in_specs=[
            pl.BlockSpec((1, s1, tn), lambda j, b: (b, 0, j)),
            pl.BlockSpec((2 * s1, s1), lambda j, b: (0, 0)),
            pl.BlockSpec((s1, tn), lambda j, b: (0, j)),
            pl.BlockSpec((s1, tn), lambda j, b: (0, j)),
        ],
        out_specs=[
            pl.BlockSpec((1, s1, tn), lambda j, b: (b, 0, j)),
            pl.BlockSpec((1, s1, tn), lambda j, b: (b, 0, j)),
        ],
        out_shape=[jax.ShapeDtypeStruct((nseq, s1, ncol), BF16)] * 2,
        compiler_params=_cparams(("parallel", "parallel")),
        name="fourier_stage1",
    )(a, w1, twr, twi)
    zr = zr.reshape(nseq, s1, s2, D_FOURIER)
    zi = zi.reshape(nseq, s1, s2, D_FOURIER)
    c2, sn2 = _dft_mats(s2)
    w2a = jnp.concatenate([c2, -sn2], axis=0).astype(BF16)
    w2b = jnp.concatenate([sn2, c2], axis=0).astype(BF16)
    gc, gs = _dft_mats(HEAD_DIM)
    scale = 1.0 / math.sqrt(S * HEAD_DIM)
    eye = jnp.eye(N_FOURIER_GROUPS, dtype=F32)
    mix = (jnp.concatenate([jnp.kron(eye, gc), jnp.kron(eye, gs)], axis=0) * scale).astype(BF16)
    cb = min(8, s1)
    out = pl.pallas_call(
        functools.partial(_fft2_kernel, cb=cb),
        grid=(nseq, s1 // cb),
        in_specs=[
            pl.BlockSpec((1, cb, s2, D_FOURIER), lambda b, j: (b, j, 0, 0)),
            pl.BlockSpec((1, cb, s2, D_FOURIER), lambda b, j: (b, j, 0, 0)),
            pl.BlockSpec((2 * s2, s2), lambda b, j: (0, 0)),
            pl.BlockSpec((2 * s2, s2), lambda b, j: (0, 0)),
            pl.BlockSpec((2 * D_FOURIER, D_FOURIER), lambda b, j: (0, 0)),
        ],
        out_specs=pl.BlockSpec((1, s2, cb * D_FOURIER), lambda b, j: (b, 0, j)),
        out_shape=jax.ShapeDtypeStruct((nseq, s2, s1 * D_FOURIER), BF16),
        compiler_params=_cparams(("parallel", "parallel")),
        name="fourier_stage2",
    )(zr, zi, w2a, w2b, mix)
    return out.reshape(nseq * S, D_FOURIER)


SCORE_BOUND_MAX = 100.0


def _gattn_kernel(bounded_ref, q_ref, k_ref, v_ref, place_ref, o_ref, *, tq, tk, S):
    q3 = q_ref[...]
    qs = [q3[:, HEAD_DIM * (g + 1):HEAD_DIM * (g + 2)] for g in range(GRP_B)]

    def chunk(j):
        off = pl.multiple_of(j * tk, tk)
        return k_ref[pl.ds(off, tk), 0:HEAD_DIM], v_ref[0, pl.ds(off, tk), :]

    def scores(g, kc):
        return lax.dot_general(qs[g], kc, (((1,), (1,)), ((), ())), preferred_element_type=F32)

    def finish(accs):
        out = jnp.zeros((tq, 256), F32)
        for g in range(GRP_B):
            o = (accs[g] / accs[g][:, HEAD_DIM:HEAD_DIM + 1]).astype(BF16)
            out = out + jnp.dot(o, place_ref[g], preferred_element_type=F32)
        o_ref[...] = out.astype(BF16)

    @pl.when(bounded_ref[0] == 1)
    def _():
        def body(j, accs):
            kc, vc = chunk(j)
            return tuple(accs[g] + jnp.dot(jnp.exp2(scores(g, kc)).astype(BF16), vc,
                                           preferred_element_type=F32) for g in range(GRP_B))
        finish(lax.fori_loop(0, S // tk, body, tuple(jnp.zeros((tq, 128), F32) for _ in range(GRP_B))))

    @pl.when(bounded_ref[0] != 1)
    def _():
        def body(j, carry):
            kc, vc = chunk(j)
            new = []
            for g in range(GRP_B):
                m_prev, acc = carry[g]
                s = scores(g, kc)
                m_new = jnp.maximum(m_prev, jnp.max(s, axis=1, keepdims=True))
                p = jnp.exp2(s - m_new).astype(BF16)
                acc = jnp.exp2(m_prev - m_new) * acc + jnp.dot(p, vc, preferred_element_type=F32)
                new.append((m_new, acc))
            return tuple(new)
        init = tuple((jnp.full((tq, 1), NEG_INF, F32), jnp.zeros((tq, 128), F32)) for _ in range(GRP_B))
        fin = lax.fori_loop(0, S // tk, body, init)
        finish([fin[g][1] for g in range(GRP_B)])


def _score_bounded(q_g, k_g):
    bound = 1.05 * HEAD_DIM * (HEAD_DIM ** -0.5 * LOG2E) * jnp.max(jnp.abs(q_g)) * jnp.max(jnp.abs(k_g))
    return (bound <= SCORE_BOUND_MAX).astype(I32)[None]


def _lane_place(n_src, n_dst, groups):
    pm = np.zeros((len(groups), n_src, n_dst), np.float32)
    for gi, slot in enumerate(groups):
        pm[gi, np.arange(HEAD_DIM), HEAD_DIM * slot + np.arange(HEAD_DIM)] = 1.0
    return jnp.asarray(pm, BF16)


def _global_attention(qk, v_aug, bounded, nseq, S, tok0, tq, tk):
    n_qt = S // tq
    qt0 = tok0 // tq
    s0 = tok0 // S
    place = _lane_place(128, 256, [1, 2, 3])
    grid_spec = pltpu.PrefetchScalarGridSpec(
        num_scalar_prefetch=1,
        grid=(nseq, N_KV_B, n_qt),
        in_specs=[
            pl.BlockSpec((tq, 256), lambda b, h, i, fl: (qt0 + b * n_qt + i, h)),
            pl.BlockSpec((S, 256), lambda b, h, i, fl: (s0 + b, h)),
            pl.BlockSpec((1, S, 128), lambda b, h, i, fl: (h, s0 + b, 0)),
            pl.BlockSpec((GRP_B, 128, 256), lambda b, h, i, fl: (0, 0, 0)),
        ],
        out_specs=pl.BlockSpec((tq, 256), lambda b, h, i, fl: (b * n_qt + i, h)),
    )
    return pl.pallas_call(
        functools.partial(_gattn_kernel, tq=tq, tk=tk, S=S),
        grid_spec=grid_spec,
        out_shape=jax.ShapeDtypeStruct((nseq * S, 1024), BF16),
        compiler_params=_cparams(("parallel", "parallel", "arbitrary")),
        name="global_attention",
    )(bounded, qk, qk, v_aug, place)


def _route(x1, rw_ref, rb_ref):
    n = x1.shape[0]
    xh = x1.astype(BF16)
    xl = (x1 - xh.astype(F32)).astype(BF16)
    o = jnp.dot(jnp.concatenate([xh, xl], axis=0), rw_ref[...], preferred_element_type=F32)
    lg = o[:n, :128] + o[:n, 128:] + o[n:, :128]
    lgt = lg.T[:N_EXPERTS]
    sc = 1.0 / (1.0 + jnp.exp(-lgt))
    bi = sc + rb_ref[...]
    srow = [sc[e:e + 1] for e in range(N_EXPERTS)]
    brow = [bi[e:e + 1] for e in range(N_EXPERTS)]
    gsel = None
    best = None
    for g in range(N_GROUPS):
        a, b, c, d = brow[4 * g:4 * g + 4]
        m1, n1 = jnp.maximum(a, b), jnp.minimum(a, b)
        m2, n2 = jnp.maximum(c, d), jnp.minimum(c, d)
        gs = jnp.maximum(m1, m2) + jnp.maximum(jnp.minimum(m1, m2), jnp.maximum(n1, n2))
        if g == 0:
            best, gsel = gs, jnp.zeros(gs.shape, I32)
        else:
            better = gs > best
            gsel = jnp.where(better, g, gsel)
            best = jnp.where(better, gs, best)
    masked = [jnp.where(gsel == (e // EXPERTS_PER_GROUP), brow[e], NEG_INF) for e in range(N_EXPERTS)]
    i1 = jnp.zeros(gsel.shape, I32)
    b1 = masked[0]
    s1 = srow[0]
    for e in range(1, N_EXPERTS):
        better = masked[e] > b1
        i1 = jnp.where(better, e, i1)
        b1 = jnp.where(better, masked[e], b1)
        s1 = jnp.where(better, srow[e], s1)
    i2 = jnp.full(gsel.shape, -1, I32)
    b2 = jnp.full(b1.shape, -jnp.inf, F32)
    s2 = jnp.zeros(b1.shape, F32)
    for e in range(N_EXPERTS):
        better = (masked[e] > b2) & (i1 != e)
        i2 = jnp.where(better, e, i2)
        b2 = jnp.where(better, masked[e], b2)
        s2 = jnp.where(better, srow[e], s2)
    den = s1 + s2
    return i1, i2, s1 / den, s2 / den


ROW_SPLITS = 2


def _norm_route_store(r0, nr, m, x, g_ref, b_ref, rw_ref, rb_ref, x1_ref, idx_ref, wt_ref, alpha):
    x1 = _layer_norm(alpha * x + m, g_ref[...], b_ref[...])
    x1_ref[r0:r0 + nr, :] = x1
    i1, i2, w1, w2 = _route(x1, rw_ref, rb_ref)
    rid = lax.broadcasted_iota(I32, (8, nr), 0)
    idx_ref[:, r0:r0 + nr] = jnp.where(rid == 0, i1, jnp.where(rid == 1, i2, 0))
    wt_ref[:, r0:r0 + nr] = jnp.where(rid == 0, w1, jnp.where(rid == 1, w2, 0.0))


def _router_operands(router_w, router_b):
    rw = jnp.zeros((D_MODEL, 128), F32).at[:, :N_EXPERTS].set(router_w.astype(F32))
    rwh = rw.astype(BF16)
    rwl = (rw - rwh.astype(F32)).astype(BF16)
    rb = router_b.astype(F32)[:, None]
    return jnp.concatenate([rwh, rwl], axis=1), rb


_ROUTER_SPECS = [
    pl.BlockSpec((D_MODEL, 256), lambda i: (0, 0)),
    pl.BlockSpec((N_EXPERTS, 1), lambda i: (0, 0)),
]


def _mix_out_specs(ts, T):
    specs = [
        pl.BlockSpec((ts, D_MODEL), lambda i: (i, 0)),
        pl.BlockSpec((8, ts), lambda i: (0, i)),
        pl.BlockSpec((8, ts), lambda i: (0, i)),
    ]
    shapes = [
        jax.ShapeDtypeStruct((T, D_MODEL), F32),
        jax.ShapeDtypeStruct((8, T), I32),
        jax.ShapeDtypeStruct((8, T), F32),
    ]
    return specs, shapes


def _outproj_even_kernel(*refs, nf, na, nx, fb, ab, xb, alpha):
    f_refs, a_refs, x_refs = refs[:nf], refs[nf:nf + na], refs[nf + na:nf + na + nx]
    wf_ref, wa_ref, g_ref, b_ref, rw_ref, rb_ref, x1_ref, idx_ref, wt_ref = refs[nf + na + nx:]
    nr = x1_ref.shape[0] // ROW_SPLITS
    for h in range(ROW_SPLITS):
        r0 = h * nr
        m = (jnp.dot(_pick_rows(f_refs, fb, r0, nr), wf_ref[...], preferred_element_type=F32)
             + jnp.dot(_pick_rows(a_refs, ab, r0, nr), wa_ref[...], preferred_element_type=F32))
        xr = _pick_rows(x_refs, xb, r0, nr)
        _norm_route_store(r0, nr, m, xr, g_ref, b_ref, rw_ref, rb_ref, x1_ref, idx_ref, wt_ref, alpha)


def _outproj_even(f_parts, a_parts, x_parts, w_out, ln_g, ln_b, router, alpha, ts):
    T = sum(p.shape[0] for p in x_parts)
    f_specs, fb = _parts_specs(f_parts, ts)
    a_specs, ab = _parts_specs(a_parts, ts)
    x_specs, xb = _parts_specs(x_parts, ts)
    wf = w_out[:D_FOURIER].astype(BF16)
    wa = w_out[D_FOURIER:].reshape(N_KV_B, GRP_B * HEAD_DIM, D_MODEL)
    wa = jnp.concatenate([jnp.zeros((N_KV_B, HEAD_DIM, D_MODEL), w_out.dtype), wa], axis=1)
    wa = wa.reshape(1024, D_MODEL).astype(BF16)
    out_specs, out_shapes = _mix_out_specs(ts, T)
    return pl.pallas_call(
        functools.partial(_outproj_even_kernel, nf=len(f_parts), na=len(a_parts), nx=len(x_parts),
                          fb=fb, ab=ab, xb=xb, alpha=alpha),
        grid=(T // ts,),
        in_specs=f_specs + a_specs + x_specs + [
            pl.BlockSpec((D_FOURIER, D_MODEL), lambda i: (0, 0)),
            pl.BlockSpec((1024, D_MODEL), lambda i: (0, 0)),
            pl.BlockSpec((1, D_MODEL), lambda i: (0, 0)),
            pl.BlockSpec((1, D_MODEL), lambda i: (0, 0)),
        ] + _ROUTER_SPECS,
        out_specs=out_specs,
        out_shape=out_shapes,
        compiler_params=_cparams(("parallel",)),
        name="outproj_even",
    )(*f_parts, *a_parts, *x_parts, wf, wa, ln_g[None, :], ln_b[None, :], *router)


CHUNK = 8


def _moe_plan(idx, td, tm):
    T = idx.shape[1]
    nt = T // td
    e = idx.T.reshape(-1)
    oh = (e[:, None] == jnp.arange(N_EXPERTS, dtype=I32)[None, :]).astype(I32)
    csum = jnp.cumsum(oh, axis=0)
    tile_end = csum.reshape(nt, 2 * td, N_EXPERTS)[:, -1, :]
    base = jnp.concatenate([jnp.zeros((1, N_EXPERTS), I32), tile_end[:-1]], axis=0)
    cnt = tile_end - base
    c8 = ((cnt + CHUNK - 1) // CHUNK) * CHUNK
    off8 = jnp.cumsum(c8, axis=1) - c8
    base8 = jnp.cumsum(c8, axis=0) - c8
    seg = jnp.sum(c8, axis=0)
    padded = ((seg + tm - 1) // tm) * tm
    ends = jnp.cumsum(padded)
    starts = ends - padded
    dst = starts[None, :] + base8
    shift = jnp.repeat(off8 - base, 2 * td, axis=0)
    lpos = (jnp.sum((csum + shift) * oh, axis=1) - 1).astype(I32)
    n_rows = -(-(2 * T + nt * N_EXPERTS * (CHUNK - 1)) // tm) * tm + N_EXPERTS * tm
    tile_start = jnp.arange(n_rows // tm, dtype=I32) * tm
    tile_e = jnp.minimum(jnp.sum((ends[None, :] <= tile_start[:, None]).astype(I32), axis=1), N_EXPERTS - 1)
    n_used = (ends[-1] // tm).astype(I32)[None]
    tab = jnp.concatenate([cnt, dst], axis=1).astype(I32).reshape(nt, 1, 2 * N_EXPERTS)
    pad_lo = jnp.concatenate([starts + seg, ends[-1:]]).astype(I32)
    pad_hi = jnp.concatenate([ends, jnp.full((1,), n_rows, I32)]).astype(I32)
    lpos_tk = lpos.reshape(T, 2)
    lpos_rows = jnp.zeros((8, T), I32).at[:2].set(lpos_tk.T)
    return tab, lpos_rows, lpos_tk, tile_e.astype(I32), n_used, pad_lo, pad_hi, n_rows


def _chunk_loops(tab_ref, fn):
    off = jnp.int32(0)
    total = jnp.int32(0)
    for e in range(N_EXPERTS):
        nc = (tab_ref[0, 0, e] + (CHUNK - 1)) // CHUNK
        dst = tab_ref[0, 0, N_EXPERTS + e]

        def body(j, c, off=off, dst=dst):
            fn(pl.multiple_of(off + CHUNK * j, CHUNK), pl.multiple_of(dst + CHUNK * j, CHUNK))
            return c
        lax.fori_loop(0, nc, body, 0)
        off = off + CHUNK * nc
        total = total + nc
    return total


def _dispatch_kernel(lo_ref, hi_ref, tab_ref, lp_ref, x_ref, xs_hbm, stage, zbuf, nwait, sem, zsem, *, ns):
    i = pl.program_id(0)
    slot = i % 2

    def zero_copy(dst):
        return pltpu.make_async_copy(zbuf, xs_hbm.at[pl.ds(pl.multiple_of(dst, CHUNK), CHUNK)], zsem)

    @pl.when(i == 0)
    def _():
        nwait[0] = 0
        nwait[1] = 0
        zbuf[...] = jnp.zeros(zbuf.shape, F32)
        for e in range(N_EXPERTS + 1):
            nz = (hi_ref[e] - lo_ref[e]) // CHUNK
            lax.fori_loop(0, nz, lambda j, c, e=e: (zero_copy(lo_ref[e] + CHUNK * j).start(), c)[1], 0)
        for e in range(N_EXPERTS + 1):
            nz = (hi_ref[e] - lo_ref[e]) // CHUNK
            lax.fori_loop(0, nz, lambda j, c: (zero_copy(0).wait(), c)[1], 0)

    def piece(local_row, sorted_row, s):
        return pltpu.make_async_copy(stage.at[s, pl.ds(local_row, CHUNK)],
                                     xs_hbm.at[pl.ds(sorted_row, CHUNK)], sem.at[s])

    def drain(s):
        lax.fori_loop(0, nwait[s], lambda j, c: (piece(0, 0, s).wait(), c)[1], 0)

    drain(slot)
    lp = lp_ref[...]
    rows = lax.broadcasted_iota(I32, (ns, lp.shape[1]), 0)
    perm = jnp.where((rows == lp[0:1]) | (rows == lp[1:2]), 1.0, 0.0).astype(BF16)
    stage[slot] = jnp.dot(perm, x_ref[...].astype(BF16), preferred_element_type=F32)
    nwait[slot] = _chunk_loops(tab_ref, lambda lr, sr: piece(lr, sr, slot).start())

    @pl.when(i == pl.num_programs(0) - 1)
    def _():
        drain(0)
        drain(1)


def _moe_dispatch(x1, tab, lpos_rows, pad_lo, pad_hi, n_rows, td):
    T = x1.shape[0]
    ns = 2 * td + 128
    grid_spec = pltpu.PrefetchScalarGridSpec(
        num_scalar_prefetch=2,
        grid=(T // td,),
        in_specs=[
            pl.BlockSpec((1, 1, 2 * N_EXPERTS), lambda i, lo, hi: (i, 0, 0), memory_space=pltpu.SMEM),
            pl.BlockSpec((8, td), lambda i, lo, hi: (0, i)),
            pl.BlockSpec((td, D_MODEL), lambda i, lo, hi: (i, 0)),
        ],
        out_specs=pl.BlockSpec(memory_space=pl.ANY),
        scratch_shapes=[pltpu.VMEM((2, ns, D_MODEL), F32), pltpu.VMEM((CHUNK, D_MODEL), F32),
                        pltpu.SMEM((2,), I32), pltpu.SemaphoreType.DMA((2,)), pltpu.SemaphoreType.DMA(())],
    )
    return pl.pallas_call(
        functools.partial(_dispatch_kernel, ns=ns),
        grid_spec=grid_spec,
        out_shape=jax.ShapeDtypeStruct((n_rows, D_MODEL), F32),
        compiler_params=_cparams(("arbitrary",)),
        name="moe_dispatch",
    )(pad_lo, pad_hi, tab, lpos_rows, x1)


def _moe_kernel(te_ref, nu_ref, x_ref, wg_ref, wu_ref, wd_ref, y_ref, wgb, wub, wdb):
    i = pl.program_id(0)

    @pl.when((i == 0) | (te_ref[i] != te_ref[jnp.maximum(i - 1, 0)]))
    def _():
        wgb[...] = wg_ref[0].astype(BF16)
        wub[...] = wu_ref[0].astype(BF16)
        wdb[...] = wd_ref[0].astype(BF16)

    @pl.when(i < nu_ref[0])
    def _():
        xg = x_ref[...].astype(BF16)
        hg = jnp.dot(xg, wgb[...], preferred_element_type=F32)
        hu = jnp.dot(xg, wub[...], preferred_element_type=F32)
        hdn = (hg / (1.0 + jnp.exp(-hg)) * hu).astype(BF16)
        y_ref[...] = jnp.dot(hdn, wdb[...], preferred_element_type=F32)

    @pl.when(i >= nu_ref[0])
    def _():
        y_ref[...] = jnp.zeros(y_ref.shape, F32)


def _moe_experts(xs, tile_e, n_used, wg, wu, wd, tm):
    n_rows = xs.shape[0]
    grid_spec = pltpu.PrefetchScalarGridSpec(
        num_scalar_prefetch=2,
        grid=(n_rows // tm,),
        in_specs=[
            pl.BlockSpec((tm, D_MODEL), lambda i, te, nu: (i, 0)),
            pl.BlockSpec((1, D_MODEL, D_FF), lambda i, te, nu: (te[i], 0, 0)),
            pl.BlockSpec((1, D_MODEL, D_FF), lambda i, te, nu: (te[i], 0, 0)),
            pl.BlockSpec((1, D_FF, D_MODEL), lambda i, te, nu: (te[i], 0, 0)),
        ],
        out_specs=pl.BlockSpec((tm, D_MODEL), lambda i, te, nu: (i, 0)),
        scratch_shapes=[pltpu.VMEM((D_MODEL, D_FF), BF16), pltpu.VMEM((D_MODEL, D_FF), BF16),
                        pltpu.VMEM((D_FF, D_MODEL), BF16)],
    )
    return pl.pallas_call(
        _moe_kernel,
        grid_spec=grid_spec,
        out_shape=jax.ShapeDtypeStruct((n_rows, D_MODEL), F32),
        compiler_params=_cparams(("arbitrary",)),
        name="moe_experts",
    )(tile_e, n_used, xs, wg, wu, wd)


def _combine_kernel(tab_ref, tabn_ref, lp_ref, y_hbm, x_ref, wt_ref, g_ref, b_ref, *rest, ns, alpha, bounds):
    o_refs, (ystage, sem) = rest[:len(bounds)], rest[len(bounds):]
    i = pl.program_id(0)
    n = pl.num_programs(0)
    slot = i % 2

    def piece(local_row, sorted_row, s):
        return pltpu.make_async_copy(y_hbm.at[pl.ds(sorted_row, CHUNK)],
                                     ystage.at[s, pl.ds(local_row, CHUNK)], sem.at[s])

    @pl.when(i == 0)
    def _():
        ystage[...] = jnp.zeros(ystage.shape, F32)
        _chunk_loops(tab_ref, lambda lr, sr: piece(lr, sr, 0).start())

    @pl.when(i + 1 < n)
    def _():
        _chunk_loops(tabn_ref, lambda lr, sr: piece(lr, sr, 1 - slot).start())

    _chunk_loops(tab_ref, lambda lr, sr: piece(0, 0, slot).wait())

    ys = ystage[slot].astype(BF16)
    lp = lp_ref[...]
    cols = lax.broadcasted_iota(I32, (lp.shape[0], ns), 1)
    w = wt_ref[...]
    tc = lp.shape[0]
    sel = jnp.concatenate([jnp.where(cols == lp[:, k:k + 1], 1.0, 0.0).astype(BF16) for k in range(2)], axis=0)
    picked = jnp.dot(sel, ys, preferred_element_type=F32)
    f = w[:, 0:1] * picked[:tc] + w[:, 1:2] * picked[tc:]
    o = _layer_norm(alpha * x_ref[...] + f, g_ref[...], b_ref[...])
    if len(bounds) == 1:
        o_refs[0][...] = o
    else:
        lo = 0
        for o_ref, hi in zip(o_refs, bounds):
            @pl.when((i >= lo) & (i < hi))
            def _(o_ref=o_ref):
                o_ref[...] = o
            lo = hi


def _moe_combine(y, tab, lpos_tk, wts, x1, ln_g, ln_b, alpha, tc, out_rows):
    T = x1.shape[0]
    n_tiles = T // tc
    ns = 2 * tc + 128
    out_specs, out_shapes, bounds, lo = [], [], [], 0
    for rows in out_rows:
        n = rows // tc
        out_specs.append(pl.BlockSpec((tc, D_MODEL), lambda i, lo=lo, n=n: (jnp.clip(i - lo, 0, n - 1), 0)))
        out_shapes.append(jax.ShapeDtypeStruct((rows, D_MODEL), F32))
        lo += n
        bounds.append(lo)
    return pl.pallas_call(
        functools.partial(_combine_kernel, ns=ns, alpha=alpha, bounds=tuple(bounds)),
        grid=(n_tiles,),
        in_specs=[
            pl.BlockSpec((1, 1, 2 * N_EXPERTS), lambda i: (i, 0, 0), memory_space=pltpu.SMEM),
            pl.BlockSpec((1, 1, 2 * N_EXPERTS), lambda i: (jnp.minimum(i + 1, n_tiles - 1), 0, 0),
                         memory_space=pltpu.SMEM),
            pl.BlockSpec((tc, 2), lambda i: (i, 0)),
            pl.BlockSpec(memory_space=pl.ANY),
            pl.BlockSpec((tc, D_MODEL), lambda i: (i, 0)),
            pl.BlockSpec((tc, 2), lambda i: (i, 0)),
            pl.BlockSpec((1, D_MODEL), lambda i: (0, 0)),
            pl.BlockSpec((1, D_MODEL), lambda i: (0, 0)),
        ],
        out_specs=out_specs,
        out_shape=out_shapes,
        scratch_shapes=[pltpu.VMEM((2, ns, D_MODEL), F32), pltpu.SemaphoreType.DMA((2,))],
        compiler_params=_cparams(("arbitrary",)),
        name="moe_combine",
    )(tab, tab, lpos_tk, y, x1, wts, ln_g[None, :], ln_b[None, :])


def _moe_layer(x1, idx, wt, wg, wu, wd, ln_g, ln_b, alpha, tm, tc, out_rows):
    tab, lpos_rows, lpos_tk, tile_e, n_used, pad_lo, pad_hi, n_rows = _moe_plan(idx[:2], tc, tm)
    xs = _moe_dispatch(x1, tab, lpos_rows, pad_lo, pad_hi, n_rows, tc)
    y = _moe_experts(xs, tile_e, n_used, wg, wu, wd, tm)
    return _moe_combine(y, tab, lpos_tk, wt[:2].T, x1, ln_g, ln_b, alpha, tc, out_rows)


def _inproj_odd_kernel(x_ref, w_ref, vone_ref, bg_ref, u_ref, q_ref, k_ref, v_ref, nrm_ref):
    x = x_ref[...].astype(BF16)
    proj = jnp.dot(x, w_ref[...], preferred_element_type=F32)
    bg_ref[...] = proj[:, :512].astype(BF16)
    u_ref[...] = (proj[:, 512:1024] * proj[:, 1024:1536]).astype(BF16)
    qb = (proj[:, 1536:2048] * (HEAD_DIM ** -0.5 * LOG2E)).astype(BF16)
    kb = proj[:, 2048:2304].astype(BF16)
    q_ref[...] = qb
    k_ref[...] = kb
    v_ref[...] = (proj[:, 2304:2560] + vone_ref[...]).astype(BF16)
    qf = qb.astype(F32)
    kf = kb.astype(F32)
    qq = jnp.max(jnp.sum(qf * qf, axis=1, keepdims=True), axis=0, keepdims=True)
    kk = jnp.max(jnp.sum(kf * kf, axis=1, keepdims=True), axis=0, keepdims=True)
    rid = lax.broadcasted_iota(I32, (8, 128), 0)
    nrm_ref[0] = jnp.where(rid == 0, qq, jnp.where(rid == 1, kk, 0.0))


def _inproj_odd(x, w_in, ts):
    T = x.shape[0]
    c3 = 3 * D_CONV
    wk = w_in[:, c3 + 512:c3 + 640].reshape(D_MODEL, N_KV_D, HEAD_DIM)
    wv = w_in[:, c3 + 640:c3 + 768].reshape(D_MODEL, N_KV_D, HEAD_DIM)
    wkp = jnp.concatenate([wk, jnp.zeros_like(wk)], axis=2).reshape(D_MODEL, 256)
    wvp = jnp.concatenate([wv, jnp.zeros_like(wv)], axis=2).reshape(D_MODEL, 256)
    w = jnp.concatenate([w_in[:, :c3 + 512], wkp, wvp], axis=1).astype(BF16)
    vone_np = np.zeros((1, 256), np.float32)
    for h in range(N_KV_D):
        vone_np[0, 128 * h + 64:128 * (h + 1)] = 1.0
    n_w = w.shape[1]
    widths = [512, 512, 512, 256, 256]
    bg, u, q, kw, vw, nrm = pl.pallas_call(
        _inproj_odd_kernel,
        grid=(T // ts,),
        in_specs=[
            pl.BlockSpec((ts, D_MODEL), lambda i: (i, 0)),
            pl.BlockSpec((D_MODEL, n_w), lambda i: (0, 0)),
            pl.BlockSpec((1, 256), lambda i: (0, 0)),
        ],
        out_specs=[pl.BlockSpec((ts, n), lambda i: (i, 0)) for n in widths]
        + [pl.BlockSpec((1, 8, 128), lambda i: (i, 0, 0))],
        out_shape=[jax.ShapeDtypeStruct((T, n), BF16) for n in widths]
        + [jax.ShapeDtypeStruct((T // ts, 8, 128), F32)],
        compiler_params=_cparams(("parallel",)),
        name="inproj_odd",
    )(x, w, jnp.asarray(vone_np))
    score_bound = jnp.sqrt(jnp.max(nrm[:, 0, 0]) * jnp.max(nrm[:, 1, 0]))
    return bg, u, q, kw, vw, score_bound


def _wattn_kernel(bounded_ref, q_ref, kp_ref, kc_ref, kn_ref, vp_ref, vc_ref, vn_ref, bias_ref, sink_ref,
                  place_ref, o_ref, *, tq, seq_tiles):
    i = pl.program_id(0)
    first = i < 0
    last = i < 0
    for lo, hi, per in seq_tiles:
        inside = (i >= lo) & (i < hi)
        first = first | (inside & ((i - lo) % per == 0))
        last = last | (inside & ((i - lo) % per == per - 1))
    kfull = jnp.concatenate([kp_ref[...], kc_ref[...], kn_ref[...]], axis=0)[:, :HEAD_DIM]
    vfull = jnp.concatenate([vp_ref[...], vc_ref[...], vn_ref[...]], axis=0)
    q4 = q_ref[...]
    bias = bias_ref[0]
    sink = sink_ref[0]
    nb = tq // Q_BLOCK
    col = lax.broadcasted_iota(I32, (GRP_D * Q_BLOCK, 3 * Q_BLOCK), 1)

    def blocks(with_max):
        for n in range(nb):
            qs = jnp.concatenate([q4[Q_BLOCK * n:Q_BLOCK * (n + 1), HEAD_DIM * g:HEAD_DIM * (g + 1)]
                                  for g in range(GRP_D)], axis=0)
            keys = kfull[Q_BLOCK * n:Q_BLOCK * (n + 3)]
            vals = vfull[Q_BLOCK * n:Q_BLOCK * (n + 3)]
            s = lax.dot_general(qs, keys, (((1,), (1,)), ((), ())), preferred_element_type=F32) + bias
            if n == 0:
                s = jnp.where(first & (col < Q_BLOCK), NEG_INF, s)
            if n == nb - 1:
                s = jnp.where(last & (col >= 2 * Q_BLOCK), NEG_INF, s)
            if with_max:
                m = jnp.maximum(jnp.max(s, axis=1, keepdims=True), sink)
                s = s - m
                snk = sink - m
            else:
                snk = sink
            acc = jnp.dot(jnp.exp2(s).astype(BF16), vals, preferred_element_type=F32)
            den = acc[:, HEAD_DIM:HEAD_DIM + 1] + jnp.exp2(snk)
            o = (acc / den).astype(BF16)
            out = jnp.zeros((Q_BLOCK, 256), F32)
            for g in range(GRP_D):
                out = out + jnp.dot(o[Q_BLOCK * g:Q_BLOCK * (g + 1)], place_ref[g],
                                    preferred_element_type=F32)
            o_ref[Q_BLOCK * n:Q_BLOCK * (n + 1), :] = out.astype(BF16)

    @pl.when(bounded_ref[0] == 1)
    def _():
        blocks(False)

    @pl.when(bounded_ref[0] != 1)
    def _():
        blocks(True)


WINDOW_BOUND_MAX = 90.0


def _window_attention(q, kw, vw, sink_logits, score_bound, classes, tq):
    T = q.shape[0]
    sink_bound = jnp.max(jnp.abs(sink_logits.astype(F32))) * LOG2E
    bounded = ((1.02 * score_bound <= WINDOW_BOUND_MAX) & (sink_bound <= WINDOW_BOUND_MAX)).astype(I32)[None]
    n_tiles = T // tq
    hb = tq // Q_BLOCK
    n_hblk = T // Q_BLOCK
    r = jnp.arange(Q_BLOCK, dtype=I32)[:, None]
    j = jnp.arange(3 * Q_BLOCK, dtype=I32)[None, :]
    rel = jnp.abs(j - Q_BLOCK - r).astype(F32)
    slopes = jnp.asarray(np.array([2.0 ** (-8.0 * (h + 1) / N_HEADS_D) for h in range(N_HEADS_D)], np.float32))
    bias = jnp.where(rel[None] <= WINDOW, -slopes[:, None, None] * rel[None] * LOG2E, NEG_INF)
    bias = bias.reshape(N_KV_D, GRP_D * Q_BLOCK, 3 * Q_BLOCK)
    sink = jnp.repeat(sink_logits.astype(F32) * LOG2E, Q_BLOCK).reshape(N_KV_D, GRP_D * Q_BLOCK, 1)
    place = _lane_place(128, 256, [0, 1, 2, 3])
    seq_tiles = []
    t0 = 0
    for nseq, S, _ in classes:
        cnt = nseq * S // tq
        seq_tiles.append((t0, t0 + cnt, S // tq))
        t0 += cnt
    prev_map = lambda i, h, fl: (jnp.maximum(i * hb - 1, 0), h)
    next_map = lambda i, h, fl: (jnp.minimum((i + 1) * hb, n_hblk - 1), h)
    grid_spec = pltpu.PrefetchScalarGridSpec(
        num_scalar_prefetch=1,
        grid=(n_tiles, N_KV_D),
        in_specs=[
            pl.BlockSpec((tq, 256), lambda i, h, fl: (i, h)),
            pl.BlockSpec((Q_BLOCK, 128), prev_map),
            pl.BlockSpec((tq, 128), lambda i, h, fl: (i, h)),
            pl.BlockSpec((Q_BLOCK, 128), next_map),
            pl.BlockSpec((Q_BLOCK, 128), prev_map),
            pl.BlockSpec((tq, 128), lambda i, h, fl: (i, h)),
            pl.BlockSpec((Q_BLOCK, 128), next_map),
            pl.BlockSpec((1, GRP_D * Q_BLOCK, 3 * Q_BLOCK), lambda i, h, fl: (h, 0, 0)),
            pl.BlockSpec((1, GRP_D * Q_BLOCK, 1), lambda i, h, fl: (h, 0, 0)),
            pl.BlockSpec((GRP_D, 128, 256), lambda i, h, fl: (0, 0, 0)),
        ],
        out_specs=pl.BlockSpec((tq, 256), lambda i, h, fl: (i, h)),
    )
    return pl.pallas_call(
        functools.partial(_wattn_kernel, tq=tq, seq_tiles=tuple(seq_tiles)),
        grid_spec=grid_spec,
        out_shape=jax.ShapeDtypeStruct((T, 512), BF16),
        compiler_params=_cparams(("parallel", "parallel")),
        name="window_attention",
    )(bounded, q, kw, kw, kw, vw, vw, vw, bias, sink, place)


HALO = 16


def _outproj_odd_kernel(bg_ref, u_ref, up_ref, un_ref, a_ref, x_ref, cw_ref, cb_ref, wc_ref, wa_ref,
                        g_ref, b_ref, rw_ref, rb_ref, x1_ref, idx_ref, wt_ref,
                        *, alpha, ts, seq_tiles):
    i = pl.program_id(0)
    first = i < 0
    last = i < 0
    for lo, hi, per in seq_tiles:
        inside = (i >= lo) & (i < hi)
        first = first | (inside & ((i - lo) % per == 0))
        last = last | (inside & ((i - lo) % per == per - 1))
    u = u_ref[...].astype(F32)
    prev_row = jnp.where(first, 0.0, up_ref[HALO - 1:HALO, :].astype(F32))
    next_row = jnp.where(last, 0.0, un_ref[0:1, :].astype(F32))
    rid = lax.broadcasted_iota(I32, u.shape, 0)
    ud = jnp.where(rid == 0, prev_row, pltpu.roll(u, 1, 0))
    uu = jnp.where(rid == ts - 1, next_row, pltpu.roll(u, ts - 1, 0))
    cw = cw_ref[...]
    y = ud * cw[0:1] + u * cw[1:2] + uu * cw[2:3] + cb_ref[...]
    c = (bg_ref[...].astype(F32) * y).astype(BF16)
    nr = ts // ROW_SPLITS
    for h in range(ROW_SPLITS):
        r0 = h * nr
        m = (jnp.dot(c[r0:r0 + nr], wc_ref[...], preferred_element_type=F32)
             + jnp.dot(a_ref[r0:r0 + nr, :], wa_ref[...], preferred_element_type=F32))
        _norm_route_store(r0, nr, m, x_ref[r0:r0 + nr, :], g_ref, b_ref, rw_ref, rb_ref, x1_ref, idx_ref,
                          wt_ref, alpha)


def _outproj_odd(bg, u, a, x, conv_w, conv_b, w_out, ln_g, ln_b, router, alpha, classes, ts):
    T = x.shape[0]
    hb = ts // HALO
    n_h = T // HALO
    seq_tiles = []
    t0 = 0
    for nseq, S, _ in classes:
        cnt = nseq * S // ts
        seq_tiles.append((t0, t0 + cnt, S // ts))
        t0 += cnt
    out_specs, out_shapes = _mix_out_specs(ts, T)
    return pl.pallas_call(
        functools.partial(_outproj_odd_kernel, alpha=alpha, ts=ts, seq_tiles=tuple(seq_tiles)),
        grid=(T // ts,),
        in_specs=[
            pl.BlockSpec((ts, D_CONV), lambda i: (i, 0)),
            pl.BlockSpec((ts, D_CONV), lambda i: (i, 0)),
            pl.BlockSpec((HALO, D_CONV), lambda i: (jnp.maximum(i * hb - 1, 0), 0)),
            pl.BlockSpec((HALO, D_CONV), lambda i: (jnp.minimum((i + 1) * hb, n_h - 1), 0)),
            pl.BlockSpec((ts, 512), lambda i: (i, 0)),
            pl.BlockSpec((ts, D_MODEL), lambda i: (i, 0)),
            pl.BlockSpec((3, D_CONV), lambda i: (0, 0)),
            pl.BlockSpec((1, D_CONV), lambda i: (0, 0)),
            pl.BlockSpec((D_CONV, D_MODEL), lambda i: (0, 0)),
            pl.BlockSpec((512, D_MODEL), lambda i: (0, 0)),
            pl.BlockSpec((1, D_MODEL), lambda i: (0, 0)),
            pl.BlockSpec((1, D_MODEL), lambda i: (0, 0)),
        ] + _ROUTER_SPECS,
        out_specs=out_specs,
        out_shape=out_shapes,
        compiler_params=_cparams(("parallel",)),
        name="outproj_odd",
    )(bg, u, u, u, a, x, conv_w.astype(F32), conv_b.astype(F32)[None, :],
      w_out[:D_CONV].astype(BF16), w_out[D_CONV:].astype(BF16), ln_g[None, :], ln_b[None, :], *router)


def _tile(n, cap):
    t = cap
    while n % t:
        t //= 2
    return t


def kernel(x_prompt, x_sample, w_in_even, fourier_norm_g, q_norm_g, k_norm_g, w_out_even, w_in_odd, conv_w,
           conv_b, sink_logits, w_out_odd, ln_mix_g, ln_mix_b, ln_ffn_g, ln_ffn_b, router_w, router_b,
           w_gate, w_up, w_down):
    depth = ln_mix_g.shape[0]
    alpha = float((2 * depth) ** 0.25)
    bp, sp, _ = x_prompt.shape
    bs, ss, _ = x_sample.shape
    classes = ((bp, sp, 0), (bs, ss, bp * sp))
    T = bp * sp + bs * ss
    min_s = min(sp, ss)
    ts = _tile(min_s, 512)
    tq_w = _tile(min_s, 512)
    tm = 512
    tc = _tile(min_s, 256)
    x_parts = [x_prompt.reshape(bp * sp, D_MODEL), x_sample.reshape(bs * ss, D_MODEL)]
    router = _router_operands(router_w, router_b)
    for l in range(depth):
        i = l // 2
        if l % 2 == 0:
            uf, qk, v_aug = _inproj_even(x_parts, w_in_even[i], fourier_norm_g[i], q_norm_g[i], k_norm_g[i],
                                         classes, ts)
            bounded = _score_bounded(q_norm_g[i], k_norm_g[i])
            f_parts, a_parts = [], []
            for nseq, S, tok0 in classes:
                f_parts.append(_fourier_mix(uf[tok0:tok0 + nseq * S], nseq, S))
                a_parts.append(_global_attention(qk, v_aug, bounded, nseq, S, tok0, _tile(S, 512),
                                                 _tile(S, 2048)))
            x1, idx, wt = _outproj_even(f_parts, a_parts, x_parts, w_out_even[i], ln_mix_g[l], ln_mix_b[l],
                                        router, alpha, ts)
        else:
            x = x_parts[0]
            bg, u, q, kw, vw, score_bound = _inproj_odd(x, w_in_odd[i], ts)
            a = _window_attention(q, kw, vw, sink_logits[i], score_bound, classes, tq_w)
            x1, idx, wt = _outproj_odd(bg, u, a, x, conv_w[i], conv_b[i], w_out_odd[i], ln_mix_g[l],
                                       ln_mix_b[l], router, alpha, classes, ts)
        out_rows = [bp * sp, bs * ss] if l == depth - 1 else [T]
        x_parts = _moe_layer(x1, idx, wt, w_gate[l], w_up[l], w_down[l], ln_ffn_g[l], ln_ffn_b[l], alpha, tm,
                             tc, out_rows)
    if len(x_parts) == 1:
        x_parts = [x_parts[0][:bp * sp], x_parts[0][bp * sp:]]
    return (x_parts[0].reshape(bp, sp, D_MODEL), x_parts[1].reshape(bs, ss, D_MODEL))
```

```python
import functools
import math

import numpy as np
import jax
import jax.numpy as jnp
from jax import lax
from jax.experimental import pallas as pl
from jax.experimental.pallas import tpu as pltpu

F32 = jnp.float32
BF16 = jnp.bfloat16
I32 = jnp.int32

D_MODEL = 1024
HEAD_DIM = 64
GRID_W = 64
Q_BLOCK = 128
WINDOW = 128
ROPE_THETA = 10000.0
N_FOURIER_GROUPS = 4
D_FOURIER = 256
N_HEADS_B = 12
N_KV_B = 4
GRP_B = 3
D_CONV = 512
N_HEADS_D = 8
N_KV_D = 2
GRP_D = 4
N_EXPERTS = 16
N_GROUPS = 4
EXPERTS_PER_GROUP = 4
D_FF = 512
LN_EPS = 1e-5
RMS_EPS = 1e-6
NEG_INF = -1e30
LOG2E = 1.4426950408889634
FFT_S2 = 128
V7X_VMEM_LIMIT = 48 * 1024 * 1024


def _cparams(sem):
    return pltpu.CompilerParams(dimension_semantics=sem, vmem_limit_bytes=V7X_VMEM_LIMIT)


def _layer_norm(y, g, b):
    mu = jnp.mean(y, axis=-1, keepdims=True)
    yc = y - mu
    var = jnp.mean(yc * yc, axis=-1, keepdims=True)
    return yc * lax.rsqrt(var + LN_EPS) * g + b


N_NORM_COLS = D_FOURIER + 1024


def _parts_specs(parts, ts):
    specs, bounds, lo = [], [], 0
    for p in parts:
        n = p.shape[0] // ts
        specs.append(pl.BlockSpec((ts, p.shape[1]), lambda i, lo=lo, n=n: (jnp.clip(i - lo, 0, n - 1), 0)))
        lo += n
        bounds.append(lo)
    return specs, tuple(bounds)


def _pick_rows(refs, bounds, r0, nr):
    i = pl.program_id(0)
    out = refs[-1][r0:r0 + nr, :]
    for ref, hi in reversed(list(zip(refs[:-1], bounds[:-1]))):
        out = jnp.where(i < hi, ref[r0:r0 + nr, :], out)
    return out


def _inproj_even_kernel(*refs, nx, bounds):
    x_refs = refs[:nx]
    (w_ref, gain_ref, seg_ref, segt_ref, cos_ref, sin_ref, vone_ref, uf_ref, qk_ref, v_ref) = refs[nx:]
    nr = uf_ref.shape[0] // ROW_SPLITS
    for part in range(ROW_SPLITS):
        r0 = part * nr
        x = _pick_rows(x_refs, bounds, r0, nr).astype(BF16)
        proj = jnp.dot(x, w_ref[...], preferred_element_type=F32)
        nrm = proj[:, :N_NORM_COLS]
        sq = (nrm * nrm).astype(BF16)
        ssum = jnp.dot(sq, seg_ref[...], preferred_element_type=F32)
        r = lax.rsqrt(ssum * (1.0 / HEAD_DIM) + RMS_EPS)
        rh = r.astype(BF16)
        rl = (r - rh.astype(F32)).astype(BF16)
        rex = jnp.dot(jnp.concatenate([rh, rl], axis=1), segt_ref[...], preferred_element_type=F32)
        y = nrm * rex * gain_ref[...]
        uf_ref[r0:r0 + nr, :] = y[:, :D_FOURIER].astype(BF16)
        yq = y[:, D_FOURIER:]
        c = jnp.concatenate([cos_ref[r0:r0 + nr, :]] * 8, axis=1)
        s = jnp.concatenate([sin_ref[r0:r0 + nr, :]] * 8, axis=1)
        lane = lax.broadcasted_iota(I32, yq.shape, 1)
        first = (lane & 31) < 16
        sw = jnp.where(first, pltpu.roll(yq, 1024 - 16, 1), pltpu.roll(yq, 16, 1))
        qk_ref[r0:r0 + nr, :] = (yq * c + sw * s).astype(BF16)
        vv = proj[:, N_NORM_COLS:] + vone_ref[...]
        for h in range(N_KV_B):
            v_ref[h, r0:r0 + nr, :] = vv[:, 128 * h:128 * (h + 1)].astype(BF16)


def _rope_tables(n_pos):
    t = jnp.arange(n_pos)
    row = (t // GRID_W).astype(F32)
    col = (t % GRID_W).astype(F32)
    n_freq = HEAD_DIM // 4
    inv_freq = ROPE_THETA ** (-jnp.arange(n_freq, dtype=F32) / n_freq)
    ar = row[:, None] * inv_freq
    ac = col[:, None] * inv_freq
    ang = jnp.concatenate([ar, ar, ac, ac], axis=1)
    sign = jnp.asarray(np.tile(np.repeat(np.array([-1.0, 1.0], np.float32), 16), 2))
    cos = jnp.cos(ang)
    sin = jnp.sin(ang) * sign
    return jnp.concatenate([cos, cos], axis=1), jnp.concatenate([sin, sin], axis=1)


def _pos_block_map(classes, ts):
    bounds = []
    tile0 = 0
    for nseq, S, _ in classes:
        n_tiles = nseq * S // ts
        bounds.append((tile0, tile0 + n_tiles, S // ts))
        tile0 += n_tiles

    def fn(i):
        out = (i - bounds[-1][0]) % bounds[-1][2]
        for lo, hi, per in reversed(bounds[:-1]):
            out = jnp.where(i < hi, (i - lo) % per, out)
        return out

    return fn


def _inproj_even(x_parts, w_in, f_g, q_g, k_g, classes, ts):
    T = sum(p.shape[0] for p in x_parts)
    x_specs, bounds = _parts_specs(x_parts, ts)
    wf = w_in[:, :D_FOURIER]
    wq = w_in[:, D_FOURIER:D_FOURIER + 768].reshape(D_MODEL, N_KV_B, GRP_B * HEAD_DIM)
    wk = w_in[:, D_FOURIER + 768:D_FOURIER + 1024].reshape(D_MODEL, N_KV_B, HEAD_DIM)
    wv = w_in[:, D_FOURIER + 1024:].reshape(D_MODEL, N_KV_B, HEAD_DIM)
    wqk = jnp.concatenate([wk, wq], axis=2).reshape(D_MODEL, 1024)
    wvp = jnp.concatenate([wv, jnp.zeros_like(wv)], axis=2).reshape(D_MODEL, 512)
    w = jnp.concatenate([wf, wqk, wvp], axis=1).astype(BF16)
    qscale = HEAD_DIM ** -0.5 * LOG2E
    gqk = jnp.tile(jnp.concatenate([k_g, q_g * qscale, q_g * qscale, q_g * qscale]), N_KV_B)
    gain = jnp.concatenate([f_g, gqk])[None, :].astype(F32)
    seg_np = np.zeros((N_NORM_COLS, 128), np.float32)
    seg_np[np.arange(N_NORM_COLS), np.arange(N_NORM_COLS) // HEAD_DIM] = 1.0
    seg = jnp.asarray(seg_np, BF16)
    segt = jnp.asarray(np.concatenate([seg_np.T, seg_np.T], axis=0), BF16)
    max_s = max(S for _, S, _ in classes)
    cos, sin = _rope_tables(max_s)
    vone_np = np.zeros((1, 512), np.float32)
    for h in range(N_KV_B):
        vone_np[0, 128 * h + 64:128 * (h + 1)] = 1.0
    vone = jnp.asarray(vone_np)
    posmap = _pos_block_map(classes, ts)
    n_w = w.shape[1]
    return pl.pallas_call(
        functools.partial(_inproj_even_kernel, nx=len(x_parts), bounds=bounds),
        grid=(T // ts,),
        in_specs=x_specs + [
            pl.BlockSpec((D_MODEL, n_w), lambda i: (0, 0)),
            pl.BlockSpec((1, N_NORM_COLS), lambda i: (0, 0)),
            pl.BlockSpec((N_NORM_COLS, 128), lambda i: (0, 0)),
            pl.BlockSpec((256, N_NORM_COLS), lambda i: (0, 0)),
            pl.BlockSpec((ts, 128), lambda i: (posmap(i), 0)),
            pl.BlockSpec((ts, 128), lambda i: (posmap(i), 0)),
            pl.BlockSpec((1, 512), lambda i: (0, 0)),
        ],
        out_specs=[
            pl.BlockSpec((ts, D_FOURIER), lambda i: (i, 0)),
            pl.BlockSpec((ts, 1024), lambda i: (i, 0)),
            pl.BlockSpec((N_KV_B, ts, 128), lambda i: (0, i, 0)),
        ],
        out_shape=[
            jax.ShapeDtypeStruct((T, D_FOURIER), BF16),
            jax.ShapeDtypeStruct((T, 1024), BF16),
            jax.ShapeDtypeStruct((N_KV_B, T, 128), BF16),
        ],
        compiler_params=_cparams(("parallel",)),
        name="inproj_even",
    )(*x_parts, w, gain, seg, segt, cos, sin, vone)


def _fft1_kernel(a_ref, w1_ref, twr_ref, twi_ref, zr_ref, zi_ref, *, s1):
    z = jnp.dot(w1_ref[...], a_ref[0], preferred_element_type=F32)
    zr = z[:s1]
    zi = z[s1:]
    twr = twr_ref[...]
    twi = twi_ref[...]
    zr_ref[0] = (zr * twr - zi * twi).astype(BF16)
    zi_ref[0] = (zr * twi + zi * twr).astype(BF16)


def _fft2_kernel(zr_ref, zi_ref, w2a_ref, w2b_ref, mix_ref, o_ref, *, cb):
    s2 = FFT_S2
    for c in range(cb):
        pp = (jnp.dot(w2a_ref[...], zr_ref[0, c], preferred_element_type=F32)
              + jnp.dot(w2b_ref[...], zi_ref[0, c], preferred_element_type=F32))
        f = (jnp.dot(pp[:s2].astype(BF16), mix_ref[:D_FOURIER], preferred_element_type=F32)
             + jnp.dot(pp[s2:].astype(BF16), mix_ref[D_FOURIER:], preferred_element_type=F32))
        o_ref[0, :, D_FOURIER * c:D_FOURIER * (c + 1)] = f.astype(BF16)


def _dft_mats(n):
    k = jnp.arange(n, dtype=I32)
    ang = (2.0 * math.pi / n) * ((k[:, None] * k[None, :]) % n).astype(F32)
    return jnp.cos(ang), jnp.sin(ang)


def _fourier_mix(uf_part, nseq, S):
    s2 = FFT_S2
    s1 = S // s2
    ncol = s2 * D_FOURIER
    a = uf_part.reshape(nseq, s1, ncol)
    c1, sn1 = _dft_mats(s1)
    w1 = jnp.concatenate([c1, -sn1], axis=0).astype(BF16)
    cc = jnp.arange(s1, dtype=I32)[:, None]
    bb = jnp.arange(s2, dtype=I32)[None, :]
    ang = (2.0 * math.pi / S) * ((cc * bb) % S).astype(F32)
    twr = jnp.repeat(jnp.cos(ang), D_FOURIER, axis=1)
    twi = jnp.repeat(-jnp.sin(ang), D_FOURIER, axis=1)
    tn = min(ncol, 4096)
    zr, zi = pl.pallas_call(
        functools.partial(_fft1_kernel, s1=s1),
        grid=(ncol // tn, nseq),
        in_specs=[
            pl.BlockSpec((1, s1, tn), lambda j, b: (b, 0, j)),
            pl.BlockSpec((2 * s1, s1), lambda j, b: (0, 0)),
            pl.BlockSpec((s1, tn), lambda j, b: (0, j)),
            pl.BlockSpec((s1, tn), lambda j, b: (0, j)),
        ],
        out_specs=[
            pl.BlockSpec((1, s1, tn), lambda j, b: (b, 0, j)),
            pl.BlockSpec((1, s1, tn), lambda j, b: (b, 0, j)),
        ],
        out_shape=[jax.ShapeDtypeStruct((nseq, s1, ncol), BF16)] * 2,
        compiler_params=_cparams(("parallel", "parallel")),
        name="fourier_stage1",
    )(a, w1, twr, twi)
    zr = zr.reshape(nseq, s1, s2, D_FOURIER)
    zi = zi.reshape(nseq, s1, s2, D_FOURIER)
    c2, sn2 = _dft_mats(s2)
    w2a = jnp.concatenate([c2, -sn2], axis=0).astype(BF16)
    w2b = jnp.concatenate([sn2, c2], axis=0).astype(BF16)
    gc, gs = _dft_mats(HEAD_DIM)
    scale = 1.0 / math.sqrt(S * HEAD_DIM)
    eye = jnp.eye(N_FOURIER_GROUPS, dtype=F32)
    mix = (jnp.concatenate([jnp.kron(eye, gc), jnp.kron(eye, gs)], axis=0) * scale).astype(BF16)
    cb = min(8, s1)
    out = pl.pallas_call(
        functools.partial(_fft2_kernel, cb=cb),
        grid=(nseq, s1 // cb),
        in_specs=[
            pl.BlockSpec((1, cb, s2, D_FOURIER), lambda b, j: (b, j, 0, 0)),
            pl.BlockSpec((1, cb, s2, D_FOURIER), lambda b, j: (b, j, 0, 0)),
            pl.BlockSpec((2 * s2, s2), lambda b, j: (0, 0)),
            pl.BlockSpec((2 * s2, s2), lambda b, j: (0, 0)),
            pl.BlockSpec((2 * D_FOURIER, D_FOURIER), lambda b, j: (0, 0)),
        ],
        out_specs=pl.BlockSpec((1, s2, cb * D_FOURIER), lambda b, j: (b, 0, j)),
        out_shape=jax.ShapeDtypeStruct((nseq, s2, s1 * D_FOURIER), BF16),
        compiler_params=_cparams(("parallel", "parallel")),
        name="fourier_stage2",
    )(zr, zi, w2a, w2b, mix)
    return out.reshape(nseq * S, D_FOURIER)


SCORE_BOUND_MAX = 100.0


def _gattn_kernel(bounded_ref, q_ref, k_ref, v_ref, place_ref, o_ref, *, tq, tk, S):
    q3 = q_ref[...]
    qs = [q3[:, HEAD_DIM * (g + 1):HEAD_DIM * (g + 2)] for g in range(GRP_B)]

    def chunk(j):
        off = pl.multiple_of(j * tk, tk)
        return k_ref[pl.ds(off, tk), 0:HEAD_DIM], v_ref[0, pl.ds(off, tk), :]

    def scores(g, kc):
        return lax.dot_general(qs[g], kc, (((1,), (1,)), ((), ())), preferred_element_type=F32)

    def finish(accs):
        out = jnp.zeros((tq, 256), F32)
        for g in range(GRP_B):
            o = (accs[g] / accs[g][:, HEAD_DIM:HEAD_DIM + 1]).astype(BF16)
            out = out + jnp.dot(o, place_ref[g], preferred_element_type=F32)
        o_ref[...] = out.astype(BF16)

    @pl.when(bounded_ref[0] == 1)
    def _():
        def body(j, accs):
            kc, vc = chunk(j)
            return tuple(accs[g] + jnp.dot(jnp.exp2(scores(g, kc)).astype(BF16), vc,
                                           preferred_element_type=F32) for g in range(GRP_B))
        finish(lax.fori_loop(0, S // tk, body, tuple(jnp.zeros((tq, 128), F32) for _ in range(GRP_B))))

    @pl.when(bounded_ref[0] != 1)
    def _():
        def body(j, carry):
            kc, vc = chunk(j)
            new = []
            for g in range(GRP_B):
                m_prev, acc = carry[g]
                s = scores(g, kc)
                m_new = jnp.maximum(m_prev, jnp.max(s, axis=1, keepdims=True))
                p = jnp.exp2(s - m_new).astype(BF16)
                acc = jnp.exp2(m_prev - m_new) * acc + jnp.dot(p, vc, preferred_element_type=F32)
                new.append((m_new, acc))
            return tuple(new)
        init = tuple((jnp.full((tq, 1), NEG_INF, F32), jnp.zeros((tq, 128), F32)) for _ in range(GRP_B))
        fin = lax.fori_loop(0, S // tk, body, init)
        finish([fin[g][1] for g in range(GRP_B)])


def _score_bounded(q_g, k_g):
    bound = 1.05 * HEAD_DIM * (HEAD_DIM ** -0.5 * LOG2E) * jnp.max(jnp.abs(q_g)) * jnp.max(jnp.abs(k_g))
    return (bound <= SCORE_BOUND_MAX).astype(I32)[None]


def _lane_place(n_src, n_dst, groups):
    pm = np.zeros((len(groups), n_src, n_dst), np.float32)
    for gi, slot in enumerate(groups):
        pm[gi, np.arange(HEAD_DIM), HEAD_DIM * slot + np.arange(HEAD_DIM)] = 1.0
    return jnp.asarray(pm, BF16)


def _global_attention(qk, v_aug, bounded, nseq, S, tok0, tq, tk):
    n_qt = S // tq
    qt0 = tok0 // tq
    s0 = tok0 // S
    place = _lane_place(128, 256, [1, 2, 3])
    grid_spec = pltpu.PrefetchScalarGridSpec(
        num_scalar_prefetch=1,
        grid=(nseq, N_KV_B, n_qt),
        in_specs=[
            pl.BlockSpec((tq, 256), lambda b, h, i, fl: (qt0 + b * n_qt + i, h)),
            pl.BlockSpec((S, 256), lambda b, h, i, fl: (s0 + b, h)),
            pl.BlockSpec((1, S, 128), lambda b, h, i, fl: (h, s0 + b, 0)),
            pl.BlockSpec((GRP_B, 128, 256), lambda b, h, i, fl: (0, 0, 0)),
        ],
        out_specs=pl.BlockSpec((tq, 256), lambda b, h, i, fl: (b * n_qt + i, h)),
    )
    return pl.pallas_call(
        functools.partial(_gattn_kernel, tq=tq, tk=tk, S=S),
        grid_spec=grid_spec,
        out_shape=jax.ShapeDtypeStruct((nseq * S, 1024), BF16),
        compiler_params=_cparams(("parallel", "parallel", "arbitrary")),
        name="global_attention",
    )(bounded, qk, qk, v_aug, place)


def _route(x1, rw_ref, rb_ref):
    n = x1.shape[0]
    xh = x1.astype(BF16)
    xl = (x1 - xh.astype(F32)).astype(BF16)
    o = jnp.dot(jnp.concatenate([xh, xl], axis=0), rw_ref[...], preferred_element_type=F32)
    lg = o[:n, :128] + o[:n, 128:] + o[n:, :128]
    lgt = lg.T[:N_EXPERTS]
    sc = 1.0 / (1.0 + jnp.exp(-lgt))
    bi = sc + rb_ref[...]
    srow = [sc[e:e + 1] for e in range(N_EXPERTS)]
    brow = [bi[e:e + 1] for e in range(N_EXPERTS)]
    gsel = None
    best = None
    for g in range(N_GROUPS):
        a, b, c, d = brow[4 * g:4 * g + 4]
        m1, n1 = jnp.maximum(a, b), jnp.minimum(a, b)
        m2, n2 = jnp.maximum(c, d), jnp.minimum(c, d)
        gs = jnp.maximum(m1, m2) + jnp.maximum(jnp.minimum(m1, m2), jnp.maximum(n1, n2))
        if g == 0:
            best, gsel = gs, jnp.zeros(gs.shape, I32)
        else:
            better = gs > best
            gsel = jnp.where(better, g, gsel)
            best = jnp.where(better, gs, best)
    masked = [jnp.where(gsel == (e // EXPERTS_PER_GROUP), brow[e], NEG_INF) for e in range(N_EXPERTS)]
    i1 = jnp.zeros(gsel.shape, I32)
    b1 = masked[0]
    s1 = srow[0]
    for e in range(1, N_EXPERTS):
        better = masked[e] > b1
        i1 = jnp.where(better, e, i1)
        b1 = jnp.where(better, masked[e], b1)
        s1 = jnp.where(better, srow[e], s1)
    i2 = jnp.full(gsel.shape, -1, I32)
    b2 = jnp.full(b1.shape, -jnp.inf, F32)
    s2 = jnp.zeros(b1.shape, F32)
    for e in range(N_EXPERTS):
        better = (masked[e] > b2) & (i1 != e)
        i2 = jnp.where(better, e, i2)
        b2 = jnp.where(better, masked[e], b2)
        s2 = jnp.where(better, srow[e], s2)
    den = s1 + s2
    return i1, i2, s1 / den, s2 / den


ROW_SPLITS = 2


def _norm_route_store(r0, nr, m, x, g_ref, b_ref, rw_ref, rb_ref, x1_ref, idx_ref, wt_ref, alpha):
    x1 = _layer_norm(alpha * x + m, g_ref[...], b_ref[...])
    x1_ref[r0:r0 + nr, :] = x1
    i1, i2, w1, w2 = _route(x1, rw_ref, rb_ref)
    rid = lax.broadcasted_iota(I32, (8, nr), 0)
    idx_ref[:, r0:r0 + nr] = jnp.where(rid == 0, i1, jnp.where(rid == 1, i2, 0))
    wt_ref[:, r0:r0 + nr] = jnp.where(rid == 0, w1, jnp.where(rid == 1, w2, 0.0))


def _router_operands(router_w, router_b):
    rw = jnp.zeros((D_MODEL, 128), F32).at[:, :N_EXPERTS].set(router_w.astype(F32))
    rwh = rw.astype(BF16)
    rwl = (rw - rwh.astype(F32)).astype(BF16)
    rb = router_b.astype(F32)[:, None]
    return jnp.concatenate([rwh, rwl], axis=1), rb


_ROUTER_SPECS = [
    pl.BlockSpec((D_MODEL, 256), lambda i: (0, 0)),
    pl.BlockSpec((N_EXPERTS, 1), lambda i: (0, 0)),
]


def _mix_out_specs(ts, T):
    specs = [
        pl.BlockSpec((ts, D_MODEL), lambda i: (i, 0)),
        pl.BlockSpec((8, ts), lambda i: (0, i)),
        pl.BlockSpec((8, ts), lambda i: (0, i)),
    ]
    shapes = [
        jax.ShapeDtypeStruct((T, D_MODEL), F32),
        jax.ShapeDtypeStruct((8, T), I32),
        jax.ShapeDtypeStruct((8, T), F32),
    ]
    return specs, shapes


def _outproj_even_kernel(*refs, nf, na, nx, fb, ab, xb, alpha):
    f_refs, a_refs, x_refs = refs[:nf], refs[nf:nf + na], refs[nf + na:nf + na + nx]
    wf_ref, wa_ref, g_ref, b_ref, rw_ref, rb_ref, x1_ref, idx_ref, wt_ref = refs[nf + na + nx:]
    nr = x1_ref.shape[0] // ROW_SPLITS
    for h in range(ROW_SPLITS):
        r0 = h * nr
        m = (jnp.dot(_pick_rows(f_refs, fb, r0, nr), wf_ref[...], preferred_element_type=F32)
             + jnp.dot(_pick_rows(a_refs, ab, r0, nr), wa_ref[...], preferred_element_type=F32))
        xr = _pick_rows(x_refs, xb, r0, nr)
        _norm_route_store(r0, nr, m, xr, g_ref, b_ref, rw_ref, rb_ref, x1_ref, idx_ref, wt_ref, alpha)


def _outproj_even(f_parts, a_parts, x_parts, w_out, ln_g, ln_b, router, alpha, ts):
    T = sum(p.shape[0] for p in x_parts)
    f_specs, fb = _parts_specs(f_parts, ts)
    a_specs, ab = _parts_specs(a_parts, ts)
    x_specs, xb = _parts_specs(x_parts, ts)
    wf = w_out[:D_FOURIER].astype(BF16)
    wa = w_out[D_FOURIER:].reshape(N_KV_B, GRP_B * HEAD_DIM, D_MODEL)
    wa = jnp.concatenate([jnp.zeros((N_KV_B, HEAD_DIM, D_MODEL), w_out.dtype), wa], axis=1)
    wa = wa.reshape(1024, D_MODEL).astype(BF16)
    out_specs, out_shapes = _mix_out_specs(ts, T)
    return pl.pallas_call(
        functools.partial(_outproj_even_kernel, nf=len(f_parts), na=len(a_parts), nx=len(x_parts),
                          fb=fb, ab=ab, xb=xb, alpha=alpha),
        grid=(T // ts,),
        in_specs=f_specs + a_specs + x_specs + [
            pl.BlockSpec((D_FOURIER, D_MODEL), lambda i: (0, 0)),
            pl.BlockSpec((1024, D_MODEL), lambda i: (0, 0)),
            pl.BlockSpec((1, D_MODEL), lambda i: (0, 0)),
            pl.BlockSpec((1, D_MODEL), lambda i: (0, 0)),
        ] + _ROUTER_SPECS,
        out_specs=out_specs,
        out_shape=out_shapes,
        compiler_params=_cparams(("parallel",)),
        name="outproj_even",
    )(*f_parts, *a_parts, *x_parts, wf, wa, ln_g[None, :], ln_b[None, :], *router)


CHUNK = 8


def _tab_len(td):
    return -(-(1 + 2 * td // CHUNK + N_EXPERTS) // 128) * 128


def _moe_plan(idx, td, tm):
    T = idx.shape[1]
    nt = T // td
    e = idx.T.reshape(-1)
    oh = (e[:, None] == jnp.arange(N_EXPERTS, dtype=I32)[None, :]).astype(I32)
    csum = jnp.cumsum(oh, axis=0)
    tile_end = csum.reshape(nt, 2 * td, N_EXPERTS)[:, -1, :]
    base = jnp.concatenate([jnp.zeros((1, N_EXPERTS), I32), tile_end[:-1]], axis=0)
    cnt = tile_end - base
    c8 = ((cnt + CHUNK - 1) // CHUNK) * CHUNK
    off8 = jnp.cumsum(c8, axis=1) - c8
    base8 = jnp.cumsum(c8, axis=0) - c8
    seg = jnp.sum(c8, axis=0)
    padded = ((seg + tm - 1) // tm) * tm
    ends = jnp.cumsum(padded)
    starts = ends - padded
    dst = starts[None, :] + base8
    shift = jnp.repeat(off8 - base, 2 * td, axis=0)
    lpos = (jnp.sum((csum + shift) * oh, axis=1) - 1).astype(I32)
    n_rows = -(-(2 * T + nt * N_EXPERTS * (CHUNK - 1)) // tm) * tm + N_EXPERTS * tm
    tile_start = jnp.arange(n_rows // tm, dtype=I32) * tm
    tile_e = jnp.minimum(jnp.sum((ends[None, :] <= tile_start[:, None]).astype(I32), axis=1), N_EXPERTS - 1)
    n_used = (ends[-1] // tm).astype(I32)[None]
    nc = c8 // CHUNK
    cum = jnp.cumsum(nc, axis=1)
    n_chunk_max = 2 * td // CHUNK + N_EXPERTS
    c_idx = jnp.arange(n_chunk_max, dtype=I32)
    e_c = jnp.minimum(jnp.sum((cum[:, None, :] <= c_idx[None, :, None]).astype(I32), axis=2), N_EXPERTS - 1)
    oh_c = (e_c[:, :, None] == jnp.arange(N_EXPERTS, dtype=I32)[None, None, :]).astype(I32)
    srow = jnp.sum(oh_c * (dst - CHUNK * (cum - nc))[:, None, :], axis=2) + CHUNK * c_idx[None, :]
    tab = jnp.concatenate([cum[:, -1:], srow], axis=1).astype(I32)
    tab = jnp.pad(tab, ((0, 0), (0, _tab_len(td) - tab.shape[1]))).reshape(nt, 1, _tab_len(td))
    pad_lo = jnp.concatenate([starts + seg, ends[-1:]]).astype(I32)
    pad_hi = jnp.concatenate([ends, jnp.full((1,), n_rows, I32)]).astype(I32)
    lpos_tk = lpos.reshape(T, 2)
    lpos_rows = jnp.zeros((8, T), I32).at[:2].set(lpos_tk.T)
    return tab, lpos_rows, lpos_tk, tile_e.astype(I32), n_used, pad_lo, pad_hi, n_rows


def _chunk_loops(tab_ref, fn):
    total = tab_ref[0, 0, 0]

    def one(c, priority):
        fn(pl.multiple_of(CHUNK * c, CHUNK), pl.multiple_of(tab_ref[0, 0, 1 + c], CHUNK), priority)

    def body(j, carry):
        one(2 * j, 0)
        one(2 * j + 1, 1)
        return carry
    lax.fori_loop(0, total // 2, body, 0)

    @pl.when(total % 2 == 1)
    def _():
        one(total - 1, 0)
    return total


def _dispatch_kernel(lo_ref, hi_ref, tab_ref, lp_ref, x_ref, xs_hbm, stage, zbuf, nwait, sem, zsem, *, ns):
    i = pl.program_id(0)
    slot = i % 2

    def zero_copy(dst):
        return pltpu.make_async_copy(zbuf, xs_hbm.at[pl.ds(pl.multiple_of(dst, CHUNK), CHUNK)], zsem)

    @pl.when(i == 0)
    def _():
        nwait[0] = 0
        nwait[1] = 0
        zbuf[...] = jnp.zeros(zbuf.shape, F32)
        for e in range(N_EXPERTS + 1):
            nz = (hi_ref[e] - lo_ref[e]) // CHUNK
            lax.fori_loop(0, nz, lambda j, c, e=e: (zero_copy(lo_ref[e] + CHUNK * j).start(), c)[1], 0)
        for e in range(N_EXPERTS + 1):
            nz = (hi_ref[e] - lo_ref[e]) // CHUNK
            lax.fori_loop(0, nz, lambda j, c: (zero_copy(0).wait(), c)[1], 0)

    def piece(local_row, sorted_row, s):
        return pltpu.make_async_copy(stage.at[s, pl.ds(local_row, CHUNK)],
                                     xs_hbm.at[pl.ds(sorted_row, CHUNK)], sem.at[s])

    def drain(s):
        lax.fori_loop(0, nwait[s], lambda j, c: (piece(0, 0, s).wait(), c)[1], 0)

    drain(slot)
    lp = lp_ref[...]
    rows = lax.broadcasted_iota(I32, (ns, lp.shape[1]), 0)
    perm = jnp.where((rows == lp[0:1]) | (rows == lp[1:2]), 1.0, 0.0).astype(BF16)
    stage[slot] = jnp.dot(perm, x_ref[...].astype(BF16), preferred_element_type=F32)
    nwait[slot] = _chunk_loops(tab_ref, lambda lr, sr, pr: piece(lr, sr, slot).start(priority=pr))

    @pl.when(i == pl.num_programs(0) - 1)
    def _():
        drain(0)
        drain(1)


def _moe_dispatch(x1, tab, lpos_rows, pad_lo, pad_hi, n_rows, td):
    T = x1.shape[0]
    ns = 2 * td + 128
    grid_spec = pltpu.PrefetchScalarGridSpec(
        num_scalar_prefetch=2,
        grid=(T // td,),
        in_specs=[
            pl.BlockSpec((1, 1, _tab_len(td)), lambda i, lo, hi: (i, 0, 0), memory_space=pltpu.SMEM),
            pl.BlockSpec((8, td), lambda i, lo, hi: (0, i)),
            pl.BlockSpec((td, D_MODEL), lambda i, lo, hi: (i, 0)),
        ],
        out_specs=pl.BlockSpec(memory_space=pl.ANY),
        scratch_shapes=[pltpu.VMEM((2, ns, D_MODEL), F32), pltpu.VMEM((CHUNK, D_MODEL), F32),
                        pltpu.SMEM((2,), I32), pltpu.SemaphoreType.DMA((2,)), pltpu.SemaphoreType.DMA(())],
    )
    return pl.pallas_call(
        functools.partial(_dispatch_kernel, ns=ns),
        grid_spec=grid_spec,
        out_shape=jax.ShapeDtypeStruct((n_rows, D_MODEL), F32),
        compiler_params=_cparams(("arbitrary",)),
        name="moe_dispatch",
    )(pad_lo, pad_hi, tab, lpos_rows, x1)


def _moe_kernel(te_ref, nu_ref, x_ref, wg_ref, wu_ref, wd_ref, y_ref, wgb, wub, wdb):
    i = pl.program_id(0)

    @pl.when((i == 0) | (te_ref[i] != te_ref[jnp.maximum(i - 1, 0)]))
    def _():
        wgb[...] = wg_ref[0].astype(BF16)
        wub[...] = wu_ref[0].astype(BF16)
        wdb[...] = wd_ref[0].astype(BF16)

    @pl.when(i < nu_ref[0])
    def _():
        xg = x_ref[...].astype(BF16)
        hg = jnp.dot(xg, wgb[...], preferred_element_type=F32)
        hu = jnp.dot(xg, wub[...], preferred_element_type=F32)
        hdn = (hg / (1.0 + jnp.exp(-hg)) * hu).astype(BF16)
        y_ref[...] = jnp.dot(hdn, wdb[...], preferred_element_type=F32)

    @pl.when(i >= nu_ref[0])
    def _():
        y_ref[...] = jnp.zeros(y_ref.shape, F32)


def _moe_experts(xs, tile_e, n_used, wg, wu, wd, tm):
    n_rows = xs.shape[0]
    grid_spec = pltpu.PrefetchScalarGridSpec(
        num_scalar_prefetch=2,
        grid=(n_rows // tm,),
        in_specs=[
            pl.BlockSpec((tm, D_MODEL), lambda i, te, nu: (i, 0)),
            pl.BlockSpec((1, D_MODEL, D_FF), lambda i, te, nu: (te[i], 0, 0)),
            pl.BlockSpec((1, D_MODEL, D_FF), lambda i, te, nu: (te[i], 0, 0)),
            pl.BlockSpec((1, D_FF, D_MODEL), lambda i, te, nu: (te[i], 0, 0)),
        ],
        out_specs=pl.BlockSpec((tm, D_MODEL), lambda i, te, nu: (i, 0)),
        scratch_shapes=[pltpu.VMEM((D_MODEL, D_FF), BF16), pltpu.VMEM((D_MODEL, D_FF), BF16),
                        pltpu.VMEM((D_FF, D_MODEL), BF16)],
    )
    return pl.pallas_call(
        _moe_kernel,
        grid_spec=grid_spec,
        out_shape=jax.ShapeDtypeStruct((n_rows, D_MODEL), F32),
        compiler_params=_cparams(("arbitrary",)),
        name="moe_experts",
    )(tile_e, n_used, xs, wg, wu, wd)


def _combine_kernel(tab_ref, tabn_ref, lp_ref, y_hbm, x_ref, wt_ref, g_ref, b_ref, *rest, ns, alpha, bounds):
    o_refs, (ystage, sem) = rest[:len(bounds)], rest[len(bounds):]
    i = pl.program_id(0)
    n = pl.num_programs(0)
    slot = i % 2

    def piece(local_row, sorted_row, s):
        return pltpu.make_async_copy(y_hbm.at[pl.ds(sorted_row, CHUNK)],
                                     ystage.at[s, pl.ds(local_row, CHUNK)], sem.at[s])

    @pl.when(i == 0)
    def _():
        ystage[...] = jnp.zeros(ystage.shape, F32)
        _chunk_loops(tab_ref, lambda lr, sr, pr: piece(lr, sr, 0).start(priority=pr))

    @pl.when(i + 1 < n)
    def _():
        _chunk_loops(tabn_ref, lambda lr, sr, pr: piece(lr, sr, 1 - slot).start(priority=pr))

    lax.fori_loop(0, tab_ref[0, 0, 0], lambda j, c: (piece(0, 0, slot).wait(), c)[1], 0)

    ys = ystage[slot].astype(BF16)
    lp = lp_ref[...]
    cols = lax.broadcasted_iota(I32, (lp.shape[0], ns), 1)
    w = wt_ref[...]
    tc = lp.shape[0]
    sel = jnp.concatenate([jnp.where(cols == lp[:, k:k + 1], 1.0, 0.0).astype(BF16) for k in range(2)], axis=0)
    picked = jnp.dot(sel, ys, preferred_element_type=F32)
    f = w[:, 0:1] * picked[:tc] + w[:, 1:2] * picked[tc:]
    o = _layer_norm(alpha * x_ref[...] + f, g_ref[...], b_ref[...])
    if len(bounds) == 1:
        o_refs[0][...] = o
    else:
        lo = 0
        for o_ref, hi in zip(o_refs, bounds):
            @pl.when((i >= lo) & (i < hi))
            def _(o_ref=o_ref):
                o_ref[...] = o
            lo = hi


def _moe_combine(y, tab, lpos_tk, wts, x1, ln_g, ln_b, alpha, tc, out_rows):
    T = x1.shape[0]
    n_tiles = T // tc
    ns = 2 * tc + 128
    out_specs, out_shapes, bounds, lo = [], [], [], 0
    for rows in out_rows:
        n = rows // tc
        out_specs.append(pl.BlockSpec((tc, D_MODEL), lambda i, lo=lo, n=n: (jnp.clip(i - lo, 0, n - 1), 0)))
        out_shapes.append(jax.ShapeDtypeStruct((rows, D_MODEL), F32))
        lo += n
        bounds.append(lo)
    return pl.pallas_call(
        functools.partial(_combine_kernel, ns=ns, alpha=alpha, bounds=tuple(bounds)),
        grid=(n_tiles,),
        in_specs=[
            pl.BlockSpec((1, 1, _tab_len(tc)), lambda i: (i, 0, 0), memory_space=pltpu.SMEM),
            pl.BlockSpec((1, 1, _tab_len(tc)), lambda i: (jnp.minimum(i + 1, n_tiles - 1), 0, 0),
                         memory_space=pltpu.SMEM),
            pl.BlockSpec((tc, 2), lambda i: (i, 0)),
            pl.BlockSpec(memory_space=pl.ANY),
            pl.BlockSpec((tc, D_MODEL), lambda i: (i, 0)),
            pl.BlockSpec((tc, 2), lambda i: (i, 0)),
            pl.BlockSpec((1, D_MODEL), lambda i: (0, 0)),
            pl.BlockSpec((1, D_MODEL), lambda i: (0, 0)),
        ],
        out_specs=out_specs,
        out_shape=out_shapes,
        scratch_shapes=[pltpu.VMEM((2, ns, D_MODEL), F32), pltpu.SemaphoreType.DMA((2,))],
        compiler_params=_cparams(("arbitrary",)),
        name="moe_combine",
    )(tab, tab, lpos_tk, y, x1, wts, ln_g[None, :], ln_b[None, :])


def _moe_layer(x1, idx, wt, wg, wu, wd, ln_g, ln_b, alpha, tm, tc, out_rows):
    tab, lpos_rows, lpos_tk, tile_e, n_used, pad_lo, pad_hi, n_rows = _moe_plan(idx[:2], tc, tm)
    xs = _moe_dispatch(x1, tab, lpos_rows, pad_lo, pad_hi, n_rows, tc)
    y = _moe_experts(xs, tile_e, n_used, wg, wu, wd, tm)
    return _moe_combine(y, tab, lpos_tk, wt[:2].T, x1, ln_g, ln_b, alpha, tc, out_rows)


def _inproj_odd_kernel(x_ref, w_ref, vone_ref, bg_ref, u_ref, q_ref, k_ref, v_ref, nrm_ref):
    x = x_ref[...].astype(BF16)
    proj = jnp.dot(x, w_ref[...], preferred_element_type=F32)
    bg_ref[...] = proj[:, :512].astype(BF16)
    u_ref[...] = (proj[:, 512:1024] * proj[:, 1024:1536]).astype(BF16)
    qb = (proj[:, 1536:2048] * (HEAD_DIM ** -0.5 * LOG2E)).astype(BF16)
    kb = proj[:, 2048:2304].astype(BF16)
    q_ref[...] = qb
    k_ref[...] = kb
    v_ref[...] = (proj[:, 2304:2560] + vone_ref[...]).astype(BF16)
    qf = qb.astype(F32)
    kf = kb.astype(F32)
    qq = jnp.max(jnp.sum(qf * qf, axis=1, keepdims=True), axis=0, keepdims=True)
    kk = jnp.max(jnp.sum(kf * kf, axis=1, keepdims=True), axis=0, keepdims=True)
    rid = lax.broadcasted_iota(I32, (8, 128), 0)
    nrm_ref[0] = jnp.where(rid == 0, qq, jnp.where(rid == 1, kk, 0.0))


def _inproj_odd(x, w_in, ts):
    T = x.shape[0]
    c3 = 3 * D_CONV
    wk = w_in[:, c3 + 512:c3 + 640].reshape(D_MODEL, N_KV_D, HEAD_DIM)
    wv = w_in[:, c3 + 640:c3 + 768].reshape(D_MODEL, N_KV_D, HEAD_DIM)
    wkp = jnp.concatenate([wk, jnp.zeros_like(wk)], axis=2).reshape(D_MODEL, 256)
    wvp = jnp.concatenate([wv, jnp.zeros_like(wv)], axis=2).reshape(D_MODEL, 256)
    w = jnp.concatenate([w_in[:, :c3 + 512], wkp, wvp], axis=1).astype(BF16)
    vone_np = np.zeros((1, 256), np.float32)
    for h in range(N_KV_D):
        vone_np[0, 128 * h + 64:128 * (h + 1)] = 1.0
    n_w = w.shape[1]
    widths = [512, 512, 512, 256, 256]
    bg, u, q, kw, vw, nrm = pl.pallas_call(
        _inproj_odd_kernel,
        grid=(T // ts,),
        in_specs=[
            pl.BlockSpec((ts, D_MODEL), lambda i: (i, 0)),
            pl.BlockSpec((D_MODEL, n_w), lambda i: (0, 0)),
            pl.BlockSpec((1, 256), lambda i: (0, 0)),
        ],
        out_specs=[pl.BlockSpec((ts, n), lambda i: (i, 0)) for n in widths]
        + [pl.BlockSpec((1, 8, 128), lambda i: (i, 0, 0))],
        out_shape=[jax.ShapeDtypeStruct((T, n), BF16) for n in widths]
        + [jax.ShapeDtypeStruct((T // ts, 8, 128), F32)],
        compiler_params=_cparams(("parallel",)),
        name="inproj_odd",
    )(x, w, jnp.asarray(vone_np))
    score_bound = jnp.sqrt(jnp.max(nrm[:, 0, 0]) * jnp.max(nrm[:, 1, 0]))
    return bg, u, q, kw, vw, score_bound


def _wattn_kernel(bounded_ref, q_ref, kp_ref, kc_ref, kn_ref, vp_ref, vc_ref, vn_ref, bias_ref, sink_ref,
                  place_ref, o_ref, *, tq, seq_tiles):
    i = pl.program_id(0)
    first = i < 0
    last = i < 0
    for lo, hi, per in seq_tiles:
        inside = (i >= lo) & (i < hi)
        first = first | (inside & ((i - lo) % per == 0))
        last = last | (inside & ((i - lo) % per == per - 1))
    kfull = jnp.concatenate([kp_ref[...], kc_ref[...], kn_ref[...]], axis=0)[:, :HEAD_DIM]
    vfull = jnp.concatenate([vp_ref[...], vc_ref[...], vn_ref[...]], axis=0)
    q4 = q_ref[...]
    bias = bias_ref[0]
    sink = sink_ref[0]
    nb = tq // Q_BLOCK
    col = lax.broadcasted_iota(I32, (GRP_D * Q_BLOCK, 3 * Q_BLOCK), 1)

    def blocks(with_max):
        for n in range(nb):
            qs = jnp.concatenate([q4[Q_BLOCK * n:Q_BLOCK * (n + 1), HEAD_DIM * g:HEAD_DIM * (g + 1)]
                                  for g in range(GRP_D)], axis=0)
            keys = kfull[Q_BLOCK * n:Q_BLOCK * (n + 3)]
            vals = vfull[Q_BLOCK * n:Q_BLOCK * (n + 3)]
            s = lax.dot_general(qs, keys, (((1,), (1,)), ((), ())), preferred_element_type=F32) + bias
            if n == 0:
                s = jnp.where(first & (col < Q_BLOCK), NEG_INF, s)
            if n == nb - 1:
                s = jnp.where(last & (col >= 2 * Q_BLOCK), NEG_INF, s)
            if with_max:
                m = jnp.maximum(jnp.max(s, axis=1, keepdims=True), sink)
                s = s - m
                snk = sink - m
            else:
                snk = sink
            acc = jnp.dot(jnp.exp2(s).astype(BF16), vals, preferred_element_type=F32)
            den = acc[:, HEAD_DIM:HEAD_DIM + 1] + jnp.exp2(snk)
            o = (acc / den).astype(BF16)
            out = jnp.zeros((Q_BLOCK, 256), F32)
            for g in range(GRP_D):
                out = out + jnp.dot(o[Q_BLOCK * g:Q_BLOCK * (g + 1)], place_ref[g],
                                    preferred_element_type=F32)
            o_ref[Q_BLOCK * n:Q_BLOCK * (n + 1), :] = out.astype(BF16)

    @pl.when(bounded_ref[0] == 1)
    def _():
        blocks(False)

    @pl.when(bounded_ref[0] != 1)
    def _():
        blocks(True)


WINDOW_BOUND_MAX = 90.0


def _window_attention(q, kw, vw, sink_logits, score_bound, classes, tq):
    T = q.shape[0]
    sink_bound = jnp.max(jnp.abs(sink_logits.astype(F32))) * LOG2E
    bounded = ((1.02 * score_bound <= WINDOW_BOUND_MAX) & (sink_bound <= WINDOW_BOUND_MAX)).astype(I32)[None]
    n_tiles = T // tq
    hb = tq // Q_BLOCK
    n_hblk = T // Q_BLOCK
    r = jnp.arange(Q_BLOCK, dtype=I32)[:, None]
    j = jnp.arange(3 * Q_BLOCK, dtype=I32)[None, :]
    rel = jnp.abs(j - Q_BLOCK - r).astype(F32)
    slopes = jnp.asarray(np.array([2.0 ** (-8.0 * (h + 1) / N_HEADS_D) for h in range(N_HEADS_D)], np.float32))
    bias = jnp.where(rel[None] <= WINDOW, -slopes[:, None, None] * rel[None] * LOG2E, NEG_INF)
    bias = bias.reshape(N_KV_D, GRP_D * Q_BLOCK, 3 * Q_BLOCK)
    sink = jnp.repeat(sink_logits.astype(F32) * LOG2E, Q_BLOCK).reshape(N_KV_D, GRP_D * Q_BLOCK, 1)
    place = _lane_place(128, 256, [0, 1, 2, 3])
    seq_tiles = []
    t0 = 0
    for nseq, S, _ in classes:
        cnt = nseq * S // tq
        seq_tiles.append((t0, t0 + cnt, S // tq))
        t0 += cnt
    prev_map = lambda i, h, fl: (jnp.maximum(i * hb - 1, 0), h)
    next_map = lambda i, h, fl: (jnp.minimum((i + 1) * hb, n_hblk - 1), h)
    grid_spec = pltpu.PrefetchScalarGridSpec(
        num_scalar_prefetch=1,
        grid=(n_tiles, N_KV_D),
        in_specs=[
            pl.BlockSpec((tq, 256), lambda i, h, fl: (i, h)),
            pl.BlockSpec((Q_BLOCK, 128), prev_map),
            pl.BlockSpec((tq, 128), lambda i, h, fl: (i, h)),
            pl.BlockSpec((Q_BLOCK, 128), next_map),
            pl.BlockSpec((Q_BLOCK, 128), prev_map),
            pl.BlockSpec((tq, 128), lambda i, h, fl: (i, h)),
            pl.BlockSpec((Q_BLOCK, 128), next_map),
            pl.BlockSpec((1, GRP_D * Q_BLOCK, 3 * Q_BLOCK), lambda i, h, fl: (h, 0, 0)),
            pl.BlockSpec((1, GRP_D * Q_BLOCK, 1), lambda i, h, fl: (h, 0, 0)),
            pl.BlockSpec((GRP_D, 128, 256), lambda i, h, fl: (0, 0, 0)),
        ],
        out_specs=pl.BlockSpec((tq, 256), lambda i, h, fl: (i, h)),
    )
    return pl.pallas_call(
        functools.partial(_wattn_kernel, tq=tq, seq_tiles=tuple(seq_tiles)),
        grid_spec=grid_spec,
        out_shape=jax.ShapeDtypeStruct((T, 512), BF16),
        compiler_params=_cparams(("parallel", "parallel")),
        name="window_attention",
    )(bounded, q, kw, kw, kw, vw, vw, vw, bias, sink, place)


HALO = 16


def _outproj_odd_kernel(bg_ref, u_ref, up_ref, un_ref, a_ref, x_ref, cw_ref, cb_ref, wc_ref, wa_ref,
                        g_ref, b_ref, rw_ref, rb_ref, x1_ref, idx_ref, wt_ref,
                        *, alpha, ts, seq_tiles):
    i = pl.program_id(0)
    first = i < 0
    last = i < 0
    for lo, hi, per in seq_tiles:
        inside = (i >= lo) & (i < hi)
        first = first | (inside & ((i - lo) % per == 0))
        last = last | (inside & ((i - lo) % per == per - 1))
    u = u_ref[...].astype(F32)
    prev_row = jnp.where(first, 0.0, up_ref[HALO - 1:HALO, :].astype(F32))
    next_row = jnp.where(last, 0.0, un_ref[0:1, :].astype(F32))
    rid = lax.broadcasted_iota(I32, u.shape, 0)
    ud = jnp.where(rid == 0, prev_row, pltpu.roll(u, 1, 0))
    uu = jnp.where(rid == ts - 1, next_row, pltpu.roll(u, ts - 1, 0))
    cw = cw_ref[...]
    y = ud * cw[0:1] + u * cw[1:2] + uu * cw[2:3] + cb_ref[...]
    c = (bg_ref[...].astype(F32) * y).astype(BF16)
    nr = ts // ROW_SPLITS
    for h in range(ROW_SPLITS):
        r0 = h * nr
        m = (jnp.dot(c[r0:r0 + nr], wc_ref[...], preferred_element_type=F32)
             + jnp.dot(a_ref[r0:r0 + nr, :], wa_ref[...], preferred_element_type=F32))
        _norm_route_store(r0, nr, m, x_ref[r0:r0 + nr, :], g_ref, b_ref, rw_ref, rb_ref, x1_ref, idx_ref,
                          wt_ref, alpha)


def _outproj_odd(bg, u, a, x, conv_w, conv_b, w_out, ln_g, ln_b, router, alpha, classes, ts):
    T = x.shape[0]
    hb = ts // HALO
    n_h = T // HALO
    seq_tiles = []
    t0 = 0
    for nseq, S, _ in classes:
        cnt = nseq * S // ts
        seq_tiles.append((t0, t0 + cnt, S // ts))
        t0 += cnt
    out_specs, out_shapes = _mix_out_specs(ts, T)
    return pl.pallas_call(
        functools.partial(_outproj_odd_kernel, alpha=alpha, ts=ts, seq_tiles=tuple(seq_tiles)),
        grid=(T // ts,),
        in_specs=[
            pl.BlockSpec((ts, D_CONV), lambda i: (i, 0)),
            pl.BlockSpec((ts, D_CONV), lambda i: (i, 0)),
            pl.BlockSpec((HALO, D_CONV), lambda i: (jnp.maximum(i * hb - 1, 0), 0)),
            pl.BlockSpec((HALO, D_CONV), lambda i: (jnp.minimum((i + 1) * hb, n_h - 1), 0)),
            pl.BlockSpec((ts, 512), lambda i: (i, 0)),
            pl.BlockSpec((ts, D_MODEL), lambda i: (i, 0)),
            pl.BlockSpec((3, D_CONV), lambda i: (0, 0)),
            pl.BlockSpec((1, D_CONV), lambda i: (0, 0)),
            pl.BlockSpec((D_CONV, D_MODEL), lambda i: (0, 0)),
            pl.BlockSpec((512, D_MODEL), lambda i: (0, 0)),
            pl.BlockSpec((1, D_MODEL), lambda i: (0, 0)),
            pl.BlockSpec((1, D_MODEL), lambda i: (0, 0)),
        ] + _ROUTER_SPECS,
        out_specs=out_specs,
        out_shape=out_shapes,
        compiler_params=_cparams(("parallel",)),
        name="outproj_odd",
    )(bg, u, u, u, a, x, conv_w.astype(F32), conv_b.astype(F32)[None, :],
      w_out[:D_CONV].astype(BF16), w_out[D_CONV:].astype(BF16), ln_g[None, :], ln_b[None, :], *router)


def _tile(n, cap):
    t = cap
    while n % t:
        t //= 2
    return t


def kernel(x_prompt, x_sample, w_in_even, fourier_norm_g, q_norm_g, k_norm_g, w_out_even, w_in_odd, conv_w,
           conv_b, sink_logits, w_out_odd, ln_mix_g, ln_mix_b, ln_ffn_g, ln_ffn_b, router_w, router_b,
           w_gate, w_up, w_down):
    depth = ln_mix_g.shape[0]
    alpha = float((2 * depth) ** 0.25)
    bp, sp, _ = x_prompt.shape
    bs, ss, _ = x_sample.shape
    classes = ((bp, sp, 0), (bs, ss, bp * sp))
    T = bp * sp + bs * ss
    min_s = min(sp, ss)
    ts = _tile(min_s, 512)
    tq_w = _tile(min_s, 512)
    tm = 512
    tc = _tile(min_s, 256)
    x_parts = [x_prompt.reshape(bp * sp, D_MODEL), x_sample.reshape(bs * ss, D_MODEL)]
    router = _router_operands(router_w, router_b)
    for l in range(depth):
        i = l // 2
        if l % 2 == 0:
            uf, qk, v_aug = _inproj_even(x_parts, w_in_even[i], fourier_norm_g[i], q_norm_g[i], k_norm_g[i],
                                         classes, ts)
            bounded = _score_bounded(q_norm_g[i], k_norm_g[i])
            f_parts, a_parts = [], []
            for nseq, S, tok0 in classes:
                f_parts.append(_fourier_mix(uf[tok0:tok0 + nseq * S], nseq, S))
                a_parts.append(_global_attention(qk, v_aug, bounded, nseq, S, tok0, _tile(S, 1024),
                                                 _tile(S, 2048)))
            x1, idx, wt = _outproj_even(f_parts, a_parts, x_parts, w_out_even[i], ln_mix_g[l], ln_mix_b[l],
                                        router, alpha, ts)
        else:
            x = x_parts[0]
            bg, u, q, kw, vw, score_bound = _inproj_odd(x, w_in_odd[i], ts)
            a = _window_attention(q, kw, vw, sink_logits[i], score_bound, classes, tq_w)
            x1, idx, wt = _outproj_odd(bg, u, a, x, conv_w[i], conv_b[i], w_out_odd[i], ln_mix_g[l],
                                       ln_mix_b[l], router, alpha, classes, ts)
        out_rows = [bp * sp, bs * ss] if l == depth - 1 else [T]
        x_parts = _moe_layer(x1, idx, wt, w_gate[l], w_up[l], w_down[l], ln_ffn_g[l], ln_ffn_b[l], alpha, tm,
                             tc, out_rows)
    if len(x_parts) == 1:
        x_parts = [x_parts[0][:bp * sp], x_parts[0][bp * sp:]]
    return (x_parts[0].reshape(bp, sp, D_MODEL), x_parts[1].reshape(bs, ss, D_MODEL))
```

```python
import functools
import math

import numpy as np
import jax
import jax.numpy as jnp
from jax import lax
from jax.experimental import pallas as pl
from jax.experimental.pallas import tpu as pltpu

F32 = jnp.float32
BF16 = jnp.bfloat16
I32 = jnp.int32

D_MODEL = 1024
HEAD_DIM = 64
GRID_W = 64
Q_BLOCK = 128
WINDOW = 128
ROPE_THETA = 10000.0
N_FOURIER_GROUPS = 4
D_FOURIER = 256
N_HEADS_B = 12
N_KV_B = 4
GRP_B = 3
D_CONV = 512
N_HEADS_D = 8
N_KV_D = 2
GRP_D = 4
N_EXPERTS = 16
N_GROUPS = 4
EXPERTS_PER_GROUP = 4
D_FF = 512
LN_EPS = 1e-5
RMS_EPS = 1e-6
NEG_INF = -1e30
LOG2E = 1.4426950408889634
FFT_S2 = 128
V7X_VMEM_LIMIT = 48 * 1024 * 1024


def _cparams(sem):
    return pltpu.CompilerParams(dimension_semantics=sem, vmem_limit_bytes=V7X_VMEM_LIMIT)


def _layer_norm(y, g, b):
    mu = jnp.mean(y, axis=-1, keepdims=True)
    yc = y - mu
    var = jnp.mean(yc * yc, axis=-1, keepdims=True)
    return yc * lax.rsqrt(var + LN_EPS) * g + b


N_NORM_COLS = D_FOURIER + 1024


def _parts_specs(parts, ts):
    specs, bounds, lo = [], [], 0
    for p in parts:
        n = p.shape[0] // ts
        specs.append(pl.BlockSpec((ts, p.shape[1]), lambda i, lo=lo, n=n: (jnp.clip(i - lo, 0, n - 1), 0)))
        lo += n
        bounds.append(lo)
    return specs, tuple(bounds)


def _pick_rows(refs, bounds, r0, nr):
    i = pl.program_id(0)
    out = refs[-1][r0:r0 + nr, :]
    for ref, hi in reversed(list(zip(refs[:-1], bounds[:-1]))):
        out = jnp.where(i < hi, ref[r0:r0 + nr, :], out)
    return out


def _inproj_even_kernel(*refs, nx, bounds):
    x_refs = refs[:nx]
    (w_ref, gain_ref, seg_ref, segt_ref, cos_ref, sin_ref, vone_ref, uf_ref, qk_ref, v_ref) = refs[nx:]
    nr = uf_ref.shape[0] // ROW_SPLITS
    for part in range(ROW_SPLITS):
        r0 = part * nr
        x = _pick_rows(x_refs, bounds, r0, nr).astype(BF16)
        proj = jnp.dot(x, w_ref[...], preferred_element_type=F32)
        nrm = proj[:, :N_NORM_COLS]
        sq = (nrm * nrm).astype(BF16)
        ssum = jnp.dot(sq, seg_ref[...], preferred_element_type=F32)
        r = lax.rsqrt(ssum * (1.0 / HEAD_DIM) + RMS_EPS)
        rh = r.astype(BF16)
        rl = (r - rh.astype(F32)).astype(BF16)
        rex = jnp.dot(jnp.concatenate([rh, rl], axis=1), segt_ref[...], preferred_element_type=F32)
        y = nrm * rex * gain_ref[...]
        uf_ref[r0:r0 + nr, :] = y[:, :D_FOURIER].astype(BF16)
        yq = y[:, D_FOURIER:]
        c = jnp.concatenate([cos_ref[r0:r0 + nr, :]] * 8, axis=1)
        s = jnp.concatenate([sin_ref[r0:r0 + nr, :]] * 8, axis=1)
        lane = lax.broadcasted_iota(I32, yq.shape, 1)
        first = (lane & 31) < 16
        sw = jnp.where(first, pltpu.roll(yq, 1024 - 16, 1), pltpu.roll(yq, 16, 1))
        qk_ref[r0:r0 + nr, :] = (yq * c + sw * s).astype(BF16)
        vv = proj[:, N_NORM_COLS:] + vone_ref[...]
        for h in range(N_KV_B):
            v_ref[h, r0:r0 + nr, :] = vv[:, 128 * h:128 * (h + 1)].astype(BF16)


def _rope_tables(n_pos):
    t = jnp.arange(n_pos)
    row = (t // GRID_W).astype(F32)
    col = (t % GRID_W).astype(F32)
    n_freq = HEAD_DIM // 4
    inv_freq = ROPE_THETA ** (-jnp.arange(n_freq, dtype=F32) / n_freq)
    ar = row[:, None] * inv_freq
    ac = col[:, None] * inv_freq
    ang = jnp.concatenate([ar, ar, ac, ac], axis=1)
    sign = jnp.asarray(np.tile(np.repeat(np.array([-1.0, 1.0], np.float32), 16), 2))
    cos = jnp.cos(ang)
    sin = jnp.sin(ang) * sign
    return jnp.concatenate([cos, cos], axis=1), jnp.concatenate([sin, sin], axis=1)


def _pos_block_map(classes, ts):
    bounds = []
    tile0 = 0
    for nseq, S, _ in classes:
        n_tiles = nseq * S // ts
        bounds.append((tile0, tile0 + n_tiles, S // ts))
        tile0 += n_tiles

    def fn(i):
        out = (i - bounds[-1][0]) % bounds[-1][2]
        for lo, hi, per in reversed(bounds[:-1]):
            out = jnp.where(i < hi, (i - lo) % per, out)
        return out

    return fn


def _inproj_even(x_parts, w_in, f_g, q_g, k_g, classes, ts):
    T = sum(p.shape[0] for p in x_parts)
    x_specs, bounds = _parts_specs(x_parts, ts)
    wf = w_in[:, :D_FOURIER]
    wq = w_in[:, D_FOURIER:D_FOURIER + 768].reshape(D_MODEL, N_KV_B, GRP_B * HEAD_DIM)
    wk = w_in[:, D_FOURIER + 768:D_FOURIER + 1024].reshape(D_MODEL, N_KV_B, HEAD_DIM)
    wv = w_in[:, D_FOURIER + 1024:].reshape(D_MODEL, N_KV_B, HEAD_DIM)
    wqk = jnp.concatenate([wk, wq], axis=2).reshape(D_MODEL, 1024)
    wvp = jnp.concatenate([wv, jnp.zeros_like(wv)], axis=2).reshape(D_MODEL, 512)
    w = jnp.concatenate([wf, wqk, wvp], axis=1).astype(BF16)
    qscale = HEAD_DIM ** -0.5 * LOG2E
    gqk = jnp.tile(jnp.concatenate([k_g, q_g * qscale, q_g * qscale, q_g * qscale]), N_KV_B)
    gain = jnp.concatenate([f_g, gqk])[None, :].astype(F32)
    seg_np = np.zeros((N_NORM_COLS, 128), np.float32)
    seg_np[np.arange(N_NORM_COLS), np.arange(N_NORM_COLS) // HEAD_DIM] = 1.0
    seg = jnp.asarray(seg_np, BF16)
    segt = jnp.asarray(np.concatenate([seg_np.T, seg_np.T], axis=0), BF16)
    max_s = max(S for _, S, _ in classes)
    cos, sin = _rope_tables(max_s)
    vone_np = np.zeros((1, 512), np.float32)
    for h in range(N_KV_B):
        vone_np[0, 128 * h + 64:128 * (h + 1)] = 1.0
    vone = jnp.asarray(vone_np)
    posmap = _pos_block_map(classes, ts)
    n_w = w.shape[1]
    return pl.pallas_call(
        functools.partial(_inproj_even_kernel, nx=len(x_parts), bounds=bounds),
        grid=(T // ts,),
        in_specs=x_specs + [
            pl.BlockSpec((D_MODEL, n_w), lambda i: (0, 0)),
            pl.BlockSpec((1, N_NORM_COLS), lambda i: (0, 0)),
            pl.BlockSpec((N_NORM_COLS, 128), lambda i: (0, 0)),
            pl.BlockSpec((256, N_NORM_COLS), lambda i: (0, 0)),
            pl.BlockSpec((ts, 128), lambda i: (posmap(i), 0)),
            pl.BlockSpec((ts, 128), lambda i: (posmap(i), 0)),
            pl.BlockSpec((1, 512), lambda i: (0, 0)),
        ],
        out_specs=[
            pl.BlockSpec((ts, D_FOURIER), lambda i: (i, 0)),
            pl.BlockSpec((ts, 1024), lambda i: (i, 0)),
            pl.BlockSpec((N_KV_B, ts, 128), lambda i: (0, i, 0)),
        ],
        out_shape=[
            jax.ShapeDtypeStruct((T, D_FOURIER), BF16),
            jax.ShapeDtypeStruct((T, 1024), BF16),
            jax.ShapeDtypeStruct((N_KV_B, T, 128), BF16),
        ],
        compiler_params=_cparams(("parallel",)),
        name="inproj_even",
    )(*x_parts, w, gain, seg, segt, cos, sin, vone)


def _fft1_kernel(a_ref, w1_ref, twr_ref, twi_ref, zr_ref, zi_ref, *, s1):
    z = jnp.dot(w1_ref[...], a_ref[0], preferred_element_type=F32)
    zr = z[:s1]
    zi = z[s1:]
    twr = twr_ref[...]
    twi = twi_ref[...]
    zr_ref[0] = (zr * twr - zi * twi).astype(BF16)
    zi_ref[0] = (zr * twi + zi * twr).astype(BF16)


def _fft2_kernel(zr_ref, zi_ref, w2a_ref, w2b_ref, mix_ref, o_ref, *, cb):
    s2 = FFT_S2
    for c in range(cb):
        pp = (jnp.dot(w2a_ref[...], zr_ref[0, c], preferred_element_type=F32)
              + jnp.dot(w2b_ref[...], zi_ref[0, c], preferred_element_type=F32))
        f = (jnp.dot(pp[:s2].astype(BF16), mix_ref[:D_FOURIER], preferred_element_type=F32)
             + jnp.dot(pp[s2:].astype(BF16), mix_ref[D_FOURIER:], preferred_element_type=F32))
        o_ref[0, :, D_FOURIER * c:D_FOURIER * (c + 1)] = f.astype(BF16)


def _dft_mats(n):
    k = jnp.arange(n, dtype=I32)
    ang = (2.0 * math.pi / n) * ((k[:, None] * k[None, :]) % n).astype(F32)
    return jnp.cos(ang), jnp.sin(ang)


def _fourier_mix(uf_part, nseq, S):
    s2 = FFT_S2
    s1 = S // s2
    ncol = s2 * D_FOURIER
    a = uf_part.reshape(nseq, s1, ncol)
    c1, sn1 = _dft_mats(s1)
    w1 = jnp.concatenate([c1, -sn1], axis=0).astype(BF16)
    cc = jnp.arange(s1, dtype=I32)[:, None]
    bb = jnp.arange(s2, dtype=I32)[None, :]
    ang = (2.0 * math.pi / S) * ((cc * bb) % S).astype(F32)
    twr = jnp.repeat(jnp.cos(ang), D_FOURIER, axis=1)
    twi = jnp.repeat(-jnp.sin(ang), D_FOURIER, axis=1)
    tn = min(ncol, 4096)
    zr, zi = pl.pallas_call(
        functools.partial(_fft1_kernel, s1=s1),
        grid=(ncol // tn, nseq),
        in_specs=[
            pl.BlockSpec((1, s1, tn), lambda j, b: (b, 0, j)),
            pl.BlockSpec((2 * s1, s1), lambda j, b: (0, 0)),
            pl.BlockSpec((s1, tn), lambda j, b: (0, j)),
            pl.BlockSpec((s1, tn), lambda j, b: (0, j)),
        ],
        out_specs=[
            pl.BlockSpec((1, s1, tn), lambda j, b: (b, 0, j)),
            pl.BlockSpec((1, s1, tn), lambda j, b: (b, 0, j)),
        ],
        out_shape=[jax.ShapeDtypeStruct((nseq, s1, ncol), BF16)] * 2,
        compiler_params=_cparams(("parallel", "parallel")),
        name="fourier_stage1",
    )(a, w1, twr, twi)
    zr = zr.reshape(nseq, s1, s2, D_FOURIER)
    zi = zi.reshape(nseq, s1, s2, D_FOURIER)
    c2, sn2 = _dft_mats(s2)
    w2a = jnp.concatenate([c2, -sn2], axis=0).astype(BF16)
    w2b = jnp.concatenate([sn2, c2], axis=0).astype(BF16)
    gc, gs = _dft_mats(HEAD_DIM)
    scale = 1.0 / math.sqrt(S * HEAD_DIM)
    eye = jnp.eye(N_FOURIER_GROUPS, dtype=F32)
    mix = (jnp.concatenate([jnp.kron(eye, gc), jnp.kron(eye, gs)], axis=0) * scale).astype(BF16)
    cb = min(8, s1)
    out = pl.pallas_call(
        functools.partial(_fft2_kernel, cb=cb),
        grid=(nseq, s1 // cb),
        in_specs=[
            pl.BlockSpec((1, cb, s2, D_FOURIER), lambda b, j: (b, j, 0, 0)),
            pl.BlockSpec((1, cb, s2, D_FOURIER), lambda b, j: (b, j, 0, 0)),
            pl.BlockSpec((2 * s2, s2), lambda b, j: (0, 0)),
            pl.BlockSpec((2 * s2, s2), lambda b, j: (0, 0)),
            pl.BlockSpec((2 * D_FOURIER, D_FOURIER), lambda b, j: (0, 0)),
        ],
        out_specs=pl.BlockSpec((1, s2, cb * D_FOURIER), lambda b, j: (b, 0, j)),
        out_shape=jax.ShapeDtypeStruct((nseq, s2, s1 * D_FOURIER), BF16),
        compiler_params=_cparams(("parallel", "parallel")),
        name="fourier_stage2",
    )(zr, zi, w2a, w2b, mix)
    return out.reshape(nseq * S, D_FOURIER)


SCORE_BOUND_MAX = 100.0


def _gattn_kernel(bounded_ref, q_ref, k_ref, v_ref, place_ref, o_ref, *, tq, tk, S):
    q3 = q_ref[...]
    qs = [q3[:, HEAD_DIM * (g + 1):HEAD_DIM * (g + 2)] for g in range(GRP_B)]

    def chunk(j):
        off = pl.multiple_of(j * tk, tk)
        return k_ref[pl.ds(off, tk), 0:HEAD_DIM], v_ref[0, pl.ds(off, tk), :]

    def scores(g, kc):
        return lax.dot_general(qs[g], kc, (((1,), (1,)), ((), ())), preferred_element_type=F32)

    def finish(accs):
        out = jnp.zeros((tq, 256), F32)
        for g in range(GRP_B):
            o = (accs[g] / accs[g][:, HEAD_DIM:HEAD_DIM + 1]).astype(BF16)
            out = out + jnp.dot(o, place_ref[g], preferred_element_type=F32)
        o_ref[...] = out.astype(BF16)

    @pl.when(bounded_ref[0] == 1)
    def _():
        def body(j, accs):
            kc, vc = chunk(j)
            return tuple(accs[g] + jnp.dot(jnp.exp2(scores(g, kc)).astype(BF16), vc,
                                           preferred_element_type=F32) for g in range(GRP_B))
        finish(lax.fori_loop(0, S // tk, body, tuple(jnp.zeros((tq, 128), F32) for _ in range(GRP_B))))

    @pl.when(bounded_ref[0] != 1)
    def _():
        def body(j, carry):
            kc, vc = chunk(j)
            new = []
            for g in range(GRP_B):
                m_prev, acc = carry[g]
                s = scores(g, kc)
                m_new = jnp.maximum(m_prev, jnp.max(s, axis=1, keepdims=True))
                p = jnp.exp2(s - m_new).astype(BF16)
                acc = jnp.exp2(m_prev - m_new) * acc + jnp.dot(p, vc, preferred_element_type=F32)
                new.append((m_new, acc))
            return tuple(new)
        init = tuple((jnp.full((tq, 1), NEG_INF, F32), jnp.zeros((tq, 128), F32)) for _ in range(GRP_B))
        fin = lax.fori_loop(0, S // tk, body, init)
        finish([fin[g][1] for g in range(GRP_B)])


def _score_bounded(q_g, k_g):
    bound = 1.05 * HEAD_DIM * (HEAD_DIM ** -0.5 * LOG2E) * jnp.max(jnp.abs(q_g)) * jnp.max(jnp.abs(k_g))
    return (bound <= SCORE_BOUND_MAX).astype(I32)[None]


def _lane_place(n_src, n_dst, groups):
    pm = np.zeros((len(groups), n_src, n_dst), np.float32)
    for gi, slot in enumerate(groups):
        pm[gi, np.arange(HEAD_DIM), HEAD_DIM * slot + np.arange(HEAD_DIM)] = 1.0
    return jnp.asarray(pm, BF16)


def _global_attention(qk, v_aug, bounded, nseq, S, tok0, tq, tk):
    n_qt = S // tq
    qt0 = tok0 // tq
    s0 = tok0 // S
    place = _lane_place(128, 256, [1, 2, 3])
    grid_spec = pltpu.PrefetchScalarGridSpec(
        num_scalar_prefetch=1,
        grid=(nseq, N_KV_B, n_qt),
        in_specs=[
            pl.BlockSpec((tq, 256), lambda b, h, i, fl: (qt0 + b * n_qt + i, h)),
            pl.BlockSpec((S, 256), lambda b, h, i, fl: (s0 + b, h)),
            pl.BlockSpec((1, S, 128), lambda b, h, i, fl: (h, s0 + b, 0)),
            pl.BlockSpec((GRP_B, 128, 256), lambda b, h, i, fl: (0, 0, 0)),
        ],
        out_specs=pl.BlockSpec((tq, 256), lambda b, h, i, fl: (b * n_qt + i, h)),
    )
    return pl.pallas_call(
        functools.partial(_gattn_kernel, tq=tq, tk=tk, S=S),
        grid_spec=grid_spec,
        out_shape=jax.ShapeDtypeStruct((nseq * S, 1024), BF16),
        compiler_params=_cparams(("parallel", "parallel", "arbitrary")),
        name="global_attention",
    )(bounded, qk, qk, v_aug, place)


def _route(x1, rw_ref, rb_ref):
    n = x1.shape[0]
    xh = x1.astype(BF16)
    xl = (x1 - xh.astype(F32)).astype(BF16)
    o = jnp.dot(jnp.concatenate([xh, xl], axis=0), rw_ref[...], preferred_element_type=F32)
    lg = o[:n, :128] + o[:n, 128:] + o[n:, :128]
    lgt = lg.T[:N_EXPERTS]
    sc = 1.0 / (1.0 + jnp.exp(-lgt))
    bi = sc + rb_ref[...]
    srow = [sc[e:e + 1] for e in range(N_EXPERTS)]
    brow = [bi[e:e + 1] for e in range(N_EXPERTS)]
    gsel = None
    best = None
    for g in range(N_GROUPS):
        a, b, c, d = brow[4 * g:4 * g + 4]
        m1, n1 = jnp.maximum(a, b), jnp.minimum(a, b)
        m2, n2 = jnp.maximum(c, d), jnp.minimum(c, d)
        gs = jnp.maximum(m1, m2) + jnp.maximum(jnp.minimum(m1, m2), jnp.maximum(n1, n2))
        if g == 0:
            best, gsel = gs, jnp.zeros(gs.shape, I32)
        else:
            better = gs > best
            gsel = jnp.where(better, g, gsel)
            best = jnp.where(better, gs, best)
    masked = [jnp.where(gsel == (e // EXPERTS_PER_GROUP), brow[e], NEG_INF) for e in range(N_EXPERTS)]
    i1 = jnp.zeros(gsel.shape, I32)
    b1 = masked[0]
    s1 = srow[0]
    for e in range(1, N_EXPERTS):
        better = masked[e] > b1
        i1 = jnp.where(better, e, i1)
        b1 = jnp.where(better, masked[e], b1)
        s1 = jnp.where(better, srow[e], s1)
    i2 = jnp.full(gsel.shape, -1, I32)
    b2 = jnp.full(b1.shape, -jnp.inf, F32)
    s2 = jnp.zeros(b1.shape, F32)
    for e in range(N_EXPERTS):
        better = (masked[e] > b2) & (i1 != e)
        i2 = jnp.where(better, e, i2)
        b2 = jnp.where(better, masked[e], b2)
        s2 = jnp.where(better, srow[e], s2)
    den = s1 + s2
    return i1, i2, s1 / den, s2 / den


ROW_SPLITS = 2


def _norm_route_store(r0, nr, m, x, g_ref, b_ref, rw_ref, rb_ref, x1_ref, idx_ref, wt_ref, xb_ref, alpha):
    x1 = _layer_norm(alpha * x + m, g_ref[...], b_ref[...])
    x1_ref[r0:r0 + nr, :] = x1
    xb_ref[r0:r0 + nr, :] = x1.astype(BF16)
    i1, i2, w1, w2 = _route(x1, rw_ref, rb_ref)
    rid = lax.broadcasted_iota(I32, (8, nr), 0)
    idx_ref[:, r0:r0 + nr] = jnp.where(rid == 0, i1, jnp.where(rid == 1, i2, 0))
    wt_ref[:, r0:r0 + nr] = jnp.where(rid == 0, w1, jnp.where(rid == 1, w2, 0.0))


def _router_operands(router_w, router_b):
    rw = jnp.zeros((D_MODEL, 128), F32).at[:, :N_EXPERTS].set(router_w.astype(F32))
    rwh = rw.astype(BF16)
    rwl = (rw - rwh.astype(F32)).astype(BF16)
    rb = router_b.astype(F32)[:, None]
    return jnp.concatenate([rwh, rwl], axis=1), rb


_ROUTER_SPECS = [
    pl.BlockSpec((D_MODEL, 256), lambda i: (0, 0)),
    pl.BlockSpec((N_EXPERTS, 1), lambda i: (0, 0)),
]


def _mix_out_specs(ts, T):
    specs = [
        pl.BlockSpec((ts, D_MODEL), lambda i: (i, 0)),
        pl.BlockSpec((8, ts), lambda i: (0, i)),
        pl.BlockSpec((8, ts), lambda i: (0, i)),
        pl.BlockSpec((ts, D_MODEL), lambda i: (i, 0)),
    ]
    shapes = [
        jax.ShapeDtypeStruct((T, D_MODEL), F32),
        jax.ShapeDtypeStruct((8, T), I32),
        jax.ShapeDtypeStruct((8, T), F32),
        jax.ShapeDtypeStruct((T, D_MODEL), BF16),
    ]
    return specs, shapes


def _outproj_even_kernel(*refs, nf, na, nx, fb, ab, xb, alpha):
    f_refs, a_refs, x_refs = refs[:nf], refs[nf:nf + na], refs[nf + na:nf + na + nx]
    wf_ref, wa_ref, g_ref, b_ref, rw_ref, rb_ref, x1_ref, idx_ref, wt_ref, xb_ref = refs[nf + na + nx:]
    nr = x1_ref.shape[0] // ROW_SPLITS
    for h in range(ROW_SPLITS):
        r0 = h * nr
        m = (jnp.dot(_pick_rows(f_refs, fb, r0, nr), wf_ref[...], preferred_element_type=F32)
             + jnp.dot(_pick_rows(a_refs, ab, r0, nr), wa_ref[...], preferred_element_type=F32))
        xr = _pick_rows(x_refs, xb, r0, nr)
        _norm_route_store(r0, nr, m, xr, g_ref, b_ref, rw_ref, rb_ref, x1_ref, idx_ref, wt_ref, xb_ref, alpha)


def _outproj_even(f_parts, a_parts, x_parts, w_out, ln_g, ln_b, router, alpha, ts):
    T = sum(p.shape[0] for p in x_parts)
    f_specs, fb = _parts_specs(f_parts, ts)
    a_specs, ab = _parts_specs(a_parts, ts)
    x_specs, xb = _parts_specs(x_parts, ts)
    wf = w_out[:D_FOURIER].astype(BF16)
    wa = w_out[D_FOURIER:].reshape(N_KV_B, GRP_B * HEAD_DIM, D_MODEL)
    wa = jnp.concatenate([jnp.zeros((N_KV_B, HEAD_DIM, D_MODEL), w_out.dtype), wa], axis=1)
    wa = wa.reshape(1024, D_MODEL).astype(BF16)
    out_specs, out_shapes = _mix_out_specs(ts, T)
    return pl.pallas_call(
        functools.partial(_outproj_even_kernel, nf=len(f_parts), na=len(a_parts), nx=len(x_parts),
                          fb=fb, ab=ab, xb=xb, alpha=alpha),
        grid=(T // ts,),
        in_specs=f_specs + a_specs + x_specs + [
            pl.BlockSpec((D_FOURIER, D_MODEL), lambda i: (0, 0)),
            pl.BlockSpec((1024, D_MODEL), lambda i: (0, 0)),
            pl.BlockSpec((1, D_MODEL), lambda i: (0, 0)),
            pl.BlockSpec((1, D_MODEL), lambda i: (0, 0)),
        ] + _ROUTER_SPECS,
        out_specs=out_specs,
        out_shape=out_shapes,
        compiler_params=_cparams(("parallel",)),
        name="outproj_even",
    )(*f_parts, *a_parts, *x_parts, wf, wa, ln_g[None, :], ln_b[None, :], *router)


CHUNK = 16


def _stage_rows(td):
    return -(-(2 * td + N_EXPERTS * (CHUNK - 1)) // 256) * 256


def _tab_len(td):
    return -(-(1 + 2 * td // CHUNK + N_EXPERTS) // 128) * 128


def _moe_plan(idx, td, tm):
    T = idx.shape[1]
    nt = T // td
    e = idx.T.reshape(-1)
    oh = (e[:, None] == jnp.arange(N_EXPERTS, dtype=I32)[None, :]).astype(I32)
    csum = jnp.cumsum(oh, axis=0)
    tile_end = csum.reshape(nt, 2 * td, N_EXPERTS)[:, -1, :]
    base = jnp.concatenate([jnp.zeros((1, N_EXPERTS), I32), tile_end[:-1]], axis=0)
    cnt = tile_end - base
    c8 = ((cnt + CHUNK - 1) // CHUNK) * CHUNK
    off8 = jnp.cumsum(c8, axis=1) - c8
    base8 = jnp.cumsum(c8, axis=0) - c8
    seg = jnp.sum(c8, axis=0)
    padded = ((seg + tm - 1) // tm) * tm
    ends = jnp.cumsum(padded)
    starts = ends - padded
    dst = starts[None, :] + base8
    shift = jnp.repeat(off8 - base, 2 * td, axis=0)
    lpos = (jnp.sum((csum + shift) * oh, axis=1) - 1).astype(I32)
    n_rows = -(-(2 * T + nt * N_EXPERTS * (CHUNK - 1)) // tm) * tm + N_EXPERTS * tm
    tile_start = jnp.arange(n_rows // tm, dtype=I32) * tm
    tile_e = jnp.minimum(jnp.sum((ends[None, :] <= tile_start[:, None]).astype(I32), axis=1), N_EXPERTS - 1)
    n_used = (ends[-1] // tm).astype(I32)[None]
    nc = c8 // CHUNK
    cum = jnp.cumsum(nc, axis=1)
    n_chunk_max = 2 * td // CHUNK + N_EXPERTS
    c_idx = jnp.arange(n_chunk_max, dtype=I32)
    e_c = jnp.minimum(jnp.sum((cum[:, None, :] <= c_idx[None, :, None]).astype(I32), axis=2), N_EXPERTS - 1)
    oh_c = (e_c[:, :, None] == jnp.arange(N_EXPERTS, dtype=I32)[None, None, :]).astype(I32)
    srow = jnp.sum(oh_c * (dst - CHUNK * (cum - nc))[:, None, :], axis=2) + CHUNK * c_idx[None, :]
    tab = jnp.concatenate([cum[:, -1:], srow], axis=1).astype(I32)
    tab = jnp.pad(tab, ((0, 0), (0, _tab_len(td) - tab.shape[1]))).reshape(nt, 1, _tab_len(td))
    pad_lo = jnp.concatenate([starts + seg, ends[-1:]]).astype(I32)
    pad_hi = jnp.concatenate([ends, jnp.full((1,), n_rows, I32)]).astype(I32)
    lpos_tk = lpos.reshape(T, 2)
    lpos_rows = jnp.zeros((8, T), I32).at[:2].set(lpos_tk.T)
    return tab, lpos_rows, lpos_tk, tile_e.astype(I32), n_used, pad_lo, pad_hi, n_rows


def _chunk_loops(tab_ref, fn):
    total = tab_ref[0, 0, 0]

    def one(c, priority):
        fn(pl.multiple_of(CHUNK * c, CHUNK), pl.multiple_of(tab_ref[0, 0, 1 + c], CHUNK), priority)

    def body(j, carry):
        one(2 * j, 0)
        one(2 * j + 1, 1)
        return carry
    lax.fori_loop(0, total // 2, body, 0)

    @pl.when(total % 2 == 1)
    def _():
        one(total - 1, 0)
    return total


def _dispatch_kernel(lo_ref, hi_ref, tab_ref, lp_ref, x_ref, xs_hbm, stage, zbuf, nwait, sem, zsem, *, ns):
    i = pl.program_id(0)
    slot = i % 2

    def zero_copy(dst):
        return pltpu.make_async_copy(zbuf, xs_hbm.at[pl.ds(pl.multiple_of(dst, CHUNK), CHUNK)], zsem)

    @pl.when(i == 0)
    def _():
        nwait[0] = 0
        nwait[1] = 0
        zbuf[...] = jnp.zeros(zbuf.shape, BF16)
        for e in range(N_EXPERTS + 1):
            nz = (hi_ref[e] - lo_ref[e]) // CHUNK
            lax.fori_loop(0, nz, lambda j, c, e=e: (zero_copy(lo_ref[e] + CHUNK * j).start(), c)[1], 0)
        for e in range(N_EXPERTS + 1):
            nz = (hi_ref[e] - lo_ref[e]) // CHUNK
            lax.fori_loop(0, nz, lambda j, c: (zero_copy(0).wait(), c)[1], 0)

    def piece(local_row, sorted_row, s):
        return pltpu.make_async_copy(stage.at[s, pl.ds(local_row, CHUNK)],
                                     xs_hbm.at[pl.ds(sorted_row, CHUNK)], sem.at[s])

    def drain(s):
        lax.fori_loop(0, nwait[s], lambda j, c: (piece(0, 0, s).wait(), c)[1], 0)

    drain(slot)
    lp = lp_ref[...]
    rows = lax.broadcasted_iota(I32, (ns, lp.shape[1]), 0)
    perm = jnp.where((rows == lp[0:1]) | (rows == lp[1:2]), 1.0, 0.0).astype(BF16)
    stage[slot] = jnp.dot(perm, x_ref[...], preferred_element_type=F32).astype(BF16)
    nwait[slot] = _chunk_loops(tab_ref, lambda lr, sr, pr: piece(lr, sr, slot).start(priority=pr))

    @pl.when(i == pl.num_programs(0) - 1)
    def _():
        drain(0)
        drain(1)


def _moe_dispatch(x1, tab, lpos_rows, pad_lo, pad_hi, n_rows, td):
    T = x1.shape[0]
    ns = _stage_rows(td)
    grid_spec = pltpu.PrefetchScalarGridSpec(
        num_scalar_prefetch=2,
        grid=(T // td,),
        in_specs=[
            pl.BlockSpec((1, 1, _tab_len(td)), lambda i, lo, hi: (i, 0, 0), memory_space=pltpu.SMEM),
            pl.BlockSpec((8, td), lambda i, lo, hi: (0, i)),
            pl.BlockSpec((td, D_MODEL), lambda i, lo, hi: (i, 0)),
        ],
        out_specs=pl.BlockSpec(memory_space=pl.ANY),
        scratch_shapes=[pltpu.VMEM((2, ns, D_MODEL), BF16), pltpu.VMEM((CHUNK, D_MODEL), BF16),
                        pltpu.SMEM((2,), I32), pltpu.SemaphoreType.DMA((2,)), pltpu.SemaphoreType.DMA(())],
    )
    return pl.pallas_call(
        functools.partial(_dispatch_kernel, ns=ns),
        grid_spec=grid_spec,
        out_shape=jax.ShapeDtypeStruct((n_rows, D_MODEL), BF16),
        compiler_params=_cparams(("arbitrary",)),
        name="moe_dispatch",
    )(pad_lo, pad_hi, tab, lpos_rows, x1)


def _moe_kernel(te_ref, nu_ref, x_ref, wg_ref, wu_ref, wd_ref, y_ref, wgb, wub, wdb):
    i = pl.program_id(0)

    @pl.when((i == 0) | (te_ref[i] != te_ref[jnp.maximum(i - 1, 0)]))
    def _():
        wgb[...] = wg_ref[0].astype(BF16)
        wub[...] = wu_ref[0].astype(BF16)
        wdb[...] = wd_ref[0].astype(BF16)

    @pl.when(i < nu_ref[0])
    def _():
        xg = x_ref[...]
        hg = jnp.dot(xg, wgb[...], preferred_element_type=F32)
        hu = jnp.dot(xg, wub[...], preferred_element_type=F32)
        hdn = (hg / (1.0 + jnp.exp(-hg)) * hu).astype(BF16)
        y_ref[...] = jnp.dot(hdn, wdb[...], preferred_element_type=F32).astype(BF16)

    @pl.when(i >= nu_ref[0])
    def _():
        y_ref[...] = jnp.zeros(y_ref.shape, BF16)


def _moe_experts(xs, tile_e, n_used, wg, wu, wd, tm):
    n_rows = xs.shape[0]
    grid_spec = pltpu.PrefetchScalarGridSpec(
        num_scalar_prefetch=2,
        grid=(n_rows // tm,),
        in_specs=[
            pl.BlockSpec((tm, D_MODEL), lambda i, te, nu: (i, 0)),
            pl.BlockSpec((1, D_MODEL, D_FF), lambda i, te, nu: (te[i], 0, 0)),
            pl.BlockSpec((1, D_MODEL, D_FF), lambda i, te, nu: (te[i], 0, 0)),
            pl.BlockSpec((1, D_FF, D_MODEL), lambda i, te, nu: (te[i], 0, 0)),
        ],
        out_specs=pl.BlockSpec((tm, D_MODEL), lambda i, te, nu: (i, 0)),
        scratch_shapes=[pltpu.VMEM((D_MODEL, D_FF), BF16), pltpu.VMEM((D_MODEL, D_FF), BF16),
                        pltpu.VMEM((D_FF, D_MODEL), BF16)],
    )
    return pl.pallas_call(
        _moe_kernel,
        grid_spec=grid_spec,
        out_shape=jax.ShapeDtypeStruct((n_rows, D_MODEL), BF16),
        compiler_params=_cparams(("arbitrary",)),
        name="moe_experts",
    )(tile_e, n_used, xs, wg, wu, wd)


def _combine_kernel(tab_ref, tabn_ref, lp_ref, y_hbm, x_ref, wt_ref, g_ref, b_ref, *rest, ns, alpha, bounds):
    o_refs, (ystage, sem) = rest[:len(bounds)], rest[len(bounds):]
    i = pl.program_id(0)
    n = pl.num_programs(0)
    slot = i % 2

    def piece(local_row, sorted_row, s):
        return pltpu.make_async_copy(y_hbm.at[pl.ds(sorted_row, CHUNK)],
                                     ystage.at[s, pl.ds(local_row, CHUNK)], sem.at[s])

    @pl.when(i == 0)
    def _():
        ystage[...] = jnp.zeros(ystage.shape, BF16)
        _chunk_loops(tab_ref, lambda lr, sr, pr: piece(lr, sr, 0).start(priority=pr))

    @pl.when(i + 1 < n)
    def _():
        _chunk_loops(tabn_ref, lambda lr, sr, pr: piece(lr, sr, 1 - slot).start(priority=pr))

    lax.fori_loop(0, tab_ref[0, 0, 0], lambda j, c: (piece(0, 0, slot).wait(), c)[1], 0)

    ys = ystage[slot]
    lp = lp_ref[...]
    cols = lax.broadcasted_iota(I32, (lp.shape[0], ns), 1)
    w = wt_ref[...]
    tc = lp.shape[0]
    sel = jnp.concatenate([jnp.where(cols == lp[:, k:k + 1], 1.0, 0.0).astype(BF16) for k in range(2)], axis=0)
    picked = jnp.dot(sel, ys, preferred_element_type=F32)
    f = w[:, 0:1] * picked[:tc] + w[:, 1:2] * picked[tc:]
    o = _layer_norm(alpha * x_ref[...] + f, g_ref[...], b_ref[...])
    if len(bounds) == 1:
        o_refs[0][...] = o
    else:
        lo = 0
        for o_ref, hi in zip(o_refs, bounds):
            @pl.when((i >= lo) & (i < hi))
            def _(o_ref=o_ref):
                o_ref[...] = o
            lo = hi


def _moe_combine(y, tab, lpos_tk, wts, x1, ln_g, ln_b, alpha, tc, out_rows):
    T = x1.shape[0]
    n_tiles = T // tc
    ns = _stage_rows(tc)
    out_specs, out_shapes, bounds, lo = [], [], [], 0
    for rows in out_rows:
        n = rows // tc
        out_specs.append(pl.BlockSpec((tc, D_MODEL), lambda i, lo=lo, n=n: (jnp.clip(i - lo, 0, n - 1), 0)))
        out_shapes.append(jax.ShapeDtypeStruct((rows, D_MODEL), F32))
        lo += n
        bounds.append(lo)
    return pl.pallas_call(
        functools.partial(_combine_kernel, ns=ns, alpha=alpha, bounds=tuple(bounds)),
        grid=(n_tiles,),
        in_specs=[
            pl.BlockSpec((1, 1, _tab_len(tc)), lambda i: (i, 0, 0), memory_space=pltpu.SMEM),
            pl.BlockSpec((1, 1, _tab_len(tc)), lambda i: (jnp.minimum(i + 1, n_tiles - 1), 0, 0),
                         memory_space=pltpu.SMEM),
            pl.BlockSpec((tc, 2), lambda i: (i, 0)),
            pl.BlockSpec(memory_space=pl.ANY),
            pl.BlockSpec((tc, D_MODEL), lambda i: (i, 0)),
            pl.BlockSpec((tc, 2), lambda i: (i, 0)),
            pl.BlockSpec((1, D_MODEL), lambda i: (0, 0)),
            pl.BlockSpec((1, D_MODEL), lambda i: (0, 0)),
        ],
        out_specs=out_specs,
        out_shape=out_shapes,
        scratch_shapes=[pltpu.VMEM((2, ns, D_MODEL), BF16), pltpu.SemaphoreType.DMA((2,))],
        compiler_params=_cparams(("arbitrary",)),
        name="moe_combine",
    )(tab, tab, lpos_tk, y, x1, wts, ln_g[None, :], ln_b[None, :])


def _moe_layer(x1, xb, idx, wt, wg, wu, wd, ln_g, ln_b, alpha, tm, tc, out_rows):
    tab, lpos_rows, lpos_tk, tile_e, n_used, pad_lo, pad_hi, n_rows = _moe_plan(idx[:2], tc, tm)
    xs = _moe_dispatch(xb, tab, lpos_rows, pad_lo, pad_hi, n_rows, tc)
    y = _moe_experts(xs, tile_e, n_used, wg, wu, wd, tm)
    return _moe_combine(y, tab, lpos_tk, wt[:2].T, x1, ln_g, ln_b, alpha, tc, out_rows)


def _inproj_odd_kernel(x_ref, w_ref, vone_ref, bg_ref, u_ref, q_ref, k_ref, v_ref, nrm_ref):
    x = x_ref[...].astype(BF16)
    proj = jnp.dot(x, w_ref[...], preferred_element_type=F32)
    bg_ref[...] = proj[:, :512].astype(BF16)
    u_ref[...] = (proj[:, 512:1024] * proj[:, 1024:1536]).astype(BF16)
    qb = (proj[:, 1536:2048] * (HEAD_DIM ** -0.5 * LOG2E)).astype(BF16)
    kb = proj[:, 2048:2304].astype(BF16)
    q_ref[...] = qb
    k_ref[...] = kb
    v_ref[...] = (proj[:, 2304:2560] + vone_ref[...]).astype(BF16)
    qf = qb.astype(F32)
    kf = kb.astype(F32)
    qq = jnp.max(jnp.sum(qf * qf, axis=1, keepdims=True), axis=0, keepdims=True)
    kk = jnp.max(jnp.sum(kf * kf, axis=1, keepdims=True), axis=0, keepdims=True)
    rid = lax.broadcasted_iota(I32, (8, 128), 0)
    nrm_ref[0] = jnp.where(rid == 0, qq, jnp.where(rid == 1, kk, 0.0))


def _inproj_odd(x, w_in, ts):
    T = x.shape[0]
    c3 = 3 * D_CONV
    wk = w_in[:, c3 + 512:c3 + 640].reshape(D_MODEL, N_KV_D, HEAD_DIM)
    wv = w_in[:, c3 + 640:c3 + 768].reshape(D_MODEL, N_KV_D, HEAD_DIM)
    wkp = jnp.concatenate([wk, jnp.zeros_like(wk)], axis=2).reshape(D_MODEL, 256)
    wvp = jnp.concatenate([wv, jnp.zeros_like(wv)], axis=2).reshape(D_MODEL, 256)
    w = jnp.concatenate([w_in[:, :c3 + 512], wkp, wvp], axis=1).astype(BF16)
    vone_np = np.zeros((1, 256), np.float32)
    for h in range(N_KV_D):
        vone_np[0, 128 * h + 64:128 * (h + 1)] = 1.0
    n_w = w.shape[1]
    widths = [512, 512, 512, 256, 256]
    bg, u, q, kw, vw, nrm = pl.pallas_call(
        _inproj_odd_kernel,
        grid=(T // ts,),
        in_specs=[
            pl.BlockSpec((ts, D_MODEL), lambda i: (i, 0)),
            pl.BlockSpec((D_MODEL, n_w), lambda i: (0, 0)),
            pl.BlockSpec((1, 256), lambda i: (0, 0)),
        ],
        out_specs=[pl.BlockSpec((ts, n), lambda i: (i, 0)) for n in widths]
        + [pl.BlockSpec((1, 8, 128), lambda i: (i, 0, 0))],
        out_shape=[jax.ShapeDtypeStruct((T, n), BF16) for n in widths]
        + [jax.ShapeDtypeStruct((T // ts, 8, 128), F32)],
        compiler_params=_cparams(("parallel",)),
        name="inproj_odd",
    )(x, w, jnp.asarray(vone_np))
    score_bound = jnp.sqrt(jnp.max(nrm[:, 0, 0]) * jnp.max(nrm[:, 1, 0]))
    return bg, u, q, kw, vw, score_bound


def _wattn_kernel(bounded_ref, q_ref, kp_ref, kc_ref, kn_ref, vp_ref, vc_ref, vn_ref, bias_ref, sink_ref,
                  place_ref, o_ref, *, tq, seq_tiles):
    i = pl.program_id(0)
    first = i < 0
    last = i < 0
    for lo, hi, per in seq_tiles:
        inside = (i >= lo) & (i < hi)
        first = first | (inside & ((i - lo) % per == 0))
        last = last | (inside & ((i - lo) % per == per - 1))
    kfull = jnp.concatenate([kp_ref[...], kc_ref[...], kn_ref[...]], axis=0)[:, :HEAD_DIM]
    vfull = jnp.concatenate([vp_ref[...], vc_ref[...], vn_ref[...]], axis=0)
    q4 = q_ref[...]
    bias = bias_ref[0]
    sink = sink_ref[0]
    nb = tq // Q_BLOCK
    col = lax.broadcasted_iota(I32, (GRP_D * Q_BLOCK, 3 * Q_BLOCK), 1)

    def blocks(with_max):
        for n in range(nb):
            qs = jnp.concatenate([q4[Q_BLOCK * n:Q_BLOCK * (n + 1), HEAD_DIM * g:HEAD_DIM * (g + 1)]
                                  for g in range(GRP_D)], axis=0)
            keys = kfull[Q_BLOCK * n:Q_BLOCK * (n + 3)]
            vals = vfull[Q_BLOCK * n:Q_BLOCK * (n + 3)]
            s = lax.dot_general(qs, keys, (((1,), (1,)), ((), ())), preferred_element_type=F32) + bias
            if n == 0:
                s = jnp.where(first & (col < Q_BLOCK), NEG_INF, s)
            if n == nb - 1:
                s = jnp.where(last & (col >= 2 * Q_BLOCK), NEG_INF, s)
            if with_max:
                m = jnp.maximum(jnp.max(s, axis=1, keepdims=True), sink)
                s = s - m
                snk = sink - m
            else:
                snk = sink
            acc = jnp.dot(jnp.exp2(s).astype(BF16), vals, preferred_element_type=F32)
            den = acc[:, HEAD_DIM:HEAD_DIM + 1] + jnp.exp2(snk)
            o = (acc / den).astype(BF16)
            out = jnp.zeros((Q_BLOCK, 256), F32)
            for g in range(GRP_D):
                out = out + jnp.dot(o[Q_BLOCK * g:Q_BLOCK * (g + 1)], place_ref[g],
                                    preferred_element_type=F32)
            o_ref[Q_BLOCK * n:Q_BLOCK * (n + 1), :] = out.astype(BF16)

    @pl.when(bounded_ref[0] == 1)
    def _():
        blocks(False)

    @pl.when(bounded_ref[0] != 1)
    def _():
        blocks(True)


WINDOW_BOUND_MAX = 90.0


def _window_attention(q, kw, vw, sink_logits, score_bound, classes, tq):
    T = q.shape[0]
    sink_bound = jnp.max(jnp.abs(sink_logits.astype(F32))) * LOG2E
    bounded = ((1.02 * score_bound <= WINDOW_BOUND_MAX) & (sink_bound <= WINDOW_BOUND_MAX)).astype(I32)[None]
    n_tiles = T // tq
    hb = tq // Q_BLOCK
    n_hblk = T // Q_BLOCK
    r = jnp.arange(Q_BLOCK, dtype=I32)[:, None]
    j = jnp.arange(3 * Q_BLOCK, dtype=I32)[None, :]
    rel = jnp.abs(j - Q_BLOCK - r).astype(F32)
    slopes = jnp.asarray(np.array([2.0 ** (-8.0 * (h + 1) / N_HEADS_D) for h in range(N_HEADS_D)], np.float32))
    bias = jnp.where(rel[None] <= WINDOW, -slopes[:, None, None] * rel[None] * LOG2E, NEG_INF)
    bias = bias.reshape(N_KV_D, GRP_D * Q_BLOCK, 3 * Q_BLOCK)
    sink = jnp.repeat(sink_logits.astype(F32) * LOG2E, Q_BLOCK).reshape(N_KV_D, GRP_D * Q_BLOCK, 1)
    place = _lane_place(128, 256, [0, 1, 2, 3])
    seq_tiles = []
    t0 = 0
    for nseq, S, _ in classes:
        cnt = nseq * S // tq
        seq_tiles.append((t0, t0 + cnt, S // tq))
        t0 += cnt
    prev_map = lambda i, h, fl: (jnp.maximum(i * hb - 1, 0), h)
    next_map = lambda i, h, fl: (jnp.minimum((i + 1) * hb, n_hblk - 1), h)
    grid_spec = pltpu.PrefetchScalarGridSpec(
        num_scalar_prefetch=1,
        grid=(n_tiles, N_KV_D),
        in_specs=[
            pl.BlockSpec((tq, 256), lambda i, h, fl: (i, h)),
            pl.BlockSpec((Q_BLOCK, 128), prev_map),
            pl.BlockSpec((tq, 128), lambda i, h, fl: (i, h)),
            pl.BlockSpec((Q_BLOCK, 128), next_map),
            pl.BlockSpec((Q_BLOCK, 128), prev_map),
            pl.BlockSpec((tq, 128), lambda i, h, fl: (i, h)),
            pl.BlockSpec((Q_BLOCK, 128), next_map),
            pl.BlockSpec((1, GRP_D * Q_BLOCK, 3 * Q_BLOCK), lambda i, h, fl: (h, 0, 0)),
            pl.BlockSpec((1, GRP_D * Q_BLOCK, 1), lambda i, h, fl: (h, 0, 0)),
            pl.BlockSpec((GRP_D, 128, 256), lambda i, h, fl: (0, 0, 0)),
        ],
        out_specs=pl.BlockSpec((tq, 256), lambda i, h, fl: (i, h)),
    )
    return pl.pallas_call(
        functools.partial(_wattn_kernel, tq=tq, seq_tiles=tuple(seq_tiles)),
        grid_spec=grid_spec,
        out_shape=jax.ShapeDtypeStruct((T, 512), BF16),
        compiler_params=_cparams(("parallel", "parallel")),
        name="window_attention",
    )(bounded, q, kw, kw, kw, vw, vw, vw, bias, sink, place)


HALO = 16


def _outproj_odd_kernel(bg_ref, u_ref, up_ref, un_ref, a_ref, x_ref, cw_ref, cb_ref, wc_ref, wa_ref,
                        g_ref, b_ref, rw_ref, rb_ref, x1_ref, idx_ref, wt_ref, xb_ref,
                        *, alpha, ts, seq_tiles):
    i = pl.program_id(0)
    first = i < 0
    last = i < 0
    for lo, hi, per in seq_tiles:
        inside = (i >= lo) & (i < hi)
        first = first | (inside & ((i - lo) % per == 0))
        last = last | (inside & ((i - lo) % per == per - 1))
    u = u_ref[...].astype(F32)
    prev_row = jnp.where(first, 0.0, up_ref[HALO - 1:HALO, :].astype(F32))
    next_row = jnp.where(last, 0.0, un_ref[0:1, :].astype(F32))
    rid = lax.broadcasted_iota(I32, u.shape, 0)
    ud = jnp.where(rid == 0, prev_row, pltpu.roll(u, 1, 0))
    uu = jnp.where(rid == ts - 1, next_row, pltpu.roll(u, ts - 1, 0))
    cw = cw_ref[...]
    y = ud * cw[0:1] + u * cw[1:2] + uu * cw[2:3] + cb_ref[...]
    c = (bg_ref[...].astype(F32) * y).astype(BF16)
    nr = ts // ROW_SPLITS
    for h in range(ROW_SPLITS):
        r0 = h * nr
        m = (jnp.dot(c[r0:r0 + nr], wc_ref[...], preferred_element_type=F32)
             + jnp.dot(a_ref[r0:r0 + nr, :], wa_ref[...], preferred_element_type=F32))
        _norm_route_store(r0, nr, m, x_ref[r0:r0 + nr, :], g_ref, b_ref, rw_ref, rb_ref, x1_ref, idx_ref,
                          wt_ref, xb_ref, alpha)


def _outproj_odd(bg, u, a, x, conv_w, conv_b, w_out, ln_g, ln_b, router, alpha, classes, ts):
    T = x.shape[0]
    hb = ts // HALO
    n_h = T // HALO
    seq_tiles = []
    t0 = 0
    for nseq, S, _ in classes:
        cnt = nseq * S // ts
        seq_tiles.append((t0, t0 + cnt, S // ts))
        t0 += cnt
    out_specs, out_shapes = _mix_out_specs(ts, T)
    return pl.pallas_call(
        functools.partial(_outproj_odd_kernel, alpha=alpha, ts=ts, seq_tiles=tuple(seq_tiles)),
        grid=(T // ts,),
        in_specs=[
            pl.BlockSpec((ts, D_CONV), lambda i: (i, 0)),
            pl.BlockSpec((ts, D_CONV), lambda i: (i, 0)),
            pl.BlockSpec((HALO, D_CONV), lambda i: (jnp.maximum(i * hb - 1, 0), 0)),
            pl.BlockSpec((HALO, D_CONV), lambda i: (jnp.minimum((i + 1) * hb, n_h - 1), 0)),
            pl.BlockSpec((ts, 512), lambda i: (i, 0)),
            pl.BlockSpec((ts, D_MODEL), lambda i: (i, 0)),
            pl.BlockSpec((3, D_CONV), lambda i: (0, 0)),
            pl.BlockSpec((1, D_CONV), lambda i: (0, 0)),
            pl.BlockSpec((D_CONV, D_MODEL), lambda i: (0, 0)),
            pl.BlockSpec((512, D_MODEL), lambda i: (0, 0)),
            pl.BlockSpec((1, D_MODEL), lambda i: (0, 0)),
            pl.BlockSpec((1, D_MODEL), lambda i: (0, 0)),
        ] + _ROUTER_SPECS,
        out_specs=out_specs,
        out_shape=out_shapes,
        compiler_params=_cparams(("parallel",)),
        name="outproj_odd",
    )(bg, u, u, u, a, x, conv_w.astype(F32), conv_b.astype(F32)[None, :],
      w_out[:D_CONV].astype(BF16), w_out[D_CONV:].astype(BF16), ln_g[None, :], ln_b[None, :], *router)


def _tile(n, cap):
    t = cap
    while n % t:
        t //= 2
    return t


def kernel(x_prompt, x_sample, w_in_even, fourier_norm_g, q_norm_g, k_norm_g, w_out_even, w_in_odd, conv_w,
           conv_b, sink_logits, w_out_odd, ln_mix_g, ln_mix_b, ln_ffn_g, ln_ffn_b, router_w, router_b,
           w_gate, w_up, w_down):
    depth = ln_mix_g.shape[0]
    alpha = float((2 * depth) ** 0.25)
    bp, sp, _ = x_prompt.shape
    bs, ss, _ = x_sample.shape
    classes = ((bp, sp, 0), (bs, ss, bp * sp))
    T = bp * sp + bs * ss
    min_s = min(sp, ss)
    ts = _tile(min_s, 512)
    tq_w = _tile(min_s, 512)
    tm = 512
    tc = _tile(min_s, 256)
    x_parts = [x_prompt.reshape(bp * sp, D_MODEL), x_sample.reshape(bs * ss, D_MODEL)]
    router = _router_operands(router_w, router_b)
    for l in range(depth):
        i = l // 2
        if l % 2 == 0:
            uf, qk, v_aug = _inproj_even(x_parts, w_in_even[i], fourier_norm_g[i], q_norm_g[i], k_norm_g[i],
                                         classes, ts)
            bounded = _score_bounded(q_norm_g[i], k_norm_g[i])
            f_parts, a_parts = [], []
            for nseq, S, tok0 in classes:
                f_parts.append(_fourier_mix(uf[tok0:tok0 + nseq * S], nseq, S))
                a_parts.append(_global_attention(qk, v_aug, bounded, nseq, S, tok0, _tile(S, 1024),
                                                 _tile(S, 2048)))
            x1, idx, wt, xb = _outproj_even(f_parts, a_parts, x_parts, w_out_even[i], ln_mix_g[l], ln_mix_b[l],
                                            router, alpha, ts)
        else:
            x = x_parts[0]
            bg, u, q, kw, vw, score_bound = _inproj_odd(x, w_in_odd[i], ts)
            a = _window_attention(q, kw, vw, sink_logits[i], score_bound, classes, tq_w)
            x1, idx, wt, xb = _outproj_odd(bg, u, a, x, conv_w[i], conv_b[i], w_out_odd[i], ln_mix_g[l],
                                           ln_mix_b[l], router, alpha, classes, ts)
        out_rows = [bp * sp, bs * ss] if l == depth - 1 else [T]
        x_parts = _moe_layer(x1, xb, idx, wt, w_gate[l], w_up[l], w_down[l], ln_ffn_g[l], ln_ffn_b[l], alpha,
                             tm, tc, out_rows)
    if len(x_parts) == 1:
        x_parts = [x_parts[0][:bp * sp], x_parts[0][bp * sp:]]
    return (x_parts[0].reshape(bp, sp, D_MODEL), x_parts[1].reshape(bs, ss, D_MODEL))
```

```python
import functools
import math

import numpy as np
import jax
import jax.numpy as jnp
from jax import lax
from jax.experimental import pallas as pl
from jax.experimental.pallas import tpu as pltpu

F32 = jnp.float32
BF16 = jnp.bfloat16
I32 = jnp.int32

D_MODEL = 1024
HEAD_DIM = 64
GRID_W = 64
Q_BLOCK = 128
WINDOW = 128
ROPE_THETA = 10000.0
N_FOURIER_GROUPS = 4
D_FOURIER = 256
N_HEADS_B = 12
N_KV_B = 4
GRP_B = 3
D_CONV = 512
N_HEADS_D = 8
N_KV_D = 2
GRP_D = 4
N_EXPERTS = 16
N_GROUPS = 4
EXPERTS_PER_GROUP = 4
D_FF = 512
LN_EPS = 1e-5
RMS_EPS = 1e-6
NEG_INF = -1e30
LOG2E = 1.4426950408889634
FFT_S2 = 128
V7X_VMEM_LIMIT = 48 * 1024 * 1024


def _cparams(sem):
    return pltpu.CompilerParams(dimension_semantics=sem, vmem_limit_bytes=V7X_VMEM_LIMIT)


def _layer_norm(y, g, b):
    mu = jnp.mean(y, axis=-1, keepdims=True)
    yc = y - mu
    var = jnp.mean(yc * yc, axis=-1, keepdims=True)
    return yc * lax.rsqrt(var + LN_EPS) * g + b


N_NORM_COLS = D_FOURIER + 1024


def _parts_specs(parts, ts):
    specs, bounds, lo = [], [], 0
    for p in parts:
        n = p.shape[0] // ts
        specs.append(pl.BlockSpec((ts, p.shape[1]), lambda i, lo=lo, n=n: (jnp.clip(i - lo, 0, n - 1), 0)))
        lo += n
        bounds.append(lo)
    return specs, tuple(bounds)


def _pick_rows(refs, bounds, r0, nr):
    i = pl.program_id(0)
    out = refs[-1][r0:r0 + nr, :]
    for ref, hi in reversed(list(zip(refs[:-1], bounds[:-1]))):
        out = jnp.where(i < hi, ref[r0:r0 + nr, :], out)
    return out


def _inproj_even_kernel(*refs, nx, bounds):
    x_refs = refs[:nx]
    (w_ref, gain_ref, seg_ref, segt_ref, cos_ref, sin_ref, vone_ref, uf_ref, qk_ref, v_ref) = refs[nx:]
    nr = uf_ref.shape[0] // ROW_SPLITS
    for part in range(ROW_SPLITS):
        r0 = part * nr
        x = _pick_rows(x_refs, bounds, r0, nr).astype(BF16)
        proj = jnp.dot(x, w_ref[...], preferred_element_type=F32)
        nrm = proj[:, :N_NORM_COLS]
        sq = (nrm * nrm).astype(BF16)
        ssum = jnp.dot(sq, seg_ref[...], preferred_element_type=F32)
        r = lax.rsqrt(ssum * (1.0 / HEAD_DIM) + RMS_EPS)
        rh = r.astype(BF16)
        rl = (r - rh.astype(F32)).astype(BF16)
        rex = jnp.dot(jnp.concatenate([rh, rl], axis=1), segt_ref[...], preferred_element_type=F32)
        y = nrm * rex * gain_ref[...]
        uf_ref[r0:r0 + nr, :] = y[:, :D_FOURIER].astype(BF16)
        yq = y[:, D_FOURIER:]
        c = jnp.concatenate([cos_ref[r0:r0 + nr, :]] * 8, axis=1)
        s = jnp.concatenate([sin_ref[r0:r0 + nr, :]] * 8, axis=1)
        lane = lax.broadcasted_iota(I32, yq.shape, 1)
        first = (lane & 31) < 16
        sw = jnp.where(first, pltpu.roll(yq, 1024 - 16, 1), pltpu.roll(yq, 16, 1))
        qk_ref[r0:r0 + nr, :] = (yq * c + sw * s).astype(BF16)
        vv = proj[:, N_NORM_COLS:] + vone_ref[...]
        for h in range(N_KV_B):
            v_ref[h, r0:r0 + nr, :] = vv[:, 128 * h:128 * (h + 1)].astype(BF16)


def _rope_tables(n_pos):
    t = jnp.arange(n_pos)
    row = (t // GRID_W).astype(F32)
    col = (t % GRID_W).astype(F32)
    n_freq = HEAD_DIM // 4
    inv_freq = ROPE_THETA ** (-jnp.arange(n_freq, dtype=F32) / n_freq)
    ar = row[:, None] * inv_freq
    ac = col[:, None] * inv_freq
    ang = jnp.concatenate([ar, ar, ac, ac], axis=1)
    sign = jnp.asarray(np.tile(np.repeat(np.array([-1.0, 1.0], np.float32), 16), 2))
    cos = jnp.cos(ang)
    sin = jnp.sin(ang) * sign
    return jnp.concatenate([cos, cos], axis=1), jnp.concatenate([sin, sin], axis=1)


def _pos_block_map(classes, ts):
    bounds = []
    tile0 = 0
    for nseq, S, _ in classes:
        n_tiles = nseq * S // ts
        bounds.append((tile0, tile0 + n_tiles, S // ts))
        tile0 += n_tiles

    def fn(i):
        out = (i - bounds[-1][0]) % bounds[-1][2]
        for lo, hi, per in reversed(bounds[:-1]):
            out = jnp.where(i < hi, (i - lo) % per, out)
        return out

    return fn


def _inproj_even(x_parts, w_in, f_g, q_g, k_g, classes, ts):
    T = sum(p.shape[0] for p in x_parts)
    x_specs, bounds = _parts_specs(x_parts, ts)
    wf = w_in[:, :D_FOURIER]
    wq = w_in[:, D_FOURIER:D_FOURIER + 768].reshape(D_MODEL, N_KV_B, GRP_B * HEAD_DIM)
    wk = w_in[:, D_FOURIER + 768:D_FOURIER + 1024].reshape(D_MODEL, N_KV_B, HEAD_DIM)
    wv = w_in[:, D_FOURIER + 1024:].reshape(D_MODEL, N_KV_B, HEAD_DIM)
    wqk = jnp.concatenate([wk, wq], axis=2).reshape(D_MODEL, 1024)
    wvp = jnp.concatenate([wv, jnp.zeros_like(wv)], axis=2).reshape(D_MODEL, 512)
    w = jnp.concatenate([wf, wqk, wvp], axis=1).astype(BF16)
    qscale = HEAD_DIM ** -0.5 * LOG2E
    gqk = jnp.tile(jnp.concatenate([k_g, q_g * qscale, q_g * qscale, q_g * qscale]), N_KV_B)
    gain = jnp.concatenate([f_g, gqk])[None, :].astype(F32)
    seg_np = np.zeros((N_NORM_COLS, 128), np.float32)
    seg_np[np.arange(N_NORM_COLS), np.arange(N_NORM_COLS) // HEAD_DIM] = 1.0
    seg = jnp.asarray(seg_np, BF16)
    segt = jnp.asarray(np.concatenate([seg_np.T, seg_np.T], axis=0), BF16)
    max_s = max(S for _, S, _ in classes)
    cos, sin = _rope_tables(max_s)
    vone_np = np.zeros((1, 512), np.float32)
    for h in range(N_KV_B):
        vone_np[0, 128 * h + 64:128 * (h + 1)] = 1.0
    vone = jnp.asarray(vone_np)
    posmap = _pos_block_map(classes, ts)
    n_w = w.shape[1]
    return pl.pallas_call(
        functools.partial(_inproj_even_kernel, nx=len(x_parts), bounds=bounds),
        grid=(T // ts,),
        in_specs=x_specs + [
            pl.BlockSpec((D_MODEL, n_w), lambda i: (0, 0)),
            pl.BlockSpec((1, N_NORM_COLS), lambda i: (0, 0)),
            pl.BlockSpec((N_NORM_COLS, 128), lambda i: (0, 0)),
            pl.BlockSpec((256, N_NORM_COLS), lambda i: (0, 0)),
            pl.BlockSpec((ts, 128), lambda i: (posmap(i), 0)),
            pl.BlockSpec((ts, 128), lambda i: (posmap(i), 0)),
            pl.BlockSpec((1, 512), lambda i: (0, 0)),
        ],
        out_specs=[
            pl.BlockSpec((ts, D_FOURIER), lambda i: (i, 0)),
            pl.BlockSpec((ts, 1024), lambda i: (i, 0)),
            pl.BlockSpec((N_KV_B, ts, 128), lambda i: (0, i, 0)),
        ],
        out_shape=[
            jax.ShapeDtypeStruct((T, D_FOURIER), BF16),
            jax.ShapeDtypeStruct((T, 1024), BF16),
            jax.ShapeDtypeStruct((N_KV_B, T, 128), BF16),
        ],
        compiler_params=_cparams(("parallel",)),
        name="inproj_even",
    )(*x_parts, w, gain, seg, segt, cos, sin, vone)


def _fft1_kernel(a_ref, w1_ref, twr_ref, twi_ref, zr_ref, zi_ref, *, s1):
    z = jnp.dot(w1_ref[...], a_ref[0], preferred_element_type=F32)
    zr = z[:s1]
    zi = z[s1:]
    twr = twr_ref[...]
    twi = twi_ref[...]
    zr_ref[0] = (zr * twr - zi * twi).astype(BF16)
    zi_ref[0] = (zr * twi + zi * twr).astype(BF16)


def _fft2_kernel(zr_ref, zi_ref, w2a_ref, w2b_ref, mix_ref, o_ref, *, cb):
    s2 = FFT_S2
    for c in range(cb):
        pp = (jnp.dot(w2a_ref[...], zr_ref[0, c], preferred_element_type=F32)
              + jnp.dot(w2b_ref[...], zi_ref[0, c], preferred_element_type=F32))
        f = (jnp.dot(pp[:s2].astype(BF16), mix_ref[:D_FOURIER], preferred_element_type=F32)
             + jnp.dot(pp[s2:].astype(BF16), mix_ref[D_FOURIER:], preferred_element_type=F32))
        o_ref[0, :, D_FOURIER * c:D_FOURIER * (c + 1)] = f.astype(BF16)


def _dft_mats(n):
    k = jnp.arange(n, dtype=I32)
    ang = (2.0 * math.pi / n) * ((k[:, None] * k[None, :]) % n).astype(F32)
    return jnp.cos(ang), jnp.sin(ang)


def _fourier_mix(uf_part, nseq, S):
    s2 = FFT_S2
    s1 = S // s2
    ncol = s2 * D_FOURIER
    a = uf_part.reshape(nseq, s1, ncol)
    c1, sn1 = _dft_mats(s1)
    w1 = jnp.concatenate([c1, -sn1], axis=0).astype(BF16)
    cc = jnp.arange(s1, dtype=I32)[:, None]
    bb = jnp.arange(s2, dtype=I32)[None, :]
    ang = (2.0 * math.pi / S) * ((cc * bb) % S).astype(F32)
    twr = jnp.repeat(jnp.cos(ang), D_FOURIER, axis=1)
    twi = jnp.repeat(-jnp.sin(ang), D_FOURIER, axis=1)
    tn = min(ncol, 4096 * max(1, 64 // s1))
    zr, zi = pl.pallas_call(
        functools.partial(_fft1_kernel, s1=s1),
        grid=(ncol // tn, nseq),
        in_specs=[
            pl.BlockSpec((1, s1, tn), lambda j, b: (b, 0, j)),
            pl.BlockSpec((2 * s1, s1), lambda j, b: (0, 0)),
            pl.BlockSpec((s1, tn), lambda j, b: (0, j)),
            pl.BlockSpec((s1, tn), lambda j, b: (0, j)),
        ],
        out_specs=[
            pl.BlockSpec((1, s1, tn), lambda j, b: (b, 0, j)),
            pl.BlockSpec((1, s1, tn), lambda j, b: (b, 0, j)),
        ],
        out_shape=[jax.ShapeDtypeStruct((nseq, s1, ncol), BF16)] * 2,
        compiler_params=_cparams(("parallel", "parallel")),
        name="fourier_stage1",
    )(a, w1, twr, twi)
    zr = zr.reshape(nseq, s1, s2, D_FOURIER)
    zi = zi.reshape(nseq, s1, s2, D_FOURIER)
    c2, sn2 = _dft_mats(s2)
    w2a = jnp.concatenate([c2, -sn2], axis=0).astype(BF16)
    w2b = jnp.concatenate([sn2, c2], axis=0).astype(BF16)
    gc, gs = _dft_mats(HEAD_DIM)
    scale = 1.0 / math.sqrt(S * HEAD_DIM)
    eye = jnp.eye(N_FOURIER_GROUPS, dtype=F32)
    mix = (jnp.concatenate([jnp.kron(eye, gc), jnp.kron(eye, gs)], axis=0) * scale).astype(BF16)
    cb = min(8, s1)
    out = pl.pallas_call(
        functools.partial(_fft2_kernel, cb=cb),
        grid=(nseq, s1 // cb),
        in_specs=[
            pl.BlockSpec((1, cb, s2, D_FOURIER), lambda b, j: (b, j, 0, 0)),
            pl.BlockSpec((1, cb, s2, D_FOURIER), lambda b, j: (b, j, 0, 0)),
            pl.BlockSpec((2 * s2, s2), lambda b, j: (0, 0)),
            pl.BlockSpec((2 * s2, s2), lambda b, j: (0, 0)),
            pl.BlockSpec((2 * D_FOURIER, D_FOURIER), lambda b, j: (0, 0)),
        ],
        out_specs=pl.BlockSpec((1, s2, cb * D_FOURIER), lambda b, j: (b, 0, j)),
        out_shape=jax.ShapeDtypeStruct((nseq, s2, s1 * D_FOURIER), BF16),
        compiler_params=_cparams(("parallel", "parallel")),
        name="fourier_stage2",
    )(zr, zi, w2a, w2b, mix)
    return out.reshape(nseq * S, D_FOURIER)


SCORE_BOUND_MAX = 100.0


def _gattn_kernel(bounded_ref, q_ref, k_ref, v_ref, place_ref, o_ref, *, tq, tk, S):
    q3 = q_ref[...]
    qs = [q3[:, HEAD_DIM * (g + 1):HEAD_DIM * (g + 2)] for g in range(GRP_B)]

    def chunk(j):
        off = pl.multiple_of(j * tk, tk)
        return k_ref[pl.ds(off, tk), 0:HEAD_DIM], v_ref[0, pl.ds(off, tk), :]

    def scores(g, kc):
        return lax.dot_general(qs[g], kc, (((1,), (1,)), ((), ())), preferred_element_type=F32)

    def finish(accs):
        out = jnp.zeros((tq, 256), F32)
        for g in range(GRP_B):
            o = (accs[g] / accs[g][:, HEAD_DIM:HEAD_DIM + 1]).astype(BF16)
            out = out + jnp.dot(o, place_ref[g], preferred_element_type=F32)
        o_ref[...] = out.astype(BF16)

    @pl.when(bounded_ref[0] == 1)
    def _():
        def body(j, accs):
            kc, vc = chunk(j)
            return tuple(accs[g] + jnp.dot(jnp.exp2(scores(g, kc)).astype(BF16), vc,
                                           preferred_element_type=F32) for g in range(GRP_B))
        finish(lax.fori_loop(0, S // tk, body, tuple(jnp.zeros((tq, 128), F32) for _ in range(GRP_B))))

    @pl.when(bounded_ref[0] != 1)
    def _():
        def body(j, carry):
            kc, vc = chunk(j)
            new = []
            for g in range(GRP_B):
                m_prev, acc = carry[g]
                s = scores(g, kc)
                m_new = jnp.maximum(m_prev, jnp.max(s, axis=1, keepdims=True))
                p = jnp.exp2(s - m_new).astype(BF16)
                acc = jnp.exp2(m_prev - m_new) * acc + jnp.dot(p, vc, preferred_element_type=F32)
                new.append((m_new, acc))
            return tuple(new)
        init = tuple((jnp.full((tq, 1), NEG_INF, F32), jnp.zeros((tq, 128), F32)) for _ in range(GRP_B))
        fin = lax.fori_loop(0, S // tk, body, init)
        finish([fin[g][1] for g in range(GRP_B)])


def _score_bounded(q_g, k_g):
    bound = 1.05 * HEAD_DIM * (HEAD_DIM ** -0.5 * LOG2E) * jnp.max(jnp.abs(q_g)) * jnp.max(jnp.abs(k_g))
    return (bound <= SCORE_BOUND_MAX).astype(I32)[None]


def _lane_place(n_src, n_dst, groups):
    pm = np.zeros((len(groups), n_src, n_dst), np.float32)
    for gi, slot in enumerate(groups):
        pm[gi, np.arange(HEAD_DIM), HEAD_DIM * slot + np.arange(HEAD_DIM)] = 1.0
    return jnp.asarray(pm, BF16)


def _global_attention(qk, v_aug, bounded, nseq, S, tok0, tq, tk):
    n_qt = S // tq
    qt0 = tok0 // tq
    s0 = tok0 // S
    place = _lane_place(128, 256, [1, 2, 3])
    grid_spec = pltpu.PrefetchScalarGridSpec(
        num_scalar_prefetch=1,
        grid=(nseq, N_KV_B, n_qt),
        in_specs=[
            pl.BlockSpec((tq, 256), lambda b, h, i, fl: (qt0 + b * n_qt + i, h)),
            pl.BlockSpec((S, 256), lambda b, h, i, fl: (s0 + b, h)),
            pl.BlockSpec((1, S, 128), lambda b, h, i, fl: (h, s0 + b, 0)),
            pl.BlockSpec((GRP_B, 128, 256), lambda b, h, i, fl: (0, 0, 0)),
        ],
        out_specs=pl.BlockSpec((tq, 256), lambda b, h, i, fl: (b * n_qt + i, h)),
    )
    return pl.pallas_call(
        functools.partial(_gattn_kernel, tq=tq, tk=tk, S=S),
        grid_spec=grid_spec,
        out_shape=jax.ShapeDtypeStruct((nseq * S, 1024), BF16),
        compiler_params=_cparams(("parallel", "parallel", "arbitrary")),
        name="global_attention",
    )(bounded, qk, qk, v_aug, place)


def _route(x1, rw_ref, rb_ref):
    n = x1.shape[0]
    xh = x1.astype(BF16)
    xl = (x1 - xh.astype(F32)).astype(BF16)
    o = jnp.dot(jnp.concatenate([xh, xl], axis=0), rw_ref[...], preferred_element_type=F32)
    lg = o[:n, :128] + o[:n, 128:] + o[n:, :128]
    lgt = lg.T[:N_EXPERTS]
    sc = 1.0 / (1.0 + jnp.exp(-lgt))
    bi = sc + rb_ref[...]
    srow = [sc[e:e + 1] for e in range(N_EXPERTS)]
    brow = [bi[e:e + 1] for e in range(N_EXPERTS)]
    gsel = None
    best = None
    for g in range(N_GROUPS):
        a, b, c, d = brow[4 * g:4 * g + 4]
        m1, n1 = jnp.maximum(a, b), jnp.minimum(a, b)
        m2, n2 = jnp.maximum(c, d), jnp.minimum(c, d)
        gs = jnp.maximum(m1, m2) + jnp.maximum(jnp.minimum(m1, m2), jnp.maximum(n1, n2))
        if g == 0:
            best, gsel = gs, jnp.zeros(gs.shape, I32)
        else:
            better = gs > best
            gsel = jnp.where(better, g, gsel)
            best = jnp.where(better, gs, best)
    masked = [jnp.where(gsel == (e // EXPERTS_PER_GROUP), brow[e], NEG_INF) for e in range(N_EXPERTS)]
    i1 = jnp.zeros(gsel.shape, I32)
    b1 = masked[0]
    s1 = srow[0]
    for e in range(1, N_EXPERTS):
        better = masked[e] > b1
        i1 = jnp.where(better, e, i1)
        b1 = jnp.where(better, masked[e], b1)
        s1 = jnp.where(better, srow[e], s1)
    i2 = jnp.full(gsel.shape, -1, I32)
    b2 = jnp.full(b1.shape, -jnp.inf, F32)
    s2 = jnp.zeros(b1.shape, F32)
    for e in range(N_EXPERTS):
        better = (masked[e] > b2) & (i1 != e)
        i2 = jnp.where(better, e, i2)
        b2 = jnp.where(better, masked[e], b2)
        s2 = jnp.where(better, srow[e], s2)
    den = s1 + s2
    return i1, i2, s1 / den, s2 / den


ROW_SPLITS = 2


def _norm_route_store(r0, nr, m, x, g_ref, b_ref, rw_ref, rb_ref, x1_ref, idx_ref, wt_ref, xb_ref, alpha):
    x1 = _layer_norm(alpha * x + m, g_ref[...], b_ref[...])
    x1_ref[r0:r0 + nr, :] = x1
    xb_ref[r0:r0 + nr, :] = x1.astype(BF16)
    i1, i2, w1, w2 = _route(x1, rw_ref, rb_ref)
    rid = lax.broadcasted_iota(I32, (8, nr), 0)
    idx_ref[:, r0:r0 + nr] = jnp.where(rid == 0, i1, jnp.where(rid == 1, i2, 0))
    wt_ref[:, r0:r0 + nr] = jnp.where(rid == 0, w1, jnp.where(rid == 1, w2, 0.0))


def _router_operands(router_w, router_b):
    rw = jnp.zeros((D_MODEL, 128), F32).at[:, :N_EXPERTS].set(router_w.astype(F32))
    rwh = rw.astype(BF16)
    rwl = (rw - rwh.astype(F32)).astype(BF16)
    rb = router_b.astype(F32)[:, None]
    return jnp.concatenate([rwh, rwl], axis=1), rb


_ROUTER_SPECS = [
    pl.BlockSpec((D_MODEL, 256), lambda i: (0, 0)),
    pl.BlockSpec((N_EXPERTS, 1), lambda i: (0, 0)),
]


def _mix_out_specs(ts, T):
    specs = [
        pl.BlockSpec((ts, D_MODEL), lambda i: (i, 0)),
        pl.BlockSpec((8, ts), lambda i: (0, i)),
        pl.BlockSpec((8, ts), lambda i: (0, i)),
        pl.BlockSpec((ts, D_MODEL), lambda i: (i, 0)),
    ]
    shapes = [
        jax.ShapeDtypeStruct((T, D_MODEL), F32),
        jax.ShapeDtypeStruct((8, T), I32),
        jax.ShapeDtypeStruct((8, T), F32),
        jax.ShapeDtypeStruct((T, D_MODEL), BF16),
    ]
    return specs, shapes


def _outproj_even_kernel(*refs, nf, na, nx, fb, ab, xb, alpha):
    f_refs, a_refs, x_refs = refs[:nf], refs[nf:nf + na], refs[nf + na:nf + na + nx]
    wf_ref, wa_ref, g_ref, b_ref, rw_ref, rb_ref, x1_ref, idx_ref, wt_ref, xb_ref = refs[nf + na + nx:]
    nr = x1_ref.shape[0] // ROW_SPLITS
    for h in range(ROW_SPLITS):
        r0 = h * nr
        m = (jnp.dot(_pick_rows(f_refs, fb, r0, nr), wf_ref[...], preferred_element_type=F32)
             + jnp.dot(_pick_rows(a_refs, ab, r0, nr), wa_ref[...], preferred_element_type=F32))
        xr = _pick_rows(x_refs, xb, r0, nr)
        _norm_route_store(r0, nr, m, xr, g_ref, b_ref, rw_ref, rb_ref, x1_ref, idx_ref, wt_ref, xb_ref, alpha)


def _outproj_even(f_parts, a_parts, x_parts, w_out, ln_g, ln_b, router, alpha, ts):
    T = sum(p.shape[0] for p in x_parts)
    f_specs, fb = _parts_specs(f_parts, ts)
    a_specs, ab = _parts_specs(a_parts, ts)
    x_specs, xb = _parts_specs(x_parts, ts)
    wf = w_out[:D_FOURIER].astype(BF16)
    wa = w_out[D_FOURIER:].reshape(N_KV_B, GRP_B * HEAD_DIM, D_MODEL)
    wa = jnp.concatenate([jnp.zeros((N_KV_B, HEAD_DIM, D_MODEL), w_out.dtype), wa], axis=1)
    wa = wa.reshape(1024, D_MODEL).astype(BF16)
    out_specs, out_shapes = _mix_out_specs(ts, T)
    return pl.pallas_call(
        functools.partial(_outproj_even_kernel, nf=len(f_parts), na=len(a_parts), nx=len(x_parts),
                          fb=fb, ab=ab, xb=xb, alpha=alpha),
        grid=(T // ts,),
        in_specs=f_specs + a_specs + x_specs + [
            pl.BlockSpec((D_FOURIER, D_MODEL), lambda i: (0, 0)),
            pl.BlockSpec((1024, D_MODEL), lambda i: (0, 0)),
            pl.BlockSpec((1, D_MODEL), lambda i: (0, 0)),
            pl.BlockSpec((1, D_MODEL), lambda i: (0, 0)),
        ] + _ROUTER_SPECS,
        out_specs=out_specs,
        out_shape=out_shapes,
        compiler_params=_cparams(("parallel",)),
        name="outproj_even",
    )(*f_parts, *a_parts, *x_parts, wf, wa, ln_g[None, :], ln_b[None, :], *router)


CHUNK = 16


def _stage_rows(td):
    return -(-(2 * td + N_EXPERTS * (CHUNK - 1)) // 256) * 256


def _tab_len(td):
    return -(-(1 + 2 * td // CHUNK + N_EXPERTS) // 128) * 128


def _moe_plan(idx, td, tm):
    T = idx.shape[1]
    nt = T // td
    e = idx.T.reshape(-1)
    oh = (e[:, None] == jnp.arange(N_EXPERTS, dtype=I32)[None, :]).astype(I32)
    csum = jnp.cumsum(oh, axis=0)
    tile_end = csum.reshape(nt, 2 * td, N_EXPERTS)[:, -1, :]
    base = jnp.concatenate([jnp.zeros((1, N_EXPERTS), I32), tile_end[:-1]], axis=0)
    cnt = tile_end - base
    c8 = ((cnt + CHUNK - 1) // CHUNK) * CHUNK
    off8 = jnp.cumsum(c8, axis=1) - c8
    base8 = jnp.cumsum(c8, axis=0) - c8
    seg = jnp.sum(c8, axis=0)
    padded = ((seg + tm - 1) // tm) * tm
    ends = jnp.cumsum(padded)
    starts = ends - padded
    dst = starts[None, :] + base8
    shift = jnp.repeat(off8 - base, 2 * td, axis=0)
    lpos = (jnp.sum((csum + shift) * oh, axis=1) - 1).astype(I32)
    n_rows = -(-(2 * T + nt * N_EXPERTS * (CHUNK - 1)) // tm) * tm + N_EXPERTS * tm
    tile_start = jnp.arange(n_rows // tm, dtype=I32) * tm
    tile_e = jnp.minimum(jnp.sum((ends[None, :] <= tile_start[:, None]).astype(I32), axis=1), N_EXPERTS - 1)
    n_used = (ends[-1] // tm).astype(I32)[None]
    nc = c8 // CHUNK
    cum = jnp.cumsum(nc, axis=1)
    n_chunk_max = 2 * td // CHUNK + N_EXPERTS
    c_idx = jnp.arange(n_chunk_max, dtype=I32)
    e_c = jnp.minimum(jnp.sum((cum[:, None, :] <= c_idx[None, :, None]).astype(I32), axis=2), N_EXPERTS - 1)
    oh_c = (e_c[:, :, None] == jnp.arange(N_EXPERTS, dtype=I32)[None, None, :]).astype(I32)
    srow = jnp.sum(oh_c * (dst - CHUNK * (cum - nc))[:, None, :], axis=2) + CHUNK * c_idx[None, :]
    tab = jnp.concatenate([cum[:, -1:], srow], axis=1).astype(I32)
    tab = jnp.pad(tab, ((0, 0), (0, _tab_len(td) - tab.shape[1]))).reshape(nt, 1, _tab_len(td))
    pad_lo = jnp.concatenate([starts + seg, ends[-1:]]).astype(I32)
    pad_hi = jnp.concatenate([ends, jnp.full((1,), n_rows, I32)]).astype(I32)
    lpos_tk = lpos.reshape(T, 2)
    lpos_rows = jnp.zeros((8, T), I32).at[:2].set(lpos_tk.T)
    return tab, lpos_rows, lpos_tk, tile_e.astype(I32), n_used, pad_lo, pad_hi, n_rows


def _chunk_loops(tab_ref, fn):
    total = tab_ref[0, 0, 0]

    def one(c, priority):
        fn(pl.multiple_of(CHUNK * c, CHUNK), pl.multiple_of(tab_ref[0, 0, 1 + c], CHUNK), priority)

    def body(j, carry):
        one(2 * j, 0)
        one(2 * j + 1, 1)
        return carry
    lax.fori_loop(0, total // 2, body, 0)

    @pl.when(total % 2 == 1)
    def _():
        one(total - 1, 0)
    return total


def _dispatch_kernel(lo_ref, hi_ref, tab_ref, lp_ref, x_ref, xs_hbm, stage, zbuf, nwait, sem, zsem, *, ns):
    i = pl.program_id(0)
    slot = i % 2

    def zero_copy(dst):
        return pltpu.make_async_copy(zbuf, xs_hbm.at[pl.ds(pl.multiple_of(dst, CHUNK), CHUNK)], zsem)

    @pl.when(i == 0)
    def _():
        nwait[0] = 0
        nwait[1] = 0
        zbuf[...] = jnp.zeros(zbuf.shape, BF16)
        for e in range(N_EXPERTS + 1):
            nz = (hi_ref[e] - lo_ref[e]) // CHUNK
            lax.fori_loop(0, nz, lambda j, c, e=e: (zero_copy(lo_ref[e] + CHUNK * j).start(), c)[1], 0)
        for e in range(N_EXPERTS + 1):
            nz = (hi_ref[e] - lo_ref[e]) // CHUNK
            lax.fori_loop(0, nz, lambda j, c: (zero_copy(0).wait(), c)[1], 0)

    def piece(local_row, sorted_row, s):
        return pltpu.make_async_copy(stage.at[s, pl.ds(local_row, CHUNK)],
                                     xs_hbm.at[pl.ds(sorted_row, CHUNK)], sem.at[s])

    def drain(s):
        lax.fori_loop(0, nwait[s], lambda j, c: (piece(0, 0, s).wait(), c)[1], 0)

    drain(slot)
    lp = lp_ref[...]
    rows = lax.broadcasted_iota(I32, (ns, lp.shape[1]), 0)
    perm = jnp.where((rows == lp[0:1]) | (rows == lp[1:2]), 1.0, 0.0).astype(BF16)
    stage[slot] = jnp.dot(perm, x_ref[...], preferred_element_type=F32).astype(BF16)
    nwait[slot] = _chunk_loops(tab_ref, lambda lr, sr, pr: piece(lr, sr, slot).start(priority=pr))

    @pl.when(i == pl.num_programs(0) - 1)
    def _():
        drain(0)
        drain(1)


def _moe_dispatch(x1, tab, lpos_rows, pad_lo, pad_hi, n_rows, td):
    T = x1.shape[0]
    ns = _stage_rows(td)
    grid_spec = pltpu.PrefetchScalarGridSpec(
        num_scalar_prefetch=2,
        grid=(T // td,),
        in_specs=[
            pl.BlockSpec((1, 1, _tab_len(td)), lambda i, lo, hi: (i, 0, 0), memory_space=pltpu.SMEM),
            pl.BlockSpec((8, td), lambda i, lo, hi: (0, i)),
            pl.BlockSpec((td, D_MODEL), lambda i, lo, hi: (i, 0)),
        ],
        out_specs=pl.BlockSpec(memory_space=pl.ANY),
        scratch_shapes=[pltpu.VMEM((2, ns, D_MODEL), BF16), pltpu.VMEM((CHUNK, D_MODEL), BF16),
                        pltpu.SMEM((2,), I32), pltpu.SemaphoreType.DMA((2,)), pltpu.SemaphoreType.DMA(())],
    )
    return pl.pallas_call(
        functools.partial(_dispatch_kernel, ns=ns),
        grid_spec=grid_spec,
        out_shape=jax.ShapeDtypeStruct((n_rows, D_MODEL), BF16),
        compiler_params=_cparams(("arbitrary",)),
        name="moe_dispatch",
    )(pad_lo, pad_hi, tab, lpos_rows, x1)


def _moe_kernel(te_ref, nu_ref, x_ref, wg_ref, wu_ref, wd_ref, y_ref, wgb, wub, wdb):
    i = pl.program_id(0)

    @pl.when((i == 0) | (te_ref[i] != te_ref[jnp.maximum(i - 1, 0)]))
    def _():
        wgb[...] = wg_ref[0].astype(BF16)
        wub[...] = wu_ref[0].astype(BF16)
        wdb[...] = wd_ref[0].astype(BF16)

    @pl.when(i < nu_ref[0])
    def _():
        xg = x_ref[...]
        hg = jnp.dot(xg, wgb[...], preferred_element_type=F32)
        hu = jnp.dot(xg, wub[...], preferred_element_type=F32)
        hdn = (hg / (1.0 + jnp.exp(-hg)) * hu).astype(BF16)
        y_ref[...] = jnp.dot(hdn, wdb[...], preferred_element_type=F32).astype(BF16)

    @pl.when(i >= nu_ref[0])
    def _():
        y_ref[...] = jnp.zeros(y_ref.shape, BF16)


def _moe_experts(xs, tile_e, n_used, wg, wu, wd, tm):
    n_rows = xs.shape[0]
    grid_spec = pltpu.PrefetchScalarGridSpec(
        num_scalar_prefetch=2,
        grid=(n_rows // tm,),
        in_specs=[
            pl.BlockSpec((tm, D_MODEL), lambda i, te, nu: (i, 0)),
            pl.BlockSpec((1, D_MODEL, D_FF), lambda i, te, nu: (te[i], 0, 0)),
            pl.BlockSpec((1, D_MODEL, D_FF), lambda i, te, nu: (te[i], 0, 0)),
            pl.BlockSpec((1, D_FF, D_MODEL), lambda i, te, nu: (te[i], 0, 0)),
        ],
        out_specs=pl.BlockSpec((tm, D_MODEL), lambda i, te, nu: (i, 0)),
        scratch_shapes=[pltpu.VMEM((D_MODEL, D_FF), BF16), pltpu.VMEM((D_MODEL, D_FF), BF16),
                        pltpu.VMEM((D_FF, D_MODEL), BF16)],
    )
    return pl.pallas_call(
        _moe_kernel,
        grid_spec=grid_spec,
        out_shape=jax.ShapeDtypeStruct((n_rows, D_MODEL), BF16),
        compiler_params=_cparams(("arbitrary",)),
        name="moe_experts",
    )(tile_e, n_used, xs, wg, wu, wd)


def _combine_kernel(tab_ref, tabn_ref, lp_ref, y_hbm, x_ref, wt_ref, g_ref, b_ref, *rest, ns, alpha, bounds):
    o_refs, (ystage, sem) = rest[:len(bounds)], rest[len(bounds):]
    i = pl.program_id(0)
    n = pl.num_programs(0)
    slot = i % 2

    def piece(local_row, sorted_row, s):
        return pltpu.make_async_copy(y_hbm.at[pl.ds(sorted_row, CHUNK)],
                                     ystage.at[s, pl.ds(local_row, CHUNK)], sem.at[s])

    @pl.when(i == 0)
    def _():
        ystage[...] = jnp.zeros(ystage.shape, BF16)
        _chunk_loops(tab_ref, lambda lr, sr, pr: piece(lr, sr, 0).start(priority=pr))

    @pl.when(i + 1 < n)
    def _():
        _chunk_loops(tabn_ref, lambda lr, sr, pr: piece(lr, sr, 1 - slot).start(priority=pr))

    lax.fori_loop(0, tab_ref[0, 0, 0], lambda j, c: (piece(0, 0, slot).wait(), c)[1], 0)

    ys = ystage[slot]
    lp = lp_ref[...]
    cols = lax.broadcasted_iota(I32, (lp.shape[0], ns), 1)
    w = wt_ref[...]
    tc = lp.shape[0]
    sel = jnp.concatenate([jnp.where(cols == lp[:, k:k + 1], 1.0, 0.0).astype(BF16) for k in range(2)], axis=0)
    picked = jnp.dot(sel, ys, preferred_element_type=F32)
    f = w[:, 0:1] * picked[:tc] + w[:, 1:2] * picked[tc:]
    o = _layer_norm(alpha * x_ref[...] + f, g_ref[...], b_ref[...])
    if len(bounds) == 1:
        o_refs[0][...] = o
    else:
        lo = 0
        for o_ref, hi in zip(o_refs, bounds):
            @pl.when((i >= lo) & (i < hi))
            def _(o_ref=o_ref):
                o_ref[...] = o
            lo = hi


def _moe_combine(y, tab, lpos_tk, wts, x1, ln_g, ln_b, alpha, tc, out_rows):
    T = x1.shape[0]
    n_tiles = T // tc
    ns = _stage_rows(tc)
    out_specs, out_shapes, bounds, lo = [], [], [], 0
    for rows in out_rows:
        n = rows // tc
        out_specs.append(pl.BlockSpec((tc, D_MODEL), lambda i, lo=lo, n=n: (jnp.clip(i - lo, 0, n - 1), 0)))
        out_shapes.append(jax.ShapeDtypeStruct((rows, D_MODEL), F32))
        lo += n
        bounds.append(lo)
    return pl.pallas_call(
        functools.partial(_combine_kernel, ns=ns, alpha=alpha, bounds=tuple(bounds)),
        grid=(n_tiles,),
        in_specs=[
            pl.BlockSpec((1, 1, _tab_len(tc)), lambda i: (i, 0, 0), memory_space=pltpu.SMEM),
            pl.BlockSpec((1, 1, _tab_len(tc)), lambda i: (jnp.minimum(i + 1, n_tiles - 1), 0, 0),
                         memory_space=pltpu.SMEM),
            pl.BlockSpec((tc, 2), lambda i: (i, 0)),
            pl.BlockSpec(memory_space=pl.ANY),
            pl.BlockSpec((tc, D_MODEL), lambda i: (i, 0)),
            pl.BlockSpec((tc, 2), lambda i: (i, 0)),
            pl.BlockSpec((1, D_MODEL), lambda i: (0, 0)),
            pl.BlockSpec((1, D_MODEL), lambda i: (0, 0)),
        ],
        out_specs=out_specs,
        out_shape=out_shapes,
        scratch_shapes=[pltpu.VMEM((2, ns, D_MODEL), BF16), pltpu.SemaphoreType.DMA((2,))],
        compiler_params=_cparams(("arbitrary",)),
        name="moe_combine",
    )(tab, tab, lpos_tk, y, x1, wts, ln_g[None, :], ln_b[None, :])


def _moe_layer(x1, xb, idx, wt, wg, wu, wd, ln_g, ln_b, alpha, tm, tc, out_rows):
    tab, lpos_rows, lpos_tk, tile_e, n_used, pad_lo, pad_hi, n_rows = _moe_plan(idx[:2], tc, tm)
    xs = _moe_dispatch(xb, tab, lpos_rows, pad_lo, pad_hi, n_rows, tc)
    y = _moe_experts(xs, tile_e, n_used, wg, wu, wd, tm)
    return _moe_combine(y, tab, lpos_tk, wt[:2].T, x1, ln_g, ln_b, alpha, tc, out_rows)


def _inproj_odd_kernel(x_ref, w_ref, vone_ref, bg_ref, u_ref, q_ref, k_ref, v_ref, nrm_ref):
    x = x_ref[...].astype(BF16)
    proj = jnp.dot(x, w_ref[...], preferred_element_type=F32)
    bg_ref[...] = proj[:, :512].astype(BF16)
    u_ref[...] = (proj[:, 512:1024] * proj[:, 1024:1536]).astype(BF16)
    qb = (proj[:, 1536:2048] * (HEAD_DIM ** -0.5 * LOG2E)).astype(BF16)
    kb = proj[:, 2048:2304].astype(BF16)
    q_ref[...] = qb
    k_ref[...] = kb
    v_ref[...] = (proj[:, 2304:2560] + vone_ref[...]).astype(BF16)
    qf = qb.astype(F32)
    kf = kb.astype(F32)
    qq = jnp.max(jnp.sum(qf * qf, axis=1, keepdims=True), axis=0, keepdims=True)
    kk = jnp.max(jnp.sum(kf * kf, axis=1, keepdims=True), axis=0, keepdims=True)
    rid = lax.broadcasted_iota(I32, (8, 128), 0)
    nrm_ref[0] = jnp.where(rid == 0, qq, jnp.where(rid == 1, kk, 0.0))


def _inproj_odd(x, w_in, ts):
    T = x.shape[0]
    c3 = 3 * D_CONV
    wk = w_in[:, c3 + 512:c3 + 640].reshape(D_MODEL, N_KV_D, HEAD_DIM)
    wv = w_in[:, c3 + 640:c3 + 768].reshape(D_MODEL, N_KV_D, HEAD_DIM)
    wkp = jnp.concatenate([wk, jnp.zeros_like(wk)], axis=2).reshape(D_MODEL, 256)
    wvp = jnp.concatenate([wv, jnp.zeros_like(wv)], axis=2).reshape(D_MODEL, 256)
    w = jnp.concatenate([w_in[:, :c3 + 512], wkp, wvp], axis=1).astype(BF16)
    vone_np = np.zeros((1, 256), np.float32)
    for h in range(N_KV_D):
        vone_np[0, 128 * h + 64:128 * (h + 1)] = 1.0
    n_w = w.shape[1]
    widths = [512, 512, 512, 256, 256]
    bg, u, q, kw, vw, nrm = pl.pallas_call(
        _inproj_odd_kernel,
        grid=(T // ts,),
        in_specs=[
            pl.BlockSpec((ts, D_MODEL), lambda i: (i, 0)),
            pl.BlockSpec((D_MODEL, n_w), lambda i: (0, 0)),
            pl.BlockSpec((1, 256), lambda i: (0, 0)),
        ],
        out_specs=[pl.BlockSpec((ts, n), lambda i: (i, 0)) for n in widths]
        + [pl.BlockSpec((1, 8, 128), lambda i: (i, 0, 0))],
        out_shape=[jax.ShapeDtypeStruct((T, n), BF16) for n in widths]
        + [jax.ShapeDtypeStruct((T // ts, 8, 128), F32)],
        compiler_params=_cparams(("parallel",)),
        name="inproj_odd",
    )(x, w, jnp.asarray(vone_np))
    score_bound = jnp.sqrt(jnp.max(nrm[:, 0, 0]) * jnp.max(nrm[:, 1, 0]))
    return bg, u, q, kw, vw, score_bound


def _wattn_kernel(bounded_ref, q_ref, kp_ref, kc_ref, kn_ref, vp_ref, vc_ref, vn_ref, bias_ref, sink_ref,
                  place_ref, o_ref, *, tq, seq_tiles):
    i = pl.program_id(0)
    first = i < 0
    last = i < 0
    for lo, hi, per in seq_tiles:
        inside = (i >= lo) & (i < hi)
        first = first | (inside & ((i - lo) % per == 0))
        last = last | (inside & ((i - lo) % per == per - 1))
    kfull = jnp.concatenate([kp_ref[...], kc_ref[...], kn_ref[...]], axis=0)[:, :HEAD_DIM]
    vfull = jnp.concatenate([vp_ref[...], vc_ref[...], vn_ref[...]], axis=0)
    q4 = q_ref[...]
    bias = bias_ref[0]
    sink = sink_ref[0]
    nb = tq // Q_BLOCK
    col = lax.broadcasted_iota(I32, (GRP_D * Q_BLOCK, 3 * Q_BLOCK), 1)

    def blocks(with_max):
        for n in range(nb):
            qs = jnp.concatenate([q4[Q_BLOCK * n:Q_BLOCK * (n + 1), HEAD_DIM * g:HEAD_DIM * (g + 1)]
                                  for g in range(GRP_D)], axis=0)
            keys = kfull[Q_BLOCK * n:Q_BLOCK * (n + 3)]
            vals = vfull[Q_BLOCK * n:Q_BLOCK * (n + 3)]
            s = lax.dot_general(qs, keys, (((1,), (1,)), ((), ())), preferred_element_type=F32) + bias
            if n == 0:
                s = jnp.where(first & (col < Q_BLOCK), NEG_INF, s)
            if n == nb - 1:
                s = jnp.where(last & (col >= 2 * Q_BLOCK), NEG_INF, s)
            if with_max:
                m = jnp.maximum(jnp.max(s, axis=1, keepdims=True), sink)
                s = s - m
                snk = sink - m
            else:
                snk = sink
            acc = jnp.dot(jnp.exp2(s).astype(BF16), vals, preferred_element_type=F32)
            den = acc[:, HEAD_DIM:HEAD_DIM + 1] + jnp.exp2(snk)
            o = (acc / den).astype(BF16)
            ocat = jnp.concatenate([o[Q_BLOCK * g:Q_BLOCK * (g + 1)] for g in range(GRP_D)], axis=1)
            out = jnp.dot(ocat, place_ref[...], preferred_element_type=F32)
            o_ref[Q_BLOCK * n:Q_BLOCK * (n + 1), :] = out.astype(BF16)

    @pl.when(bounded_ref[0] == 1)
    def _():
        blocks(False)

    @pl.when(bounded_ref[0] != 1)
    def _():
        blocks(True)


WINDOW_BOUND_MAX = 90.0


def _window_attention(q, kw, vw, sink_logits, score_bound, classes, tq):
    T = q.shape[0]
    sink_bound = jnp.max(jnp.abs(sink_logits.astype(F32))) * LOG2E
    bounded = ((1.02 * score_bound <= WINDOW_BOUND_MAX) & (sink_bound <= WINDOW_BOUND_MAX)).astype(I32)[None]
    n_tiles = T // tq
    hb = tq // Q_BLOCK
    n_hblk = T // Q_BLOCK
    r = jnp.arange(Q_BLOCK, dtype=I32)[:, None]
    j = jnp.arange(3 * Q_BLOCK, dtype=I32)[None, :]
    rel = jnp.abs(j - Q_BLOCK - r).astype(F32)
    slopes = jnp.asarray(np.array([2.0 ** (-8.0 * (h + 1) / N_HEADS_D) for h in range(N_HEADS_D)], np.float32))
    bias = jnp.where(rel[None] <= WINDOW, -slopes[:, None, None] * rel[None] * LOG2E, NEG_INF)
    bias = bias.reshape(N_KV_D, GRP_D * Q_BLOCK, 3 * Q_BLOCK)
    sink = jnp.repeat(sink_logits.astype(F32) * LOG2E, Q_BLOCK).reshape(N_KV_D, GRP_D * Q_BLOCK, 1)
    place = _lane_place(128, 256, [0, 1, 2, 3]).reshape(GRP_D * 128, 256)
    seq_tiles = []
    t0 = 0
    for nseq, S, _ in classes:
        cnt = nseq * S // tq
        seq_tiles.append((t0, t0 + cnt, S // tq))
        t0 += cnt
    prev_map = lambda i, h, fl: (jnp.maximum(i * hb - 1, 0), h)
    next_map = lambda i, h, fl: (jnp.minimum((i + 1) * hb, n_hblk - 1), h)
    grid_spec = pltpu.PrefetchScalarGridSpec(
        num_scalar_prefetch=1,
        grid=(n_tiles, N_KV_D),
        in_specs=[
            pl.BlockSpec((tq, 256), lambda i, h, fl: (i, h)),
            pl.BlockSpec((Q_BLOCK, 128), prev_map),
            pl.BlockSpec((tq, 128), lambda i, h, fl: (i, h)),
            pl.BlockSpec((Q_BLOCK, 128), next_map),
            pl.BlockSpec((Q_BLOCK, 128), prev_map),
            pl.BlockSpec((tq, 128), lambda i, h, fl: (i, h)),
            pl.BlockSpec((Q_BLOCK, 128), next_map),
            pl.BlockSpec((1, GRP_D * Q_BLOCK, 3 * Q_BLOCK), lambda i, h, fl: (h, 0, 0)),
            pl.BlockSpec((1, GRP_D * Q_BLOCK, 1), lambda i, h, fl: (h, 0, 0)),
            pl.BlockSpec((GRP_D * 128, 256), lambda i, h, fl: (0, 0)),
        ],
        out_specs=pl.BlockSpec((tq, 256), lambda i, h, fl: (i, h)),
    )
    return pl.pallas_call(
        functools.partial(_wattn_kernel, tq=tq, seq_tiles=tuple(seq_tiles)),
        grid_spec=grid_spec,
        out_shape=jax.ShapeDtypeStruct((T, 512), BF16),
        compiler_params=_cparams(("parallel", "parallel")),
        name="window_attention",
    )(bounded, q, kw, kw, kw, vw, vw, vw, bias, sink, place)


HALO = 16


def _outproj_odd_kernel(bg_ref, u_ref, up_ref, un_ref, a_ref, x_ref, cw_ref, cb_ref, wc_ref, wa_ref,
                        g_ref, b_ref, rw_ref, rb_ref, x1_ref, idx_ref, wt_ref, xb_ref,
                        *, alpha, ts, seq_tiles):
    i = pl.program_id(0)
    first = i < 0
    last = i < 0
    for lo, hi, per in seq_tiles:
        inside = (i >= lo) & (i < hi)
        first = first | (inside & ((i - lo) % per == 0))
        last = last | (inside & ((i - lo) % per == per - 1))
    u = u_ref[...].astype(F32)
    prev_row = jnp.where(first, 0.0, up_ref[HALO - 1:HALO, :].astype(F32))
    next_row = jnp.where(last, 0.0, un_ref[0:1, :].astype(F32))
    rid = lax.broadcasted_iota(I32, u.shape, 0)
    ud = jnp.where(rid == 0, prev_row, pltpu.roll(u, 1, 0))
    uu = jnp.where(rid == ts - 1, next_row, pltpu.roll(u, ts - 1, 0))
    cw = cw_ref[...]
    y = ud * cw[0:1] + u * cw[1:2] + uu * cw[2:3] + cb_ref[...]
    c = (bg_ref[...].astype(F32) * y).astype(BF16)
    nr = ts // ROW_SPLITS
    for h in range(ROW_SPLITS):
        r0 = h * nr
        m = (jnp.dot(c[r0:r0 + nr], wc_ref[...], preferred_element_type=F32)
             + jnp.dot(a_ref[r0:r0 + nr, :], wa_ref[...], preferred_element_type=F32))
        _norm_route_store(r0, nr, m, x_ref[r0:r0 + nr, :], g_ref, b_ref, rw_ref, rb_ref, x1_ref, idx_ref,
                          wt_ref, xb_ref, alpha)


def _outproj_odd(bg, u, a, x, conv_w, conv_b, w_out, ln_g, ln_b, router, alpha, classes, ts):
    T = x.shape[0]
    hb = ts // HALO
    n_h = T // HALO
    seq_tiles = []
    t0 = 0
    for nseq, S, _ in classes:
        cnt = nseq * S // ts
        seq_tiles.append((t0, t0 + cnt, S // ts))
        t0 += cnt
    out_specs, out_shapes = _mix_out_specs(ts, T)
    return pl.pallas_call(
        functools.partial(_outproj_odd_kernel, alpha=alpha, ts=ts, seq_tiles=tuple(seq_tiles)),
        grid=(T // ts,),
        in_specs=[
            pl.BlockSpec((ts, D_CONV), lambda i: (i, 0)),
            pl.BlockSpec((ts, D_CONV), lambda i: (i, 0)),
            pl.BlockSpec((HALO, D_CONV), lambda i: (jnp.maximum(i * hb - 1, 0), 0)),
            pl.BlockSpec((HALO, D_CONV), lambda i: (jnp.minimum((i + 1) * hb, n_h - 1), 0)),
            pl.BlockSpec((ts, 512), lambda i: (i, 0)),
            pl.BlockSpec((ts, D_MODEL), lambda i: (i, 0)),
            pl.BlockSpec((3, D_CONV), lambda i: (0, 0)),
            pl.BlockSpec((1, D_CONV), lambda i: (0, 0)),
            pl.BlockSpec((D_CONV, D_MODEL), lambda i: (0, 0)),
            pl.BlockSpec((512, D_MODEL), lambda i: (0, 0)),
            pl.BlockSpec((1, D_MODEL), lambda i: (0, 0)),
            pl.BlockSpec((1, D_MODEL), lambda i: (0, 0)),
        ] + _ROUTER_SPECS,
        out_specs=out_specs,
        out_shape=out_shapes,
        compiler_params=_cparams(("parallel",)),
        name="outproj_odd",
    )(bg, u, u, u, a, x, conv_w.astype(F32), conv_b.astype(F32)[None, :],
      w_out[:D_CONV].astype(BF16), w_out[D_CONV:].astype(BF16), ln_g[None, :], ln_b[None, :], *router)


def _tile(n, cap):
    t = cap
    while n % t:
        t //= 2
    return t


def kernel(x_prompt, x_sample, w_in_even, fourier_norm_g, q_norm_g, k_norm_g, w_out_even, w_in_odd, conv_w,
           conv_b, sink_logits, w_out_odd, ln_mix_g, ln_mix_b, ln_ffn_g, ln_ffn_b, router_w, router_b,
           w_gate, w_up, w_down):
    depth = ln_mix_g.shape[0]
    alpha = float((2 * depth) ** 0.25)
    bp, sp, _ = x_prompt.shape
    bs, ss, _ = x_sample.shape
    classes = ((bp, sp, 0), (bs, ss, bp * sp))
    T = bp * sp + bs * ss
    min_s = min(sp, ss)
    ts = _tile(min_s, 512)
    tq_w = _tile(min_s, 512)
    tm = 1024
    tc = _tile(min_s, 256)
    x_parts = [x_prompt.reshape(bp * sp, D_MODEL), x_sample.reshape(bs * ss, D_MODEL)]
    router = _router_operands(router_w, router_b)
    for l in range(depth):
        i = l // 2
        if l % 2 == 0:
            uf, qk, v_aug = _inproj_even(x_parts, w_in_even[i], fourier_norm_g[i], q_norm_g[i], k_norm_g[i],
                                         classes, ts)
            bounded = _score_bounded(q_norm_g[i], k_norm_g[i])
            f_parts, a_parts = [], []
            for nseq, S, tok0 in classes:
                f_parts.append(_fourier_mix(uf[tok0:tok0 + nseq * S], nseq, S))
                a_parts.append(_global_attention(qk, v_aug, bounded, nseq, S, tok0, _tile(S, 1024),
                                                 _tile(S, 2048)))
            x1, idx, wt, xb = _outproj_even(f_parts, a_parts, x_parts, w_out_even[i], ln_mix_g[l], ln_mix_b[l],
                                            router, alpha, ts)
        else:
            x = x_parts[0]
            bg, u, q, kw, vw, score_bound = _inproj_odd(x, w_in_odd[i], ts)
            a = _window_attention(q, kw, vw, sink_logits[i], score_bound, classes, tq_w)
            x1, idx, wt, xb = _outproj_odd(bg, u, a, x, conv_w[i], conv_b[i], w_out_odd[i], ln_mix_g[l],
                                           ln_mix_b[l], router, alpha, classes, ts)
        out_rows = [bp * sp, bs * ss] if l == depth - 1 else [T]
        x_parts = _moe_layer(x1, xb, idx, wt, w_gate[l], w_up[l], w_down[l], ln_ffn_g[l], ln_ffn_b[l], alpha,
                             tm, tc, out_rows)
    if len(x_parts) == 1:
        x_parts = [x_parts[0][:bp * sp], x_parts[0][bp * sp:]]
    return (x_parts[0].reshape(bp, sp, D_MODEL), x_parts[1].reshape(bs, ss, D_MODEL))
```

```python
import functools
import math

import numpy as np
import jax
import jax.numpy as jnp
from jax import lax
from jax.experimental import pallas as pl
from jax.experimental.pallas import tpu as pltpu

F32 = jnp.float32
BF16 = jnp.bfloat16
I32 = jnp.int32

D_MODEL = 1024
HEAD_DIM = 64
GRID_W = 64
Q_BLOCK = 128
WINDOW = 128
ROPE_THETA = 10000.0
N_FOURIER_GROUPS = 4
D_FOURIER = 256
N_HEADS_B = 12
N_KV_B = 4
GRP_B = 3
D_CONV = 512
N_HEADS_D = 8
N_KV_D = 2
GRP_D = 4
N_EXPERTS = 16
N_GROUPS = 4
EXPERTS_PER_GROUP = 4
D_FF = 512
LN_EPS = 1e-5
RMS_EPS = 1e-6
NEG_INF = -1e30
LOG2E = 1.4426950408889634
FFT_S2 = 128
V7X_VMEM_LIMIT = 48 * 1024 * 1024


def _cparams(sem):
    return pltpu.CompilerParams(dimension_semantics=sem, vmem_limit_bytes=V7X_VMEM_LIMIT)


def _layer_norm(y, g, b):
    mu = jnp.mean(y, axis=-1, keepdims=True)
    yc = y - mu
    var = jnp.mean(yc * yc, axis=-1, keepdims=True)
    return yc * lax.rsqrt(var + LN_EPS) * g + b


N_NORM_COLS = D_FOURIER + 1024


def _parts_specs(parts, ts):
    specs, bounds, lo = [], [], 0
    for p in parts:
        n = p.shape[0] // ts
        specs.append(pl.BlockSpec((ts, p.shape[1]), lambda i, lo=lo, n=n: (jnp.clip(i - lo, 0, n - 1), 0)))
        lo += n
        bounds.append(lo)
    return specs, tuple(bounds)


def _pick_rows(refs, bounds, r0, nr):
    i = pl.program_id(0)
    out = refs[-1][r0:r0 + nr, :]
    for ref, hi in reversed(list(zip(refs[:-1], bounds[:-1]))):
        out = jnp.where(i < hi, ref[r0:r0 + nr, :], out)
    return out


def _inproj_even_kernel(*refs, nx, bounds):
    x_refs = refs[:nx]
    (w_ref, gain_ref, seg_ref, segt_ref, cos_ref, sin_ref, vone_ref, uf_ref, qk_ref, v_ref) = refs[nx:]
    nr = uf_ref.shape[0] // ROW_SPLITS
    for part in range(ROW_SPLITS):
        r0 = part * nr
        x = _pick_rows(x_refs, bounds, r0, nr).astype(BF16)
        proj = jnp.dot(x, w_ref[...], preferred_element_type=F32)
        nrm = proj[:, :N_NORM_COLS]
        sq = (nrm * nrm).astype(BF16)
        ssum = jnp.dot(sq, seg_ref[...], preferred_element_type=F32)
        r = lax.rsqrt(ssum * (1.0 / HEAD_DIM) + RMS_EPS)
        rh = r.astype(BF16)
        rl = (r - rh.astype(F32)).astype(BF16)
        rex = jnp.dot(jnp.concatenate([rh, rl], axis=1), segt_ref[...], preferred_element_type=F32)
        y = nrm * rex * gain_ref[...]
        uf_ref[r0:r0 + nr, :] = y[:, :D_FOURIER].astype(BF16)
        yq = y[:, D_FOURIER:]
        c = jnp.concatenate([cos_ref[r0:r0 + nr, :]] * 8, axis=1)
        s = jnp.concatenate([sin_ref[r0:r0 + nr, :]] * 8, axis=1)
        lane = lax.broadcasted_iota(I32, yq.shape, 1)
        first = (lane & 31) < 16
        sw = jnp.where(first, pltpu.roll(yq, 1024 - 16, 1), pltpu.roll(yq, 16, 1))
        qk_ref[r0:r0 + nr, :] = (yq * c + sw * s).astype(BF16)
        vv = proj[:, N_NORM_COLS:] + vone_ref[...]
        for h in range(N_KV_B):
            v_ref[h, r0:r0 + nr, :] = vv[:, 128 * h:128 * (h + 1)].astype(BF16)


def _rope_tables(n_pos):
    t = jnp.arange(n_pos)
    row = (t // GRID_W).astype(F32)
    col = (t % GRID_W).astype(F32)
    n_freq = HEAD_DIM // 4
    inv_freq = ROPE_THETA ** (-jnp.arange(n_freq, dtype=F32) / n_freq)
    ar = row[:, None] * inv_freq
    ac = col[:, None] * inv_freq
    ang = jnp.concatenate([ar, ar, ac, ac], axis=1)
    sign = jnp.asarray(np.tile(np.repeat(np.array([-1.0, 1.0], np.float32), 16), 2))
    cos = jnp.cos(ang)
    sin = jnp.sin(ang) * sign
    return jnp.concatenate([cos, cos], axis=1), jnp.concatenate([sin, sin], axis=1)


def _pos_block_map(classes, ts):
    bounds = []
    tile0 = 0
    for nseq, S, _ in classes:
        n_tiles = nseq * S // ts
        bounds.append((tile0, tile0 + n_tiles, S // ts))
        tile0 += n_tiles

    def fn(i):
        out = (i - bounds[-1][0]) % bounds[-1][2]
        for lo, hi, per in reversed(bounds[:-1]):
            out = jnp.where(i < hi, (i - lo) % per, out)
        return out

    return fn


def _inproj_even(x_parts, w_in, f_g, q_g, k_g, classes, ts):
    T = sum(p.shape[0] for p in x_parts)
    x_specs, bounds = _parts_specs(x_parts, ts)
    wf = w_in[:, :D_FOURIER]
    wq = w_in[:, D_FOURIER:D_FOURIER + 768].reshape(D_MODEL, N_KV_B, GRP_B * HEAD_DIM)
    wk = w_in[:, D_FOURIER + 768:D_FOURIER + 1024].reshape(D_MODEL, N_KV_B, HEAD_DIM)
    wv = w_in[:, D_FOURIER + 1024:].reshape(D_MODEL, N_KV_B, HEAD_DIM)
    wqk = jnp.concatenate([wk, wq], axis=2).reshape(D_MODEL, 1024)
    wvp = jnp.concatenate([wv, jnp.zeros_like(wv)], axis=2).reshape(D_MODEL, 512)
    w = jnp.concatenate([wf, wqk, wvp], axis=1).astype(BF16)
    qscale = HEAD_DIM ** -0.5 * LOG2E
    gqk = jnp.tile(jnp.concatenate([k_g, q_g * qscale, q_g * qscale, q_g * qscale]), N_KV_B)
    gain = jnp.concatenate([f_g, gqk])[None, :].astype(F32)
    seg_np = np.zeros((N_NORM_COLS, 128), np.float32)
    seg_np[np.arange(N_NORM_COLS), np.arange(N_NORM_COLS) // HEAD_DIM] = 1.0
    seg = jnp.asarray(seg_np, BF16)
    segt = jnp.asarray(np.concatenate([seg_np.T, seg_np.T], axis=0), BF16)
    max_s = max(S for _, S, _ in classes)
    cos, sin = _rope_tables(max_s)
    vone_np = np.zeros((1, 512), np.float32)
    for h in range(N_KV_B):
        vone_np[0, 128 * h + 64:128 * (h + 1)] = 1.0
    vone = jnp.asarray(vone_np)
    posmap = _pos_block_map(classes, ts)
    n_w = w.shape[1]
    return pl.pallas_call(
        functools.partial(_inproj_even_kernel, nx=len(x_parts), bounds=bounds),
        grid=(T // ts,),
        in_specs=x_specs + [
            pl.BlockSpec((D_MODEL, n_w), lambda i: (0, 0)),
            pl.BlockSpec((1, N_NORM_COLS), lambda i: (0, 0)),
            pl.BlockSpec((N_NORM_COLS, 128), lambda i: (0, 0)),
            pl.BlockSpec((256, N_NORM_COLS), lambda i: (0, 0)),
            pl.BlockSpec((ts, 128), lambda i: (posmap(i), 0)),
            pl.BlockSpec((ts, 128), lambda i: (posmap(i), 0)),
            pl.BlockSpec((1, 512), lambda i: (0, 0)),
        ],
        out_specs=[
            pl.BlockSpec((ts, D_FOURIER), lambda i: (i, 0)),
            pl.BlockSpec((ts, 1024), lambda i: (i, 0)),
            pl.BlockSpec((N_KV_B, ts, 128), lambda i: (0, i, 0)),
        ],
        out_shape=[
            jax.ShapeDtypeStruct((T, D_FOURIER), BF16),
            jax.ShapeDtypeStruct((T, 1024), BF16),
            jax.ShapeDtypeStruct((N_KV_B, T, 128), BF16),
        ],
        compiler_params=_cparams(("parallel",)),
        name="inproj_even",
    )(*x_parts, w, gain, seg, segt, cos, sin, vone)


FFT_TB = 16


def _fft1_kernel(x_ref, kw_ref, twr_ref, twi_ref, zr_ref, zi_ref, *, s1, bs):
    n = s1 * FFT_TB
    twr = twr_ref[0]
    twi = twi_ref[0]
    for q in range(bs):
        x2 = x_ref[q * s1:(q + 1) * s1].reshape(n, D_FOURIER)
        z = jnp.dot(kw_ref[...], x2, preferred_element_type=F32)
        zr = z[:n]
        zi = z[n:]
        zr_ref[q * s1:(q + 1) * s1] = (zr * twr - zi * twi).astype(BF16).reshape(s1, FFT_TB, D_FOURIER)
        zi_ref[q * s1:(q + 1) * s1] = (zr * twi + zi * twr).astype(BF16).reshape(s1, FFT_TB, D_FOURIER)


def _fft2_kernel(zr_ref, zi_ref, w2a_ref, w2b_ref, mix_ref, o_ref, *, cb):
    s2 = FFT_S2
    for c in range(cb):
        pp = (jnp.dot(w2a_ref[...], zr_ref[0, c], preferred_element_type=F32)
              + jnp.dot(w2b_ref[...], zi_ref[0, c], preferred_element_type=F32))
        f = (jnp.dot(pp[:s2].astype(BF16), mix_ref[:D_FOURIER], preferred_element_type=F32)
             + jnp.dot(pp[s2:].astype(BF16), mix_ref[D_FOURIER:], preferred_element_type=F32))
        o_ref[0, :, D_FOURIER * c:D_FOURIER * (c + 1)] = f.astype(BF16)


def _dft_mats(n):
    k = jnp.arange(n, dtype=I32)
    ang = (2.0 * math.pi / n) * ((k[:, None] * k[None, :]) % n).astype(F32)
    return jnp.cos(ang), jnp.sin(ang)


def _fourier_mix(uf, nseq, S, tok0):
    s2 = FFT_S2
    s1 = S // s2
    tb = FFT_TB
    n = s1 * tb
    x3 = uf.reshape(uf.shape[0] // s2, s2, D_FOURIER)
    blk0 = tok0 // S
    c1, sn1 = _dft_mats(s1)
    w1 = jnp.concatenate([c1, -sn1], axis=0)
    kw = jnp.kron(w1, jnp.eye(tb, dtype=F32)).astype(BF16)
    cc = jnp.arange(s1, dtype=I32)[:, None]
    bb = jnp.arange(s2, dtype=I32)[None, :]
    ang = (2.0 * math.pi / S) * ((cc * bb) % S).astype(F32)

    def table(t):
        t = t.reshape(s1, s2 // tb, tb).transpose(1, 0, 2).reshape(s2 // tb, n, 1)
        return jnp.broadcast_to(t, (s2 // tb, n, D_FOURIER))
    twr = table(jnp.cos(ang))
    twi = table(-jnp.sin(ang))
    bs = math.gcd(math.gcd(nseq, max(1, 64 // s1)), blk0) if blk0 else math.gcd(nseq, max(1, 64 // s1))
    zr, zi = pl.pallas_call(
        functools.partial(_fft1_kernel, s1=s1, bs=bs),
        grid=(s2 // tb, nseq // bs),
        in_specs=[
            pl.BlockSpec((bs * s1, tb, D_FOURIER), lambda j, b: (blk0 // bs + b, j, 0)),
            pl.BlockSpec((2 * n, n), lambda j, b: (0, 0)),
            pl.BlockSpec((1, n, D_FOURIER), lambda j, b: (j, 0, 0)),
            pl.BlockSpec((1, n, D_FOURIER), lambda j, b: (j, 0, 0)),
        ],
        out_specs=[
            pl.BlockSpec((bs * s1, tb, D_FOURIER), lambda j, b: (b, j, 0)),
            pl.BlockSpec((bs * s1, tb, D_FOURIER), lambda j, b: (b, j, 0)),
        ],
        out_shape=[jax.ShapeDtypeStruct((nseq * s1, s2, D_FOURIER), BF16)] * 2,
        compiler_params=_cparams(("parallel", "parallel")),
        name="fourier_stage1",
    )(x3, kw, twr, twi)
    zr = zr.reshape(nseq, s1, s2, D_FOURIER)
    zi = zi.reshape(nseq, s1, s2, D_FOURIER)
    c2, sn2 = _dft_mats(s2)
    w2a = jnp.concatenate([c2, -sn2], axis=0).astype(BF16)
    w2b = jnp.concatenate([sn2, c2], axis=0).astype(BF16)
    gc, gs = _dft_mats(HEAD_DIM)
    scale = 1.0 / math.sqrt(S * HEAD_DIM)
    eye = jnp.eye(N_FOURIER_GROUPS, dtype=F32)
    mix = (jnp.concatenate([jnp.kron(eye, gc), jnp.kron(eye, gs)], axis=0) * scale).astype(BF16)
    cb = min(8, s1)
    out = pl.pallas_call(
        functools.partial(_fft2_kernel, cb=cb),
        grid=(nseq, s1 // cb),
        in_specs=[
            pl.BlockSpec((1, cb, s2, D_FOURIER), lambda b, j: (b, j, 0, 0)),
            pl.BlockSpec((1, cb, s2, D_FOURIER), lambda b, j: (b, j, 0, 0)),
            pl.BlockSpec((2 * s2, s2), lambda b, j: (0, 0)),
            pl.BlockSpec((2 * s2, s2), lambda b, j: (0, 0)),
            pl.BlockSpec((2 * D_FOURIER, D_FOURIER), lambda b, j: (0, 0)),
        ],
        out_specs=pl.BlockSpec((1, s2, cb * D_FOURIER), lambda b, j: (b, 0, j)),
        out_shape=jax.ShapeDtypeStruct((nseq, s2, s1 * D_FOURIER), BF16),
        compiler_params=_cparams(("parallel", "parallel")),
        name="fourier_stage2",
    )(zr, zi, w2a, w2b, mix)
    return out.reshape(nseq * S, D_FOURIER)


SCORE_BOUND_MAX = 100.0


def _gattn_kernel(bounded_ref, q_ref, k_ref, v_ref, place_ref, o_ref, *, tq, tk, S):
    q3 = q_ref[...]
    qs = [q3[:, HEAD_DIM * (g + 1):HEAD_DIM * (g + 2)] for g in range(GRP_B)]

    def chunk(j):
        off = pl.multiple_of(j * tk, tk)
        return k_ref[pl.ds(off, tk), 0:HEAD_DIM], v_ref[0, pl.ds(off, tk), :]

    def scores(g, kc):
        return lax.dot_general(qs[g], kc, (((1,), (1,)), ((), ())), preferred_element_type=F32)

    def finish(accs):
        out = jnp.zeros((tq, 256), F32)
        for g in range(GRP_B):
            o = (accs[g] / accs[g][:, HEAD_DIM:HEAD_DIM + 1]).astype(BF16)
            out = out + jnp.dot(o, place_ref[g], preferred_element_type=F32)
        o_ref[...] = out.astype(BF16)

    @pl.when(bounded_ref[0] == 1)
    def _():
        def body(j, accs):
            kc, vc = chunk(j)
            return tuple(accs[g] + jnp.dot(jnp.exp2(scores(g, kc)).astype(BF16), vc,
                                           preferred_element_type=F32) for g in range(GRP_B))
        finish(lax.fori_loop(0, S // tk, body, tuple(jnp.zeros((tq, 128), F32) for _ in range(GRP_B))))

    @pl.when(bounded_ref[0] != 1)
    def _():
        def body(j, carry):
            kc, vc = chunk(j)
            new = []
            for g in range(GRP_B):
                m_prev, acc = carry[g]
                s = scores(g, kc)
                m_new = jnp.maximum(m_prev, jnp.max(s, axis=1, keepdims=True))
                p = jnp.exp2(s - m_new).astype(BF16)
                acc = jnp.exp2(m_prev - m_new) * acc + jnp.dot(p, vc, preferred_element_type=F32)
                new.append((m_new, acc))
            return tuple(new)
        init = tuple((jnp.full((tq, 1), NEG_INF, F32), jnp.zeros((tq, 128), F32)) for _ in range(GRP_B))
        fin = lax.fori_loop(0, S // tk, body, init)
        finish([fin[g][1] for g in range(GRP_B)])


def _score_bounded(q_g, k_g):
    bound = 1.05 * HEAD_DIM * (HEAD_DIM ** -0.5 * LOG2E) * jnp.max(jnp.abs(q_g)) * jnp.max(jnp.abs(k_g))
    return (bound <= SCORE_BOUND_MAX).astype(I32)[None]


def _lane_place(n_src, n_dst, groups):
    pm = np.zeros((len(groups), n_src, n_dst), np.float32)
    for gi, slot in enumerate(groups):
        pm[gi, np.arange(HEAD_DIM), HEAD_DIM * slot + np.arange(HEAD_DIM)] = 1.0
    return jnp.asarray(pm, BF16)


def _global_attention(qk, v_aug, bounded, nseq, S, tok0, tq, tk):
    n_qt = S // tq
    qt0 = tok0 // tq
    s0 = tok0 // S
    place = _lane_place(128, 256, [1, 2, 3])
    grid_spec = pltpu.PrefetchScalarGridSpec(
        num_scalar_prefetch=1,
        grid=(nseq, N_KV_B, n_qt),
        in_specs=[
            pl.BlockSpec((tq, 256), lambda b, h, i, fl: (qt0 + b * n_qt + i, h)),
            pl.BlockSpec((S, 256), lambda b, h, i, fl: (s0 + b, h)),
            pl.BlockSpec((1, S, 128), lambda b, h, i, fl: (h, s0 + b, 0)),
            pl.BlockSpec((GRP_B, 128, 256), lambda b, h, i, fl: (0, 0, 0)),
        ],
        out_specs=pl.BlockSpec((tq, 256), lambda b, h, i, fl: (b * n_qt + i, h)),
    )
    return pl.pallas_call(
        functools.partial(_gattn_kernel, tq=tq, tk=tk, S=S),
        grid_spec=grid_spec,
        out_shape=jax.ShapeDtypeStruct((nseq * S, 1024), BF16),
        compiler_params=_cparams(("parallel", "parallel", "arbitrary")),
        name="global_attention",
    )(bounded, qk, qk, v_aug, place)


def _route(x1, rw_ref, rb_ref):
    n = x1.shape[0]
    xh = x1.astype(BF16)
    xl = (x1 - xh.astype(F32)).astype(BF16)
    o = jnp.dot(jnp.concatenate([xh, xl], axis=0), rw_ref[...], preferred_element_type=F32)
    lg = o[:n, :128] + o[:n, 128:] + o[n:, :128]
    lgt = lg.T[:N_EXPERTS]
    sc = 1.0 / (1.0 + jnp.exp(-lgt))
    bi = sc + rb_ref[...]
    srow = [sc[e:e + 1] for e in range(N_EXPERTS)]
    brow = [bi[e:e + 1] for e in range(N_EXPERTS)]
    gsel = None
    best = None
    for g in range(N_GROUPS):
        a, b, c, d = brow[4 * g:4 * g + 4]
        m1, n1 = jnp.maximum(a, b), jnp.minimum(a, b)
        m2, n2 = jnp.maximum(c, d), jnp.minimum(c, d)
        gs = jnp.maximum(m1, m2) + jnp.maximum(jnp.minimum(m1, m2), jnp.maximum(n1, n2))
        if g == 0:
            best, gsel = gs, jnp.zeros(gs.shape, I32)
        else:
            better = gs > best
            gsel = jnp.where(better, g, gsel)
            best = jnp.where(better, gs, best)
    masked = [jnp.where(gsel == (e // EXPERTS_PER_GROUP), brow[e], NEG_INF) for e in range(N_EXPERTS)]
    i1 = jnp.zeros(gsel.shape, I32)
    b1 = masked[0]
    s1 = srow[0]
    for e in range(1, N_EXPERTS):
        better = masked[e] > b1
        i1 = jnp.where(better, e, i1)
        b1 = jnp.where(better, masked[e], b1)
        s1 = jnp.where(better, srow[e], s1)
    i2 = jnp.full(gsel.shape, -1, I32)
    b2 = jnp.full(b1.shape, -jnp.inf, F32)
    s2 = jnp.zeros(b1.shape, F32)
    for e in range(N_EXPERTS):
        better = (masked[e] > b2) & (i1 != e)
        i2 = jnp.where(better, e, i2)
        b2 = jnp.where(better, masked[e], b2)
        s2 = jnp.where(better, srow[e], s2)
    den = s1 + s2
    return i1, i2, s1 / den, s2 / den


ROW_SPLITS = 2


def _norm_route_store(r0, nr, m, x, g_ref, b_ref, rw_ref, rb_ref, x1_ref, idx_ref, wt_ref, xb_ref, alpha):
    x1 = _layer_norm(alpha * x + m, g_ref[...], b_ref[...])
    x1_ref[r0:r0 + nr, :] = x1
    xb_ref[r0:r0 + nr, :] = x1.astype(BF16)
    i1, i2, w1, w2 = _route(x1, rw_ref, rb_ref)
    rid = lax.broadcasted_iota(I32, (8, nr), 0)
    idx_ref[:, r0:r0 + nr] = jnp.where(rid == 0, i1, jnp.where(rid == 1, i2, 0))
    wt_ref[:, r0:r0 + nr] = jnp.where(rid == 0, w1, jnp.where(rid == 1, w2, 0.0))


def _router_operands(router_w, router_b):
    rw = jnp.zeros((D_MODEL, 128), F32).at[:, :N_EXPERTS].set(router_w.astype(F32))
    rwh = rw.astype(BF16)
    rwl = (rw - rwh.astype(F32)).astype(BF16)
    rb = router_b.astype(F32)[:, None]
    return jnp.concatenate([rwh, rwl], axis=1), rb


_ROUTER_SPECS = [
    pl.BlockSpec((D_MODEL, 256), lambda i: (0, 0)),
    pl.BlockSpec((N_EXPERTS, 1), lambda i: (0, 0)),
]


def _mix_out_specs(ts, T):
    specs = [
        pl.BlockSpec((ts, D_MODEL), lambda i: (i, 0)),
        pl.BlockSpec((8, ts), lambda i: (0, i)),
        pl.BlockSpec((8, ts), lambda i: (0, i)),
        pl.BlockSpec((ts, D_MODEL), lambda i: (i, 0)),
    ]
    shapes = [
        jax.ShapeDtypeStruct((T, D_MODEL), F32),
        jax.ShapeDtypeStruct((8, T), I32),
        jax.ShapeDtypeStruct((8, T), F32),
        jax.ShapeDtypeStruct((T, D_MODEL), BF16),
    ]
    return specs, shapes


def _outproj_even_kernel(*refs, nf, na, nx, fb, ab, xb, alpha):
    f_refs, a_refs, x_refs = refs[:nf], refs[nf:nf + na], refs[nf + na:nf + na + nx]
    wf_ref, wa_ref, g_ref, b_ref, rw_ref, rb_ref, x1_ref, idx_ref, wt_ref, xb_ref = refs[nf + na + nx:]
    nr = x1_ref.shape[0] // ROW_SPLITS
    for h in range(ROW_SPLITS):
        r0 = h * nr
        m = (jnp.dot(_pick_rows(f_refs, fb, r0, nr), wf_ref[...], preferred_element_type=F32)
             + jnp.dot(_pick_rows(a_refs, ab, r0, nr), wa_ref[...], preferred_element_type=F32))
        xr = _pick_rows(x_refs, xb, r0, nr)
        _norm_route_store(r0, nr, m, xr, g_ref, b_ref, rw_ref, rb_ref, x1_ref, idx_ref, wt_ref, xb_ref, alpha)


def _outproj_even(f_parts, a_parts, x_parts, w_out, ln_g, ln_b, router, alpha, ts):
    T = sum(p.shape[0] for p in x_parts)
    f_specs, fb = _parts_specs(f_parts, ts)
    a_specs, ab = _parts_specs(a_parts, ts)
    x_specs, xb = _parts_specs(x_parts, ts)
    wf = w_out[:D_FOURIER].astype(BF16)
    wa = w_out[D_FOURIER:].reshape(N_KV_B, GRP_B * HEAD_DIM, D_MODEL)
    wa = jnp.concatenate([jnp.zeros((N_KV_B, HEAD_DIM, D_MODEL), w_out.dtype), wa], axis=1)
    wa = wa.reshape(1024, D_MODEL).astype(BF16)
    out_specs, out_shapes = _mix_out_specs(ts, T)
    return pl.pallas_call(
        functools.partial(_outproj_even_kernel, nf=len(f_parts), na=len(a_parts), nx=len(x_parts),
                          fb=fb, ab=ab, xb=xb, alpha=alpha),
        grid=(T // ts,),
        in_specs=f_specs + a_specs + x_specs + [
            pl.BlockSpec((D_FOURIER, D_MODEL), lambda i: (0, 0)),
            pl.BlockSpec((1024, D_MODEL), lambda i: (0, 0)),
            pl.BlockSpec((1, D_MODEL), lambda i: (0, 0)),
            pl.BlockSpec((1, D_MODEL), lambda i: (0, 0)),
        ] + _ROUTER_SPECS,
        out_specs=out_specs,
        out_shape=out_shapes,
        compiler_params=_cparams(("parallel",)),
        name="outproj_even",
    )(*f_parts, *a_parts, *x_parts, wf, wa, ln_g[None, :], ln_b[None, :], *router)


CHUNK = 16


def _stage_rows(td):
    return -(-(2 * td + N_EXPERTS * (CHUNK - 1)) // 256) * 256


def _tab_len(td):
    return -(-(1 + 2 * td // CHUNK + N_EXPERTS) // 128) * 128


def _moe_plan(idx, td, tm):
    T = idx.shape[1]
    nt = T // td
    e = idx.T.reshape(-1)
    oh = (e[:, None] == jnp.arange(N_EXPERTS, dtype=I32)[None, :]).astype(I32)
    csum = jnp.cumsum(oh, axis=0)
    tile_end = csum.reshape(nt, 2 * td, N_EXPERTS)[:, -1, :]
    base = jnp.concatenate([jnp.zeros((1, N_EXPERTS), I32), tile_end[:-1]], axis=0)
    cnt = tile_end - base
    c8 = ((cnt + CHUNK - 1) // CHUNK) * CHUNK
    off8 = jnp.cumsum(c8, axis=1) - c8
    base8 = jnp.cumsum(c8, axis=0) - c8
    seg = jnp.sum(c8, axis=0)
    padded = ((seg + tm - 1) // tm) * tm
    ends = jnp.cumsum(padded)
    starts = ends - padded
    dst = starts[None, :] + base8
    shift = jnp.repeat(off8 - base, 2 * td, axis=0)
    lpos = (jnp.sum((csum + shift) * oh, axis=1) - 1).astype(I32)
    n_rows = -(-(2 * T + nt * N_EXPERTS * (CHUNK - 1)) // tm) * tm + N_EXPERTS * tm
    tile_start = jnp.arange(n_rows // tm, dtype=I32) * tm
    tile_e = jnp.minimum(jnp.sum((ends[None, :] <= tile_start[:, None]).astype(I32), axis=1), N_EXPERTS - 1)
    n_used = (ends[-1] // tm).astype(I32)[None]
    nc = c8 // CHUNK
    cum = jnp.cumsum(nc, axis=1)
    n_chunk_max = 2 * td // CHUNK + N_EXPERTS
    c_idx = jnp.arange(n_chunk_max, dtype=I32)
    e_c = jnp.minimum(jnp.sum((cum[:, None, :] <= c_idx[None, :, None]).astype(I32), axis=2), N_EXPERTS - 1)
    oh_c = (e_c[:, :, None] == jnp.arange(N_EXPERTS, dtype=I32)[None, None, :]).astype(I32)
    srow = jnp.sum(oh_c * (dst - CHUNK * (cum - nc))[:, None, :], axis=2) + CHUNK * c_idx[None, :]
    tab = jnp.concatenate([cum[:, -1:], srow], axis=1).astype(I32)
    tab = jnp.pad(tab, ((0, 0), (0, _tab_len(td) - tab.shape[1]))).reshape(nt, 1, _tab_len(td))
    pad_lo = jnp.concatenate([starts + seg, ends[-1:]]).astype(I32)
    pad_hi = jnp.concatenate([ends, jnp.full((1,), n_rows, I32)]).astype(I32)
    lpos_tk = lpos.reshape(T, 2)
    lpos_rows = jnp.zeros((8, T), I32).at[:2].set(lpos_tk.T)
    return tab, lpos_rows, lpos_tk, tile_e.astype(I32), n_used, pad_lo, pad_hi, n_rows


def _chunk_loops(tab_ref, fn):
    total = tab_ref[0, 0, 0]

    def one(c, priority):
        fn(pl.multiple_of(CHUNK * c, CHUNK), pl.multiple_of(tab_ref[0, 0, 1 + c], CHUNK), priority)

    def body(j, carry):
        one(2 * j, 0)
        one(2 * j + 1, 1)
        return carry
    lax.fori_loop(0, total // 2, body, 0)

    @pl.when(total % 2 == 1)
    def _():
        one(total - 1, 0)
    return total


def _dispatch_kernel(lo_ref, hi_ref, tab_ref, lp_ref, x_ref, xs_hbm, stage, zbuf, nwait, sem, zsem, *, ns):
    i = pl.program_id(0)
    slot = i % 2

    def zero_copy(dst):
        return pltpu.make_async_copy(zbuf, xs_hbm.at[pl.ds(pl.multiple_of(dst, CHUNK), CHUNK)], zsem)

    @pl.when(i == 0)
    def _():
        nwait[0] = 0
        nwait[1] = 0
        zbuf[...] = jnp.zeros(zbuf.shape, BF16)
        for e in range(N_EXPERTS + 1):
            nz = (hi_ref[e] - lo_ref[e]) // CHUNK
            lax.fori_loop(0, nz, lambda j, c, e=e: (zero_copy(lo_ref[e] + CHUNK * j).start(), c)[1], 0)
        for e in range(N_EXPERTS + 1):
            nz = (hi_ref[e] - lo_ref[e]) // CHUNK
            lax.fori_loop(0, nz, lambda j, c: (zero_copy(0).wait(), c)[1], 0)

    def piece(local_row, sorted_row, s):
        return pltpu.make_async_copy(stage.at[s, pl.ds(local_row, CHUNK)],
                                     xs_hbm.at[pl.ds(sorted_row, CHUNK)], sem.at[s])

    def drain(s):
        lax.fori_loop(0, nwait[s], lambda j, c: (piece(0, 0, s).wait(), c)[1], 0)

    drain(slot)
    lp = lp_ref[...]
    rows = lax.broadcasted_iota(I32, (ns, lp.shape[1]), 0)
    perm = jnp.where((rows == lp[0:1]) | (rows == lp[1:2]), 1.0, 0.0).astype(BF16)
    stage[slot] = jnp.dot(perm, x_ref[...], preferred_element_type=F32).astype(BF16)
    nwait[slot] = _chunk_loops(tab_ref, lambda lr, sr, pr: piece(lr, sr, slot).start(priority=pr))

    @pl.when(i == pl.num_programs(0) - 1)
    def _():
        drain(0)
        drain(1)


def _moe_dispatch(x1, tab, lpos_rows, pad_lo, pad_hi, n_rows, td):
    T = x1.shape[0]
    ns = _stage_rows(td)
    grid_spec = pltpu.PrefetchScalarGridSpec(
        num_scalar_prefetch=2,
        grid=(T // td,),
        in_specs=[
            pl.BlockSpec((1, 1, _tab_len(td)), lambda i, lo, hi: (i, 0, 0), memory_space=pltpu.SMEM),
            pl.BlockSpec((8, td), lambda i, lo, hi: (0, i)),
            pl.BlockSpec((td, D_MODEL), lambda i, lo, hi: (i, 0)),
        ],
        out_specs=pl.BlockSpec(memory_space=pl.ANY),
        scratch_shapes=[pltpu.VMEM((2, ns, D_MODEL), BF16), pltpu.VMEM((CHUNK, D_MODEL), BF16),
                        pltpu.SMEM((2,), I32), pltpu.SemaphoreType.DMA((2,)), pltpu.SemaphoreType.DMA(())],
    )
    return pl.pallas_call(
        functools.partial(_dispatch_kernel, ns=ns),
        grid_spec=grid_spec,
        out_shape=jax.ShapeDtypeStruct((n_rows, D_MODEL), BF16),
        compiler_params=_cparams(("arbitrary",)),
        name="moe_dispatch",
    )(pad_lo, pad_hi, tab, lpos_rows, x1)


def _moe_kernel(te_ref, nu_ref, x_ref, wg_ref, wu_ref, wd_ref, y_ref, wgb, wub, wdb):
    i = pl.program_id(0)

    @pl.when((i == 0) | (te_ref[i] != te_ref[jnp.maximum(i - 1, 0)]))
    def _():
        wgb[...] = wg_ref[0].astype(BF16)
        wub[...] = wu_ref[0].astype(BF16)
        wdb[...] = wd_ref[0].astype(BF16)

    @pl.when(i < nu_ref[0])
    def _():
        xg = x_ref[...]
        hg = jnp.dot(xg, wgb[...], preferred_element_type=F32)
        hu = jnp.dot(xg, wub[...], preferred_element_type=F32)
        hdn = (hg / (1.0 + jnp.exp(-hg)) * hu).astype(BF16)
        y_ref[...] = jnp.dot(hdn, wdb[...], preferred_element_type=F32).astype(BF16)

    @pl.when(i >= nu_ref[0])
    def _():
        y_ref[...] = jnp.zeros(y_ref.shape, BF16)


def _moe_experts(xs, tile_e, n_used, wg, wu, wd, tm):
    n_rows = xs.shape[0]
    grid_spec = pltpu.PrefetchScalarGridSpec(
        num_scalar_prefetch=2,
        grid=(n_rows // tm,),
        in_specs=[
            pl.BlockSpec((tm, D_MODEL), lambda i, te, nu: (i, 0)),
            pl.BlockSpec((1, D_MODEL, D_FF), lambda i, te, nu: (te[i], 0, 0)),
            pl.BlockSpec((1, D_MODEL, D_FF), lambda i, te, nu: (te[i], 0, 0)),
            pl.BlockSpec((1, D_FF, D_MODEL), lambda i, te, nu: (te[i], 0, 0)),
        ],
        out_specs=pl.BlockSpec((tm, D_MODEL), lambda i, te, nu: (i, 0)),
        scratch_shapes=[pltpu.VMEM((D_MODEL, D_FF), BF16), pltpu.VMEM((D_MODEL, D_FF), BF16),
                        pltpu.VMEM((D_FF, D_MODEL), BF16)],
    )
    return pl.pallas_call(
        _moe_kernel,
        grid_spec=grid_spec,
        out_shape=jax.ShapeDtypeStruct((n_rows, D_MODEL), BF16),
        compiler_params=_cparams(("arbitrary",)),
        name="moe_experts",
    )(tile_e, n_used, xs, wg, wu, wd)


def _combine_kernel(tab_ref, tabn_ref, lp_ref, y_hbm, x_ref, wt_ref, g_ref, b_ref, *rest, ns, alpha, bounds):
    o_refs, (ystage, sem) = rest[:len(bounds)], rest[len(bounds):]
    i = pl.program_id(0)
    n = pl.num_programs(0)
    slot = i % 2

    def piece(local_row, sorted_row, s):
        return pltpu.make_async_copy(y_hbm.at[pl.ds(sorted_row, CHUNK)],
                                     ystage.at[s, pl.ds(local_row, CHUNK)], sem.at[s])

    @pl.when(i == 0)
    def _():
        ystage[...] = jnp.zeros(ystage.shape, BF16)
        _chunk_loops(tab_ref, lambda lr, sr, pr: piece(lr, sr, 0).start(priority=pr))

    @pl.when(i + 1 < n)
    def _():
        _chunk_loops(tabn_ref, lambda lr, sr, pr: piece(lr, sr, 1 - slot).start(priority=pr))

    lax.fori_loop(0, tab_ref[0, 0, 0], lambda j, c: (piece(0, 0, slot).wait(), c)[1], 0)

    ys = ystage[slot]
    lp = lp_ref[...]
    cols = lax.broadcasted_iota(I32, (lp.shape[0], ns), 1)
    w = wt_ref[...]
    tc = lp.shape[0]
    sel = jnp.concatenate([jnp.where(cols == lp[:, k:k + 1], 1.0, 0.0).astype(BF16) for k in range(2)], axis=0)
    picked = jnp.dot(sel, ys, preferred_element_type=F32)
    f = w[:, 0:1] * picked[:tc] + w[:, 1:2] * picked[tc:]
    o = _layer_norm(alpha * x_ref[...] + f, g_ref[...], b_ref[...])
    if len(bounds) == 1:
        o_refs[0][...] = o
    else:
        lo = 0
        for o_ref, hi in zip(o_refs, bounds):
            @pl.when((i >= lo) & (i < hi))
            def _(o_ref=o_ref):
                o_ref[...] = o
            lo = hi


def _moe_combine(y, tab, lpos_tk, wts, x1, ln_g, ln_b, alpha, tc, out_rows):
    T = x1.shape[0]
    n_tiles = T // tc
    ns = _stage_rows(tc)
    out_specs, out_shapes, bounds, lo = [], [], [], 0
    for rows in out_rows:
        n = rows // tc
        out_specs.append(pl.BlockSpec((tc, D_MODEL), lambda i, lo=lo, n=n: (jnp.clip(i - lo, 0, n - 1), 0)))
        out_shapes.append(jax.ShapeDtypeStruct((rows, D_MODEL), F32))
        lo += n
        bounds.append(lo)
    return pl.pallas_call(
        functools.partial(_combine_kernel, ns=ns, alpha=alpha, bounds=tuple(bounds)),
        grid=(n_tiles,),
        in_specs=[
            pl.BlockSpec((1, 1, _tab_len(tc)), lambda i: (i, 0, 0), memory_space=pltpu.SMEM),
            pl.BlockSpec((1, 1, _tab_len(tc)), lambda i: (jnp.minimum(i + 1, n_tiles - 1), 0, 0),
                         memory_space=pltpu.SMEM),
            pl.BlockSpec((tc, 2), lambda i: (i, 0)),
            pl.BlockSpec(memory_space=pl.ANY),
            pl.BlockSpec((tc, D_MODEL), lambda i: (i, 0)),
            pl.BlockSpec((tc, 2), lambda i: (i, 0)),
            pl.BlockSpec((1, D_MODEL), lambda i: (0, 0)),
            pl.BlockSpec((1, D_MODEL), lambda i: (0, 0)),
        ],
        out_specs=out_specs,
        out_shape=out_shapes,
        scratch_shapes=[pltpu.VMEM((2, ns, D_MODEL), BF16), pltpu.SemaphoreType.DMA((2,))],
        compiler_params=_cparams(("arbitrary",)),
        name="moe_combine",
    )(tab, tab, lpos_tk, y, x1, wts, ln_g[None, :], ln_b[None, :])


def _moe_layer(x1, xb, idx, wt, wg, wu, wd, ln_g, ln_b, alpha, tm, tc, out_rows):
    tab, lpos_rows, lpos_tk, tile_e, n_used, pad_lo, pad_hi, n_rows = _moe_plan(idx[:2], tc, tm)
    xs = _moe_dispatch(xb, tab, lpos_rows, pad_lo, pad_hi, n_rows, tc)
    y = _moe_experts(xs, tile_e, n_used, wg, wu, wd, tm)
    return _moe_combine(y, tab, lpos_tk, wt[:2].T, x1, ln_g, ln_b, alpha, tc, out_rows)


def _inproj_odd_kernel(x_ref, w_ref, vone_ref, bg_ref, u_ref, q_ref, k_ref, v_ref, nrm_ref):
    x = x_ref[...].astype(BF16)
    proj = jnp.dot(x, w_ref[...], preferred_element_type=F32)
    bg_ref[...] = proj[:, :512].astype(BF16)
    u_ref[...] = (proj[:, 512:1024] * proj[:, 1024:1536]).astype(BF16)
    qb = (proj[:, 1536:2048] * (HEAD_DIM ** -0.5 * LOG2E)).astype(BF16)
    kb = proj[:, 2048:2304].astype(BF16)
    q_ref[...] = qb
    k_ref[...] = kb
    v_ref[...] = (proj[:, 2304:2560] + vone_ref[...]).astype(BF16)
    qf = qb.astype(F32)
    kf = kb.astype(F32)
    qq = jnp.max(jnp.sum(qf * qf, axis=1, keepdims=True), axis=0, keepdims=True)
    kk = jnp.max(jnp.sum(kf * kf, axis=1, keepdims=True), axis=0, keepdims=True)
    rid = lax.broadcasted_iota(I32, (8, 128), 0)
    nrm_ref[0] = jnp.where(rid == 0, qq, jnp.where(rid == 1, kk, 0.0))


def _inproj_odd(x, w_in, ts):
    T = x.shape[0]
    c3 = 3 * D_CONV
    wk = w_in[:, c3 + 512:c3 + 640].reshape(D_MODEL, N_KV_D, HEAD_DIM)
    wv = w_in[:, c3 + 640:c3 + 768].reshape(D_MODEL, N_KV_D, HEAD_DIM)
    wkp = jnp.concatenate([wk, jnp.zeros_like(wk)], axis=2).reshape(D_MODEL, 256)
    wvp = jnp.concatenate([wv, jnp.zeros_like(wv)], axis=2).reshape(D_MODEL, 256)
    w = jnp.concatenate([w_in[:, :c3 + 512], wkp, wvp], axis=1).astype(BF16)
    vone_np = np.zeros((1, 256), np.float32)
    for h in range(N_KV_D):
        vone_np[0, 128 * h + 64:128 * (h + 1)] = 1.0
    n_w = w.shape[1]
    widths = [512, 512, 512, 256, 256]
    bg, u, q, kw, vw, nrm = pl.pallas_call(
        _inproj_odd_kernel,
        grid=(T // ts,),
        in_specs=[
            pl.BlockSpec((ts, D_MODEL), lambda i: (i, 0)),
            pl.BlockSpec((D_MODEL, n_w), lambda i: (0, 0)),
            pl.BlockSpec((1, 256), lambda i: (0, 0)),
        ],
        out_specs=[pl.BlockSpec((ts, n), lambda i: (i, 0)) for n in widths]
        + [pl.BlockSpec((1, 8, 128), lambda i: (i, 0, 0))],
        out_shape=[jax.ShapeDtypeStruct((T, n), BF16) for n in widths]
        + [jax.ShapeDtypeStruct((T // ts, 8, 128), F32)],
        compiler_params=_cparams(("parallel",)),
        name="inproj_odd",
    )(x, w, jnp.asarray(vone_np))
    score_bound = jnp.sqrt(jnp.max(nrm[:, 0, 0]) * jnp.max(nrm[:, 1, 0]))
    return bg, u, q, kw, vw, score_bound


def _wattn_kernel(bounded_ref, q_ref, kp_ref, kc_ref, kn_ref, vp_ref, vc_ref, vn_ref, bias_ref, sink_ref,
                  place_ref, o_ref, *, tq, seq_tiles):
    i = pl.program_id(0)
    first = i < 0
    last = i < 0
    for lo, hi, per in seq_tiles:
        inside = (i >= lo) & (i < hi)
        first = first | (inside & ((i - lo) % per == 0))
        last = last | (inside & ((i - lo) % per == per - 1))
    kfull = jnp.concatenate([kp_ref[...], kc_ref[...], kn_ref[...]], axis=0)[:, :HEAD_DIM]
    vfull = jnp.concatenate([vp_ref[...], vc_ref[...], vn_ref[...]], axis=0)
    q4 = q_ref[...]
    bias = bias_ref[0]
    sink = sink_ref[0]
    nb = tq // Q_BLOCK
    col = lax.broadcasted_iota(I32, (GRP_D * Q_BLOCK, 3 * Q_BLOCK), 1)

    def blocks(with_max):
        for n in range(nb):
            qs = jnp.concatenate([q4[Q_BLOCK * n:Q_BLOCK * (n + 1), HEAD_DIM * g:HEAD_DIM * (g + 1)]
                                  for g in range(GRP_D)], axis=0)
            keys = kfull[Q_BLOCK * n:Q_BLOCK * (n + 3)]
            vals = vfull[Q_BLOCK * n:Q_BLOCK * (n + 3)]
            s = lax.dot_general(qs, keys, (((1,), (1,)), ((), ())), preferred_element_type=F32) + bias
            if n == 0:
                s = jnp.where(first & (col < Q_BLOCK), NEG_INF, s)
            if n == nb - 1:
                s = jnp.where(last & (col >= 2 * Q_BLOCK), NEG_INF, s)
            if with_max:
                m = jnp.maximum(jnp.max(s, axis=1, keepdims=True), sink)
                s = s - m
                snk = sink - m
            else:
                snk = sink
            acc = jnp.dot(jnp.exp2(s).astype(BF16), vals, preferred_element_type=F32)
            den = acc[:, HEAD_DIM:HEAD_DIM + 1] + jnp.exp2(snk)
            o = (acc / den).astype(BF16)
            ocat = jnp.concatenate([o[Q_BLOCK * g:Q_BLOCK * (g + 1)] for g in range(GRP_D)], axis=1)
            out = jnp.dot(ocat, place_ref[...], preferred_element_type=F32)
            o_ref[Q_BLOCK * n:Q_BLOCK * (n + 1), :] = out.astype(BF16)

    @pl.when(bounded_ref[0] == 1)
    def _():
        blocks(False)

    @pl.when(bounded_ref[0] != 1)
    def _():
        blocks(True)


WINDOW_BOUND_MAX = 90.0


def _window_attention(q, kw, vw, sink_logits, score_bound, classes, tq):
    T = q.shape[0]
    sink_bound = jnp.max(jnp.abs(sink_logits.astype(F32))) * LOG2E
    bounded = ((1.02 * score_bound <= WINDOW_BOUND_MAX) & (sink_bound <= WINDOW_BOUND_MAX)).astype(I32)[None]
    n_tiles = T // tq
    hb = tq // Q_BLOCK
    n_hblk = T // Q_BLOCK
    r = jnp.arange(Q_BLOCK, dtype=I32)[:, None]
    j = jnp.arange(3 * Q_BLOCK, dtype=I32)[None, :]
    rel = jnp.abs(j - Q_BLOCK - r).astype(F32)
    slopes = jnp.asarray(np.array([2.0 ** (-8.0 * (h + 1) / N_HEADS_D) for h in range(N_HEADS_D)], np.float32))
    bias = jnp.where(rel[None] <= WINDOW, -slopes[:, None, None] * rel[None] * LOG2E, NEG_INF)
    bias = bias.reshape(N_KV_D, GRP_D * Q_BLOCK, 3 * Q_BLOCK)
    sink = jnp.repeat(sink_logits.astype(F32) * LOG2E, Q_BLOCK).reshape(N_KV_D, GRP_D * Q_BLOCK, 1)
    place = _lane_place(128, 256, [0, 1, 2, 3]).reshape(GRP_D * 128, 256)
    seq_tiles = []
    t0 = 0
    for nseq, S, _ in classes:
        cnt = nseq * S // tq
        seq_tiles.append((t0, t0 + cnt, S // tq))
        t0 += cnt
    prev_map = lambda i, h, fl: (jnp.maximum(i * hb - 1, 0), h)
    next_map = lambda i, h, fl: (jnp.minimum((i + 1) * hb, n_hblk - 1), h)
    grid_spec = pltpu.PrefetchScalarGridSpec(
        num_scalar_prefetch=1,
        grid=(n_tiles, N_KV_D),
        in_specs=[
            pl.BlockSpec((tq, 256), lambda i, h, fl: (i, h)),
            pl.BlockSpec((Q_BLOCK, 128), prev_map),
            pl.BlockSpec((tq, 128), lambda i, h, fl: (i, h)),
            pl.BlockSpec((Q_BLOCK, 128), next_map),
            pl.BlockSpec((Q_BLOCK, 128), prev_map),
            pl.BlockSpec((tq, 128), lambda i, h, fl: (i, h)),
            pl.BlockSpec((Q_BLOCK, 128), next_map),
            pl.BlockSpec((1, GRP_D * Q_BLOCK, 3 * Q_BLOCK), lambda i, h, fl: (h, 0, 0)),
            pl.BlockSpec((1, GRP_D * Q_BLOCK, 1), lambda i, h, fl: (h, 0, 0)),
            pl.BlockSpec((GRP_D * 128, 256), lambda i, h, fl: (0, 0)),
        ],
        out_specs=pl.BlockSpec((tq, 256), lambda i, h, fl: (i, h)),
    )
    return pl.pallas_call(
        functools.partial(_wattn_kernel, tq=tq, seq_tiles=tuple(seq_tiles)),
        grid_spec=grid_spec,
        out_shape=jax.ShapeDtypeStruct((T, 512), BF16),
        compiler_params=_cparams(("parallel", "parallel")),
        name="window_attention",
    )(bounded, q, kw, kw, kw, vw, vw, vw, bias, sink, place)


HALO = 16


def _outproj_odd_kernel(bg_ref, u_ref, up_ref, un_ref, a_ref, x_ref, cw_ref, cb_ref, wc_ref, wa_ref,
                        g_ref, b_ref, rw_ref, rb_ref, x1_ref, idx_ref, wt_ref, xb_ref,
                        *, alpha, ts, seq_tiles):
    i = pl.program_id(0)
    first = i < 0
    last = i < 0
    for lo, hi, per in seq_tiles:
        inside = (i >= lo) & (i < hi)
        first = first | (inside & ((i - lo) % per == 0))
        last = last | (inside & ((i - lo) % per == per - 1))
    u = u_ref[...].astype(F32)
    prev_row = jnp.where(first, 0.0, up_ref[HALO - 1:HALO, :].astype(F32))
    next_row = jnp.where(last, 0.0, un_ref[0:1, :].astype(F32))
    rid = lax.broadcasted_iota(I32, u.shape, 0)
    ud = jnp.where(rid == 0, prev_row, pltpu.roll(u, 1, 0))
    uu = jnp.where(rid == ts - 1, next_row, pltpu.roll(u, ts - 1, 0))
    cw = cw_ref[...]
    y = ud * cw[0:1] + u * cw[1:2] + uu * cw[2:3] + cb_ref[...]
    c = (bg_ref[...].astype(F32) * y).astype(BF16)
    nr = ts // ROW_SPLITS
    for h in range(ROW_SPLITS):
        r0 = h * nr
        m = (jnp.dot(c[r0:r0 + nr], wc_ref[...], preferred_element_type=F32)
             + jnp.dot(a_ref[r0:r0 + nr, :], wa_ref[...], preferred_element_type=F32))
        _norm_route_store(r0, nr, m, x_ref[r0:r0 + nr, :], g_ref, b_ref, rw_ref, rb_ref, x1_ref, idx_ref,
                          wt_ref, xb_ref, alpha)


def _outproj_odd(bg, u, a, x, conv_w, conv_b, w_out, ln_g, ln_b, router, alpha, classes, ts):
    T = x.shape[0]
    hb = ts // HALO
    n_h = T // HALO
    seq_tiles = []
    t0 = 0
    for nseq, S, _ in classes:
        cnt = nseq * S // ts
        seq_tiles.append((t0, t0 + cnt, S // ts))
        t0 += cnt
    out_specs, out_shapes = _mix_out_specs(ts, T)
    return pl.pallas_call(
        functools.partial(_outproj_odd_kernel, alpha=alpha, ts=ts, seq_tiles=tuple(seq_tiles)),
        grid=(T // ts,),
        in_specs=[
            pl.BlockSpec((ts, D_CONV), lambda i: (i, 0)),
            pl.BlockSpec((ts, D_CONV), lambda i: (i, 0)),
            pl.BlockSpec((HALO, D_CONV), lambda i: (jnp.maximum(i * hb - 1, 0), 0)),
            pl.BlockSpec((HALO, D_CONV), lambda i: (jnp.minimum((i + 1) * hb, n_h - 1), 0)),
            pl.BlockSpec((ts, 512), lambda i: (i, 0)),
            pl.BlockSpec((ts, D_MODEL), lambda i: (i, 0)),
            pl.BlockSpec((3, D_CONV), lambda i: (0, 0)),
            pl.BlockSpec((1, D_CONV), lambda i: (0, 0)),
            pl.BlockSpec((D_CONV, D_MODEL), lambda i: (0, 0)),
            pl.BlockSpec((512, D_MODEL), lambda i: (0, 0)),
            pl.BlockSpec((1, D_MODEL), lambda i: (0, 0)),
            pl.BlockSpec((1, D_MODEL), lambda i: (0, 0)),
        ] + _ROUTER_SPECS,
        out_specs=out_specs,
        out_shape=out_shapes,
        compiler_params=_cparams(("parallel",)),
        name="outproj_odd",
    )(bg, u, u, u, a, x, conv_w.astype(F32), conv_b.astype(F32)[None, :],
      w_out[:D_CONV].astype(BF16), w_out[D_CONV:].astype(BF16), ln_g[None, :], ln_b[None, :], *router)


def _tile(n, cap):
    t = cap
    while n % t:
        t //= 2
    return t


def kernel(x_prompt, x_sample, w_in_even, fourier_norm_g, q_norm_g, k_norm_g, w_out_even, w_in_odd, conv_w,
           conv_b, sink_logits, w_out_odd, ln_mix_g, ln_mix_b, ln_ffn_g, ln_ffn_b, router_w, router_b,
           w_gate, w_up, w_down):
    depth = ln_mix_g.shape[0]
    alpha = float((2 * depth) ** 0.25)
    bp, sp, _ = x_prompt.shape
    bs, ss, _ = x_sample.shape
    classes = ((bp, sp, 0), (bs, ss, bp * sp))
    T = bp * sp + bs * ss
    min_s = min(sp, ss)
    ts = _tile(min_s, 512)
    tq_w = _tile(min_s, 512)
    tm = 1024
    tc = _tile(min_s, 256)
    x_parts = [x_prompt.reshape(bp * sp, D_MODEL), x_sample.reshape(bs * ss, D_MODEL)]
    router = _router_operands(router_w, router_b)
    for l in range(depth):
        i = l // 2
        if l % 2 == 0:
            uf, qk, v_aug = _inproj_even(x_parts, w_in_even[i], fourier_norm_g[i], q_norm_g[i], k_norm_g[i],
                                         classes, ts)
            bounded = _score_bounded(q_norm_g[i], k_norm_g[i])
            f_parts, a_parts = [], []
            for nseq, S, tok0 in classes:
                f_parts.append(_fourier_mix(uf, nseq, S, tok0))
                a_parts.append(_global_attention(qk, v_aug, bounded, nseq, S, tok0, _tile(S, 1024),
                                                 _tile(S, 2048)))
            x1, idx, wt, xb = _outproj_even(f_parts, a_parts, x_parts, w_out_even[i], ln_mix_g[l], ln_mix_b[l],
                                            router, alpha, ts)
        else:
            x = x_parts[0]
            bg, u, q, kw, vw, score_bound = _inproj_odd(x, w_in_odd[i], ts)
            a = _window_attention(q, kw, vw, sink_logits[i], score_bound, classes, tq_w)
            x1, idx, wt, xb = _outproj_odd(bg, u, a, x, conv_w[i], conv_b[i], w_out_odd[i], ln_mix_g[l],
                                           ln_mix_b[l], router, alpha, classes, ts)
        out_rows = [bp * sp, bs * ss] if l == depth - 1 else [T]
        x_parts = _moe_layer(x1, xb, idx, wt, w_gate[l], w_up[l], w_down[l], ln_ffn_g[l], ln_ffn_b[l], alpha,
                             tm, tc, out_rows)
    if len(x_parts) == 1:
        x_parts = [x_parts[0][:bp * sp], x_parts[0][bp * sp:]]
    return (x_parts[0].reshape(bp, sp, D_MODEL), x_parts[1].reshape(bs, ss, D_MODEL))
```

```python
import functools
import math

import numpy as np
import jax
import jax.numpy as jnp
from jax import lax
from jax.experimental import pallas as pl
from jax.experimental.pallas import tpu as pltpu

F32 = jnp.float32
BF16 = jnp.bfloat16
I32 = jnp.int32

D_MODEL = 1024
HEAD_DIM = 64
GRID_W = 64
Q_BLOCK = 128
WINDOW = 128
ROPE_THETA = 10000.0
N_FOURIER_GROUPS = 4
D_FOURIER = 256
N_HEADS_B = 12
N_KV_B = 4
GRP_B = 3
D_CONV = 512
N_HEADS_D = 8
N_KV_D = 2
GRP_D = 4
N_EXPERTS = 16
N_GROUPS = 4
EXPERTS_PER_GROUP = 4
D_FF = 512
LN_EPS = 1e-5
RMS_EPS = 1e-6
NEG_INF = -1e30
LOG2E = 1.4426950408889634
FFT_S2 = 128
V7X_VMEM_LIMIT = 48 * 1024 * 1024


def _cparams(sem):
    return pltpu.CompilerParams(dimension_semantics=sem, vmem_limit_bytes=V7X_VMEM_LIMIT)


def _layer_norm(y, g, b):
    mu = jnp.mean(y, axis=-1, keepdims=True)
    yc = y - mu
    var = jnp.mean(yc * yc, axis=-1, keepdims=True)
    return yc * lax.rsqrt(var + LN_EPS) * g + b


N_NORM_COLS = D_FOURIER + 1024


def _parts_specs(parts, ts):
    specs, bounds, lo = [], [], 0
    for p in parts:
        n = p.shape[0] // ts
        specs.append(pl.BlockSpec((ts, p.shape[1]), lambda i, lo=lo, n=n: (jnp.clip(i - lo, 0, n - 1), 0)))
        lo += n
        bounds.append(lo)
    return specs, tuple(bounds)


def _pick_rows(refs, bounds, r0, nr):
    i = pl.program_id(0)
    out = refs[-1][r0:r0 + nr, :]
    for ref, hi in reversed(list(zip(refs[:-1], bounds[:-1]))):
        out = jnp.where(i < hi, ref[r0:r0 + nr, :], out)
    return out


def _inproj_even_kernel(*refs, nx, bounds):
    x_refs = refs[:nx]
    (w_ref, gain_ref, seg_ref, segt_ref, cos_ref, sin_ref, vone_ref, uf_ref, qk_ref, v_ref) = refs[nx:]
    nr = uf_ref.shape[0] // ROW_SPLITS
    for part in range(ROW_SPLITS):
        r0 = part * nr
        x = _pick_rows(x_refs, bounds, r0, nr).astype(BF16)
        proj = jnp.dot(x, w_ref[...], preferred_element_type=F32)
        nrm = proj[:, :N_NORM_COLS]
        sq = (nrm * nrm).astype(BF16)
        ssum = jnp.dot(sq, seg_ref[...], preferred_element_type=F32)
        r = lax.rsqrt(ssum * (1.0 / HEAD_DIM) + RMS_EPS)
        rh = r.astype(BF16)
        rl = (r - rh.astype(F32)).astype(BF16)
        rex = jnp.dot(jnp.concatenate([rh, rl], axis=1), segt_ref[...], preferred_element_type=F32)
        y = nrm * rex * gain_ref[...]
        uf_ref[r0:r0 + nr, :] = y[:, :D_FOURIER].astype(BF16)
        yq = y[:, D_FOURIER:]
        c = jnp.concatenate([cos_ref[r0:r0 + nr, :]] * 8, axis=1)
        s = jnp.concatenate([sin_ref[r0:r0 + nr, :]] * 8, axis=1)
        lane = lax.broadcasted_iota(I32, yq.shape, 1)
        first = (lane & 31) < 16
        sw = jnp.where(first, pltpu.roll(yq, 1024 - 16, 1), pltpu.roll(yq, 16, 1))
        qk_ref[r0:r0 + nr, :] = (yq * c + sw * s).astype(BF16)
        vv = proj[:, N_NORM_COLS:] + vone_ref[...]
        for h in range(N_KV_B):
            v_ref[h, r0:r0 + nr, :] = vv[:, 128 * h:128 * (h + 1)].astype(BF16)


def _rope_tables(n_pos):
    t = jnp.arange(n_pos)
    row = (t // GRID_W).astype(F32)
    col = (t % GRID_W).astype(F32)
    n_freq = HEAD_DIM // 4
    inv_freq = ROPE_THETA ** (-jnp.arange(n_freq, dtype=F32) / n_freq)
    ar = row[:, None] * inv_freq
    ac = col[:, None] * inv_freq
    ang = jnp.concatenate([ar, ar, ac, ac], axis=1)
    sign = jnp.asarray(np.tile(np.repeat(np.array([-1.0, 1.0], np.float32), 16), 2))
    cos = jnp.cos(ang)
    sin = jnp.sin(ang) * sign
    return jnp.concatenate([cos, cos], axis=1), jnp.concatenate([sin, sin], axis=1)


def _pos_block_map(classes, ts):
    bounds = []
    tile0 = 0
    for nseq, S, _ in classes:
        n_tiles = nseq * S // ts
        bounds.append((tile0, tile0 + n_tiles, S // ts))
        tile0 += n_tiles

    def fn(i):
        out = (i - bounds[-1][0]) % bounds[-1][2]
        for lo, hi, per in reversed(bounds[:-1]):
            out = jnp.where(i < hi, (i - lo) % per, out)
        return out

    return fn


def _inproj_even(x_parts, w_in, f_g, q_g, k_g, classes, ts):
    T = sum(p.shape[0] for p in x_parts)
    x_specs, bounds = _parts_specs(x_parts, ts)
    wf = w_in[:, :D_FOURIER]
    wq = w_in[:, D_FOURIER:D_FOURIER + 768].reshape(D_MODEL, N_KV_B, GRP_B * HEAD_DIM)
    wk = w_in[:, D_FOURIER + 768:D_FOURIER + 1024].reshape(D_MODEL, N_KV_B, HEAD_DIM)
    wv = w_in[:, D_FOURIER + 1024:].reshape(D_MODEL, N_KV_B, HEAD_DIM)
    wqk = jnp.concatenate([wk, wq], axis=2).reshape(D_MODEL, 1024)
    wvp = jnp.concatenate([wv, jnp.zeros_like(wv)], axis=2).reshape(D_MODEL, 512)
    w = jnp.concatenate([wf, wqk, wvp], axis=1).astype(BF16)
    qscale = HEAD_DIM ** -0.5 * LOG2E
    gqk = jnp.tile(jnp.concatenate([k_g, q_g * qscale, q_g * qscale, q_g * qscale]), N_KV_B)
    gain = jnp.concatenate([f_g, gqk])[None, :].astype(F32)
    seg_np = np.zeros((N_NORM_COLS, 128), np.float32)
    seg_np[np.arange(N_NORM_COLS), np.arange(N_NORM_COLS) // HEAD_DIM] = 1.0
    seg = jnp.asarray(seg_np, BF16)
    segt = jnp.asarray(np.concatenate([seg_np.T, seg_np.T], axis=0), BF16)
    max_s = max(S for _, S, _ in classes)
    cos, sin = _rope_tables(max_s)
    vone_np = np.zeros((1, 512), np.float32)
    for h in range(N_KV_B):
        vone_np[0, 128 * h + 64:128 * (h + 1)] = 1.0
    vone = jnp.asarray(vone_np)
    posmap = _pos_block_map(classes, ts)
    n_w = w.shape[1]
    return pl.pallas_call(
        functools.partial(_inproj_even_kernel, nx=len(x_parts), bounds=bounds),
        grid=(T // ts,),
        in_specs=x_specs + [
            pl.BlockSpec((D_MODEL, n_w), lambda i: (0, 0)),
            pl.BlockSpec((1, N_NORM_COLS), lambda i: (0, 0)),
            pl.BlockSpec((N_NORM_COLS, 128), lambda i: (0, 0)),
            pl.BlockSpec((256, N_NORM_COLS), lambda i: (0, 0)),
            pl.BlockSpec((ts, 128), lambda i: (posmap(i), 0)),
            pl.BlockSpec((ts, 128), lambda i: (posmap(i), 0)),
            pl.BlockSpec((1, 512), lambda i: (0, 0)),
        ],
        out_specs=[
            pl.BlockSpec((ts, D_FOURIER), lambda i: (i, 0)),
            pl.BlockSpec((ts, 1024), lambda i: (i, 0)),
            pl.BlockSpec((N_KV_B, ts, 128), lambda i: (0, i, 0)),
        ],
        out_shape=[
            jax.ShapeDtypeStruct((T, D_FOURIER), BF16),
            jax.ShapeDtypeStruct((T, 1024), BF16),
            jax.ShapeDtypeStruct((N_KV_B, T, 128), BF16),
        ],
        compiler_params=_cparams(("parallel",)),
        name="inproj_even",
    )(*x_parts, w, gain, seg, segt, cos, sin, vone)


FFT_TB = 16


def _fft1_kernel(x_ref, kw_ref, twr_ref, twi_ref, zr_ref, zi_ref, *, s1, bs):
    n = s1 * FFT_TB
    twr = twr_ref[0]
    twi = twi_ref[0]
    for q in range(bs):
        x2 = x_ref[q * s1:(q + 1) * s1].reshape(n, D_FOURIER)
        z = jnp.dot(kw_ref[...], x2, preferred_element_type=F32)
        zr = z[:n]
        zi = z[n:]
        zr_ref[q * s1:(q + 1) * s1] = (zr * twr - zi * twi).astype(BF16).reshape(s1, FFT_TB, D_FOURIER)
        zi_ref[q * s1:(q + 1) * s1] = (zr * twi + zi * twr).astype(BF16).reshape(s1, FFT_TB, D_FOURIER)


def _fft2_kernel(zr_ref, zi_ref, w2a_ref, w2b_ref, mix_ref, o_ref, *, cb):
    s2 = FFT_S2
    for c in range(cb):
        pp = (jnp.dot(w2a_ref[...], zr_ref[0, c], preferred_element_type=F32)
              + jnp.dot(w2b_ref[...], zi_ref[0, c], preferred_element_type=F32))
        f = (jnp.dot(pp[:s2].astype(BF16), mix_ref[:D_FOURIER], preferred_element_type=F32)
             + jnp.dot(pp[s2:].astype(BF16), mix_ref[D_FOURIER:], preferred_element_type=F32))
        o_ref[0, :, D_FOURIER * c:D_FOURIER * (c + 1)] = f.astype(BF16)


def _dft_mats(n):
    k = jnp.arange(n, dtype=I32)
    ang = (2.0 * math.pi / n) * ((k[:, None] * k[None, :]) % n).astype(F32)
    return jnp.cos(ang), jnp.sin(ang)


def _fourier_mix(uf, nseq, S, tok0):
    s2 = FFT_S2
    s1 = S // s2
    tb = FFT_TB
    n = s1 * tb
    x3 = uf.reshape(uf.shape[0] // s2, s2, D_FOURIER)
    blk0 = tok0 // S
    c1, sn1 = _dft_mats(s1)
    w1 = jnp.concatenate([c1, -sn1], axis=0)
    kw = jnp.kron(w1, jnp.eye(tb, dtype=F32)).astype(BF16)
    cc = jnp.arange(s1, dtype=I32)[:, None]
    bb = jnp.arange(s2, dtype=I32)[None, :]
    ang = (2.0 * math.pi / S) * ((cc * bb) % S).astype(F32)

    def table(t):
        t = t.reshape(s1, s2 // tb, tb).transpose(1, 0, 2).reshape(s2 // tb, n, 1)
        return jnp.broadcast_to(t, (s2 // tb, n, D_FOURIER))
    twr = table(jnp.cos(ang))
    twi = table(-jnp.sin(ang))
    bs = math.gcd(math.gcd(nseq, max(1, 64 // s1)), blk0) if blk0 else math.gcd(nseq, max(1, 64 // s1))
    zr, zi = pl.pallas_call(
        functools.partial(_fft1_kernel, s1=s1, bs=bs),
        grid=(s2 // tb, nseq // bs),
        in_specs=[
            pl.BlockSpec((bs * s1, tb, D_FOURIER), lambda j, b: (blk0 // bs + b, j, 0)),
            pl.BlockSpec((2 * n, n), lambda j, b: (0, 0)),
            pl.BlockSpec((1, n, D_FOURIER), lambda j, b: (j, 0, 0)),
            pl.BlockSpec((1, n, D_FOURIER), lambda j, b: (j, 0, 0)),
        ],
        out_specs=[
            pl.BlockSpec((bs * s1, tb, D_FOURIER), lambda j, b: (b, j, 0)),
            pl.BlockSpec((bs * s1, tb, D_FOURIER), lambda j, b: (b, j, 0)),
        ],
        out_shape=[jax.ShapeDtypeStruct((nseq * s1, s2, D_FOURIER), BF16)] * 2,
        compiler_params=_cparams(("parallel", "parallel")),
        name="fourier_stage1",
    )(x3, kw, twr, twi)
    zr = zr.reshape(nseq, s1, s2, D_FOURIER)
    zi = zi.reshape(nseq, s1, s2, D_FOURIER)
    c2, sn2 = _dft_mats(s2)
    w2a = jnp.concatenate([c2, -sn2], axis=0).astype(BF16)
    w2b = jnp.concatenate([sn2, c2], axis=0).astype(BF16)
    gc, gs = _dft_mats(HEAD_DIM)
    scale = 1.0 / math.sqrt(S * HEAD_DIM)
    eye = jnp.eye(N_FOURIER_GROUPS, dtype=F32)
    mix = (jnp.concatenate([jnp.kron(eye, gc), jnp.kron(eye, gs)], axis=0) * scale).astype(BF16)
    cb = min(8, s1)
    out = pl.pallas_call(
        functools.partial(_fft2_kernel, cb=cb),
        grid=(nseq, s1 // cb),
        in_specs=[
            pl.BlockSpec((1, cb, s2, D_FOURIER), lambda b, j: (b, j, 0, 0)),
            pl.BlockSpec((1, cb, s2, D_FOURIER), lambda b, j: (b, j, 0, 0)),
            pl.BlockSpec((2 * s2, s2), lambda b, j: (0, 0)),
            pl.BlockSpec((2 * s2, s2), lambda b, j: (0, 0)),
            pl.BlockSpec((2 * D_FOURIER, D_FOURIER), lambda b, j: (0, 0)),
        ],
        out_specs=pl.BlockSpec((1, s2, cb * D_FOURIER), lambda b, j: (b, 0, j)),
        out_shape=jax.ShapeDtypeStruct((nseq, s2, s1 * D_FOURIER), BF16),
        compiler_params=_cparams(("parallel", "parallel")),
        name="fourier_stage2",
    )(zr, zi, w2a, w2b, mix)
    return out.reshape(nseq * S, D_FOURIER)


SCORE_BOUND_MAX = 100.0


def _gattn_kernel(bounded_ref, q_ref, k_ref, v_ref, place_ref, o_ref, *, tq, tk, S):
    q3 = q_ref[...]
    qs = [q3[:, HEAD_DIM * (g + 1):HEAD_DIM * (g + 2)] for g in range(GRP_B)]

    def chunk(j):
        off = pl.multiple_of(j * tk, tk)
        return k_ref[pl.ds(off, tk), 0:HEAD_DIM], v_ref[0, pl.ds(off, tk), :]

    def scores(g, kc):
        return lax.dot_general(qs[g], kc, (((1,), (1,)), ((), ())), preferred_element_type=F32)

    def finish(accs):
        out = jnp.zeros((tq, 256), F32)
        for g in range(GRP_B):
            o = (accs[g] / accs[g][:, HEAD_DIM:HEAD_DIM + 1]).astype(BF16)
            out = out + jnp.dot(o, place_ref[g], preferred_element_type=F32)
        o_ref[...] = out.astype(BF16)

    @pl.when(bounded_ref[0] == 1)
    def _():
        def body(j, accs):
            kc, vc = chunk(j)
            return tuple(accs[g] + jnp.dot(jnp.exp2(scores(g, kc)).astype(BF16), vc,
                                           preferred_element_type=F32) for g in range(GRP_B))
        finish(lax.fori_loop(0, S // tk, body, tuple(jnp.zeros((tq, 128), F32) for _ in range(GRP_B))))

    @pl.when(bounded_ref[0] != 1)
    def _():
        def body(j, carry):
            kc, vc = chunk(j)
            new = []
            for g in range(GRP_B):
                m_prev, acc = carry[g]
                s = scores(g, kc)
                m_new = jnp.maximum(m_prev, jnp.max(s, axis=1, keepdims=True))
                p = jnp.exp2(s - m_new).astype(BF16)
                acc = jnp.exp2(m_prev - m_new) * acc + jnp.dot(p, vc, preferred_element_type=F32)
                new.append((m_new, acc))
            return tuple(new)
        init = tuple((jnp.full((tq, 1), NEG_INF, F32), jnp.zeros((tq, 128), F32)) for _ in range(GRP_B))
        fin = lax.fori_loop(0, S // tk, body, init)
        finish([fin[g][1] for g in range(GRP_B)])


def _score_bounded(q_g, k_g):
    bound = 1.05 * HEAD_DIM * (HEAD_DIM ** -0.5 * LOG2E) * jnp.max(jnp.abs(q_g)) * jnp.max(jnp.abs(k_g))
    return (bound <= SCORE_BOUND_MAX).astype(I32)[None]


def _lane_place(n_src, n_dst, groups):
    pm = np.zeros((len(groups), n_src, n_dst), np.float32)
    for gi, slot in enumerate(groups):
        pm[gi, np.arange(HEAD_DIM), HEAD_DIM * slot + np.arange(HEAD_DIM)] = 1.0
    return jnp.asarray(pm, BF16)


def _global_attention(qk, v_aug, bounded, nseq, S, tok0, tq, tk):
    n_qt = S // tq
    qt0 = tok0 // tq
    s0 = tok0 // S
    place = _lane_place(128, 256, [1, 2, 3])
    grid_spec = pltpu.PrefetchScalarGridSpec(
        num_scalar_prefetch=1,
        grid=(nseq, N_KV_B, n_qt),
        in_specs=[
            pl.BlockSpec((tq, 256), lambda b, h, i, fl: (qt0 + b * n_qt + i, h)),
            pl.BlockSpec((S, 256), lambda b, h, i, fl: (s0 + b, h)),
            pl.BlockSpec((1, S, 128), lambda b, h, i, fl: (h, s0 + b, 0)),
            pl.BlockSpec((GRP_B, 128, 256), lambda b, h, i, fl: (0, 0, 0)),
        ],
        out_specs=pl.BlockSpec((tq, 256), lambda b, h, i, fl: (b * n_qt + i, h)),
    )
    return pl.pallas_call(
        functools.partial(_gattn_kernel, tq=tq, tk=tk, S=S),
        grid_spec=grid_spec,
        out_shape=jax.ShapeDtypeStruct((nseq * S, 1024), BF16),
        compiler_params=_cparams(("parallel", "parallel", "arbitrary")),
        name="global_attention",
    )(bounded, qk, qk, v_aug, place)


def _route(x1, rw_ref, rb_ref):
    n = x1.shape[0]
    xh = x1.astype(BF16)
    xl = (x1 - xh.astype(F32)).astype(BF16)
    o = jnp.dot(jnp.concatenate([xh, xl], axis=0), rw_ref[...], preferred_element_type=F32)
    lg = o[:n, :128] + o[:n, 128:] + o[n:, :128]
    lgt = lg.T[:N_EXPERTS]
    sc = 1.0 / (1.0 + jnp.exp(-lgt))
    bi = sc + rb_ref[...]
    srow = [sc[e:e + 1] for e in range(N_EXPERTS)]
    brow = [bi[e:e + 1] for e in range(N_EXPERTS)]
    gsel = None
    best = None
    for g in range(N_GROUPS):
        a, b, c, d = brow[4 * g:4 * g + 4]
        m1, n1 = jnp.maximum(a, b), jnp.minimum(a, b)
        m2, n2 = jnp.maximum(c, d), jnp.minimum(c, d)
        gs = jnp.maximum(m1, m2) + jnp.maximum(jnp.minimum(m1, m2), jnp.maximum(n1, n2))
        if g == 0:
            best, gsel = gs, jnp.zeros(gs.shape, I32)
        else:
            better = gs > best
            gsel = jnp.where(better, g, gsel)
            best = jnp.where(better, gs, best)
    masked = [jnp.where(gsel == (e // EXPERTS_PER_GROUP), brow[e], NEG_INF) for e in range(N_EXPERTS)]
    i1 = jnp.zeros(gsel.shape, I32)
    b1 = masked[0]
    s1 = srow[0]
    for e in range(1, N_EXPERTS):
        better = masked[e] > b1
        i1 = jnp.where(better, e, i1)
        b1 = jnp.where(better, masked[e], b1)
        s1 = jnp.where(better, srow[e], s1)
    i2 = jnp.full(gsel.shape, -1, I32)
    b2 = jnp.full(b1.shape, -jnp.inf, F32)
    s2 = jnp.zeros(b1.shape, F32)
    for e in range(N_EXPERTS):
        better = (masked[e] > b2) & (i1 != e)
        i2 = jnp.where(better, e, i2)
        b2 = jnp.where(better, masked[e], b2)
        s2 = jnp.where(better, srow[e], s2)
    den = s1 + s2
    return i1, i2, s1 / den, s2 / den


ROW_SPLITS = 2


def _norm_route_store(r0, nr, m, x, g_ref, b_ref, rw_ref, rb_ref, x1_ref, idx_ref, wt_ref, xb_ref, alpha):
    x1 = _layer_norm(alpha * x + m, g_ref[...], b_ref[...])
    x1_ref[r0:r0 + nr, :] = x1
    xb_ref[r0:r0 + nr, :] = x1.astype(BF16)
    i1, i2, w1, w2 = _route(x1, rw_ref, rb_ref)
    rid = lax.broadcasted_iota(I32, (8, nr), 0)
    idx_ref[:, r0:r0 + nr] = jnp.where(rid == 0, i1, jnp.where(rid == 1, i2, 0))
    wt_ref[:, r0:r0 + nr] = jnp.where(rid == 0, w1, jnp.where(rid == 1, w2, 0.0))


def _router_operands(router_w, router_b):
    rw = jnp.zeros((D_MODEL, 128), F32).at[:, :N_EXPERTS].set(router_w.astype(F32))
    rwh = rw.astype(BF16)
    rwl = (rw - rwh.astype(F32)).astype(BF16)
    rb = router_b.astype(F32)[:, None]
    return jnp.concatenate([rwh, rwl], axis=1), rb


_ROUTER_SPECS = [
    pl.BlockSpec((D_MODEL, 256), lambda i: (0, 0)),
    pl.BlockSpec((N_EXPERTS, 1), lambda i: (0, 0)),
]


def _mix_out_specs(ts, T):
    specs = [
        pl.BlockSpec((ts, D_MODEL), lambda i: (i, 0)),
        pl.BlockSpec((8, ts), lambda i: (0, i)),
        pl.BlockSpec((8, ts), lambda i: (0, i)),
        pl.BlockSpec((ts, D_MODEL), lambda i: (i, 0)),
    ]
    shapes = [
        jax.ShapeDtypeStruct((T, D_MODEL), F32),
        jax.ShapeDtypeStruct((8, T), I32),
        jax.ShapeDtypeStruct((8, T), F32),
        jax.ShapeDtypeStruct((T, D_MODEL), BF16),
    ]
    return specs, shapes


def _outproj_even_kernel(*refs, nf, na, nx, fb, ab, xb, alpha):
    f_refs, a_refs, x_refs = refs[:nf], refs[nf:nf + na], refs[nf + na:nf + na + nx]
    wf_ref, wa_ref, g_ref, b_ref, rw_ref, rb_ref, x1_ref, idx_ref, wt_ref, xb_ref = refs[nf + na + nx:]
    nr = x1_ref.shape[0] // ROW_SPLITS
    for h in range(ROW_SPLITS):
        r0 = h * nr
        m = (jnp.dot(_pick_rows(f_refs, fb, r0, nr), wf_ref[...], preferred_element_type=F32)
             + jnp.dot(_pick_rows(a_refs, ab, r0, nr), wa_ref[...], preferred_element_type=F32))
        xr = _pick_rows(x_refs, xb, r0, nr)
        _norm_route_store(r0, nr, m, xr, g_ref, b_ref, rw_ref, rb_ref, x1_ref, idx_ref, wt_ref, xb_ref, alpha)


def _outproj_even(f_parts, a_parts, x_parts, w_out, ln_g, ln_b, router, alpha, ts):
    T = sum(p.shape[0] for p in x_parts)
    f_specs, fb = _parts_specs(f_parts, ts)
    a_specs, ab = _parts_specs(a_parts, ts)
    x_specs, xb = _parts_specs(x_parts, ts)
    wf = w_out[:D_FOURIER].astype(BF16)
    wa = w_out[D_FOURIER:].reshape(N_KV_B, GRP_B * HEAD_DIM, D_MODEL)
    wa = jnp.concatenate([jnp.zeros((N_KV_B, HEAD_DIM, D_MODEL), w_out.dtype), wa], axis=1)
    wa = wa.reshape(1024, D_MODEL).astype(BF16)
    out_specs, out_shapes = _mix_out_specs(ts, T)
    return pl.pallas_call(
        functools.partial(_outproj_even_kernel, nf=len(f_parts), na=len(a_parts), nx=len(x_parts),
                          fb=fb, ab=ab, xb=xb, alpha=alpha),
        grid=(T // ts,),
        in_specs=f_specs + a_specs + x_specs + [
            pl.BlockSpec((D_FOURIER, D_MODEL), lambda i: (0, 0)),
            pl.BlockSpec((1024, D_MODEL), lambda i: (0, 0)),
            pl.BlockSpec((1, D_MODEL), lambda i: (0, 0)),
            pl.BlockSpec((1, D_MODEL), lambda i: (0, 0)),
        ] + _ROUTER_SPECS,
        out_specs=out_specs,
        out_shape=out_shapes,
        compiler_params=_cparams(("parallel",)),
        name="outproj_even",
    )(*f_parts, *a_parts, *x_parts, wf, wa, ln_g[None, :], ln_b[None, :], *router)


CHUNK = 16


def _stage_rows(td):
    return -(-(2 * td + N_EXPERTS * (CHUNK - 1)) // 256) * 256


def _tab_len(td):
    return -(-(1 + 2 * td // CHUNK + N_EXPERTS) // 128) * 128


def _moe_plan(idx, td, tm):
    T = idx.shape[1]
    nt = T // td
    ep = idx.reshape(2, nt, td).transpose(1, 0, 2).reshape(nt, 2 * td)
    oh = (ep[None, :, :] == jnp.arange(N_EXPERTS, dtype=I32)[:, None, None]).astype(I32)
    cs = jnp.cumsum(oh, axis=2)
    cnt = cs[:, :, -1].T
    c8 = ((cnt + CHUNK - 1) // CHUNK) * CHUNK
    off8 = jnp.cumsum(c8, axis=1) - c8
    base8 = jnp.cumsum(c8, axis=0) - c8
    seg = jnp.sum(c8, axis=0)
    padded = ((seg + tm - 1) // tm) * tm
    ends = jnp.cumsum(padded)
    starts = ends - padded
    dst = starts[None, :] + base8
    lpos = jnp.sum(oh * (cs - 1 + off8.T[:, :, None]), axis=0).astype(I32)
    n_rows = -(-(2 * T + nt * N_EXPERTS * (CHUNK - 1)) // tm) * tm + N_EXPERTS * tm
    tile_start = jnp.arange(n_rows // tm, dtype=I32) * tm
    tile_e = jnp.minimum(jnp.sum((ends[None, :] <= tile_start[:, None]).astype(I32), axis=1), N_EXPERTS - 1)
    n_used = (ends[-1] // tm).astype(I32)[None]
    nc = c8 // CHUNK
    cum = jnp.cumsum(nc, axis=1)
    n_chunk_max = 2 * td // CHUNK + N_EXPERTS
    c_idx = jnp.arange(n_chunk_max, dtype=I32)
    e_c = jnp.minimum(jnp.sum((cum[:, None, :] <= c_idx[None, :, None]).astype(I32), axis=2), N_EXPERTS - 1)
    oh_c = (e_c[:, :, None] == jnp.arange(N_EXPERTS, dtype=I32)[None, None, :]).astype(I32)
    srow = jnp.sum(oh_c * (dst - CHUNK * (cum - nc))[:, None, :], axis=2) + CHUNK * c_idx[None, :]
    tab = jnp.concatenate([cum[:, -1:], srow], axis=1).astype(I32)
    tab = jnp.pad(tab, ((0, 0), (0, _tab_len(td) - tab.shape[1]))).reshape(nt, 1, _tab_len(td))
    pad_lo = jnp.concatenate([starts + seg, ends[-1:]]).astype(I32)
    pad_hi = jnp.concatenate([ends, jnp.full((1,), n_rows, I32)]).astype(I32)
    lpos_k = lpos.reshape(nt, 2, td).transpose(1, 0, 2).reshape(2, T)
    lpos_rows = jnp.zeros((8, T), I32).at[:2].set(lpos_k)
    return tab, lpos_rows, lpos_k.T, tile_e.astype(I32), n_used, pad_lo, pad_hi, n_rows


def _chunk_loops(tab_ref, fn):
    total = tab_ref[0, 0, 0]

    def one(c, priority):
        fn(pl.multiple_of(CHUNK * c, CHUNK), pl.multiple_of(tab_ref[0, 0, 1 + c], CHUNK), priority)

    def body(j, carry):
        one(2 * j, 0)
        one(2 * j + 1, 1)
        return carry
    lax.fori_loop(0, total // 2, body, 0)

    @pl.when(total % 2 == 1)
    def _():
        one(total - 1, 0)
    return total


def _dispatch_kernel(lo_ref, hi_ref, tab_ref, lp_ref, x_ref, xs_hbm, stage, zbuf, nwait, sem, zsem, *, ns):
    i = pl.program_id(0)
    slot = i % 2

    def zero_copy(dst):
        return pltpu.make_async_copy(zbuf, xs_hbm.at[pl.ds(pl.multiple_of(dst, CHUNK), CHUNK)], zsem)

    @pl.when(i == 0)
    def _():
        nwait[0] = 0
        nwait[1] = 0
        zbuf[...] = jnp.zeros(zbuf.shape, BF16)
        for e in range(N_EXPERTS + 1):
            nz = (hi_ref[e] - lo_ref[e]) // CHUNK
            lax.fori_loop(0, nz, lambda j, c, e=e: (zero_copy(lo_ref[e] + CHUNK * j).start(), c)[1], 0)
        for e in range(N_EXPERTS + 1):
            nz = (hi_ref[e] - lo_ref[e]) // CHUNK
            lax.fori_loop(0, nz, lambda j, c: (zero_copy(0).wait(), c)[1], 0)

    def piece(local_row, sorted_row, s):
        return pltpu.make_async_copy(stage.at[s, pl.ds(local_row, CHUNK)],
                                     xs_hbm.at[pl.ds(sorted_row, CHUNK)], sem.at[s])

    def drain(s):
        lax.fori_loop(0, nwait[s], lambda j, c: (piece(0, 0, s).wait(), c)[1], 0)

    drain(slot)
    lp = lp_ref[...]
    rows = lax.broadcasted_iota(I32, (ns, lp.shape[1]), 0)
    perm = jnp.where((rows == lp[0:1]) | (rows == lp[1:2]), 1.0, 0.0).astype(BF16)
    stage[slot] = jnp.dot(perm, x_ref[...], preferred_element_type=F32).astype(BF16)
    nwait[slot] = _chunk_loops(tab_ref, lambda lr, sr, pr: piece(lr, sr, slot).start(priority=pr))

    @pl.when(i == pl.num_programs(0) - 1)
    def _():
        drain(0)
        drain(1)


def _moe_dispatch(x1, tab, lpos_rows, pad_lo, pad_hi, n_rows, td):
    T = x1.shape[0]
    ns = _stage_rows(td)
    grid_spec = pltpu.PrefetchScalarGridSpec(
        num_scalar_prefetch=2,
        grid=(T // td,),
        in_specs=[
            pl.BlockSpec((1, 1, _tab_len(td)), lambda i, lo, hi: (i, 0, 0), memory_space=pltpu.SMEM),
            pl.BlockSpec((8, td), lambda i, lo, hi: (0, i)),
            pl.BlockSpec((td, D_MODEL), lambda i, lo, hi: (i, 0)),
        ],
        out_specs=pl.BlockSpec(memory_space=pl.ANY),
        scratch_shapes=[pltpu.VMEM((2, ns, D_MODEL), BF16), pltpu.VMEM((CHUNK, D_MODEL), BF16),
                        pltpu.SMEM((2,), I32), pltpu.SemaphoreType.DMA((2,)), pltpu.SemaphoreType.DMA(())],
    )
    return pl.pallas_call(
        functools.partial(_dispatch_kernel, ns=ns),
        grid_spec=grid_spec,
        out_shape=jax.ShapeDtypeStruct((n_rows, D_MODEL), BF16),
        compiler_params=_cparams(("arbitrary",)),
        name="moe_dispatch",
    )(pad_lo, pad_hi, tab, lpos_rows, x1)


def _moe_kernel(te_ref, nu_ref, x_ref, wg_ref, wu_ref, wd_ref, y_ref, wgb, wub, wdb):
    i = pl.program_id(0)

    @pl.when((i == 0) | (te_ref[i] != te_ref[jnp.maximum(i - 1, 0)]))
    def _():
        wgb[...] = wg_ref[0].astype(BF16)
        wub[...] = wu_ref[0].astype(BF16)
        wdb[...] = wd_ref[0].astype(BF16)

    @pl.when(i < nu_ref[0])
    def _():
        xg = x_ref[...]
        hg = jnp.dot(xg, wgb[...], preferred_element_type=F32)
        hu = jnp.dot(xg, wub[...], preferred_element_type=F32)
        hdn = (hg / (1.0 + jnp.exp(-hg)) * hu).astype(BF16)
        y_ref[...] = jnp.dot(hdn, wdb[...], preferred_element_type=F32).astype(BF16)

    @pl.when(i >= nu_ref[0])
    def _():
        y_ref[...] = jnp.zeros(y_ref.shape, BF16)


def _moe_experts(xs, tile_e, n_used, wg, wu, wd, tm):
    n_rows = xs.shape[0]
    grid_spec = pltpu.PrefetchScalarGridSpec(
        num_scalar_prefetch=2,
        grid=(n_rows // tm,),
        in_specs=[
            pl.BlockSpec((tm, D_MODEL), lambda i, te, nu: (i, 0)),
            pl.BlockSpec((1, D_MODEL, D_FF), lambda i, te, nu: (te[i], 0, 0)),
            pl.BlockSpec((1, D_MODEL, D_FF), lambda i, te, nu: (te[i], 0, 0)),
            pl.BlockSpec((1, D_FF, D_MODEL), lambda i, te, nu: (te[i], 0, 0)),
        ],
        out_specs=pl.BlockSpec((tm, D_MODEL), lambda i, te, nu: (i, 0)),
        scratch_shapes=[pltpu.VMEM((D_MODEL, D_FF), BF16), pltpu.VMEM((D_MODEL, D_FF), BF16),
                        pltpu.VMEM((D_FF, D_MODEL), BF16)],
    )
    return pl.pallas_call(
        _moe_kernel,
        grid_spec=grid_spec,
        out_shape=jax.ShapeDtypeStruct((n_rows, D_MODEL), BF16),
        compiler_params=_cparams(("arbitrary",)),
        name="moe_experts",
    )(tile_e, n_used, xs, wg, wu, wd)


def _combine_kernel(tab_ref, tabn_ref, lp_ref, y_hbm, x_ref, wt_ref, g_ref, b_ref, *rest, ns, alpha, bounds):
    o_refs, (ystage, sem) = rest[:len(bounds)], rest[len(bounds):]
    i = pl.program_id(0)
    n = pl.num_programs(0)
    slot = i % 2

    def piece(local_row, sorted_row, s):
        return pltpu.make_async_copy(y_hbm.at[pl.ds(sorted_row, CHUNK)],
                                     ystage.at[s, pl.ds(local_row, CHUNK)], sem.at[s])

    @pl.when(i == 0)
    def _():
        ystage[...] = jnp.zeros(ystage.shape, BF16)
        _chunk_loops(tab_ref, lambda lr, sr, pr: piece(lr, sr, 0).start(priority=pr))

    @pl.when(i + 1 < n)
    def _():
        _chunk_loops(tabn_ref, lambda lr, sr, pr: piece(lr, sr, 1 - slot).start(priority=pr))

    lax.fori_loop(0, tab_ref[0, 0, 0], lambda j, c: (piece(0, 0, slot).wait(), c)[1], 0)

    ys = ystage[slot]
    lp = lp_ref[...]
    cols = lax.broadcasted_iota(I32, (lp.shape[0], ns), 1)
    w = wt_ref[...]
    tc = lp.shape[0]
    sel = jnp.concatenate([jnp.where(cols == lp[:, k:k + 1], 1.0, 0.0).astype(BF16) for k in range(2)], axis=0)
    picked = jnp.dot(sel, ys, preferred_element_type=F32)
    f = w[:, 0:1] * picked[:tc] + w[:, 1:2] * picked[tc:]
    o = _layer_norm(alpha * x_ref[...] + f, g_ref[...], b_ref[...])
    if len(bounds) == 1:
        o_refs[0][...] = o
    else:
        lo = 0
        for o_ref, hi in zip(o_refs, bounds):
            @pl.when((i >= lo) & (i < hi))
            def _(o_ref=o_ref):
                o_ref[...] = o
            lo = hi


def _moe_combine(y, tab, lpos_tk, wts, x1, ln_g, ln_b, alpha, tc, out_rows):
    T = x1.shape[0]
    n_tiles = T // tc
    ns = _stage_rows(tc)
    out_specs, out_shapes, bounds, lo = [], [], [], 0
    for rows in out_rows:
        n = rows // tc
        out_specs.append(pl.BlockSpec((tc, D_MODEL), lambda i, lo=lo, n=n: (jnp.clip(i - lo, 0, n - 1), 0)))
        out_shapes.append(jax.ShapeDtypeStruct((rows, D_MODEL), F32))
        lo += n
        bounds.append(lo)
    return pl.pallas_call(
        functools.partial(_combine_kernel, ns=ns, alpha=alpha, bounds=tuple(bounds)),
        grid=(n_tiles,),
        in_specs=[
            pl.BlockSpec((1, 1, _tab_len(tc)), lambda i: (i, 0, 0), memory_space=pltpu.SMEM),
            pl.BlockSpec((1, 1, _tab_len(tc)), lambda i: (jnp.minimum(i + 1, n_tiles - 1), 0, 0),
                         memory_space=pltpu.SMEM),
            pl.BlockSpec((tc, 2), lambda i: (i, 0)),
            pl.BlockSpec(memory_space=pl.ANY),
            pl.BlockSpec((tc, D_MODEL), lambda i: (i, 0)),
            pl.BlockSpec((tc, 2), lambda i: (i, 0)),
            pl.BlockSpec((1, D_MODEL), lambda i: (0, 0)),
            pl.BlockSpec((1, D_MODEL), lambda i: (0, 0)),
        ],
        out_specs=out_specs,
        out_shape=out_shapes,
        scratch_shapes=[pltpu.VMEM((2, ns, D_MODEL), BF16), pltpu.SemaphoreType.DMA((2,))],
        compiler_params=_cparams(("arbitrary",)),
        name="moe_combine",
    )(tab, tab, lpos_tk, y, x1, wts, ln_g[None, :], ln_b[None, :])


def _moe_layer(x1, xb, idx, wt, wg, wu, wd, ln_g, ln_b, alpha, tm, tc, out_rows):
    tab, lpos_rows, lpos_tk, tile_e, n_used, pad_lo, pad_hi, n_rows = _moe_plan(idx[:2], tc, tm)
    xs = _moe_dispatch(xb, tab, lpos_rows, pad_lo, pad_hi, n_rows, tc)
    y = _moe_experts(xs, tile_e, n_used, wg, wu, wd, tm)
    return _moe_combine(y, tab, lpos_tk, wt[:2].T, x1, ln_g, ln_b, alpha, tc, out_rows)


def _inproj_odd_kernel(x_ref, w_ref, vone_ref, bg_ref, u_ref, q_ref, k_ref, v_ref, nrm_ref):
    x = x_ref[...].astype(BF16)
    proj = jnp.dot(x, w_ref[...], preferred_element_type=F32)
    bg_ref[...] = proj[:, :512].astype(BF16)
    u_ref[...] = (proj[:, 512:1024] * proj[:, 1024:1536]).astype(BF16)
    qb = (proj[:, 1536:2048] * (HEAD_DIM ** -0.5 * LOG2E)).astype(BF16)
    kb = proj[:, 2048:2304].astype(BF16)
    q_ref[...] = qb
    k_ref[...] = kb
    v_ref[...] = (proj[:, 2304:2560] + vone_ref[...]).astype(BF16)
    qf = qb.astype(F32)
    kf = kb.astype(F32)
    qq = jnp.max(jnp.sum(qf * qf, axis=1, keepdims=True), axis=0, keepdims=True)
    kk = jnp.max(jnp.sum(kf * kf, axis=1, keepdims=True), axis=0, keepdims=True)
    rid = lax.broadcasted_iota(I32, (8, 128), 0)
    nrm_ref[0] = jnp.where(rid == 0, qq, jnp.where(rid == 1, kk, 0.0))


def _inproj_odd(x, w_in, ts):
    T = x.shape[0]
    c3 = 3 * D_CONV
    wk = w_in[:, c3 + 512:c3 + 640].reshape(D_MODEL, N_KV_D, HEAD_DIM)
    wv = w_in[:, c3 + 640:c3 + 768].reshape(D_MODEL, N_KV_D, HEAD_DIM)
    wkp = jnp.concatenate([wk, jnp.zeros_like(wk)], axis=2).reshape(D_MODEL, 256)
    wvp = jnp.concatenate([wv, jnp.zeros_like(wv)], axis=2).reshape(D_MODEL, 256)
    w = jnp.concatenate([w_in[:, :c3 + 512], wkp, wvp], axis=1).astype(BF16)
    vone_np = np.zeros((1, 256), np.float32)
    for h in range(N_KV_D):
        vone_np[0, 128 * h + 64:128 * (h + 1)] = 1.0
    n_w = w.shape[1]
    widths = [512, 512, 512, 256, 256]
    bg, u, q, kw, vw, nrm = pl.pallas_call(
        _inproj_odd_kernel,
        grid=(T // ts,),
        in_specs=[
            pl.BlockSpec((ts, D_MODEL), lambda i: (i, 0)),
            pl.BlockSpec((D_MODEL, n_w), lambda i: (0, 0)),
            pl.BlockSpec((1, 256), lambda i: (0, 0)),
        ],
        out_specs=[pl.BlockSpec((ts, n), lambda i: (i, 0)) for n in widths]
        + [pl.BlockSpec((1, 8, 128), lambda i: (i, 0, 0))],
        out_shape=[jax.ShapeDtypeStruct((T, n), BF16) for n in widths]
        + [jax.ShapeDtypeStruct((T // ts, 8, 128), F32)],
        compiler_params=_cparams(("parallel",)),
        name="inproj_odd",
    )(x, w, jnp.asarray(vone_np))
    score_bound = jnp.sqrt(jnp.max(nrm[:, 0, 0]) * jnp.max(nrm[:, 1, 0]))
    return bg, u, q, kw, vw, score_bound


def _wattn_kernel(bounded_ref, q_ref, kp_ref, kc_ref, kn_ref, vp_ref, vc_ref, vn_ref, bias_ref, sink_ref,
                  place_ref, o_ref, *, tq, seq_tiles):
    i = pl.program_id(0)
    first = i < 0
    last = i < 0
    for lo, hi, per in seq_tiles:
        inside = (i >= lo) & (i < hi)
        first = first | (inside & ((i - lo) % per == 0))
        last = last | (inside & ((i - lo) % per == per - 1))
    kfull = jnp.concatenate([kp_ref[...], kc_ref[...], kn_ref[...]], axis=0)[:, :HEAD_DIM]
    vfull = jnp.concatenate([vp_ref[...], vc_ref[...], vn_ref[...]], axis=0)
    q4 = q_ref[...]
    bias = bias_ref[0]
    sink = sink_ref[0]
    nb = tq // Q_BLOCK
    col = lax.broadcasted_iota(I32, (GRP_D * Q_BLOCK, 3 * Q_BLOCK), 1)

    def blocks(with_max):
        for n in range(nb):
            qs = jnp.concatenate([q4[Q_BLOCK * n:Q_BLOCK * (n + 1), HEAD_DIM * g:HEAD_DIM * (g + 1)]
                                  for g in range(GRP_D)], axis=0)
            keys = kfull[Q_BLOCK * n:Q_BLOCK * (n + 3)]
            vals = vfull[Q_BLOCK * n:Q_BLOCK * (n + 3)]
            s = lax.dot_general(qs, keys, (((1,), (1,)), ((), ())), preferred_element_type=F32) + bias
            if n == 0:
                s = jnp.where(first & (col < Q_BLOCK), NEG_INF, s)
            if n == nb - 1:
                s = jnp.where(last & (col >= 2 * Q_BLOCK), NEG_INF, s)
            if with_max:
                m = jnp.maximum(jnp.max(s, axis=1, keepdims=True), sink)
                s = s - m
                snk = sink - m
            else:
                snk = sink
            acc = jnp.dot(jnp.exp2(s).astype(BF16), vals, preferred_element_type=F32)
            den = acc[:, HEAD_DIM:HEAD_DIM + 1] + jnp.exp2(snk)
            o = (acc / den).astype(BF16)
            ocat = jnp.concatenate([o[Q_BLOCK * g:Q_BLOCK * (g + 1)] for g in range(GRP_D)], axis=1)
            out = jnp.dot(ocat, place_ref[...], preferred_element_type=F32)
            o_ref[Q_BLOCK * n:Q_BLOCK * (n + 1), :] = out.astype(BF16)

    @pl.when(bounded_ref[0] == 1)
    def _():
        blocks(False)

    @pl.when(bounded_ref[0] != 1)
    def _():
        blocks(True)


WINDOW_BOUND_MAX = 90.0


def _window_attention(q, kw, vw, sink_logits, score_bound, classes, tq):
    T = q.shape[0]
    sink_bound = jnp.max(jnp.abs(sink_logits.astype(F32))) * LOG2E
    bounded = ((1.02 * score_bound <= WINDOW_BOUND_MAX) & (sink_bound <= WINDOW_BOUND_MAX)).astype(I32)[None]
    n_tiles = T // tq
    hb = tq // Q_BLOCK
    n_hblk = T // Q_BLOCK
    r = jnp.arange(Q_BLOCK, dtype=I32)[:, None]
    j = jnp.arange(3 * Q_BLOCK, dtype=I32)[None, :]
    rel = jnp.abs(j - Q_BLOCK - r).astype(F32)
    slopes = jnp.asarray(np.array([2.0 ** (-8.0 * (h + 1) / N_HEADS_D) for h in range(N_HEADS_D)], np.float32))
    bias = jnp.where(rel[None] <= WINDOW, -slopes[:, None, None] * rel[None] * LOG2E, NEG_INF)
    bias = bias.reshape(N_KV_D, GRP_D * Q_BLOCK, 3 * Q_BLOCK)
    sink = jnp.repeat(sink_logits.astype(F32) * LOG2E, Q_BLOCK).reshape(N_KV_D, GRP_D * Q_BLOCK, 1)
    place = _lane_place(128, 256, [0, 1, 2, 3]).reshape(GRP_D * 128, 256)
    seq_tiles = []
    t0 = 0
    for nseq, S, _ in classes:
        cnt = nseq * S // tq
        seq_tiles.append((t0, t0 + cnt, S // tq))
        t0 += cnt
    prev_map = lambda i, h, fl: (jnp.maximum(i * hb - 1, 0), h)
    next_map = lambda i, h, fl: (jnp.minimum((i + 1) * hb, n_hblk - 1), h)
    grid_spec = pltpu.PrefetchScalarGridSpec(
        num_scalar_prefetch=1,
        grid=(n_tiles, N_KV_D),
        in_specs=[
            pl.BlockSpec((tq, 256), lambda i, h, fl: (i, h)),
            pl.BlockSpec((Q_BLOCK, 128), prev_map),
            pl.BlockSpec((tq, 128), lambda i, h, fl: (i, h)),
            pl.BlockSpec((Q_BLOCK, 128), next_map),
            pl.BlockSpec((Q_BLOCK, 128), prev_map),
            pl.BlockSpec((tq, 128), lambda i, h, fl: (i, h)),
            pl.BlockSpec((Q_BLOCK, 128), next_map),
            pl.BlockSpec((1, GRP_D * Q_BLOCK, 3 * Q_BLOCK), lambda i, h, fl: (h, 0, 0)),
            pl.BlockSpec((1, GRP_D * Q_BLOCK, 1), lambda i, h, fl: (h, 0, 0)),
            pl.BlockSpec((GRP_D * 128, 256), lambda i, h, fl: (0, 0)),
        ],
        out_specs=pl.BlockSpec((tq, 256), lambda i, h, fl: (i, h)),
    )
    return pl.pallas_call(
        functools.partial(_wattn_kernel, tq=tq, seq_tiles=tuple(seq_tiles)),
        grid_spec=grid_spec,
        out_shape=jax.ShapeDtypeStruct((T, 512), BF16),
        compiler_params=_cparams(("parallel", "parallel")),
        name="window_attention",
    )(bounded, q, kw, kw, kw, vw, vw, vw, bias, sink, place)


HALO = 16


def _outproj_odd_kernel(bg_ref, u_ref, up_ref, un_ref, a_ref, x_ref, cw_ref, cb_ref, wc_ref, wa_ref,
                        g_ref, b_ref, rw_ref, rb_ref, x1_ref, idx_ref, wt_ref, xb_ref,
                        *, alpha, ts, seq_tiles):
    i = pl.program_id(0)
    first = i < 0
    last = i < 0
    for lo, hi, per in seq_tiles:
        inside = (i >= lo) & (i < hi)
        first = first | (inside & ((i - lo) % per == 0))
        last = last | (inside & ((i - lo) % per == per - 1))
    u = u_ref[...].astype(F32)
    prev_row = jnp.where(first, 0.0, up_ref[HALO - 1:HALO, :].astype(F32))
    next_row = jnp.where(last, 0.0, un_ref[0:1, :].astype(F32))
    rid = lax.broadcasted_iota(I32, u.shape, 0)
    ud = jnp.where(rid == 0, prev_row, pltpu.roll(u, 1, 0))
    uu = jnp.where(rid == ts - 1, next_row, pltpu.roll(u, ts - 1, 0))
    cw = cw_ref[...]
    y = ud * cw[0:1] + u * cw[1:2] + uu * cw[2:3] + cb_ref[...]
    c = (bg_ref[...].astype(F32) * y).astype(BF16)
    nr = ts // ROW_SPLITS
    for h in range(ROW_SPLITS):
        r0 = h * nr
        m = (jnp.dot(c[r0:r0 + nr], wc_ref[...], preferred_element_type=F32)
             + jnp.dot(a_ref[r0:r0 + nr, :], wa_ref[...], preferred_element_type=F32))
        _norm_route_store(r0, nr, m, x_ref[r0:r0 + nr, :], g_ref, b_ref, rw_ref, rb_ref, x1_ref, idx_ref,
                          wt_ref, xb_ref, alpha)


def _outproj_odd(bg, u, a, x, conv_w, conv_b, w_out, ln_g, ln_b, router, alpha, classes, ts):
    T = x.shape[0]
    hb = ts // HALO
    n_h = T // HALO
    seq_tiles = []
    t0 = 0
    for nseq, S, _ in classes:
        cnt = nseq * S // ts
        seq_tiles.append((t0, t0 + cnt, S // ts))
        t0 += cnt
    out_specs, out_shapes = _mix_out_specs(ts, T)
    return pl.pallas_call(
        functools.partial(_outproj_odd_kernel, alpha=alpha, ts=ts, seq_tiles=tuple(seq_tiles)),
        grid=(T // ts,),
        in_specs=[
            pl.BlockSpec((ts, D_CONV), lambda i: (i, 0)),
            pl.BlockSpec((ts, D_CONV), lambda i: (i, 0)),
            pl.BlockSpec((HALO, D_CONV), lambda i: (jnp.maximum(i * hb - 1, 0), 0)),
            pl.BlockSpec((HALO, D_CONV), lambda i: (jnp.minimum((i + 1) * hb, n_h - 1), 0)),
            pl.BlockSpec((ts, 512), lambda i: (i, 0)),
            pl.BlockSpec((ts, D_MODEL), lambda i: (i, 0)),
            pl.BlockSpec((3, D_CONV), lambda i: (0, 0)),
            pl.BlockSpec((1, D_CONV), lambda i: (0, 0)),
            pl.BlockSpec((D_CONV, D_MODEL), lambda i: (0, 0)),
            pl.BlockSpec((512, D_MODEL), lambda i: (0, 0)),
            pl.BlockSpec((1, D_MODEL), lambda i: (0, 0)),
            pl.BlockSpec((1, D_MODEL), lambda i: (0, 0)),
        ] + _ROUTER_SPECS,
        out_specs=out_specs,
        out_shape=out_shapes,
        compiler_params=_cparams(("parallel",)),
        name="outproj_odd",
    )(bg, u, u, u, a, x, conv_w.astype(F32), conv_b.astype(F32)[None, :],
      w_out[:D_CONV].astype(BF16), w_out[D_CONV:].astype(BF16), ln_g[None, :], ln_b[None, :], *router)


def _tile(n, cap):
    t = cap
    while n % t:
        t //= 2
    return t


def kernel(x_prompt, x_sample, w_in_even, fourier_norm_g, q_norm_g, k_norm_g, w_out_even, w_in_odd, conv_w,
           conv_b, sink_logits, w_out_odd, ln_mix_g, ln_mix_b, ln_ffn_g, ln_ffn_b, router_w, router_b,
           w_gate, w_up, w_down):
    depth = ln_mix_g.shape[0]
    alpha = float((2 * depth) ** 0.25)
    bp, sp, _ = x_prompt.shape
    bs, ss, _ = x_sample.shape
    classes = ((bp, sp, 0), (bs, ss, bp * sp))
    T = bp * sp + bs * ss
    min_s = min(sp, ss)
    ts = _tile(min_s, 512)
    tq_w = _tile(min_s, 512)
    tm = 1024
    tc = _tile(min_s, 256)
    x_parts = [x_prompt.reshape(bp * sp, D_MODEL), x_sample.reshape(bs * ss, D_MODEL)]
    router = _router_operands(router_w, router_b)
    for l in range(depth):
        i = l // 2
        if l % 2 == 0:
            uf, qk, v_aug = _inproj_even(x_parts, w_in_even[i], fourier_norm_g[i], q_norm_g[i], k_norm_g[i],
                                         classes, ts)
            bounded = _score_bounded(q_norm_g[i], k_norm_g[i])
            f_parts, a_parts = [], []
            for nseq, S, tok0 in classes:
                f_parts.append(_fourier_mix(uf, nseq, S, tok0))
                a_parts.append(_global_attention(qk, v_aug, bounded, nseq, S, tok0, _tile(S, 1024),
                                                 _tile(S, 2048)))
            x1, idx, wt, xb = _outproj_even(f_parts, a_parts, x_parts, w_out_even[i], ln_mix_g[l], ln_mix_b[l],
                                            router, alpha, ts)
        else:
            x = x_parts[0]
            bg, u, q, kw, vw, score_bound = _inproj_odd(x, w_in_odd[i], ts)
            a = _window_attention(q, kw, vw, sink_logits[i], score_bound, classes, tq_w)
            x1, idx, wt, xb = _outproj_odd(bg, u, a, x, conv_w[i], conv_b[i], w_out_odd[i], ln_mix_g[l],
                                           ln_mix_b[l], router, alpha, classes, ts)
        out_rows = [bp * sp, bs * ss] if l == depth - 1 else [T]
        x_parts = _moe_layer(x1, xb, idx, wt, w_gate[l], w_up[l], w_down[l], ln_ffn_g[l], ln_ffn_b[l], alpha,
                             tm, tc, out_rows)
    if len(x_parts) == 1:
        x_parts = [x_parts[0][:bp * sp], x_parts[0][bp * sp:]]
    return (x_parts[0].reshape(bp, sp, D_MODEL), x_parts[1].reshape(bs, ss, D_MODEL))
```

```python
import functools
import math

import numpy as np
import jax
import jax.numpy as jnp
from jax import lax
from jax.experimental import pallas as pl
from jax.experimental.pallas import tpu as pltpu

F32 = jnp.float32
BF16 = jnp.bfloat16
I32 = jnp.int32

D_MODEL = 1024
HEAD_DIM = 64
GRID_W = 64
Q_BLOCK = 128
WINDOW = 128
ROPE_THETA = 10000.0
N_FOURIER_GROUPS = 4
D_FOURIER = 256
N_HEADS_B = 12
N_KV_B = 4
GRP_B = 3
D_CONV = 512
N_HEADS_D = 8
N_KV_D = 2
GRP_D = 4
N_EXPERTS = 16
N_GROUPS = 4
EXPERTS_PER_GROUP = 4
D_FF = 512
LN_EPS = 1e-5
RMS_EPS = 1e-6
NEG_INF = -1e30
LOG2E = 1.4426950408889634
FFT_S2 = 128
V7X_VMEM_LIMIT = 48 * 1024 * 1024


def _cparams(sem):
    return pltpu.CompilerParams(dimension_semantics=sem, vmem_limit_bytes=V7X_VMEM_LIMIT)


def _layer_norm(y, g, b):
    mu = jnp.mean(y, axis=-1, keepdims=True)
    yc = y - mu
    var = jnp.mean(yc * yc, axis=-1, keepdims=True)
    return yc * lax.rsqrt(var + LN_EPS) * g + b


N_NORM_COLS = D_FOURIER + 1024


def _parts_specs(parts, ts):
    specs, bounds, lo = [], [], 0
    for p in parts:
        n = p.shape[0] // ts
        specs.append(pl.BlockSpec((ts, p.shape[1]), lambda i, lo=lo, n=n: (jnp.clip(i - lo, 0, n - 1), 0)))
        lo += n
        bounds.append(lo)
    return specs, tuple(bounds)


def _pick_rows(refs, bounds, r0, nr):
    i = pl.program_id(0)
    out = refs[-1][r0:r0 + nr, :]
    for ref, hi in reversed(list(zip(refs[:-1], bounds[:-1]))):
        out = jnp.where(i < hi, ref[r0:r0 + nr, :], out)
    return out


def _inproj_even_kernel(*refs, nx, bounds):
    x_refs = refs[:nx]
    (w_ref, gain_ref, seg_ref, segt_ref, cos_ref, sin_ref, vone_ref, uf_ref, qk_ref, v_ref) = refs[nx:]
    nr = uf_ref.shape[0] // ROW_SPLITS
    for part in range(ROW_SPLITS):
        r0 = part * nr
        x = _pick_rows(x_refs, bounds, r0, nr).astype(BF16)
        proj = jnp.dot(x, w_ref[...], preferred_element_type=F32)
        nrm = proj[:, :N_NORM_COLS]
        sq = (nrm * nrm).astype(BF16)
        ssum = jnp.dot(sq, seg_ref[...], preferred_element_type=F32)
        r = lax.rsqrt(ssum * (1.0 / HEAD_DIM) + RMS_EPS)
        rh = r.astype(BF16)
        rl = (r - rh.astype(F32)).astype(BF16)
        rex = jnp.dot(jnp.concatenate([rh, rl], axis=1), segt_ref[...], preferred_element_type=F32)
        y = nrm * rex * gain_ref[...]
        uf_ref[r0:r0 + nr, :] = y[:, :D_FOURIER].astype(BF16)
        yq = y[:, D_FOURIER:]
        c = jnp.concatenate([cos_ref[r0:r0 + nr, :]] * 8, axis=1)
        s = jnp.concatenate([sin_ref[r0:r0 + nr, :]] * 8, axis=1)
        lane = lax.broadcasted_iota(I32, yq.shape, 1)
        first = (lane & 31) < 16
        sw = jnp.where(first, pltpu.roll(yq, 1024 - 16, 1), pltpu.roll(yq, 16, 1))
        qk_ref[r0:r0 + nr, :] = (yq * c + sw * s).astype(BF16)
        vv = proj[:, N_NORM_COLS:] + vone_ref[...]
        for h in range(N_KV_B):
            v_ref[h, r0:r0 + nr, :] = vv[:, 128 * h:128 * (h + 1)].astype(BF16)


def _rope_tables(n_pos):
    t = jnp.arange(n_pos)
    row = (t // GRID_W).astype(F32)
    col = (t % GRID_W).astype(F32)
    n_freq = HEAD_DIM // 4
    inv_freq = ROPE_THETA ** (-jnp.arange(n_freq, dtype=F32) / n_freq)
    ar = row[:, None] * inv_freq
    ac = col[:, None] * inv_freq
    ang = jnp.concatenate([ar, ar, ac, ac], axis=1)
    sign = jnp.asarray(np.tile(np.repeat(np.array([-1.0, 1.0], np.float32), 16), 2))
    cos = jnp.cos(ang)
    sin = jnp.sin(ang) * sign
    return jnp.concatenate([cos, cos], axis=1), jnp.concatenate([sin, sin], axis=1)


def _pos_block_map(classes, ts):
    bounds = []
    tile0 = 0
    for nseq, S, _ in classes:
        n_tiles = nseq * S // ts
        bounds.append((tile0, tile0 + n_tiles, S // ts))
        tile0 += n_tiles

    def fn(i):
        out = (i - bounds[-1][0]) % bounds[-1][2]
        for lo, hi, per in reversed(bounds[:-1]):
            out = jnp.where(i < hi, (i - lo) % per, out)
        return out

    return fn


def _inproj_even(x_parts, w_in, f_g, q_g, k_g, classes, ts):
    T = sum(p.shape[0] for p in x_parts)
    x_specs, bounds = _parts_specs(x_parts, ts)
    wf = w_in[:, :D_FOURIER]
    wq = w_in[:, D_FOURIER:D_FOURIER + 768].reshape(D_MODEL, N_KV_B, GRP_B * HEAD_DIM)
    wk = w_in[:, D_FOURIER + 768:D_FOURIER + 1024].reshape(D_MODEL, N_KV_B, HEAD_DIM)
    wv = w_in[:, D_FOURIER + 1024:].reshape(D_MODEL, N_KV_B, HEAD_DIM)
    wqk = jnp.concatenate([wk, wq], axis=2).reshape(D_MODEL, 1024)
    wvp = jnp.concatenate([wv, jnp.zeros_like(wv)], axis=2).reshape(D_MODEL, 512)
    w = jnp.concatenate([wf, wqk, wvp], axis=1).astype(BF16)
    qscale = HEAD_DIM ** -0.5 * LOG2E
    gqk = jnp.tile(jnp.concatenate([k_g, q_g * qscale, q_g * qscale, q_g * qscale]), N_KV_B)
    gain = jnp.concatenate([f_g, gqk])[None, :].astype(F32)
    seg_np = np.zeros((N_NORM_COLS, 128), np.float32)
    seg_np[np.arange(N_NORM_COLS), np.arange(N_NORM_COLS) // HEAD_DIM] = 1.0
    seg = jnp.asarray(seg_np, BF16)
    segt = jnp.asarray(np.concatenate([seg_np.T, seg_np.T], axis=0), BF16)
    max_s = max(S for _, S, _ in classes)
    cos, sin = _rope_tables(max_s)
    vone_np = np.zeros((1, 512), np.float32)
    for h in range(N_KV_B):
        vone_np[0, 128 * h + 64:128 * (h + 1)] = 1.0
    vone = jnp.asarray(vone_np)
    posmap = _pos_block_map(classes, ts)
    n_w = w.shape[1]
    return pl.pallas_call(
        functools.partial(_inproj_even_kernel, nx=len(x_parts), bounds=bounds),
        grid=(T // ts,),
        in_specs=x_specs + [
            pl.BlockSpec((D_MODEL, n_w), lambda i: (0, 0)),
            pl.BlockSpec((1, N_NORM_COLS), lambda i: (0, 0)),
            pl.BlockSpec((N_NORM_COLS, 128), lambda i: (0, 0)),
            pl.BlockSpec((256, N_NORM_COLS), lambda i: (0, 0)),
            pl.BlockSpec((ts, 128), lambda i: (posmap(i), 0)),
            pl.BlockSpec((ts, 128), lambda i: (posmap(i), 0)),
            pl.BlockSpec((1, 512), lambda i: (0, 0)),
        ],
        out_specs=[
            pl.BlockSpec((ts, D_FOURIER), lambda i: (i, 0)),
            pl.BlockSpec((ts, 1024), lambda i: (i, 0)),
            pl.BlockSpec((N_KV_B, ts, 128), lambda i: (0, i, 0)),
        ],
        out_shape=[
            jax.ShapeDtypeStruct((T, D_FOURIER), BF16),
            jax.ShapeDtypeStruct((T, 1024), BF16),
            jax.ShapeDtypeStruct((N_KV_B, T, 128), BF16),
        ],
        compiler_params=_cparams(("parallel",)),
        name="inproj_even",
    )(*x_parts, w, gain, seg, segt, cos, sin, vone)


FFT_TB = 16


def _fft1_kernel(x_ref, kw_ref, twr_ref, twi_ref, zr_ref, zi_ref, *, s1, bs):
    n = s1 * FFT_TB
    twr = twr_ref[0]
    twi = twi_ref[0]
    for q in range(bs):
        x2 = x_ref[q * s1:(q + 1) * s1].reshape(n, D_FOURIER)
        z = jnp.dot(kw_ref[...], x2, preferred_element_type=F32)
        zr = z[:n]
        zi = z[n:]
        zr_ref[q * s1:(q + 1) * s1] = (zr * twr - zi * twi).astype(BF16).reshape(s1, FFT_TB, D_FOURIER)
        zi_ref[q * s1:(q + 1) * s1] = (zr * twi + zi * twr).astype(BF16).reshape(s1, FFT_TB, D_FOURIER)


def _fft2_kernel(zr_ref, zi_ref, w2a_ref, w2b_ref, mix_ref, o_ref, *, cb):
    s2 = FFT_S2
    for c in range(cb):
        pp = (jnp.dot(w2a_ref[...], zr_ref[0, c], preferred_element_type=F32)
              + jnp.dot(w2b_ref[...], zi_ref[0, c], preferred_element_type=F32))
        f = (jnp.dot(pp[:s2].astype(BF16), mix_ref[:D_FOURIER], preferred_element_type=F32)
             + jnp.dot(pp[s2:].astype(BF16), mix_ref[D_FOURIER:], preferred_element_type=F32))
        o_ref[0, :, D_FOURIER * c:D_FOURIER * (c + 1)] = f.astype(BF16)


def _dft_mats(n):
    k = jnp.arange(n, dtype=I32)
    ang = (2.0 * math.pi / n) * ((k[:, None] * k[None, :]) % n).astype(F32)
    return jnp.cos(ang), jnp.sin(ang)


def _fourier_mix(uf, nseq, S, tok0):
    s2 = FFT_S2
    s1 = S // s2
    tb = FFT_TB
    n = s1 * tb
    x3 = uf.reshape(uf.shape[0] // s2, s2, D_FOURIER)
    blk0 = tok0 // S
    row = jnp.arange(2 * n, dtype=I32)[:, None]
    col = jnp.arange(n, dtype=I32)[None, :]
    ang1 = (2.0 * math.pi / s1) * ((((row % n) // tb) * (col // tb)) % s1).astype(F32)
    w1v = jnp.where(row < n, jnp.cos(ang1), -jnp.sin(ang1))
    kw = jnp.where((row % tb) == (col % tb), w1v, 0.0).astype(BF16)
    cc = jnp.arange(s1, dtype=I32)[:, None]
    bb = jnp.arange(s2, dtype=I32)[None, :]
    ang = (2.0 * math.pi / S) * ((cc * bb) % S).astype(F32)

    def table(t):
        t = t.reshape(s1, s2 // tb, tb).transpose(1, 0, 2).reshape(s2 // tb, n, 1)
        return jnp.broadcast_to(t, (s2 // tb, n, D_FOURIER))
    twr = table(jnp.cos(ang))
    twi = table(-jnp.sin(ang))
    bs = math.gcd(math.gcd(nseq, max(1, 64 // s1)), blk0) if blk0 else math.gcd(nseq, max(1, 64 // s1))
    zr, zi = pl.pallas_call(
        functools.partial(_fft1_kernel, s1=s1, bs=bs),
        grid=(s2 // tb, nseq // bs),
        in_specs=[
            pl.BlockSpec((bs * s1, tb, D_FOURIER), lambda j, b: (blk0 // bs + b, j, 0)),
            pl.BlockSpec((2 * n, n), lambda j, b: (0, 0)),
            pl.BlockSpec((1, n, D_FOURIER), lambda j, b: (j, 0, 0)),
            pl.BlockSpec((1, n, D_FOURIER), lambda j, b: (j, 0, 0)),
        ],
        out_specs=[
            pl.BlockSpec((bs * s1, tb, D_FOURIER), lambda j, b: (b, j, 0)),
            pl.BlockSpec((bs * s1, tb, D_FOURIER), lambda j, b: (b, j, 0)),
        ],
        out_shape=[jax.ShapeDtypeStruct((nseq * s1, s2, D_FOURIER), BF16)] * 2,
        compiler_params=_cparams(("parallel", "parallel")),
        name="fourier_stage1",
    )(x3, kw, twr, twi)
    zr = zr.reshape(nseq, s1, s2, D_FOURIER)
    zi = zi.reshape(nseq, s1, s2, D_FOURIER)
    c2, sn2 = _dft_mats(s2)
    w2a = jnp.concatenate([c2, -sn2], axis=0).astype(BF16)
    w2b = jnp.concatenate([sn2, c2], axis=0).astype(BF16)
    gc, gs = _dft_mats(HEAD_DIM)
    scale = 1.0 / math.sqrt(S * HEAD_DIM)
    eye = jnp.eye(N_FOURIER_GROUPS, dtype=F32)
    mix = (jnp.concatenate([jnp.kron(eye, gc), jnp.kron(eye, gs)], axis=0) * scale).astype(BF16)
    cb = min(8, s1)
    out = pl.pallas_call(
        functools.partial(_fft2_kernel, cb=cb),
        grid=(nseq, s1 // cb),
        in_specs=[
            pl.BlockSpec((1, cb, s2, D_FOURIER), lambda b, j: (b, j, 0, 0)),
            pl.BlockSpec((1, cb, s2, D_FOURIER), lambda b, j: (b, j, 0, 0)),
            pl.BlockSpec((2 * s2, s2), lambda b, j: (0, 0)),
            pl.BlockSpec((2 * s2, s2), lambda b, j: (0, 0)),
            pl.BlockSpec((2 * D_FOURIER, D_FOURIER), lambda b, j: (0, 0)),
        ],
        out_specs=pl.BlockSpec((1, s2, cb * D_FOURIER), lambda b, j: (b, 0, j)),
        out_shape=jax.ShapeDtypeStruct((nseq, s2, s1 * D_FOURIER), BF16),
        compiler_params=_cparams(("parallel", "parallel")),
        name="fourier_stage2",
    )(zr, zi, w2a, w2b, mix)
    return out.reshape(nseq * S, D_FOURIER)


SCORE_BOUND_MAX = 100.0


def _gattn_kernel(bounded_ref, q_ref, k_ref, v_ref, place_ref, o_ref, *, tq, tk, S):
    q3 = q_ref[...]
    qs = [q3[:, HEAD_DIM * (g + 1):HEAD_DIM * (g + 2)] for g in range(GRP_B)]

    def chunk(j):
        off = pl.multiple_of(j * tk, tk)
        return k_ref[pl.ds(off, tk), 0:HEAD_DIM], v_ref[0, pl.ds(off, tk), :]

    def scores(g, kc):
        return lax.dot_general(qs[g], kc, (((1,), (1,)), ((), ())), preferred_element_type=F32)

    def finish(accs):
        out = jnp.zeros((tq, 256), F32)
        for g in range(GRP_B):
            o = (accs[g] / accs[g][:, HEAD_DIM:HEAD_DIM + 1]).astype(BF16)
            out = out + jnp.dot(o, place_ref[g], preferred_element_type=F32)
        o_ref[...] = out.astype(BF16)

    @pl.when(bounded_ref[0] == 1)
    def _():
        def body(j, accs):
            kc, vc = chunk(j)
            return tuple(accs[g] + jnp.dot(jnp.exp2(scores(g, kc)).astype(BF16), vc,
                                           preferred_element_type=F32) for g in range(GRP_B))
        finish(lax.fori_loop(0, S // tk, body, tuple(jnp.zeros((tq, 128), F32) for _ in range(GRP_B))))

    @pl.when(bounded_ref[0] != 1)
    def _():
        def body(j, carry):
            kc, vc = chunk(j)
            new = []
            for g in range(GRP_B):
                m_prev, acc = carry[g]
                s = scores(g, kc)
                m_new = jnp.maximum(m_prev, jnp.max(s, axis=1, keepdims=True))
                p = jnp.exp2(s - m_new).astype(BF16)
                acc = jnp.exp2(m_prev - m_new) * acc + jnp.dot(p, vc, preferred_element_type=F32)
                new.append((m_new, acc))
            return tuple(new)
        init = tuple((jnp.full((tq, 1), NEG_INF, F32), jnp.zeros((tq, 128), F32)) for _ in range(GRP_B))
        fin = lax.fori_loop(0, S // tk, body, init)
        finish([fin[g][1] for g in range(GRP_B)])


def _score_bounded(q_g, k_g):
    bound = 1.05 * HEAD_DIM * (HEAD_DIM ** -0.5 * LOG2E) * jnp.max(jnp.abs(q_g)) * jnp.max(jnp.abs(k_g))
    return (bound <= SCORE_BOUND_MAX).astype(I32)[None]


def _lane_place(n_src, n_dst, groups):
    pm = np.zeros((len(groups), n_src, n_dst), np.float32)
    for gi, slot in enumerate(groups):
        pm[gi, np.arange(HEAD_DIM), HEAD_DIM * slot + np.arange(HEAD_DIM)] = 1.0
    return jnp.asarray(pm, BF16)


def _global_attention(qk, v_aug, bounded, nseq, S, tok0, tq, tk):
    n_qt = S // tq
    qt0 = tok0 // tq
    s0 = tok0 // S
    place = _lane_place(128, 256, [1, 2, 3])
    grid_spec = pltpu.PrefetchScalarGridSpec(
        num_scalar_prefetch=1,
        grid=(nseq, N_KV_B, n_qt),
        in_specs=[
            pl.BlockSpec((tq, 256), lambda b, h, i, fl: (qt0 + b * n_qt + i, h)),
            pl.BlockSpec((S, 256), lambda b, h, i, fl: (s0 + b, h)),
            pl.BlockSpec((1, S, 128), lambda b, h, i, fl: (h, s0 + b, 0)),
            pl.BlockSpec((GRP_B, 128, 256), lambda b, h, i, fl: (0, 0, 0)),
        ],
        out_specs=pl.BlockSpec((tq, 256), lambda b, h, i, fl: (b * n_qt + i, h)),
    )
    return pl.pallas_call(
        functools.partial(_gattn_kernel, tq=tq, tk=tk, S=S),
        grid_spec=grid_spec,
        out_shape=jax.ShapeDtypeStruct((nseq * S, 1024), BF16),
        compiler_params=_cparams(("parallel", "parallel", "arbitrary")),
        name="global_attention",
    )(bounded, qk, qk, v_aug, place)


def _route(x1, rw_ref, rb_ref):
    n = x1.shape[0]
    xh = x1.astype(BF16)
    xl = (x1 - xh.astype(F32)).astype(BF16)
    o = jnp.dot(jnp.concatenate([xh, xl], axis=0), rw_ref[...], preferred_element_type=F32)
    lg = o[:n, :128] + o[:n, 128:] + o[n:, :128]
    lgt = lg.T[:N_EXPERTS]
    sc = 1.0 / (1.0 + jnp.exp(-lgt))
    bi = sc + rb_ref[...]
    srow = [sc[e:e + 1] for e in range(N_EXPERTS)]
    brow = [bi[e:e + 1] for e in range(N_EXPERTS)]
    gsel = None
    best = None
    for g in range(N_GROUPS):
        a, b, c, d = brow[4 * g:4 * g + 4]
        m1, n1 = jnp.maximum(a, b), jnp.minimum(a, b)
        m2, n2 = jnp.maximum(c, d), jnp.minimum(c, d)
        gs = jnp.maximum(m1, m2) + jnp.maximum(jnp.minimum(m1, m2), jnp.maximum(n1, n2))
        if g == 0:
            best, gsel = gs, jnp.zeros(gs.shape, I32)
        else:
            better = gs > best
            gsel = jnp.where(better, g, gsel)
            best = jnp.where(better, gs, best)
    masked = [jnp.where(gsel == (e // EXPERTS_PER_GROUP), brow[e], NEG_INF) for e in range(N_EXPERTS)]
    i1 = jnp.zeros(gsel.shape, I32)
    b1 = masked[0]
    s1 = srow[0]
    for e in range(1, N_EXPERTS):
        better = masked[e] > b1
        i1 = jnp.where(better, e, i1)
        b1 = jnp.where(better, masked[e], b1)
        s1 = jnp.where(better, srow[e], s1)
    i2 = jnp.full(gsel.shape, -1, I32)
    b2 = jnp.full(b1.shape, -jnp.inf, F32)
    s2 = jnp.zeros(b1.shape, F32)
    for e in range(N_EXPERTS):
        better = (masked[e] > b2) & (i1 != e)
        i2 = jnp.where(better, e, i2)
        b2 = jnp.where(better, masked[e], b2)
        s2 = jnp.where(better, srow[e], s2)
    den = s1 + s2
    return i1, i2, s1 / den, s2 / den


ROW_SPLITS = 2


def _norm_route_store(r0, nr, m, x, g_ref, b_ref, rw_ref, rb_ref, x1_ref, idx_ref, wt_ref, xb_ref, alpha):
    x1 = _layer_norm(alpha * x + m, g_ref[...], b_ref[...])
    x1_ref[r0:r0 + nr, :] = x1
    xb_ref[r0:r0 + nr, :] = x1.astype(BF16)
    i1, i2, w1, w2 = _route(x1, rw_ref, rb_ref)
    rid = lax.broadcasted_iota(I32, (8, nr), 0)
    idx_ref[:, r0:r0 + nr] = jnp.where(rid == 0, i1, jnp.where(rid == 1, i2, 0))
    wt_ref[:, r0:r0 + nr] = jnp.where(rid == 0, w1, jnp.where(rid == 1, w2, 0.0))


def _router_operands(router_w, router_b):
    rw = jnp.zeros((D_MODEL, 128), F32).at[:, :N_EXPERTS].set(router_w.astype(F32))
    rwh = rw.astype(BF16)
    rwl = (rw - rwh.astype(F32)).astype(BF16)
    rb = router_b.astype(F32)[:, None]
    return jnp.concatenate([rwh, rwl], axis=1), rb


_ROUTER_SPECS = [
    pl.BlockSpec((D_MODEL, 256), lambda i: (0, 0)),
    pl.BlockSpec((N_EXPERTS, 1), lambda i: (0, 0)),
]


def _mix_out_specs(ts, T):
    specs = [
        pl.BlockSpec((ts, D_MODEL), lambda i: (i, 0)),
        pl.BlockSpec((8, ts), lambda i: (0, i)),
        pl.BlockSpec((8, ts), lambda i: (0, i)),
        pl.BlockSpec((ts, D_MODEL), lambda i: (i, 0)),
    ]
    shapes = [
        jax.ShapeDtypeStruct((T, D_MODEL), F32),
        jax.ShapeDtypeStruct((8, T), I32),
        jax.ShapeDtypeStruct((8, T), F32),
        jax.ShapeDtypeStruct((T, D_MODEL), BF16),
    ]
    return specs, shapes


def _outproj_even_kernel(*refs, nf, na, nx, fb, ab, xb, alpha):
    f_refs, a_refs, x_refs = refs[:nf], refs[nf:nf + na], refs[nf + na:nf + na + nx]
    wf_ref, wa_ref, g_ref, b_ref, rw_ref, rb_ref, x1_ref, idx_ref, wt_ref, xb_ref = refs[nf + na + nx:]
    nr = x1_ref.shape[0] // ROW_SPLITS
    for h in range(ROW_SPLITS):
        r0 = h * nr
        m = (jnp.dot(_pick_rows(f_refs, fb, r0, nr), wf_ref[...], preferred_element_type=F32)
             + jnp.dot(_pick_rows(a_refs, ab, r0, nr), wa_ref[...], preferred_element_type=F32))
        xr = _pick_rows(x_refs, xb, r0, nr)
        _norm_route_store(r0, nr, m, xr, g_ref, b_ref, rw_ref, rb_ref, x1_ref, idx_ref, wt_ref, xb_ref, alpha)


def _outproj_even(f_parts, a_parts, x_parts, w_out, ln_g, ln_b, router, alpha, ts):
    T = sum(p.shape[0] for p in x_parts)
    f_specs, fb = _parts_specs(f_parts, ts)
    a_specs, ab = _parts_specs(a_parts, ts)
    x_specs, xb = _parts_specs(x_parts, ts)
    wf = w_out[:D_FOURIER].astype(BF16)
    wa = w_out[D_FOURIER:].reshape(N_KV_B, GRP_B * HEAD_DIM, D_MODEL)
    wa = jnp.concatenate([jnp.zeros((N_KV_B, HEAD_DIM, D_MODEL), w_out.dtype), wa], axis=1)
    wa = wa.reshape(1024, D_MODEL).astype(BF16)
    out_specs, out_shapes = _mix_out_specs(ts, T)
    return pl.pallas_call(
        functools.partial(_outproj_even_kernel, nf=len(f_parts), na=len(a_parts), nx=len(x_parts),
                          fb=fb, ab=ab, xb=xb, alpha=alpha),
        grid=(T // ts,),
        in_specs=f_specs + a_specs + x_specs + [
            pl.BlockSpec((D_FOURIER, D_MODEL), lambda i: (0, 0)),
            pl.BlockSpec((1024, D_MODEL), lambda i: (0, 0)),
            pl.BlockSpec((1, D_MODEL), lambda i: (0, 0)),
            pl.BlockSpec((1, D_MODEL), lambda i: (0, 0)),
        ] + _ROUTER_SPECS,
        out_specs=out_specs,
        out_shape=out_shapes,
        compiler_params=_cparams(("parallel",)),
        name="outproj_even",
    )(*f_parts, *a_parts, *x_parts, wf, wa, ln_g[None, :], ln_b[None, :], *router)


CHUNK = 16


def _stage_rows(td):
    return -(-(2 * td + N_EXPERTS * (CHUNK - 1)) // 256) * 256


def _tab_len(td):
    return -(-(1 + 2 * td // CHUNK + N_EXPERTS) // 128) * 128


def _moe_plan(idx, td, tm):
    T = idx.shape[1]
    nt = T // td
    ep = idx.reshape(2, nt, td).transpose(1, 0, 2).reshape(nt, 2 * td)
    oh = (ep[None, :, :] == jnp.arange(N_EXPERTS, dtype=I32)[:, None, None]).astype(I32)
    cs = jnp.cumsum(oh, axis=2)
    cnt = cs[:, :, -1].T
    c8 = ((cnt + CHUNK - 1) // CHUNK) * CHUNK
    off8 = jnp.cumsum(c8, axis=1) - c8
    base8 = jnp.cumsum(c8, axis=0) - c8
    seg = jnp.sum(c8, axis=0)
    padded = ((seg + tm - 1) // tm) * tm
    ends = jnp.cumsum(padded)
    starts = ends - padded
    dst = starts[None, :] + base8
    lpos = jnp.sum(oh * (cs - 1 + off8.T[:, :, None]), axis=0).astype(I32)
    n_rows = -(-(2 * T + nt * N_EXPERTS * (CHUNK - 1)) // tm) * tm + N_EXPERTS * tm
    tile_start = jnp.arange(n_rows // tm, dtype=I32) * tm
    tile_e = jnp.minimum(jnp.sum((ends[None, :] <= tile_start[:, None]).astype(I32), axis=1), N_EXPERTS - 1)
    n_used = (ends[-1] // tm).astype(I32)[None]
    nc = c8 // CHUNK
    cum = jnp.cumsum(nc, axis=1)
    n_chunk_max = 2 * td // CHUNK + N_EXPERTS
    c_idx = jnp.arange(n_chunk_max, dtype=I32)
    e_c = jnp.minimum(jnp.sum((cum[:, None, :] <= c_idx[None, :, None]).astype(I32), axis=2), N_EXPERTS - 1)
    oh_c = (e_c[:, :, None] == jnp.arange(N_EXPERTS, dtype=I32)[None, None, :]).astype(I32)
    srow = jnp.sum(oh_c * (dst - CHUNK * (cum - nc))[:, None, :], axis=2) + CHUNK * c_idx[None, :]
    tab = jnp.concatenate([cum[:, -1:], srow], axis=1).astype(I32)
    tab = jnp.pad(tab, ((0, 0), (0, _tab_len(td) - tab.shape[1]))).reshape(nt, 1, _tab_len(td))
    pad_lo = jnp.concatenate([starts + seg, ends[-1:]]).astype(I32)
    pad_hi = jnp.concatenate([ends, jnp.full((1,), n_rows, I32)]).astype(I32)
    lpos_k = lpos.reshape(nt, 2, td).transpose(1, 0, 2).reshape(2, T)
    lpos_rows = jnp.zeros((8, T), I32).at[:2].set(lpos_k)
    return tab, lpos_rows, lpos_k.T, tile_e.astype(I32), n_used, pad_lo, pad_hi, n_rows


def _chunk_loops(tab_ref, fn):
    total = tab_ref[0, 0, 0]

    def one(c, priority):
        fn(pl.multiple_of(CHUNK * c, CHUNK), pl.multiple_of(tab_ref[0, 0, 1 + c], CHUNK), priority)

    def body(j, carry):
        one(2 * j, 0)
        one(2 * j + 1, 1)
        return carry
    lax.fori_loop(0, total // 2, body, 0)

    @pl.when(total % 2 == 1)
    def _():
        one(total - 1, 0)
    return total


def _dispatch_kernel(lo_ref, hi_ref, tab_ref, lp_ref, x_ref, xs_hbm, stage, zbuf, nwait, sem, zsem, *, ns):
    i = pl.program_id(0)
    slot = i % 2

    def zero_copy(dst):
        return pltpu.make_async_copy(zbuf, xs_hbm.at[pl.ds(pl.multiple_of(dst, CHUNK), CHUNK)], zsem)

    @pl.when(i == 0)
    def _():
        nwait[0] = 0
        nwait[1] = 0
        zbuf[...] = jnp.zeros(zbuf.shape, BF16)
        for e in range(N_EXPERTS + 1):
            nz = (hi_ref[e] - lo_ref[e]) // CHUNK
            lax.fori_loop(0, nz, lambda j, c, e=e: (zero_copy(lo_ref[e] + CHUNK * j).start(), c)[1], 0)
        for e in range(N_EXPERTS + 1):
            nz = (hi_ref[e] - lo_ref[e]) // CHUNK
            lax.fori_loop(0, nz, lambda j, c: (zero_copy(0).wait(), c)[1], 0)

    def piece(local_row, sorted_row, s):
        return pltpu.make_async_copy(stage.at[s, pl.ds(local_row, CHUNK)],
                                     xs_hbm.at[pl.ds(sorted_row, CHUNK)], sem.at[s])

    def drain(s):
        lax.fori_loop(0, nwait[s], lambda j, c: (piece(0, 0, s).wait(), c)[1], 0)

    drain(slot)
    lp = lp_ref[...]
    rows = lax.broadcasted_iota(I32, (ns, lp.shape[1]), 0)
    perm = jnp.where((rows == lp[0:1]) | (rows == lp[1:2]), 1.0, 0.0).astype(BF16)
    stage[slot] = jnp.dot(perm, x_ref[...], preferred_element_type=F32).astype(BF16)
    nwait[slot] = _chunk_loops(tab_ref, lambda lr, sr, pr: piece(lr, sr, slot).start(priority=pr))

    @pl.when(i == pl.num_programs(0) - 1)
    def _():
        drain(0)
        drain(1)


def _moe_dispatch(x1, tab, lpos_rows, pad_lo, pad_hi, n_rows, td):
    T = x1.shape[0]
    ns = _stage_rows(td)
    grid_spec = pltpu.PrefetchScalarGridSpec(
        num_scalar_prefetch=2,
        grid=(T // td,),
        in_specs=[
            pl.BlockSpec((1, 1, _tab_len(td)), lambda i, lo, hi: (i, 0, 0), memory_space=pltpu.SMEM),
            pl.BlockSpec((8, td), lambda i, lo, hi: (0, i)),
            pl.BlockSpec((td, D_MODEL), lambda i, lo, hi: (i, 0)),
        ],
        out_specs=pl.BlockSpec(memory_space=pl.ANY),
        scratch_shapes=[pltpu.VMEM((2, ns, D_MODEL), BF16), pltpu.VMEM((CHUNK, D_MODEL), BF16),
                        pltpu.SMEM((2,), I32), pltpu.SemaphoreType.DMA((2,)), pltpu.SemaphoreType.DMA(())],
    )
    return pl.pallas_call(
        functools.partial(_dispatch_kernel, ns=ns),
        grid_spec=grid_spec,
        out_shape=jax.ShapeDtypeStruct((n_rows, D_MODEL), BF16),
        compiler_params=_cparams(("arbitrary",)),
        name="moe_dispatch",
    )(pad_lo, pad_hi, tab, lpos_rows, x1)


def _moe_kernel(te_ref, nu_ref, x_ref, wg_ref, wu_ref, wd_ref, y_ref, wgb, wub, wdb):
    i = pl.program_id(0)

    @pl.when((i == 0) | (te_ref[i] != te_ref[jnp.maximum(i - 1, 0)]))
    def _():
        wgb[...] = wg_ref[0, 0].astype(BF16)
        wub[...] = wu_ref[0, 0].astype(BF16)
        wdb[...] = wd_ref[0, 0].astype(BF16)

    @pl.when(i < nu_ref[0])
    def _():
        xg = x_ref[...]
        hg = jnp.dot(xg, wgb[...], preferred_element_type=F32)
        hu = jnp.dot(xg, wub[...], preferred_element_type=F32)
        hdn = (hg / (1.0 + jnp.exp(-hg)) * hu).astype(BF16)
        y_ref[...] = jnp.dot(hdn, wdb[...], preferred_element_type=F32).astype(BF16)

    @pl.when(i >= nu_ref[0])
    def _():
        y_ref[...] = jnp.zeros(y_ref.shape, BF16)


def _moe_experts(xs, tile_e, n_used, wg, wu, wd, layer, tm):
    n_rows = xs.shape[0]
    grid_spec = pltpu.PrefetchScalarGridSpec(
        num_scalar_prefetch=2,
        grid=(n_rows // tm,),
        in_specs=[
            pl.BlockSpec((tm, D_MODEL), lambda i, te, nu: (i, 0)),
            pl.BlockSpec((1, 1, D_MODEL, D_FF), lambda i, te, nu: (layer, te[i], 0, 0)),
            pl.BlockSpec((1, 1, D_MODEL, D_FF), lambda i, te, nu: (layer, te[i], 0, 0)),
            pl.BlockSpec((1, 1, D_FF, D_MODEL), lambda i, te, nu: (layer, te[i], 0, 0)),
        ],
        out_specs=pl.BlockSpec((tm, D_MODEL), lambda i, te, nu: (i, 0)),
        scratch_shapes=[pltpu.VMEM((D_MODEL, D_FF), BF16), pltpu.VMEM((D_MODEL, D_FF), BF16),
                        pltpu.VMEM((D_FF, D_MODEL), BF16)],
    )
    return pl.pallas_call(
        _moe_kernel,
        grid_spec=grid_spec,
        out_shape=jax.ShapeDtypeStruct((n_rows, D_MODEL), BF16),
        compiler_params=_cparams(("arbitrary",)),
        name="moe_experts",
    )(tile_e, n_used, xs, wg, wu, wd)


def _combine_kernel(tab_ref, tabn_ref, lp_ref, y_hbm, x_ref, wt_ref, g_ref, b_ref, *rest, ns, alpha, bounds):
    o_refs, (ystage, sem) = rest[:len(bounds)], rest[len(bounds):]
    i = pl.program_id(0)
    n = pl.num_programs(0)
    slot = i % 2

    def piece(local_row, sorted_row, s):
        return pltpu.make_async_copy(y_hbm.at[pl.ds(sorted_row, CHUNK)],
                                     ystage.at[s, pl.ds(local_row, CHUNK)], sem.at[s])

    @pl.when(i == 0)
    def _():
        ystage[...] = jnp.zeros(ystage.shape, BF16)
        _chunk_loops(tab_ref, lambda lr, sr, pr: piece(lr, sr, 0).start(priority=pr))

    @pl.when(i + 1 < n)
    def _():
        _chunk_loops(tabn_ref, lambda lr, sr, pr: piece(lr, sr, 1 - slot).start(priority=pr))

    lax.fori_loop(0, tab_ref[0, 0, 0], lambda j, c: (piece(0, 0, slot).wait(), c)[1], 0)

    ys = ystage[slot]
    lp = lp_ref[...]
    cols = lax.broadcasted_iota(I32, (lp.shape[0], ns), 1)
    w = wt_ref[...]
    tc = lp.shape[0]
    sel = jnp.concatenate([jnp.where(cols == lp[:, k:k + 1], 1.0, 0.0).astype(BF16) for k in range(2)], axis=0)
    picked = jnp.dot(sel, ys, preferred_element_type=F32)
    f = w[:, 0:1] * picked[:tc] + w[:, 1:2] * picked[tc:]
    o = _layer_norm(alpha * x_ref[...] + f, g_ref[...], b_ref[...])
    if len(bounds) == 1:
        o_refs[0][...] = o
    else:
        lo = 0
        for o_ref, hi in zip(o_refs, bounds):
            @pl.when((i >= lo) & (i < hi))
            def _(o_ref=o_ref):
                o_ref[...] = o
            lo = hi


def _moe_combine(y, tab, lpos_tk, wts, x1, ln_g, ln_b, alpha, tc, out_rows):
    T = x1.shape[0]
    n_tiles = T // tc
    ns = _stage_rows(tc)
    out_specs, out_shapes, bounds, lo = [], [], [], 0
    for rows in out_rows:
        n = rows // tc
        out_specs.append(pl.BlockSpec((tc, D_MODEL), lambda i, lo=lo, n=n: (jnp.clip(i - lo, 0, n - 1), 0)))
        out_shapes.append(jax.ShapeDtypeStruct((rows, D_MODEL), F32))
        lo += n
        bounds.append(lo)
    return pl.pallas_call(
        functools.partial(_combine_kernel, ns=ns, alpha=alpha, bounds=tuple(bounds)),
        grid=(n_tiles,),
        in_specs=[
            pl.BlockSpec((1, 1, _tab_len(tc)), lambda i: (i, 0, 0), memory_space=pltpu.SMEM),
            pl.BlockSpec((1, 1, _tab_len(tc)), lambda i: (jnp.minimum(i + 1, n_tiles - 1), 0, 0),
                         memory_space=pltpu.SMEM),
            pl.BlockSpec((tc, 2), lambda i: (i, 0)),
            pl.BlockSpec(memory_space=pl.ANY),
            pl.BlockSpec((tc, D_MODEL), lambda i: (i, 0)),
            pl.BlockSpec((tc, 2), lambda i: (i, 0)),
            pl.BlockSpec((1, D_MODEL), lambda i: (0, 0)),
            pl.BlockSpec((1, D_MODEL), lambda i: (0, 0)),
        ],
        out_specs=out_specs,
        out_shape=out_shapes,
        scratch_shapes=[pltpu.VMEM((2, ns, D_MODEL), BF16), pltpu.SemaphoreType.DMA((2,))],
        compiler_params=_cparams(("arbitrary",)),
        name="moe_combine",
    )(tab, tab, lpos_tk, y, x1, wts, ln_g[None, :], ln_b[None, :])


def _moe_layer(x1, xb, idx, wt, wg, wu, wd, layer, ln_g, ln_b, alpha, tm, tc, out_rows):
    tab, lpos_rows, lpos_tk, tile_e, n_used, pad_lo, pad_hi, n_rows = _moe_plan(idx[:2], tc, tm)
    xs = _moe_dispatch(xb, tab, lpos_rows, pad_lo, pad_hi, n_rows, tc)
    y = _moe_experts(xs, tile_e, n_used, wg, wu, wd, layer, tm)
    return _moe_combine(y, tab, lpos_tk, wt[:2].T, x1, ln_g, ln_b, alpha, tc, out_rows)


def _inproj_odd_kernel(x_ref, w_ref, vone_ref, bg_ref, u_ref, q_ref, k_ref, v_ref, nrm_ref):
    x = x_ref[...].astype(BF16)
    proj = jnp.dot(x, w_ref[...], preferred_element_type=F32)
    bg_ref[...] = proj[:, :512].astype(BF16)
    u_ref[...] = (proj[:, 512:1024] * proj[:, 1024:1536]).astype(BF16)
    qb = (proj[:, 1536:2048] * (HEAD_DIM ** -0.5 * LOG2E)).astype(BF16)
    kb = proj[:, 2048:2304].astype(BF16)
    q_ref[...] = qb
    k_ref[...] = kb
    v_ref[...] = (proj[:, 2304:2560] + vone_ref[...]).astype(BF16)
    qf = qb.astype(F32)
    kf = kb.astype(F32)
    qq = jnp.max(jnp.sum(qf * qf, axis=1, keepdims=True), axis=0, keepdims=True)
    kk = jnp.max(jnp.sum(kf * kf, axis=1, keepdims=True), axis=0, keepdims=True)
    rid = lax.broadcasted_iota(I32, (8, 128), 0)
    nrm_ref[0] = jnp.where(rid == 0, qq, jnp.where(rid == 1, kk, 0.0))


def _inproj_odd(x, w_in, ts):
    T = x.shape[0]
    c3 = 3 * D_CONV
    wk = w_in[:, c3 + 512:c3 + 640].reshape(D_MODEL, N_KV_D, HEAD_DIM)
    wv = w_in[:, c3 + 640:c3 + 768].reshape(D_MODEL, N_KV_D, HEAD_DIM)
    wkp = jnp.concatenate([wk, jnp.zeros_like(wk)], axis=2).reshape(D_MODEL, 256)
    wvp = jnp.concatenate([wv, jnp.zeros_like(wv)], axis=2).reshape(D_MODEL, 256)
    w = jnp.concatenate([w_in[:, :c3 + 512], wkp, wvp], axis=1).astype(BF16)
    vone_np = np.zeros((1, 256), np.float32)
    for h in range(N_KV_D):
        vone_np[0, 128 * h + 64:128 * (h + 1)] = 1.0
    n_w = w.shape[1]
    widths = [512, 512, 512, 256, 256]
    bg, u, q, kw, vw, nrm = pl.pallas_call(
        _inproj_odd_kernel,
        grid=(T // ts,),
        in_specs=[
            pl.BlockSpec((ts, D_MODEL), lambda i: (i, 0)),
            pl.BlockSpec((D_MODEL, n_w), lambda i: (0, 0)),
            pl.BlockSpec((1, 256), lambda i: (0, 0)),
        ],
        out_specs=[pl.BlockSpec((ts, n), lambda i: (i, 0)) for n in widths]
        + [pl.BlockSpec((1, 8, 128), lambda i: (i, 0, 0))],
        out_shape=[jax.ShapeDtypeStruct((T, n), BF16) for n in widths]
        + [jax.ShapeDtypeStruct((T // ts, 8, 128), F32)],
        compiler_params=_cparams(("parallel",)),
        name="inproj_odd",
    )(x, w, jnp.asarray(vone_np))
    score_bound = jnp.sqrt(jnp.max(nrm[:, 0, 0]) * jnp.max(nrm[:, 1, 0]))
    return bg, u, q, kw, vw, score_bound


def _wattn_kernel(bounded_ref, q_ref, kp_ref, kc_ref, kn_ref, vp_ref, vc_ref, vn_ref, bias_ref, sink_ref,
                  place_ref, o_ref, *, tq, seq_tiles):
    i = pl.program_id(0)
    first = i < 0
    last = i < 0
    for lo, hi, per in seq_tiles:
        inside = (i >= lo) & (i < hi)
        first = first | (inside & ((i - lo) % per == 0))
        last = last | (inside & ((i - lo) % per == per - 1))
    kfull = jnp.concatenate([kp_ref[...], kc_ref[...], kn_ref[...]], axis=0)[:, :HEAD_DIM]
    vfull = jnp.concatenate([vp_ref[...], vc_ref[...], vn_ref[...]], axis=0)
    q4 = q_ref[...]
    bias = bias_ref[0]
    sink = sink_ref[0]
    nb = tq // Q_BLOCK
    col = lax.broadcasted_iota(I32, (GRP_D * Q_BLOCK, 3 * Q_BLOCK), 1)

    def blocks(with_max):
        for n in range(nb):
            qs = jnp.concatenate([q4[Q_BLOCK * n:Q_BLOCK * (n + 1), HEAD_DIM * g:HEAD_DIM * (g + 1)]
                                  for g in range(GRP_D)], axis=0)
            keys = kfull[Q_BLOCK * n:Q_BLOCK * (n + 3)]
            vals = vfull[Q_BLOCK * n:Q_BLOCK * (n + 3)]
            s = lax.dot_general(qs, keys, (((1,), (1,)), ((), ())), preferred_element_type=F32) + bias
            if n == 0:
                s = jnp.where(first & (col < Q_BLOCK), NEG_INF, s)
            if n == nb - 1:
                s = jnp.where(last & (col >= 2 * Q_BLOCK), NEG_INF, s)
            if with_max:
                m = jnp.maximum(jnp.max(s, axis=1, keepdims=True), sink)
                s = s - m
                snk = sink - m
            else:
                snk = sink
            acc = jnp.dot(jnp.exp2(s).astype(BF16), vals, preferred_element_type=F32)
            den = acc[:, HEAD_DIM:HEAD_DIM + 1] + jnp.exp2(snk)
            o = (acc / den).astype(BF16)
            ocat = jnp.concatenate([o[Q_BLOCK * g:Q_BLOCK * (g + 1)] for g in range(GRP_D)], axis=1)
            out = jnp.dot(ocat, place_ref[...], preferred_element_type=F32)
            o_ref[Q_BLOCK * n:Q_BLOCK * (n + 1), :] = out.astype(BF16)

    @pl.when(bounded_ref[0] == 1)
    def _():
        blocks(False)

    @pl.when(bounded_ref[0] != 1)
    def _():
        blocks(True)


WINDOW_BOUND_MAX = 90.0


def _window_attention(q, kw, vw, sink_logits, score_bound, classes, tq):
    T = q.shape[0]
    sink_bound = jnp.max(jnp.abs(sink_logits.astype(F32))) * LOG2E
    bounded = ((1.02 * score_bound <= WINDOW_BOUND_MAX) & (sink_bound <= WINDOW_BOUND_MAX)).astype(I32)[None]
    n_tiles = T // tq
    hb = tq // Q_BLOCK
    n_hblk = T // Q_BLOCK
    r = jnp.arange(Q_BLOCK, dtype=I32)[:, None]
    j = jnp.arange(3 * Q_BLOCK, dtype=I32)[None, :]
    rel = jnp.abs(j - Q_BLOCK - r).astype(F32)
    slopes = jnp.asarray(np.array([2.0 ** (-8.0 * (h + 1) / N_HEADS_D) for h in range(N_HEADS_D)], np.float32))
    bias = jnp.where(rel[None] <= WINDOW, -slopes[:, None, None] * rel[None] * LOG2E, NEG_INF)
    bias = bias.reshape(N_KV_D, GRP_D * Q_BLOCK, 3 * Q_BLOCK)
    sink = jnp.repeat(sink_logits.astype(F32) * LOG2E, Q_BLOCK).reshape(N_KV_D, GRP_D * Q_BLOCK, 1)
    place = _lane_place(128, 256, [0, 1, 2, 3]).reshape(GRP_D * 128, 256)
    seq_tiles = []
    t0 = 0
    for nseq, S, _ in classes:
        cnt = nseq * S // tq
        seq_tiles.append((t0, t0 + cnt, S // tq))
        t0 += cnt
    prev_map = lambda i, h, fl: (jnp.maximum(i * hb - 1, 0), h)
    next_map = lambda i, h, fl: (jnp.minimum((i + 1) * hb, n_hblk - 1), h)
    grid_spec = pltpu.PrefetchScalarGridSpec(
        num_scalar_prefetch=1,
        grid=(n_tiles, N_KV_D),
        in_specs=[
            pl.BlockSpec((tq, 256), lambda i, h, fl: (i, h)),
            pl.BlockSpec((Q_BLOCK, 128), prev_map),
            pl.BlockSpec((tq, 128), lambda i, h, fl: (i, h)),
            pl.BlockSpec((Q_BLOCK, 128), next_map),
            pl.BlockSpec((Q_BLOCK, 128), prev_map),
            pl.BlockSpec((tq, 128), lambda i, h, fl: (i, h)),
            pl.BlockSpec((Q_BLOCK, 128), next_map),
            pl.BlockSpec((1, GRP_D * Q_BLOCK, 3 * Q_BLOCK), lambda i, h, fl: (h, 0, 0)),
            pl.BlockSpec((1, GRP_D * Q_BLOCK, 1), lambda i, h, fl: (h, 0, 0)),
            pl.BlockSpec((GRP_D * 128, 256), lambda i, h, fl: (0, 0)),
        ],
        out_specs=pl.BlockSpec((tq, 256), lambda i, h, fl: (i, h)),
    )
    return pl.pallas_call(
        functools.partial(_wattn_kernel, tq=tq, seq_tiles=tuple(seq_tiles)),
        grid_spec=grid_spec,
        out_shape=jax.ShapeDtypeStruct((T, 512), BF16),
        compiler_params=_cparams(("parallel", "parallel")),
        name="window_attention",
    )(bounded, q, kw, kw, kw, vw, vw, vw, bias, sink, place)


HALO = 16


def _outproj_odd_kernel(bg_ref, u_ref, up_ref, un_ref, a_ref, x_ref, cw_ref, cb_ref, wc_ref, wa_ref,
                        g_ref, b_ref, rw_ref, rb_ref, x1_ref, idx_ref, wt_ref, xb_ref,
                        *, alpha, ts, seq_tiles):
    i = pl.program_id(0)
    first = i < 0
    last = i < 0
    for lo, hi, per in seq_tiles:
        inside = (i >= lo) & (i < hi)
        first = first | (inside & ((i - lo) % per == 0))
        last = last | (inside & ((i - lo) % per == per - 1))
    u = u_ref[...].astype(F32)
    prev_row = jnp.where(first, 0.0, up_ref[HALO - 1:HALO, :].astype(F32))
    next_row = jnp.where(last, 0.0, un_ref[0:1, :].astype(F32))
    rid = lax.broadcasted_iota(I32, u.shape, 0)
    ud = jnp.where(rid == 0, prev_row, pltpu.roll(u, 1, 0))
    uu = jnp.where(rid == ts - 1, next_row, pltpu.roll(u, ts - 1, 0))
    cw = cw_ref[...]
    y = ud * cw[0:1] + u * cw[1:2] + uu * cw[2:3] + cb_ref[...]
    c = (bg_ref[...].astype(F32) * y).astype(BF16)
    nr = ts // ROW_SPLITS
    for h in range(ROW_SPLITS):
        r0 = h * nr
        m = (jnp.dot(c[r0:r0 + nr], wc_ref[...], preferred_element_type=F32)
             + jnp.dot(a_ref[r0:r0 + nr, :], wa_ref[...], preferred_element_type=F32))
        _norm_route_store(r0, nr, m, x_ref[r0:r0 + nr, :], g_ref, b_ref, rw_ref, rb_ref, x1_ref, idx_ref,
                          wt_ref, xb_ref, alpha)


def _outproj_odd(bg, u, a, x, conv_w, conv_b, w_out, ln_g, ln_b, router, alpha, classes, ts):
    T = x.shape[0]
    hb = ts // HALO
    n_h = T // HALO
    seq_tiles = []
    t0 = 0
    for nseq, S, _ in classes:
        cnt = nseq * S // ts
        seq_tiles.append((t0, t0 + cnt, S // ts))
        t0 += cnt
    out_specs, out_shapes = _mix_out_specs(ts, T)
    return pl.pallas_call(
        functools.partial(_outproj_odd_kernel, alpha=alpha, ts=ts, seq_tiles=tuple(seq_tiles)),
        grid=(T // ts,),
        in_specs=[
            pl.BlockSpec((ts, D_CONV), lambda i: (i, 0)),
            pl.BlockSpec((ts, D_CONV), lambda i: (i, 0)),
            pl.BlockSpec((HALO, D_CONV), lambda i: (jnp.maximum(i * hb - 1, 0), 0)),
            pl.BlockSpec((HALO, D_CONV), lambda i: (jnp.minimum((i + 1) * hb, n_h - 1), 0)),
            pl.BlockSpec((ts, 512), lambda i: (i, 0)),
            pl.BlockSpec((ts, D_MODEL), lambda i: (i, 0)),
            pl.BlockSpec((3, D_CONV), lambda i: (0, 0)),
            pl.BlockSpec((1, D_CONV), lambda i: (0, 0)),
            pl.BlockSpec((D_CONV, D_MODEL), lambda i: (0, 0)),
            pl.BlockSpec((512, D_MODEL), lambda i: (0, 0)),
            pl.BlockSpec((1, D_MODEL), lambda i: (0, 0)),
            pl.BlockSpec((1, D_MODEL), lambda i: (0, 0)),
        ] + _ROUTER_SPECS,
        out_specs=out_specs,
        out_shape=out_shapes,
        compiler_params=_cparams(("parallel",)),
        name="outproj_odd",
    )(bg, u, u, u, a, x, conv_w.astype(F32), conv_b.astype(F32)[None, :],
      w_out[:D_CONV].astype(BF16), w_out[D_CONV:].astype(BF16), ln_g[None, :], ln_b[None, :], *router)


def _tile(n, cap):
    t = cap
    while n % t:
        t //= 2
    return t


def kernel(x_prompt, x_sample, w_in_even, fourier_norm_g, q_norm_g, k_norm_g, w_out_even, w_in_odd, conv_w,
           conv_b, sink_logits, w_out_odd, ln_mix_g, ln_mix_b, ln_ffn_g, ln_ffn_b, router_w, router_b,
           w_gate, w_up, w_down):
    depth = ln_mix_g.shape[0]
    alpha = float((2 * depth) ** 0.25)
    bp, sp, _ = x_prompt.shape
    bs, ss, _ = x_sample.shape
    classes = ((bp, sp, 0), (bs, ss, bp * sp))
    T = bp * sp + bs * ss
    min_s = min(sp, ss)
    ts = _tile(min_s, 512)
    tq_w = _tile(min_s, 512)
    tm = 1024
    tc = _tile(min_s, 256)
    x_parts = [x_prompt.reshape(bp * sp, D_MODEL), x_sample.reshape(bs * ss, D_MODEL)]
    router = _router_operands(router_w, router_b)
    for l in range(depth):
        i = l // 2
        if l % 2 == 0:
            uf, qk, v_aug = _inproj_even(x_parts, w_in_even[i], fourier_norm_g[i], q_norm_g[i], k_norm_g[i],
                                         classes, ts)
            bounded = _score_bounded(q_norm_g[i], k_norm_g[i])
            f_parts, a_parts = [], []
            for nseq, S, tok0 in classes:
                f_parts.append(_fourier_mix(uf, nseq, S, tok0))
                a_parts.append(_global_attention(qk, v_aug, bounded, nseq, S, tok0, _tile(S, 1024),
                                                 _tile(S, 2048)))
            x1, idx, wt, xb = _outproj_even(f_parts, a_parts, x_parts, w_out_even[i], ln_mix_g[l], ln_mix_b[l],
                                            router, alpha, ts)
        else:
            x = x_parts[0]
            bg, u, q, kw, vw, score_bound = _inproj_odd(x, w_in_odd[i], ts)
            a = _window_attention(q, kw, vw, sink_logits[i], score_bound, classes, tq_w)
            x1, idx, wt, xb = _outproj_odd(bg, u, a, x, conv_w[i], conv_b[i], w_out_odd[i], ln_mix_g[l],
                                           ln_mix_b[l], router, alpha, classes, ts)
        out_rows = [bp * sp, bs * ss] if l == depth - 1 else [T]
        x_parts = _moe_layer(x1, xb, idx, wt, w_gate, w_up, w_down, l, ln_ffn_g[l], ln_ffn_b[l], alpha,
                             tm, tc, out_rows)
    if len(x_parts) == 1:
        x_parts = [x_parts[0][:bp * sp], x_parts[0][bp * sp:]]
    return (x_parts[0].reshape(bp, sp, D_MODEL), x_parts[1].reshape(bs, ss, D_MODEL))
```

```python
import functools
import math

import numpy as np
import jax
import jax.numpy as jnp
from jax import lax
from jax.experimental import pallas as pl
from jax.experimental.pallas import tpu as pltpu

F32 = jnp.float32
BF16 = jnp.bfloat16
I32 = jnp.int32

D_MODEL = 1024
HEAD_DIM = 64
GRID_W = 64
Q_BLOCK = 128
WINDOW = 128
ROPE_THETA = 10000.0
N_FOURIER_GROUPS = 4
D_FOURIER = 256
N_HEADS_B = 12
N_KV_B = 4
GRP_B = 3
D_CONV = 512
N_HEADS_D = 8
N_KV_D = 2
GRP_D = 4
N_EXPERTS = 16
N_GROUPS = 4
EXPERTS_PER_GROUP = 4
D_FF = 512
LN_EPS = 1e-5
RMS_EPS = 1e-6
NEG_INF = -1e30
LOG2E = 1.4426950408889634
FFT_S2 = 128
V7X_VMEM_LIMIT = 48 * 1024 * 1024


def _cparams(sem):
    return pltpu.CompilerParams(dimension_semantics=sem, vmem_limit_bytes=V7X_VMEM_LIMIT)


def _layer_norm(y, g, b):
    mu = jnp.mean(y, axis=-1, keepdims=True)
    yc = y - mu
    var = jnp.mean(yc * yc, axis=-1, keepdims=True)
    return yc * lax.rsqrt(var + LN_EPS) * g + b


N_NORM_COLS = D_FOURIER + 1024


def _parts_specs(parts, ts):
    specs, bounds, lo = [], [], 0
    for p in parts:
        n = p.shape[0] // ts
        specs.append(pl.BlockSpec((ts, p.shape[1]), lambda i, lo=lo, n=n: (jnp.clip(i - lo, 0, n - 1), 0)))
        lo += n
        bounds.append(lo)
    return specs, tuple(bounds)


def _pick_rows(refs, bounds, r0, nr):
    i = pl.program_id(0)
    out = refs[-1][r0:r0 + nr, :]
    for ref, hi in reversed(list(zip(refs[:-1], bounds[:-1]))):
        out = jnp.where(i < hi, ref[r0:r0 + nr, :], out)
    return out


def _inproj_even_kernel(*refs, nx, bounds):
    x_refs = refs[:nx]
    (w_ref, gain_ref, seg_ref, segt_ref, cos_ref, sin_ref, vone_ref, uf_ref, qk_ref, v_ref) = refs[nx:]
    nr = uf_ref.shape[0] // ROW_SPLITS
    for part in range(ROW_SPLITS):
        r0 = part * nr
        x = _pick_rows(x_refs, bounds, r0, nr).astype(BF16)
        proj = jnp.dot(x, w_ref[...], preferred_element_type=F32)
        nrm = proj[:, :N_NORM_COLS]
        sq = (nrm * nrm).astype(BF16)
        ssum = jnp.dot(sq, seg_ref[...], preferred_element_type=F32)
        r = lax.rsqrt(ssum * (1.0 / HEAD_DIM) + RMS_EPS)
        rh = r.astype(BF16)
        rl = (r - rh.astype(F32)).astype(BF16)
        rex = jnp.dot(jnp.concatenate([rh, rl], axis=1), segt_ref[...], preferred_element_type=F32)
        y = nrm * rex * gain_ref[...]
        uf_ref[r0:r0 + nr, :] = y[:, :D_FOURIER].astype(BF16)
        yq = y[:, D_FOURIER:]
        c = jnp.concatenate([cos_ref[r0:r0 + nr, :]] * 8, axis=1)
        s = jnp.concatenate([sin_ref[r0:r0 + nr, :]] * 8, axis=1)
        lane = lax.broadcasted_iota(I32, yq.shape, 1)
        first = (lane & 31) < 16
        sw = jnp.where(first, pltpu.roll(yq, 1024 - 16, 1), pltpu.roll(yq, 16, 1))
        qk_ref[r0:r0 + nr, :] = (yq * c + sw * s).astype(BF16)
        vv = proj[:, N_NORM_COLS:] + vone_ref[...]
        for h in range(N_KV_B):
            v_ref[h, r0:r0 + nr, :] = vv[:, 128 * h:128 * (h + 1)].astype(BF16)


def _rope_tables(n_pos):
    t = jnp.arange(n_pos)
    row = (t // GRID_W).astype(F32)
    col = (t % GRID_W).astype(F32)
    n_freq = HEAD_DIM // 4
    inv_freq = ROPE_THETA ** (-jnp.arange(n_freq, dtype=F32) / n_freq)
    ar = row[:, None] * inv_freq
    ac = col[:, None] * inv_freq
    ang = jnp.concatenate([ar, ar, ac, ac], axis=1)
    sign = jnp.asarray(np.tile(np.repeat(np.array([-1.0, 1.0], np.float32), 16), 2))
    cos = jnp.cos(ang)
    sin = jnp.sin(ang) * sign
    return jnp.concatenate([cos, cos], axis=1), jnp.concatenate([sin, sin], axis=1)


def _pos_block_map(classes, ts):
    bounds = []
    tile0 = 0
    for nseq, S, _ in classes:
        n_tiles = nseq * S // ts
        bounds.append((tile0, tile0 + n_tiles, S // ts))
        tile0 += n_tiles

    def fn(i):
        out = (i - bounds[-1][0]) % bounds[-1][2]
        for lo, hi, per in reversed(bounds[:-1]):
            out = jnp.where(i < hi, (i - lo) % per, out)
        return out

    return fn


def _inproj_even(x_parts, w_in, f_g, q_g, k_g, classes, ts):
    T = sum(p.shape[0] for p in x_parts)
    x_specs, bounds = _parts_specs(x_parts, ts)
    wf = w_in[:, :D_FOURIER]
    wq = w_in[:, D_FOURIER:D_FOURIER + 768].reshape(D_MODEL, N_KV_B, GRP_B * HEAD_DIM)
    wk = w_in[:, D_FOURIER + 768:D_FOURIER + 1024].reshape(D_MODEL, N_KV_B, HEAD_DIM)
    wv = w_in[:, D_FOURIER + 1024:].reshape(D_MODEL, N_KV_B, HEAD_DIM)
    wqk = jnp.concatenate([wk, wq], axis=2).reshape(D_MODEL, 1024)
    wvp = jnp.concatenate([wv, jnp.zeros_like(wv)], axis=2).reshape(D_MODEL, 512)
    w = jnp.concatenate([wf, wqk, wvp], axis=1).astype(BF16)
    qscale = HEAD_DIM ** -0.5 * LOG2E
    gqk = jnp.tile(jnp.concatenate([k_g, q_g * qscale, q_g * qscale, q_g * qscale]), N_KV_B)
    gain = jnp.concatenate([f_g, gqk])[None, :].astype(F32)
    seg_np = np.zeros((N_NORM_COLS, 128), np.float32)
    seg_np[np.arange(N_NORM_COLS), np.arange(N_NORM_COLS) // HEAD_DIM] = 1.0
    seg = jnp.asarray(seg_np, BF16)
    segt = jnp.asarray(np.concatenate([seg_np.T, seg_np.T], axis=0), BF16)
    max_s = max(S for _, S, _ in classes)
    cos, sin = _rope_tables(max_s)
    vone_np = np.zeros((1, 512), np.float32)
    for h in range(N_KV_B):
        vone_np[0, 128 * h + 64:128 * (h + 1)] = 1.0
    vone = jnp.asarray(vone_np)
    posmap = _pos_block_map(classes, ts)
    n_w = w.shape[1]
    return pl.pallas_call(
        functools.partial(_inproj_even_kernel, nx=len(x_parts), bounds=bounds),
        grid=(T // ts,),
        in_specs=x_specs + [
            pl.BlockSpec((D_MODEL, n_w), lambda i: (0, 0)),
            pl.BlockSpec((1, N_NORM_COLS), lambda i: (0, 0)),
            pl.BlockSpec((N_NORM_COLS, 128), lambda i: (0, 0)),
            pl.BlockSpec((256, N_NORM_COLS), lambda i: (0, 0)),
            pl.BlockSpec((ts, 128), lambda i: (posmap(i), 0)),
            pl.BlockSpec((ts, 128), lambda i: (posmap(i), 0)),
            pl.BlockSpec((1, 512), lambda i: (0, 0)),
        ],
        out_specs=[
            pl.BlockSpec((ts, D_FOURIER), lambda i: (i, 0)),
            pl.BlockSpec((ts, 1024), lambda i: (i, 0)),
            pl.BlockSpec((N_KV_B, ts, 128), lambda i: (0, i, 0)),
        ],
        out_shape=[
            jax.ShapeDtypeStruct((T, D_FOURIER), BF16),
            jax.ShapeDtypeStruct((T, 1024), BF16),
            jax.ShapeDtypeStruct((N_KV_B, T, 128), BF16),
        ],
        compiler_params=_cparams(("parallel",)),
        name="inproj_even",
    )(*x_parts, w, gain, seg, segt, cos, sin, vone)


FFT_TB = 16


def _fft1_kernel(x_ref, kw_ref, twr_ref, twi_ref, zr_ref, zi_ref, *, s1, bs):
    n = s1 * FFT_TB
    twr = twr_ref[0]
    twi = twi_ref[0]
    for q in range(bs):
        x2 = x_ref[q * s1:(q + 1) * s1].reshape(n, D_FOURIER)
        z = jnp.dot(kw_ref[...], x2, preferred_element_type=F32)
        zr = z[:n]
        zi = z[n:]
        zr_ref[q * s1:(q + 1) * s1] = (zr * twr - zi * twi).astype(BF16).reshape(s1, FFT_TB, D_FOURIER)
        zi_ref[q * s1:(q + 1) * s1] = (zr * twi + zi * twr).astype(BF16).reshape(s1, FFT_TB, D_FOURIER)


def _fft2_kernel(zr_ref, zi_ref, w2a_ref, w2b_ref, mix_ref, o_ref, *, cb):
    s2 = FFT_S2
    for c in range(cb):
        pp = (jnp.dot(w2a_ref[...], zr_ref[0, c], preferred_element_type=F32)
              + jnp.dot(w2b_ref[...], zi_ref[0, c], preferred_element_type=F32))
        f = (jnp.dot(pp[:s2].astype(BF16), mix_ref[:D_FOURIER], preferred_element_type=F32)
             + jnp.dot(pp[s2:].astype(BF16), mix_ref[D_FOURIER:], preferred_element_type=F32))
        o_ref[0, :, D_FOURIER * c:D_FOURIER * (c + 1)] = f.astype(BF16)


def _dft_mats(n):
    k = jnp.arange(n, dtype=I32)
    ang = (2.0 * math.pi / n) * ((k[:, None] * k[None, :]) % n).astype(F32)
    return jnp.cos(ang), jnp.sin(ang)


def _fourier_mix(uf, nseq, S, tok0):
    s2 = FFT_S2
    s1 = S // s2
    tb = FFT_TB
    n = s1 * tb
    x3 = uf.reshape(uf.shape[0] // s2, s2, D_FOURIER)
    blk0 = tok0 // S
    row = jnp.arange(2 * n, dtype=I32)[:, None]
    col = jnp.arange(n, dtype=I32)[None, :]
    ang1 = (2.0 * math.pi / s1) * ((((row % n) // tb) * (col // tb)) % s1).astype(F32)
    w1v = jnp.where(row < n, jnp.cos(ang1), -jnp.sin(ang1))
    kw = jnp.where((row % tb) == (col % tb), w1v, 0.0).astype(BF16)
    cc = jnp.arange(s1, dtype=I32)[:, None]
    bb = jnp.arange(s2, dtype=I32)[None, :]
    ang = (2.0 * math.pi / S) * ((cc * bb) % S).astype(F32)

    def table(t):
        t = t.reshape(s1, s2 // tb, tb).transpose(1, 0, 2).reshape(s2 // tb, n, 1)
        return jnp.broadcast_to(t, (s2 // tb, n, D_FOURIER))
    twr = table(jnp.cos(ang))
    twi = table(-jnp.sin(ang))
    bs = math.gcd(math.gcd(nseq, max(1, 64 // s1)), blk0) if blk0 else math.gcd(nseq, max(1, 64 // s1))
    zr, zi = pl.pallas_call(
        functools.partial(_fft1_kernel, s1=s1, bs=bs),
        grid=(s2 // tb, nseq // bs),
        in_specs=[
            pl.BlockSpec((bs * s1, tb, D_FOURIER), lambda j, b: (blk0 // bs + b, j, 0)),
            pl.BlockSpec((2 * n, n), lambda j, b: (0, 0)),
            pl.BlockSpec((1, n, D_FOURIER), lambda j, b: (j, 0, 0)),
            pl.BlockSpec((1, n, D_FOURIER), lambda j, b: (j, 0, 0)),
        ],
        out_specs=[
            pl.BlockSpec((bs * s1, tb, D_FOURIER), lambda j, b: (b, j, 0)),
            pl.BlockSpec((bs * s1, tb, D_FOURIER), lambda j, b: (b, j, 0)),
        ],
        out_shape=[jax.ShapeDtypeStruct((nseq * s1, s2, D_FOURIER), BF16)] * 2,
        compiler_params=_cparams(("parallel", "parallel")),
        name="fourier_stage1",
    )(x3, kw, twr, twi)
    zr = zr.reshape(nseq, s1, s2, D_FOURIER)
    zi = zi.reshape(nseq, s1, s2, D_FOURIER)
    c2, sn2 = _dft_mats(s2)
    w2a = jnp.concatenate([c2, -sn2], axis=0).astype(BF16)
    w2b = jnp.concatenate([sn2, c2], axis=0).astype(BF16)
    gc, gs = _dft_mats(HEAD_DIM)
    scale = 1.0 / math.sqrt(S * HEAD_DIM)
    eye = jnp.eye(N_FOURIER_GROUPS, dtype=F32)
    mix = (jnp.concatenate([jnp.kron(eye, gc), jnp.kron(eye, gs)], axis=0) * scale).astype(BF16)
    cb = min(8, s1)
    out = pl.pallas_call(
        functools.partial(_fft2_kernel, cb=cb),
        grid=(nseq, s1 // cb),
        in_specs=[
            pl.BlockSpec((1, cb, s2, D_FOURIER), lambda b, j: (b, j, 0, 0)),
            pl.BlockSpec((1, cb, s2, D_FOURIER), lambda b, j: (b, j, 0, 0)),
            pl.BlockSpec((2 * s2, s2), lambda b, j: (0, 0)),
            pl.BlockSpec((2 * s2, s2), lambda b, j: (0, 0)),
            pl.BlockSpec((2 * D_FOURIER, D_FOURIER), lambda b, j: (0, 0)),
        ],
        out_specs=pl.BlockSpec((1, s2, cb * D_FOURIER), lambda b, j: (b, 0, j)),
        out_shape=jax.ShapeDtypeStruct((nseq, s2, s1 * D_FOURIER), BF16),
        compiler_params=_cparams(("parallel", "parallel")),
        name="fourier_stage2",
    )(zr, zi, w2a, w2b, mix)
    return out.reshape(nseq * S, D_FOURIER)


SCORE_BOUND_MAX = 100.0


def _gattn_kernel(bounded_ref, q_ref, k_ref, v_ref, place_ref, o_ref, *, tq, tk, S):
    q3 = q_ref[...]
    qs = [q3[:, HEAD_DIM * (g + 1):HEAD_DIM * (g + 2)] for g in range(GRP_B)]

    def chunk(j):
        off = pl.multiple_of(j * tk, tk)
        return k_ref[pl.ds(off, tk), 0:HEAD_DIM], v_ref[0, pl.ds(off, tk), :]

    def scores(g, kc):
        return lax.dot_general(qs[g], kc, (((1,), (1,)), ((), ())), preferred_element_type=F32)

    def finish(accs):
        out = jnp.zeros((tq, 256), F32)
        for g in range(GRP_B):
            o = (accs[g] / accs[g][:, HEAD_DIM:HEAD_DIM + 1]).astype(BF16)
            out = out + jnp.dot(o, place_ref[g], preferred_element_type=F32)
        o_ref[...] = out.astype(BF16)

    @pl.when(bounded_ref[0] == 1)
    def _():
        def body(j, accs):
            kc, vc = chunk(j)
            return tuple(accs[g] + jnp.dot(jnp.exp2(scores(g, kc)).astype(BF16), vc,
                                           preferred_element_type=F32) for g in range(GRP_B))
        finish(lax.fori_loop(0, S // tk, body, tuple(jnp.zeros((tq, 128), F32) for _ in range(GRP_B))))

    @pl.when(bounded_ref[0] != 1)
    def _():
        def body(j, carry):
            kc, vc = chunk(j)
            new = []
            for g in range(GRP_B):
                m_prev, acc = carry[g]
                s = scores(g, kc)
                m_new = jnp.maximum(m_prev, jnp.max(s, axis=1, keepdims=True))
                p = jnp.exp2(s - m_new).astype(BF16)
                acc = jnp.exp2(m_prev - m_new) * acc + jnp.dot(p, vc, preferred_element_type=F32)
                new.append((m_new, acc))
            return tuple(new)
        init = tuple((jnp.full((tq, 1), NEG_INF, F32), jnp.zeros((tq, 128), F32)) for _ in range(GRP_B))
        fin = lax.fori_loop(0, S // tk, body, init)
        finish([fin[g][1] for g in range(GRP_B)])


def _score_bounded(q_g, k_g):
    bound = 1.05 * HEAD_DIM * (HEAD_DIM ** -0.5 * LOG2E) * jnp.max(jnp.abs(q_g)) * jnp.max(jnp.abs(k_g))
    return (bound <= SCORE_BOUND_MAX).astype(I32)[None]


def _lane_place(n_src, n_dst, groups):
    pm = np.zeros((len(groups), n_src, n_dst), np.float32)
    for gi, slot in enumerate(groups):
        pm[gi, np.arange(HEAD_DIM), HEAD_DIM * slot + np.arange(HEAD_DIM)] = 1.0
    return jnp.asarray(pm, BF16)


def _global_attention(qk, v_aug, bounded, nseq, S, tok0, tq, tk):
    n_qt = S // tq
    qt0 = tok0 // tq
    s0 = tok0 // S
    place = _lane_place(128, 256, [1, 2, 3])
    grid_spec = pltpu.PrefetchScalarGridSpec(
        num_scalar_prefetch=1,
        grid=(nseq, N_KV_B, n_qt),
        in_specs=[
            pl.BlockSpec((tq, 256), lambda b, h, i, fl: (qt0 + b * n_qt + i, h)),
            pl.BlockSpec((S, 256), lambda b, h, i, fl: (s0 + b, h)),
            pl.BlockSpec((1, S, 128), lambda b, h, i, fl: (h, s0 + b, 0)),
            pl.BlockSpec((GRP_B, 128, 256), lambda b, h, i, fl: (0, 0, 0)),
        ],
        out_specs=pl.BlockSpec((tq, 256), lambda b, h, i, fl: (b * n_qt + i, h)),
    )
    return pl.pallas_call(
        functools.partial(_gattn_kernel, tq=tq, tk=tk, S=S),
        grid_spec=grid_spec,
        out_shape=jax.ShapeDtypeStruct((nseq * S, 1024), BF16),
        compiler_params=_cparams(("parallel", "parallel", "arbitrary")),
        name="global_attention",
    )(bounded, qk, qk, v_aug, place)


def _route(x1, rw_ref, rb_ref):
    n = x1.shape[0]
    xh = x1.astype(BF16)
    xl = (x1 - xh.astype(F32)).astype(BF16)
    o = jnp.dot(jnp.concatenate([xh, xl], axis=0), rw_ref[...], preferred_element_type=F32)
    lg = o[:n, :128] + o[:n, 128:] + o[n:, :128]
    lgt = lg.T[:N_EXPERTS]
    sc = 1.0 / (1.0 + jnp.exp(-lgt))
    bi = sc + rb_ref[...]
    srow = [sc[e:e + 1] for e in range(N_EXPERTS)]
    brow = [bi[e:e + 1] for e in range(N_EXPERTS)]
    gsel = None
    best = None
    for g in range(N_GROUPS):
        a, b, c, d = brow[4 * g:4 * g + 4]
        m1, n1 = jnp.maximum(a, b), jnp.minimum(a, b)
        m2, n2 = jnp.maximum(c, d), jnp.minimum(c, d)
        gs = jnp.maximum(m1, m2) + jnp.maximum(jnp.minimum(m1, m2), jnp.maximum(n1, n2))
        if g == 0:
            best, gsel = gs, jnp.zeros(gs.shape, I32)
        else:
            better = gs > best
            gsel = jnp.where(better, g, gsel)
            best = jnp.where(better, gs, best)
    masked = [jnp.where(gsel == (e // EXPERTS_PER_GROUP), brow[e], NEG_INF) for e in range(N_EXPERTS)]
    i1 = jnp.zeros(gsel.shape, I32)
    b1 = masked[0]
    s1 = srow[0]
    for e in range(1, N_EXPERTS):
        better = masked[e] > b1
        i1 = jnp.where(better, e, i1)
        b1 = jnp.where(better, masked[e], b1)
        s1 = jnp.where(better, srow[e], s1)
    i2 = jnp.full(gsel.shape, -1, I32)
    b2 = jnp.full(b1.shape, -jnp.inf, F32)
    s2 = jnp.zeros(b1.shape, F32)
    for e in range(N_EXPERTS):
        better = (masked[e] > b2) & (i1 != e)
        i2 = jnp.where(better, e, i2)
        b2 = jnp.where(better, masked[e], b2)
        s2 = jnp.where(better, srow[e], s2)
    den = s1 + s2
    return i1, i2, s1 / den, s2 / den


ROW_SPLITS = 2


def _norm_route_store(r0, nr, m, x, g_ref, b_ref, rw_ref, rb_ref, x1_ref, idx_ref, wt_ref, xb_ref, alpha):
    x1 = _layer_norm(alpha * x + m, g_ref[...], b_ref[...])
    x1_ref[r0:r0 + nr, :] = x1
    xb_ref[r0:r0 + nr, :] = x1.astype(BF16)
    i1, i2, w1, w2 = _route(x1, rw_ref, rb_ref)
    rid = lax.broadcasted_iota(I32, (8, nr), 0)
    idx_ref[:, r0:r0 + nr] = jnp.where(rid == 0, i1, jnp.where(rid == 1, i2, 0))
    wt_ref[:, r0:r0 + nr] = jnp.where(rid == 0, w1, jnp.where(rid == 1, w2, 0.0))


def _router_operands(router_w, router_b):
    rw = jnp.zeros((D_MODEL, 128), F32).at[:, :N_EXPERTS].set(router_w.astype(F32))
    rwh = rw.astype(BF16)
    rwl = (rw - rwh.astype(F32)).astype(BF16)
    rb = router_b.astype(F32)[:, None]
    return jnp.concatenate([rwh, rwl], axis=1), rb


_ROUTER_SPECS = [
    pl.BlockSpec((D_MODEL, 256), lambda i: (0, 0)),
    pl.BlockSpec((N_EXPERTS, 1), lambda i: (0, 0)),
]


def _mix_out_specs(ts, T):
    specs = [
        pl.BlockSpec((ts, D_MODEL), lambda i: (i, 0)),
        pl.BlockSpec((8, ts), lambda i: (0, i)),
        pl.BlockSpec((8, ts), lambda i: (0, i)),
        pl.BlockSpec((ts, D_MODEL), lambda i: (i, 0)),
    ]
    shapes = [
        jax.ShapeDtypeStruct((T, D_MODEL), F32),
        jax.ShapeDtypeStruct((8, T), I32),
        jax.ShapeDtypeStruct((8, T), F32),
        jax.ShapeDtypeStruct((T, D_MODEL), BF16),
    ]
    return specs, shapes


def _outproj_even_kernel(*refs, nf, na, nx, fb, ab, xb, alpha):
    f_refs, a_refs, x_refs = refs[:nf], refs[nf:nf + na], refs[nf + na:nf + na + nx]
    wf_ref, wa_ref, g_ref, b_ref, rw_ref, rb_ref, x1_ref, idx_ref, wt_ref, xb_ref = refs[nf + na + nx:]
    nr = x1_ref.shape[0] // ROW_SPLITS
    for h in range(ROW_SPLITS):
        r0 = h * nr
        m = (jnp.dot(_pick_rows(f_refs, fb, r0, nr), wf_ref[...], preferred_element_type=F32)
             + jnp.dot(_pick_rows(a_refs, ab, r0, nr), wa_ref[...], preferred_element_type=F32))
        xr = _pick_rows(x_refs, xb, r0, nr)
        _norm_route_store(r0, nr, m, xr, g_ref, b_ref, rw_ref, rb_ref, x1_ref, idx_ref, wt_ref, xb_ref, alpha)


def _outproj_even(f_parts, a_parts, x_parts, w_out, ln_g, ln_b, router, alpha, ts):
    T = sum(p.shape[0] for p in x_parts)
    f_specs, fb = _parts_specs(f_parts, ts)
    a_specs, ab = _parts_specs(a_parts, ts)
    x_specs, xb = _parts_specs(x_parts, ts)
    wf = w_out[:D_FOURIER].astype(BF16)
    wa = w_out[D_FOURIER:].reshape(N_KV_B, GRP_B * HEAD_DIM, D_MODEL)
    wa = jnp.concatenate([jnp.zeros((N_KV_B, HEAD_DIM, D_MODEL), w_out.dtype), wa], axis=1)
    wa = wa.reshape(1024, D_MODEL).astype(BF16)
    out_specs, out_shapes = _mix_out_specs(ts, T)
    return pl.pallas_call(
        functools.partial(_outproj_even_kernel, nf=len(f_parts), na=len(a_parts), nx=len(x_parts),
                          fb=fb, ab=ab, xb=xb, alpha=alpha),
        grid=(T // ts,),
        in_specs=f_specs + a_specs + x_specs + [
            pl.BlockSpec((D_FOURIER, D_MODEL), lambda i: (0, 0)),
            pl.BlockSpec((1024, D_MODEL), lambda i: (0, 0)),
            pl.BlockSpec((1, D_MODEL), lambda i: (0, 0)),
            pl.BlockSpec((1, D_MODEL), lambda i: (0, 0)),
        ] + _ROUTER_SPECS,
        out_specs=out_specs,
        out_shape=out_shapes,
        compiler_params=_cparams(("parallel",)),
        name="outproj_even",
    )(*f_parts, *a_parts, *x_parts, wf, wa, ln_g[None, :], ln_b[None, :], *router)


CHUNK = 16


def _stage_rows(td):
    return -(-(2 * td + N_EXPERTS * (CHUNK - 1)) // 256) * 256


def _tab_len(td):
    return -(-(1 + 2 * td // CHUNK + N_EXPERTS) // 128) * 128


def _moe_plan(idx, td, tm):
    T = idx.shape[1]
    nt = T // td
    ep = idx.reshape(2, nt, td).transpose(1, 0, 2).reshape(nt, 2 * td)
    oh = (ep[None, :, :] == jnp.arange(N_EXPERTS, dtype=I32)[:, None, None]).astype(I32)
    cs = jnp.cumsum(oh, axis=2)
    cnt = cs[:, :, -1].T
    c8 = ((cnt + CHUNK - 1) // CHUNK) * CHUNK
    off8 = jnp.cumsum(c8, axis=1) - c8
    base8 = jnp.cumsum(c8, axis=0) - c8
    seg = jnp.sum(c8, axis=0)
    padded = ((seg + tm - 1) // tm) * tm
    ends = jnp.cumsum(padded)
    starts = ends - padded
    dst = starts[None, :] + base8
    lpos = jnp.sum(oh * (cs - 1 + off8.T[:, :, None]), axis=0).astype(I32)
    n_rows = -(-(2 * T + nt * N_EXPERTS * (CHUNK - 1)) // tm) * tm + N_EXPERTS * tm
    tile_start = jnp.arange(n_rows // tm, dtype=I32) * tm
    tile_e = jnp.minimum(jnp.sum((ends[None, :] <= tile_start[:, None]).astype(I32), axis=1), N_EXPERTS - 1)
    n_used = (ends[-1] // tm).astype(I32)[None]
    nc = c8 // CHUNK
    cum = jnp.cumsum(nc, axis=1)
    n_chunk_max = 2 * td // CHUNK + N_EXPERTS
    c_idx = jnp.arange(n_chunk_max, dtype=I32)
    e_c = jnp.minimum(jnp.sum((cum[:, None, :] <= c_idx[None, :, None]).astype(I32), axis=2), N_EXPERTS - 1)
    oh_c = (e_c[:, :, None] == jnp.arange(N_EXPERTS, dtype=I32)[None, None, :]).astype(I32)
    srow = jnp.sum(oh_c * (dst - CHUNK * (cum - nc))[:, None, :], axis=2) + CHUNK * c_idx[None, :]
    tab = jnp.concatenate([cum[:, -1:], srow], axis=1).astype(I32)
    tab = jnp.pad(tab, ((0, 0), (0, _tab_len(td) - tab.shape[1]))).reshape(nt, 1, _tab_len(td))
    pad_lo = jnp.concatenate([starts + seg, ends[-1:]]).astype(I32)
    pad_hi = jnp.concatenate([ends, jnp.full((1,), n_rows, I32)]).astype(I32)
    lpos_k = lpos.reshape(nt, 2, td).transpose(1, 0, 2).reshape(2, T)
    lpos_rows = jnp.zeros((8, T), I32).at[:2].set(lpos_k)
    return tab, lpos_rows, lpos_k.T, tile_e.astype(I32), n_used, pad_lo, pad_hi, n_rows


def _chunk_loops(tab_ref, fn):
    total = tab_ref[0, 0, 0]

    def one(c, priority):
        fn(pl.multiple_of(CHUNK * c, CHUNK), pl.multiple_of(tab_ref[0, 0, 1 + c], CHUNK), priority)

    def body(j, carry):
        one(2 * j, 0)
        one(2 * j + 1, 1)
        return carry
    lax.fori_loop(0, total // 2, body, 0)

    @pl.when(total % 2 == 1)
    def _():
        one(total - 1, 0)
    return total


def _dispatch_kernel(lo_ref, hi_ref, tab_ref, lp_ref, x_ref, xs_hbm, stage, zbuf, nwait, sem, zsem, *, ns):
    i = pl.program_id(0)
    slot = i % 2

    def zero_copy(dst):
        return pltpu.make_async_copy(zbuf, xs_hbm.at[pl.ds(pl.multiple_of(dst, CHUNK), CHUNK)], zsem)

    @pl.when(i == 0)
    def _():
        nwait[0] = 0
        nwait[1] = 0
        zbuf[...] = jnp.zeros(zbuf.shape, BF16)
        for e in range(N_EXPERTS + 1):
            nz = (hi_ref[e] - lo_ref[e]) // CHUNK
            lax.fori_loop(0, nz, lambda j, c, e=e: (zero_copy(lo_ref[e] + CHUNK * j).start(), c)[1], 0)
        for e in range(N_EXPERTS + 1):
            nz = (hi_ref[e] - lo_ref[e]) // CHUNK
            lax.fori_loop(0, nz, lambda j, c: (zero_copy(0).wait(), c)[1], 0)

    def piece(local_row, sorted_row, s):
        return pltpu.make_async_copy(stage.at[s, pl.ds(local_row, CHUNK)],
                                     xs_hbm.at[pl.ds(sorted_row, CHUNK)], sem.at[s])

    def drain(s):
        lax.fori_loop(0, nwait[s], lambda j, c: (piece(0, 0, s).wait(), c)[1], 0)

    drain(slot)
    lp = lp_ref[...]
    rows = lax.broadcasted_iota(I32, (ns, lp.shape[1]), 0)
    perm = jnp.where((rows == lp[0:1]) | (rows == lp[1:2]), 1.0, 0.0).astype(BF16)
    stage[slot] = jnp.dot(perm, x_ref[...], preferred_element_type=F32).astype(BF16)
    nwait[slot] = _chunk_loops(tab_ref, lambda lr, sr, pr: piece(lr, sr, slot).start(priority=pr))

    @pl.when(i == pl.num_programs(0) - 1)
    def _():
        drain(0)
        drain(1)


def _moe_dispatch(x1, tab, lpos_rows, pad_lo, pad_hi, n_rows, td):
    T = x1.shape[0]
    ns = _stage_rows(td)
    grid_spec = pltpu.PrefetchScalarGridSpec(
        num_scalar_prefetch=2,
        grid=(T // td,),
        in_specs=[
            pl.BlockSpec((1, 1, _tab_len(td)), lambda i, lo, hi: (i, 0, 0), memory_space=pltpu.SMEM),
            pl.BlockSpec((8, td), lambda i, lo, hi: (0, i)),
            pl.BlockSpec((td, D_MODEL), lambda i, lo, hi: (i, 0)),
        ],
        out_specs=pl.BlockSpec(memory_space=pl.ANY),
        scratch_shapes=[pltpu.VMEM((2, ns, D_MODEL), BF16), pltpu.VMEM((CHUNK, D_MODEL), BF16),
                        pltpu.SMEM((2,), I32), pltpu.SemaphoreType.DMA((2,)), pltpu.SemaphoreType.DMA(())],
    )
    return pl.pallas_call(
        functools.partial(_dispatch_kernel, ns=ns),
        grid_spec=grid_spec,
        out_shape=jax.ShapeDtypeStruct((n_rows, D_MODEL), BF16),
        compiler_params=_cparams(("arbitrary",)),
        name="moe_dispatch",
    )(pad_lo, pad_hi, tab, lpos_rows, x1)


def _moe_kernel(te_ref, nu_ref, x_ref, wg_ref, wu_ref, wd_ref, y_ref, wgb, wub, wdb):
    i = pl.program_id(0)

    @pl.when((i == 0) | (te_ref[i] != te_ref[jnp.maximum(i - 1, 0)]))
    def _():
        wgb[...] = wg_ref[0, 0].astype(BF16)
        wub[...] = wu_ref[0, 0].astype(BF16)
        wdb[...] = wd_ref[0, 0].astype(BF16)

    @pl.when(i < nu_ref[0])
    def _():
        xg = x_ref[...]
        hg = jnp.dot(xg, wgb[...], preferred_element_type=F32)
        hu = jnp.dot(xg, wub[...], preferred_element_type=F32)
        hdn = (hg / (1.0 + jnp.exp(-hg)) * hu).astype(BF16)
        y_ref[...] = jnp.dot(hdn, wdb[...], preferred_element_type=F32).astype(BF16)

    @pl.when(i >= nu_ref[0])
    def _():
        y_ref[...] = jnp.zeros(y_ref.shape, BF16)


def _moe_experts(xs, tile_e, n_used, wg, wu, wd, layer, tm):
    n_rows = xs.shape[0]
    grid_spec = pltpu.PrefetchScalarGridSpec(
        num_scalar_prefetch=2,
        grid=(n_rows // tm,),
        in_specs=[
            pl.BlockSpec((tm, D_MODEL), lambda i, te, nu: (jnp.minimum(i, jnp.maximum(nu[0] - 1, 0)), 0)),
            pl.BlockSpec((1, 1, D_MODEL, D_FF), lambda i, te, nu: (layer, te[i], 0, 0)),
            pl.BlockSpec((1, 1, D_MODEL, D_FF), lambda i, te, nu: (layer, te[i], 0, 0)),
            pl.BlockSpec((1, 1, D_FF, D_MODEL), lambda i, te, nu: (layer, te[i], 0, 0)),
        ],
        out_specs=pl.BlockSpec((tm, D_MODEL), lambda i, te, nu: (i, 0)),
        scratch_shapes=[pltpu.VMEM((D_MODEL, D_FF), BF16), pltpu.VMEM((D_MODEL, D_FF), BF16),
                        pltpu.VMEM((D_FF, D_MODEL), BF16)],
    )
    return pl.pallas_call(
        _moe_kernel,
        grid_spec=grid_spec,
        out_shape=jax.ShapeDtypeStruct((n_rows, D_MODEL), BF16),
        compiler_params=_cparams(("arbitrary",)),
        name="moe_experts",
    )(tile_e, n_used, xs, wg, wu, wd)


def _combine_kernel(tab_ref, tabn_ref, lp_ref, y_hbm, x_ref, wt_ref, g_ref, b_ref, *rest, ns, alpha, bounds):
    o_refs, (ystage, sem) = rest[:len(bounds)], rest[len(bounds):]
    i = pl.program_id(0)
    n = pl.num_programs(0)
    slot = i % 2

    def piece(local_row, sorted_row, s):
        return pltpu.make_async_copy(y_hbm.at[pl.ds(sorted_row, CHUNK)],
                                     ystage.at[s, pl.ds(local_row, CHUNK)], sem.at[s])

    @pl.when(i == 0)
    def _():
        ystage[...] = jnp.zeros(ystage.shape, BF16)
        _chunk_loops(tab_ref, lambda lr, sr, pr: piece(lr, sr, 0).start(priority=pr))

    @pl.when(i + 1 < n)
    def _():
        _chunk_loops(tabn_ref, lambda lr, sr, pr: piece(lr, sr, 1 - slot).start(priority=pr))

    lax.fori_loop(0, tab_ref[0, 0, 0], lambda j, c: (piece(0, 0, slot).wait(), c)[1], 0)

    ys = ystage[slot]
    lp = lp_ref[...]
    cols = lax.broadcasted_iota(I32, (lp.shape[0], ns), 1)
    w = wt_ref[...]
    tc = lp.shape[0]
    sel = jnp.concatenate([jnp.where(cols == lp[:, k:k + 1], 1.0, 0.0).astype(BF16) for k in range(2)], axis=0)
    picked = jnp.dot(sel, ys, preferred_element_type=F32)
    f = w[:, 0:1] * picked[:tc] + w[:, 1:2] * picked[tc:]
    o = _layer_norm(alpha * x_ref[...] + f, g_ref[...], b_ref[...])
    if len(bounds) == 1:
        o_refs[0][...] = o
    else:
        lo = 0
        for o_ref, hi in zip(o_refs, bounds):
            @pl.when((i >= lo) & (i < hi))
            def _(o_ref=o_ref):
                o_ref[...] = o
            lo = hi


def _moe_combine(y, tab, lpos_tk, wts, x1, ln_g, ln_b, alpha, tc, out_rows):
    T = x1.shape[0]
    n_tiles = T // tc
    ns = _stage_rows(tc)
    out_specs, out_shapes, bounds, lo = [], [], [], 0
    for rows in out_rows:
        n = rows // tc
        out_specs.append(pl.BlockSpec((tc, D_MODEL), lambda i, lo=lo, n=n: (jnp.clip(i - lo, 0, n - 1), 0)))
        out_shapes.append(jax.ShapeDtypeStruct((rows, D_MODEL), F32))
        lo += n
        bounds.append(lo)
    return pl.pallas_call(
        functools.partial(_combine_kernel, ns=ns, alpha=alpha, bounds=tuple(bounds)),
        grid=(n_tiles,),
        in_specs=[
            pl.BlockSpec((1, 1, _tab_len(tc)), lambda i: (i, 0, 0), memory_space=pltpu.SMEM),
            pl.BlockSpec((1, 1, _tab_len(tc)), lambda i: (jnp.minimum(i + 1, n_tiles - 1), 0, 0),
                         memory_space=pltpu.SMEM),
            pl.BlockSpec((tc, 2), lambda i: (i, 0)),
            pl.BlockSpec(memory_space=pl.ANY),
            pl.BlockSpec((tc, D_MODEL), lambda i: (i, 0)),
            pl.BlockSpec((tc, 2), lambda i: (i, 0)),
            pl.BlockSpec((1, D_MODEL), lambda i: (0, 0)),
            pl.BlockSpec((1, D_MODEL), lambda i: (0, 0)),
        ],
        out_specs=out_specs,
        out_shape=out_shapes,
        scratch_shapes=[pltpu.VMEM((2, ns, D_MODEL), BF16), pltpu.SemaphoreType.DMA((2,))],
        compiler_params=_cparams(("arbitrary",)),
        name="moe_combine",
    )(tab, tab, lpos_tk, y, x1, wts, ln_g[None, :], ln_b[None, :])


def _moe_layer(x1, xb, idx, wt, wg, wu, wd, layer, ln_g, ln_b, alpha, tm, tc, out_rows):
    tab, lpos_rows, lpos_tk, tile_e, n_used, pad_lo, pad_hi, n_rows = _moe_plan(idx[:2], tc, tm)
    xs = _moe_dispatch(xb, tab, lpos_rows, pad_lo, pad_hi, n_rows, tc)
    y = _moe_experts(xs, tile_e, n_used, wg, wu, wd, layer, tm)
    return _moe_combine(y, tab, lpos_tk, wt[:2].T, x1, ln_g, ln_b, alpha, tc, out_rows)


def _inproj_odd_kernel(x_ref, w_ref, vone_ref, bg_ref, u_ref, q_ref, k_ref, v_ref, nrm_ref):
    x = x_ref[...].astype(BF16)
    proj = jnp.dot(x, w_ref[...], preferred_element_type=F32)
    bg_ref[...] = proj[:, :512].astype(BF16)
    u_ref[...] = (proj[:, 512:1024] * proj[:, 1024:1536]).astype(BF16)
    qb = (proj[:, 1536:2048] * (HEAD_DIM ** -0.5 * LOG2E)).astype(BF16)
    kb = proj[:, 2048:2304].astype(BF16)
    q_ref[...] = qb
    k_ref[...] = kb
    v_ref[...] = (proj[:, 2304:2560] + vone_ref[...]).astype(BF16)
    qf = qb.astype(F32)
    kf = kb.astype(F32)
    qq = jnp.max(jnp.sum(qf * qf, axis=1, keepdims=True), axis=0, keepdims=True)
    kk = jnp.max(jnp.sum(kf * kf, axis=1, keepdims=True), axis=0, keepdims=True)
    rid = lax.broadcasted_iota(I32, (8, 128), 0)
    nrm_ref[0] = jnp.where(rid == 0, qq, jnp.where(rid == 1, kk, 0.0))


def _inproj_odd(x, w_in, ts):
    T = x.shape[0]
    c3 = 3 * D_CONV
    wk = w_in[:, c3 + 512:c3 + 640].reshape(D_MODEL, N_KV_D, HEAD_DIM)
    wv = w_in[:, c3 + 640:c3 + 768].reshape(D_MODEL, N_KV_D, HEAD_DIM)
    wkp = jnp.concatenate([wk, jnp.zeros_like(wk)], axis=2).reshape(D_MODEL, 256)
    wvp = jnp.concatenate([wv, jnp.zeros_like(wv)], axis=2).reshape(D_MODEL, 256)
    w = jnp.concatenate([w_in[:, :c3 + 512], wkp, wvp], axis=1).astype(BF16)
    vone_np = np.zeros((1, 256), np.float32)
    for h in range(N_KV_D):
        vone_np[0, 128 * h + 64:128 * (h + 1)] = 1.0
    n_w = w.shape[1]
    widths = [512, 512, 512, 256, 256]
    bg, u, q, kw, vw, nrm = pl.pallas_call(
        _inproj_odd_kernel,
        grid=(T // ts,),
        in_specs=[
            pl.BlockSpec((ts, D_MODEL), lambda i: (i, 0)),
            pl.BlockSpec((D_MODEL, n_w), lambda i: (0, 0)),
            pl.BlockSpec((1, 256), lambda i: (0, 0)),
        ],
        out_specs=[pl.BlockSpec((ts, n), lambda i: (i, 0)) for n in widths]
        + [pl.BlockSpec((1, 8, 128), lambda i: (i, 0, 0))],
        out_shape=[jax.ShapeDtypeStruct((T, n), BF16) for n in widths]
        + [jax.ShapeDtypeStruct((T // ts, 8, 128), F32)],
        compiler_params=_cparams(("parallel",)),
        name="inproj_odd",
    )(x, w, jnp.asarray(vone_np))
    score_bound = jnp.sqrt(jnp.max(nrm[:, 0, 0]) * jnp.max(nrm[:, 1, 0]))
    return bg, u, q, kw, vw, score_bound


def _wattn_kernel(bounded_ref, q_ref, kp_ref, kc_ref, kn_ref, vp_ref, vc_ref, vn_ref, bias_ref, sink_ref,
                  place_ref, o_ref, *, tq, seq_tiles):
    i = pl.program_id(0)
    first = i < 0
    last = i < 0
    for lo, hi, per in seq_tiles:
        inside = (i >= lo) & (i < hi)
        first = first | (inside & ((i - lo) % per == 0))
        last = last | (inside & ((i - lo) % per == per - 1))
    kfull = jnp.concatenate([kp_ref[...], kc_ref[...], kn_ref[...]], axis=0)[:, :HEAD_DIM]
    vfull = jnp.concatenate([vp_ref[...], vc_ref[...], vn_ref[...]], axis=0)
    q4 = q_ref[...]
    bias = bias_ref[0]
    sink = sink_ref[0]
    nb = tq // Q_BLOCK
    col = lax.broadcasted_iota(I32, (GRP_D * Q_BLOCK, 3 * Q_BLOCK), 1)

    def blocks(with_max):
        for n in range(nb):
            qs = jnp.concatenate([q4[Q_BLOCK * n:Q_BLOCK * (n + 1), HEAD_DIM * g:HEAD_DIM * (g + 1)]
                                  for g in range(GRP_D)], axis=0)
            keys = kfull[Q_BLOCK * n:Q_BLOCK * (n + 3)]
            vals = vfull[Q_BLOCK * n:Q_BLOCK * (n + 3)]
            s = lax.dot_general(qs, keys, (((1,), (1,)), ((), ())), preferred_element_type=F32) + bias
            if n == 0:
                s = jnp.where(first & (col < Q_BLOCK), NEG_INF, s)
            if n == nb - 1:
                s = jnp.where(last & (col >= 2 * Q_BLOCK), NEG_INF, s)
            if with_max:
                m = jnp.maximum(jnp.max(s, axis=1, keepdims=True), sink)
                s = s - m
                snk = sink - m
            else:
                snk = sink
            acc = jnp.dot(jnp.exp2(s).astype(BF16), vals, preferred_element_type=F32)
            den = acc[:, HEAD_DIM:HEAD_DIM + 1] + jnp.exp2(snk)
            o = (acc / den).astype(BF16)
            ocat = jnp.concatenate([o[Q_BLOCK * g:Q_BLOCK * (g + 1)] for g in range(GRP_D)], axis=1)
            out = jnp.dot(ocat, place_ref[...], preferred_element_type=F32)
            o_ref[Q_BLOCK * n:Q_BLOCK * (n + 1), :] = out.astype(BF16)

    @pl.when(bounded_ref[0] == 1)
    def _():
        blocks(False)

    @pl.when(bounded_ref[0] != 1)
    def _():
        blocks(True)


WINDOW_BOUND_MAX = 90.0


def _window_attention(q, kw, vw, sink_logits, score_bound, classes, tq):
    T = q.shape[0]
    sink_bound = jnp.max(jnp.abs(sink_logits.astype(F32))) * LOG2E
    bounded = ((1.02 * score_bound <= WINDOW_BOUND_MAX) & (sink_bound <= WINDOW_BOUND_MAX)).astype(I32)[None]
    n_tiles = T // tq
    hb = tq // Q_BLOCK
    n_hblk = T // Q_BLOCK
    r = jnp.arange(Q_BLOCK, dtype=I32)[:, None]
    j = jnp.arange(3 * Q_BLOCK, dtype=I32)[None, :]
    rel = jnp.abs(j - Q_BLOCK - r).astype(F32)
    slopes = jnp.asarray(np.array([2.0 ** (-8.0 * (h + 1) / N_HEADS_D) for h in range(N_HEADS_D)], np.float32))
    bias = jnp.where(rel[None] <= WINDOW, -slopes[:, None, None] * rel[None] * LOG2E, NEG_INF)
    bias = bias.reshape(N_KV_D, GRP_D * Q_BLOCK, 3 * Q_BLOCK)
    sink = jnp.repeat(sink_logits.astype(F32) * LOG2E, Q_BLOCK).reshape(N_KV_D, GRP_D * Q_BLOCK, 1)
    place = _lane_place(128, 256, [0, 1, 2, 3]).reshape(GRP_D * 128, 256)
    seq_tiles = []
    t0 = 0
    for nseq, S, _ in classes:
        cnt = nseq * S // tq
        seq_tiles.append((t0, t0 + cnt, S // tq))
        t0 += cnt
    prev_map = lambda i, h, fl: (jnp.maximum(i * hb - 1, 0), h)
    next_map = lambda i, h, fl: (jnp.minimum((i + 1) * hb, n_hblk - 1), h)
    grid_spec = pltpu.PrefetchScalarGridSpec(
        num_scalar_prefetch=1,
        grid=(n_tiles, N_KV_D),
        in_specs=[
            pl.BlockSpec((tq, 256), lambda i, h, fl: (i, h)),
            pl.BlockSpec((Q_BLOCK, 128), prev_map),
            pl.BlockSpec((tq, 128), lambda i, h, fl: (i, h)),
            pl.BlockSpec((Q_BLOCK, 128), next_map),
            pl.BlockSpec((Q_BLOCK, 128), prev_map),
            pl.BlockSpec((tq, 128), lambda i, h, fl: (i, h)),
            pl.BlockSpec((Q_BLOCK, 128), next_map),
            pl.BlockSpec((1, GRP_D * Q_BLOCK, 3 * Q_BLOCK), lambda i, h, fl: (h, 0, 0)),
            pl.BlockSpec((1, GRP_D * Q_BLOCK, 1), lambda i, h, fl: (h, 0, 0)),
            pl.BlockSpec((GRP_D * 128, 256), lambda i, h, fl: (0, 0)),
        ],
        out_specs=pl.BlockSpec((tq, 256), lambda i, h, fl: (i, h)),
    )
    return pl.pallas_call(
        functools.partial(_wattn_kernel, tq=tq, seq_tiles=tuple(seq_tiles)),
        grid_spec=grid_spec,
        out_shape=jax.ShapeDtypeStruct((T, 512), BF16),
        compiler_params=_cparams(("parallel", "parallel")),
        name="window_attention",
    )(bounded, q, kw, kw, kw, vw, vw, vw, bias, sink, place)


HALO = 16


def _outproj_odd_kernel(bg_ref, u_ref, up_ref, un_ref, a_ref, x_ref, cw_ref, cb_ref, wc_ref, wa_ref,
                        g_ref, b_ref, rw_ref, rb_ref, x1_ref, idx_ref, wt_ref, xb_ref,
                        *, alpha, ts, seq_tiles):
    i = pl.program_id(0)
    first = i < 0
    last = i < 0
    for lo, hi, per in seq_tiles:
        inside = (i >= lo) & (i < hi)
        first = first | (inside & ((i - lo) % per == 0))
        last = last | (inside & ((i - lo) % per == per - 1))
    u = u_ref[...].astype(F32)
    prev_row = jnp.where(first, 0.0, up_ref[HALO - 1:HALO, :].astype(F32))
    next_row = jnp.where(last, 0.0, un_ref[0:1, :].astype(F32))
    rid = lax.broadcasted_iota(I32, u.shape, 0)
    ud = jnp.where(rid == 0, prev_row, pltpu.roll(u, 1, 0))
    uu = jnp.where(rid == ts - 1, next_row, pltpu.roll(u, ts - 1, 0))
    cw = cw_ref[...]
    y = ud * cw[0:1] + u * cw[1:2] + uu * cw[2:3] + cb_ref[...]
    c = (bg_ref[...].astype(F32) * y).astype(BF16)
    nr = ts // ROW_SPLITS
    for h in range(ROW_SPLITS):
        r0 = h * nr
        m = (jnp.dot(c[r0:r0 + nr], wc_ref[...], preferred_element_type=F32)
             + jnp.dot(a_ref[r0:r0 + nr, :], wa_ref[...], preferred_element_type=F32))
        _norm_route_store(r0, nr, m, x_ref[r0:r0 + nr, :], g_ref, b_ref, rw_ref, rb_ref, x1_ref, idx_ref,
                          wt_ref, xb_ref, alpha)


def _outproj_odd(bg, u, a, x, conv_w, conv_b, w_out, ln_g, ln_b, router, alpha, classes, ts):
    T = x.shape[0]
    hb = ts // HALO
    n_h = T // HALO
    seq_tiles = []
    t0 = 0
    for nseq, S, _ in classes:
        cnt = nseq * S // ts
        seq_tiles.append((t0, t0 + cnt, S // ts))
        t0 += cnt
    out_specs, out_shapes = _mix_out_specs(ts, T)
    return pl.pallas_call(
        functools.partial(_outproj_odd_kernel, alpha=alpha, ts=ts, seq_tiles=tuple(seq_tiles)),
        grid=(T // ts,),
        in_specs=[
            pl.BlockSpec((ts, D_CONV), lambda i: (i, 0)),
            pl.BlockSpec((ts, D_CONV), lambda i: (i, 0)),
            pl.BlockSpec((HALO, D_CONV), lambda i: (jnp.maximum(i * hb - 1, 0), 0)),
            pl.BlockSpec((HALO, D_CONV), lambda i: (jnp.minimum((i + 1) * hb, n_h - 1), 0)),
            pl.BlockSpec((ts, 512), lambda i: (i, 0)),
            pl.BlockSpec((ts, D_MODEL), lambda i: (i, 0)),
            pl.BlockSpec((3, D_CONV), lambda i: (0, 0)),
            pl.BlockSpec((1, D_CONV), lambda i: (0, 0)),
            pl.BlockSpec((D_CONV, D_MODEL), lambda i: (0, 0)),
            pl.BlockSpec((512, D_MODEL), lambda i: (0, 0)),
            pl.BlockSpec((1, D_MODEL), lambda i: (0, 0)),
            pl.BlockSpec((1, D_MODEL), lambda i: (0, 0)),
        ] + _ROUTER_SPECS,
        out_specs=out_specs,
        out_shape=out_shapes,
        compiler_params=_cparams(("parallel",)),
        name="outproj_odd",
    )(bg, u, u, u, a, x, conv_w.astype(F32), conv_b.astype(F32)[None, :],
      w_out[:D_CONV].astype(BF16), w_out[D_CONV:].astype(BF16), ln_g[None, :], ln_b[None, :], *router)


TOKEN_TILE = 512
GLOBAL_Q_TILE = 1024
GLOBAL_KV_CHUNK = 2048
WINDOW_Q_TILE = 512
EXPERT_ROW_TILE = 1024
MOE_TOKEN_TILE = 256


def _tile(n, cap):
    t = cap
    while n % t:
        t //= 2
    return t


def kernel(x_prompt, x_sample, w_in_even, fourier_norm_g, q_norm_g, k_norm_g, w_out_even, w_in_odd, conv_w,
           conv_b, sink_logits, w_out_odd, ln_mix_g, ln_mix_b, ln_ffn_g, ln_ffn_b, router_w, router_b,
           w_gate, w_up, w_down):
    depth = ln_mix_g.shape[0]
    alpha = float((2 * depth) ** 0.25)
    bp, sp, _ = x_prompt.shape
    bs, ss, _ = x_sample.shape
    classes = ((bp, sp, 0), (bs, ss, bp * sp))
    T = bp * sp + bs * ss
    min_s = min(sp, ss)
    ts = _tile(min_s, TOKEN_TILE)
    tq_w = _tile(min_s, WINDOW_Q_TILE)
    tm = EXPERT_ROW_TILE
    tc = _tile(min_s, MOE_TOKEN_TILE)
    x_parts = [x_prompt.reshape(bp * sp, D_MODEL), x_sample.reshape(bs * ss, D_MODEL)]
    router = _router_operands(router_w, router_b)
    for l in range(depth):
        i = l // 2
        if l % 2 == 0:
            uf, qk, v_aug = _inproj_even(x_parts, w_in_even[i], fourier_norm_g[i], q_norm_g[i], k_norm_g[i],
                                         classes, ts)
            bounded = _score_bounded(q_norm_g[i], k_norm_g[i])
            f_parts, a_parts = [], []
            for nseq, S, tok0 in classes:
                f_parts.append(_fourier_mix(uf, nseq, S, tok0))
                a_parts.append(_global_attention(qk, v_aug, bounded, nseq, S, tok0, _tile(S, GLOBAL_Q_TILE),
                                                 _tile(S, GLOBAL_KV_CHUNK)))
            x1, idx, wt, xb = _outproj_even(f_parts, a_parts, x_parts, w_out_even[i], ln_mix_g[l], ln_mix_b[l],
                                            router, alpha, ts)
        else:
            x = x_parts[0]
            bg, u, q, kw, vw, score_bound = _inproj_odd(x, w_in_odd[i], ts)
            a = _window_attention(q, kw, vw, sink_logits[i], score_bound, classes, tq_w)
            x1, idx, wt, xb = _outproj_odd(bg, u, a, x, conv_w[i], conv_b[i], w_out_odd[i], ln_mix_g[l],
                                           ln_mix_b[l], router, alpha, classes, ts)
        out_rows = [bp * sp, bs * ss] if l == depth - 1 else [T]
        x_parts = _moe_layer(x1, xb, idx, wt, w_gate, w_up, w_down, l, ln_ffn_g[l], ln_ffn_b[l], alpha,
                             tm, tc, out_rows)
    if len(x_parts) == 1:
        x_parts = [x_parts[0][:bp * sp], x_parts[0][bp * sp:]]
    return (x_parts[0].reshape(bp, sp, D_MODEL), x_parts[1].reshape(bs, ss, D_MODEL))
```

```python
import functools
import math

import numpy as np
import jax
import jax.numpy as jnp
from jax import lax
from jax.experimental import pallas as pl
from jax.experimental.pallas import tpu as pltpu

F32 = jnp.float32
BF16 = jnp.bfloat16
I32 = jnp.int32

D_MODEL = 1024
HEAD_DIM = 64
GRID_W = 64
Q_BLOCK = 128
WINDOW = 128
ROPE_THETA = 10000.0
N_FOURIER_GROUPS = 4
D_FOURIER = 256
N_HEADS_B = 12
N_KV_B = 4
GRP_B = 3
D_CONV = 512
N_HEADS_D = 8
N_KV_D = 2
GRP_D = 4
N_EXPERTS = 16
N_GROUPS = 4
EXPERTS_PER_GROUP = 4
D_FF = 512
LN_EPS = 1e-5
RMS_EPS = 1e-6
NEG_INF = -1e30
LOG2E = 1.4426950408889634
FFT_S2 = 128
V7X_VMEM_LIMIT = 48 * 1024 * 1024


def _cparams(sem):
    return pltpu.CompilerParams(dimension_semantics=sem, vmem_limit_bytes=V7X_VMEM_LIMIT)


def _layer_norm(y, g, b):
    mu = jnp.mean(y, axis=-1, keepdims=True)
    yc = y - mu
    var = jnp.mean(yc * yc, axis=-1, keepdims=True)
    return yc * lax.rsqrt(var + LN_EPS) * g + b


N_NORM_COLS = D_FOURIER + 1024


def _parts_specs(parts, ts):
    specs, bounds, lo = [], [], 0
    for p in parts:
        n = p.shape[0] // ts
        specs.append(pl.BlockSpec((ts, p.shape[1]), lambda i, lo=lo, n=n: (jnp.clip(i - lo, 0, n - 1), 0)))
        lo += n
        bounds.append(lo)
    return specs, tuple(bounds)


def _pick_rows(refs, bounds, r0, nr):
    i = pl.program_id(0)
    out = refs[-1][r0:r0 + nr, :]
    for ref, hi in reversed(list(zip(refs[:-1], bounds[:-1]))):
        out = jnp.where(i < hi, ref[r0:r0 + nr, :], out)
    return out


def _inproj_even_kernel(*refs, nx, bounds):
    x_refs = refs[:nx]
    (w_ref, gain_ref, seg_ref, segt_ref, cos_ref, sin_ref, vone_ref, uf_ref, qk_ref, v_ref) = refs[nx:]
    nr = uf_ref.shape[0] // ROW_SPLITS
    for part in range(ROW_SPLITS):
        r0 = part * nr
        x = _pick_rows(x_refs, bounds, r0, nr).astype(BF16)
        proj = jnp.dot(x, w_ref[...], preferred_element_type=F32)
        nrm = proj[:, :N_NORM_COLS]
        sq = (nrm * nrm).astype(BF16)
        ssum = jnp.dot(sq, seg_ref[...], preferred_element_type=F32)
        r = lax.rsqrt(ssum * (1.0 / HEAD_DIM) + RMS_EPS)
        rh = r.astype(BF16)
        rl = (r - rh.astype(F32)).astype(BF16)
        rex = jnp.dot(jnp.concatenate([rh, rl], axis=1), segt_ref[...], preferred_element_type=F32)
        y = nrm * rex * gain_ref[...]
        uf_ref[r0:r0 + nr, :] = y[:, :D_FOURIER].astype(BF16)
        yq = y[:, D_FOURIER:]
        c = jnp.concatenate([cos_ref[r0:r0 + nr, :]] * 8, axis=1)
        s = jnp.concatenate([sin_ref[r0:r0 + nr, :]] * 8, axis=1)
        lane = lax.broadcasted_iota(I32, yq.shape, 1)
        first = (lane & 31) < 16
        sw = jnp.where(first, pltpu.roll(yq, 1024 - 16, 1), pltpu.roll(yq, 16, 1))
        qk_ref[r0:r0 + nr, :] = (yq * c + sw * s).astype(BF16)
        vv = proj[:, N_NORM_COLS:] + vone_ref[...]
        for h in range(N_KV_B):
            v_ref[h, r0:r0 + nr, :] = vv[:, 128 * h:128 * (h + 1)].astype(BF16)


def _rope_tables(n_pos):
    t = jnp.arange(n_pos)
    row = (t // GRID_W).astype(F32)
    col = (t % GRID_W).astype(F32)
    n_freq = HEAD_DIM // 4
    inv_freq = ROPE_THETA ** (-jnp.arange(n_freq, dtype=F32) / n_freq)
    ar = row[:, None] * inv_freq
    ac = col[:, None] * inv_freq
    ang = jnp.concatenate([ar, ar, ac, ac], axis=1)
    sign = jnp.asarray(np.tile(np.repeat(np.array([-1.0, 1.0], np.float32), 16), 2))
    cos = jnp.cos(ang)
    sin = jnp.sin(ang) * sign
    return jnp.concatenate([cos, cos], axis=1), jnp.concatenate([sin, sin], axis=1)


def _pos_block_map(classes, ts):
    bounds = []
    tile0 = 0
    for nseq, S, _ in classes:
        n_tiles = nseq * S // ts
        bounds.append((tile0, tile0 + n_tiles, S // ts))
        tile0 += n_tiles

    def fn(i):
        out = (i - bounds[-1][0]) % bounds[-1][2]
        for lo, hi, per in reversed(bounds[:-1]):
            out = jnp.where(i < hi, (i - lo) % per, out)
        return out

    return fn


def _inproj_even(x_parts, w_in, f_g, q_g, k_g, classes, ts):
    T = sum(p.shape[0] for p in x_parts)
    x_specs, bounds = _parts_specs(x_parts, ts)
    wf = w_in[:, :D_FOURIER]
    wq = w_in[:, D_FOURIER:D_FOURIER + 768].reshape(D_MODEL, N_KV_B, GRP_B * HEAD_DIM)
    wk = w_in[:, D_FOURIER + 768:D_FOURIER + 1024].reshape(D_MODEL, N_KV_B, HEAD_DIM)
    wv = w_in[:, D_FOURIER + 1024:].reshape(D_MODEL, N_KV_B, HEAD_DIM)
    wqk = jnp.concatenate([wk, wq], axis=2).reshape(D_MODEL, 1024)
    wvp = jnp.concatenate([wv, jnp.zeros_like(wv)], axis=2).reshape(D_MODEL, 512)
    w = jnp.concatenate([wf, wqk, wvp], axis=1).astype(BF16)
    qscale = HEAD_DIM ** -0.5 * LOG2E
    gqk = jnp.tile(jnp.concatenate([k_g, q_g * qscale, q_g * qscale, q_g * qscale]), N_KV_B)
    gain = jnp.concatenate([f_g, gqk])[None, :].astype(F32)
    seg_np = np.zeros((N_NORM_COLS, 128), np.float32)
    seg_np[np.arange(N_NORM_COLS), np.arange(N_NORM_COLS) // HEAD_DIM] = 1.0
    seg = jnp.asarray(seg_np, BF16)
    segt = jnp.asarray(np.concatenate([seg_np.T, seg_np.T], axis=0), BF16)
    max_s = max(S for _, S, _ in classes)
    cos, sin = _rope_tables(max_s)
    vone_np = np.zeros((1, 512), np.float32)
    for h in range(N_KV_B):
        vone_np[0, 128 * h + 64:128 * (h + 1)] = 1.0
    vone = jnp.asarray(vone_np)
    posmap = _pos_block_map(classes, ts)
    n_w = w.shape[1]
    return pl.pallas_call(
        functools.partial(_inproj_even_kernel, nx=len(x_parts), bounds=bounds),
        grid=(T // ts,),
        in_specs=x_specs + [
            pl.BlockSpec((D_MODEL, n_w), lambda i: (0, 0)),
            pl.BlockSpec((1, N_NORM_COLS), lambda i: (0, 0)),
            pl.BlockSpec((N_NORM_COLS, 128), lambda i: (0, 0)),
            pl.BlockSpec((256, N_NORM_COLS), lambda i: (0, 0)),
            pl.BlockSpec((ts, 128), lambda i: (posmap(i), 0)),
            pl.BlockSpec((ts, 128), lambda i: (posmap(i), 0)),
            pl.BlockSpec((1, 512), lambda i: (0, 0)),
        ],
        out_specs=[
            pl.BlockSpec((ts, D_FOURIER), lambda i: (i, 0)),
            pl.BlockSpec((ts, 1024), lambda i: (i, 0)),
            pl.BlockSpec((N_KV_B, ts, 128), lambda i: (0, i, 0)),
        ],
        out_shape=[
            jax.ShapeDtypeStruct((T, D_FOURIER), BF16),
            jax.ShapeDtypeStruct((T, 1024), BF16),
            jax.ShapeDtypeStruct((N_KV_B, T, 128), BF16),
        ],
        compiler_params=_cparams(("parallel",)),
        name="inproj_even",
    )(*x_parts, w, gain, seg, segt, cos, sin, vone)


FFT_TB = 16


def _fft1_kernel(x_ref, kw_ref, twr_ref, twi_ref, zr_ref, zi_ref, *, s1, bs):
    n = s1 * FFT_TB
    twr = twr_ref[0]
    twi = twi_ref[0]
    for q in range(bs):
        x2 = x_ref[q * s1:(q + 1) * s1].reshape(n, D_FOURIER)
        z = jnp.dot(kw_ref[...], x2, preferred_element_type=F32)
        zr = z[:n]
        zi = z[n:]
        zr_ref[q * s1:(q + 1) * s1] = (zr * twr - zi * twi).astype(BF16).reshape(s1, FFT_TB, D_FOURIER)
        zi_ref[q * s1:(q + 1) * s1] = (zr * twi + zi * twr).astype(BF16).reshape(s1, FFT_TB, D_FOURIER)


def _fft2_kernel(zr_ref, zi_ref, w2a_ref, w2b_ref, mix_ref, o_ref, *, cb):
    s2 = FFT_S2
    for c in range(cb):
        pp = (jnp.dot(w2a_ref[...], zr_ref[0, c], preferred_element_type=F32)
              + jnp.dot(w2b_ref[...], zi_ref[0, c], preferred_element_type=F32))
        f = (jnp.dot(pp[:s2].astype(BF16), mix_ref[:D_FOURIER], preferred_element_type=F32)
             + jnp.dot(pp[s2:].astype(BF16), mix_ref[D_FOURIER:], preferred_element_type=F32))
        o_ref[0, :, D_FOURIER * c:D_FOURIER * (c + 1)] = f.astype(BF16)


def _dft_mats(n):
    k = jnp.arange(n, dtype=I32)
    ang = (2.0 * math.pi / n) * ((k[:, None] * k[None, :]) % n).astype(F32)
    return jnp.cos(ang), jnp.sin(ang)


def _fourier_mix(uf, nseq, S, tok0):
    s2 = FFT_S2
    s1 = S // s2
    tb = FFT_TB
    n = s1 * tb
    x3 = uf.reshape(uf.shape[0] // s2, s2, D_FOURIER)
    blk0 = tok0 // S
    row = jnp.arange(2 * n, dtype=I32)[:, None]
    col = jnp.arange(n, dtype=I32)[None, :]
    ang1 = (2.0 * math.pi / s1) * ((((row % n) // tb) * (col // tb)) % s1).astype(F32)
    w1v = jnp.where(row < n, jnp.cos(ang1), -jnp.sin(ang1))
    kw = jnp.where((row % tb) == (col % tb), w1v, 0.0).astype(BF16)
    cc = jnp.arange(s1, dtype=I32)[:, None]
    bb = jnp.arange(s2, dtype=I32)[None, :]
    ang = (2.0 * math.pi / S) * ((cc * bb) % S).astype(F32)

    def table(t):
        t = t.reshape(s1, s2 // tb, tb).transpose(1, 0, 2).reshape(s2 // tb, n, 1)
        return jnp.broadcast_to(t, (s2 // tb, n, D_FOURIER))
    twr = table(jnp.cos(ang))
    twi = table(-jnp.sin(ang))
    bs = math.gcd(math.gcd(nseq, max(1, 64 // s1)), blk0) if blk0 else math.gcd(nseq, max(1, 64 // s1))
    zr, zi = pl.pallas_call(
        functools.partial(_fft1_kernel, s1=s1, bs=bs),
        grid=(s2 // tb, nseq // bs),
        in_specs=[
            pl.BlockSpec((bs * s1, tb, D_FOURIER), lambda j, b: (blk0 // bs + b, j, 0)),
            pl.BlockSpec((2 * n, n), lambda j, b: (0, 0)),
            pl.BlockSpec((1, n, D_FOURIER), lambda j, b: (j, 0, 0)),
            pl.BlockSpec((1, n, D_FOURIER), lambda j, b: (j, 0, 0)),
        ],
        out_specs=[
            pl.BlockSpec((bs * s1, tb, D_FOURIER), lambda j, b: (b, j, 0)),
            pl.BlockSpec((bs * s1, tb, D_FOURIER), lambda j, b: (b, j, 0)),
        ],
        out_shape=[jax.ShapeDtypeStruct((nseq * s1, s2, D_FOURIER), BF16)] * 2,
        compiler_params=_cparams(("parallel", "parallel")),
        name="fourier_stage1",
    )(x3, kw, twr, twi)
    zr = zr.reshape(nseq, s1, s2, D_FOURIER)
    zi = zi.reshape(nseq, s1, s2, D_FOURIER)
    c2, sn2 = _dft_mats(s2)
    w2a = jnp.concatenate([c2, -sn2], axis=0).astype(BF16)
    w2b = jnp.concatenate([sn2, c2], axis=0).astype(BF16)
    gc, gs = _dft_mats(HEAD_DIM)
    scale = 1.0 / math.sqrt(S * HEAD_DIM)
    eye = jnp.eye(N_FOURIER_GROUPS, dtype=F32)
    mix = (jnp.concatenate([jnp.kron(eye, gc), jnp.kron(eye, gs)], axis=0) * scale).astype(BF16)
    cb = min(8, s1)
    out = pl.pallas_call(
        functools.partial(_fft2_kernel, cb=cb),
        grid=(nseq, s1 // cb),
        in_specs=[
            pl.BlockSpec((1, cb, s2, D_FOURIER), lambda b, j: (b, j, 0, 0)),
            pl.BlockSpec((1, cb, s2, D_FOURIER), lambda b, j: (b, j, 0, 0)),
            pl.BlockSpec((2 * s2, s2), lambda b, j: (0, 0)),
            pl.BlockSpec((2 * s2, s2), lambda b, j: (0, 0)),
            pl.BlockSpec((2 * D_FOURIER, D_FOURIER), lambda b, j: (0, 0)),
        ],
        out_specs=pl.BlockSpec((1, s2, cb * D_FOURIER), lambda b, j: (b, 0, j)),
        out_shape=jax.ShapeDtypeStruct((nseq, s2, s1 * D_FOURIER), BF16),
        compiler_params=_cparams(("parallel", "parallel")),
        name="fourier_stage2",
    )(zr, zi, w2a, w2b, mix)
    return out.reshape(nseq * S, D_FOURIER)


SCORE_BOUND_MAX = 100.0


def _gattn_kernel(bounded_ref, q_ref, k_ref, v_ref, place_ref, o_ref, *, tq, tk, S):
    q3 = q_ref[...]
    qs = [q3[:, HEAD_DIM * (g + 1):HEAD_DIM * (g + 2)] for g in range(GRP_B)]

    def chunk(j):
        off = pl.multiple_of(j * tk, tk)
        return k_ref[pl.ds(off, tk), 0:HEAD_DIM], v_ref[0, pl.ds(off, tk), :]

    def scores(g, kc):
        return lax.dot_general(qs[g], kc, (((1,), (1,)), ((), ())), preferred_element_type=F32)

    def finish(accs):
        out = jnp.zeros((tq, 256), F32)
        for g in range(GRP_B):
            o = (accs[g] / accs[g][:, HEAD_DIM:HEAD_DIM + 1]).astype(BF16)
            out = out + jnp.dot(o, place_ref[g], preferred_element_type=F32)
        o_ref[...] = out.astype(BF16)

    @pl.when(bounded_ref[0] == 1)
    def _():
        def body(j, accs):
            kc, vc = chunk(j)
            return tuple(accs[g] + jnp.dot(jnp.exp2(scores(g, kc)).astype(BF16), vc,
                                           preferred_element_type=F32) for g in range(GRP_B))
        finish(lax.fori_loop(0, S // tk, body, tuple(jnp.zeros((tq, 128), F32) for _ in range(GRP_B))))

    @pl.when(bounded_ref[0] != 1)
    def _():
        def body(j, carry):
            kc, vc = chunk(j)
            new = []
            for g in range(GRP_B):
                m_prev, acc = carry[g]
                s = scores(g, kc)
                m_new = jnp.maximum(m_prev, jnp.max(s, axis=1, keepdims=True))
                p = jnp.exp2(s - m_new).astype(BF16)
                acc = jnp.exp2(m_prev - m_new) * acc + jnp.dot(p, vc, preferred_element_type=F32)
                new.append((m_new, acc))
            return tuple(new)
        init = tuple((jnp.full((tq, 1), NEG_INF, F32), jnp.zeros((tq, 128), F32)) for _ in range(GRP_B))
        fin = lax.fori_loop(0, S // tk, body, init)
        finish([fin[g][1] for g in range(GRP_B)])


def _score_bounded(q_g, k_g):
    bound = 1.05 * HEAD_DIM * (HEAD_DIM ** -0.5 * LOG2E) * jnp.max(jnp.abs(q_g)) * jnp.max(jnp.abs(k_g))
    return (bound <= SCORE_BOUND_MAX).astype(I32)[None]


def _lane_place(n_src, n_dst, groups):
    pm = np.zeros((len(groups), n_src, n_dst), np.float32)
    for gi, slot in enumerate(groups):
        pm[gi, np.arange(HEAD_DIM), HEAD_DIM * slot + np.arange(HEAD_DIM)] = 1.0
    return jnp.asarray(pm, BF16)


def _global_attention(qk, v_aug, bounded, nseq, S, tok0, tq, tk):
    n_qt = S // tq
    qt0 = tok0 // tq
    s0 = tok0 // S
    place = _lane_place(128, 256, [1, 2, 3])
    grid_spec = pltpu.PrefetchScalarGridSpec(
        num_scalar_prefetch=1,
        grid=(nseq, N_KV_B, n_qt),
        in_specs=[
            pl.BlockSpec((tq, 256), lambda b, h, i, fl: (qt0 + b * n_qt + i, h)),
            pl.BlockSpec((S, 256), lambda b, h, i, fl: (s0 + b, h)),
            pl.BlockSpec((1, S, 128), lambda b, h, i, fl: (h, s0 + b, 0)),
            pl.BlockSpec((GRP_B, 128, 256), lambda b, h, i, fl: (0, 0, 0)),
        ],
        out_specs=pl.BlockSpec((tq, 256), lambda b, h, i, fl: (b * n_qt + i, h)),
    )
    return pl.pallas_call(
        functools.partial(_gattn_kernel, tq=tq, tk=tk, S=S),
        grid_spec=grid_spec,
        out_shape=jax.ShapeDtypeStruct((nseq * S, 1024), BF16),
        compiler_params=_cparams(("parallel", "parallel", "arbitrary")),
        name="global_attention",
    )(bounded, qk, qk, v_aug, place)


def _route(x1, rw_ref, rb_ref):
    n = x1.shape[0]
    xh = x1.astype(BF16)
    xl = (x1 - xh.astype(F32)).astype(BF16)
    o = jnp.dot(jnp.concatenate([xh, xl], axis=0), rw_ref[...], preferred_element_type=F32)
    lg = o[:n, :128] + o[:n, 128:] + o[n:, :128]
    lgt = lg.T[:N_EXPERTS]
    sc = 1.0 / (1.0 + jnp.exp(-lgt))
    bi = sc + rb_ref[...]
    srow = [sc[e:e + 1] for e in range(N_EXPERTS)]
    brow = [bi[e:e + 1] for e in range(N_EXPERTS)]
    gsel = None
    best = None
    for g in range(N_GROUPS):
        a, b, c, d = brow[4 * g:4 * g + 4]
        m1, n1 = jnp.maximum(a, b), jnp.minimum(a, b)
        m2, n2 = jnp.maximum(c, d), jnp.minimum(c, d)
        gs = jnp.maximum(m1, m2) + jnp.maximum(jnp.minimum(m1, m2), jnp.maximum(n1, n2))
        if g == 0:
            best, gsel = gs, jnp.zeros(gs.shape, I32)
        else:
            better = gs > best
            gsel = jnp.where(better, g, gsel)
            best = jnp.where(better, gs, best)
    masked = [jnp.where(gsel == (e // EXPERTS_PER_GROUP), brow[e], NEG_INF) for e in range(N_EXPERTS)]
    i1 = jnp.zeros(gsel.shape, I32)
    b1 = masked[0]
    s1 = srow[0]
    for e in range(1, N_EXPERTS):
        better = masked[e] > b1
        i1 = jnp.where(better, e, i1)
        b1 = jnp.where(better, masked[e], b1)
        s1 = jnp.where(better, srow[e], s1)
    i2 = jnp.full(gsel.shape, -1, I32)
    b2 = jnp.full(b1.shape, -jnp.inf, F32)
    s2 = jnp.zeros(b1.shape, F32)
    for e in range(N_EXPERTS):
        better = (masked[e] > b2) & (i1 != e)
        i2 = jnp.where(better, e, i2)
        b2 = jnp.where(better, masked[e], b2)
        s2 = jnp.where(better, srow[e], s2)
    den = s1 + s2
    return i1, i2, s1 / den, s2 / den


ROW_SPLITS = 2


def _norm_route_store(r0, nr, m, x, g_ref, b_ref, rw_ref, rb_ref, x1_ref, idx_ref, wt_ref, xb_ref, alpha):
    x1 = _layer_norm(alpha * x + m, g_ref[...], b_ref[...])
    x1_ref[r0:r0 + nr, :] = x1
    xb_ref[r0:r0 + nr, :] = x1.astype(BF16)
    i1, i2, w1, w2 = _route(x1, rw_ref, rb_ref)
    rid = lax.broadcasted_iota(I32, (8, nr), 0)
    idx_ref[:, r0:r0 + nr] = jnp.where(rid == 0, i1, jnp.where(rid == 1, i2, 0))
    wt_ref[:, r0:r0 + nr] = jnp.where(rid == 0, w1, jnp.where(rid == 1, w2, 0.0))


def _router_operands(router_w, router_b):
    rw = jnp.zeros((D_MODEL, 128), F32).at[:, :N_EXPERTS].set(router_w.astype(F32))
    rwh = rw.astype(BF16)
    rwl = (rw - rwh.astype(F32)).astype(BF16)
    rb = router_b.astype(F32)[:, None]
    return jnp.concatenate([rwh, rwl], axis=1), rb


_ROUTER_SPECS = [
    pl.BlockSpec((D_MODEL, 256), lambda i: (0, 0)),
    pl.BlockSpec((N_EXPERTS, 1), lambda i: (0, 0)),
]


def _mix_out_specs(ts, T):
    specs = [
        pl.BlockSpec((ts, D_MODEL), lambda i: (i, 0)),
        pl.BlockSpec((8, ts), lambda i: (0, i)),
        pl.BlockSpec((8, ts), lambda i: (0, i)),
        pl.BlockSpec((ts, D_MODEL), lambda i: (i, 0)),
    ]
    shapes = [
        jax.ShapeDtypeStruct((T, D_MODEL), F32),
        jax.ShapeDtypeStruct((8, T), I32),
        jax.ShapeDtypeStruct((8, T), F32),
        jax.ShapeDtypeStruct((T, D_MODEL), BF16),
    ]
    return specs, shapes


def _outproj_even_kernel(*refs, nf, na, nx, fb, ab, xb, alpha):
    f_refs, a_refs, x_refs = refs[:nf], refs[nf:nf + na], refs[nf + na:nf + na + nx]
    wf_ref, wa_ref, g_ref, b_ref, rw_ref, rb_ref, x1_ref, idx_ref, wt_ref, xb_ref = refs[nf + na + nx:]
    nr = x1_ref.shape[0] // ROW_SPLITS
    for h in range(ROW_SPLITS):
        r0 = h * nr
        m = (jnp.dot(_pick_rows(f_refs, fb, r0, nr), wf_ref[...], preferred_element_type=F32)
             + jnp.dot(_pick_rows(a_refs, ab, r0, nr), wa_ref[...], preferred_element_type=F32))
        xr = _pick_rows(x_refs, xb, r0, nr)
        _norm_route_store(r0, nr, m, xr, g_ref, b_ref, rw_ref, rb_ref, x1_ref, idx_ref, wt_ref, xb_ref, alpha)


def _outproj_even(f_parts, a_parts, x_parts, w_out, ln_g, ln_b, router, alpha, ts):
    T = sum(p.shape[0] for p in x_parts)
    f_specs, fb = _parts_specs(f_parts, ts)
    a_specs, ab = _parts_specs(a_parts, ts)
    x_specs, xb = _parts_specs(x_parts, ts)
    wf = w_out[:D_FOURIER].astype(BF16)
    wa = w_out[D_FOURIER:].reshape(N_KV_B, GRP_B * HEAD_DIM, D_MODEL)
    wa = jnp.concatenate([jnp.zeros((N_KV_B, HEAD_DIM, D_MODEL), w_out.dtype), wa], axis=1)
    wa = wa.reshape(1024, D_MODEL).astype(BF16)
    out_specs, out_shapes = _mix_out_specs(ts, T)
    return pl.pallas_call(
        functools.partial(_outproj_even_kernel, nf=len(f_parts), na=len(a_parts), nx=len(x_parts),
                          fb=fb, ab=ab, xb=xb, alpha=alpha),
        grid=(T // ts,),
        in_specs=f_specs + a_specs + x_specs + [
            pl.BlockSpec((D_FOURIER, D_MODEL), lambda i: (0, 0)),
            pl.BlockSpec((1024, D_MODEL), lambda i: (0, 0)),
            pl.BlockSpec((1, D_MODEL), lambda i: (0, 0)),
            pl.BlockSpec((1, D_MODEL), lambda i: (0, 0)),
        ] + _ROUTER_SPECS,
        out_specs=out_specs,
        out_shape=out_shapes,
        compiler_params=_cparams(("parallel",)),
        name="outproj_even",
    )(*f_parts, *a_parts, *x_parts, wf, wa, ln_g[None, :], ln_b[None, :], *router)


CHUNK = 16


def _stage_rows(td):
    return -(-(2 * td + N_EXPERTS * (CHUNK - 1)) // 256) * 256


def _tab_len(td):
    return -(-(1 + 2 * td // CHUNK + N_EXPERTS) // 128) * 128


def _moe_plan(idx, td, tm):
    T = idx.shape[1]
    nt = T // td
    ep = idx.reshape(2, nt, td).transpose(1, 0, 2).reshape(nt, 2 * td)
    oh = (ep[None, :, :] == jnp.arange(N_EXPERTS, dtype=I32)[:, None, None]).astype(I32)
    cs = jnp.cumsum(oh, axis=2)
    cnt = cs[:, :, -1].T
    c8 = ((cnt + CHUNK - 1) // CHUNK) * CHUNK
    off8 = jnp.cumsum(c8, axis=1) - c8
    base8 = jnp.cumsum(c8, axis=0) - c8
    seg = jnp.sum(c8, axis=0)
    padded = ((seg + tm - 1) // tm) * tm
    ends = jnp.cumsum(padded)
    starts = ends - padded
    dst = starts[None, :] + base8
    lpos = jnp.sum(oh * (cs - 1 + off8.T[:, :, None]), axis=0).astype(I32)
    n_rows = -(-(2 * T + nt * N_EXPERTS * (CHUNK - 1)) // tm) * tm + N_EXPERTS * tm
    tile_start = jnp.arange(n_rows // tm, dtype=I32) * tm
    tile_e = jnp.minimum(jnp.sum((ends[None, :] <= tile_start[:, None]).astype(I32), axis=1), N_EXPERTS - 1)
    n_used = (ends[-1] // tm).astype(I32)[None]
    nc = c8 // CHUNK
    cum = jnp.cumsum(nc, axis=1)
    n_chunk_max = 2 * td // CHUNK + N_EXPERTS
    c_idx = jnp.arange(n_chunk_max, dtype=I32)
    e_c = jnp.minimum(jnp.sum((cum[:, None, :] <= c_idx[None, :, None]).astype(I32), axis=2), N_EXPERTS - 1)
    oh_c = (e_c[:, :, None] == jnp.arange(N_EXPERTS, dtype=I32)[None, None, :]).astype(I32)
    srow = jnp.sum(oh_c * (dst - CHUNK * (cum - nc))[:, None, :], axis=2) + CHUNK * c_idx[None, :]
    tab = jnp.concatenate([cum[:, -1:], srow], axis=1).astype(I32)
    tab = jnp.pad(tab, ((0, 0), (0, _tab_len(td) - tab.shape[1]))).reshape(nt, 1, _tab_len(td))
    pad_lo = jnp.concatenate([starts + seg, ends[-1:]]).astype(I32)
    pad_hi = jnp.concatenate([ends, jnp.full((1,), n_rows, I32)]).astype(I32)
    lpos_k = lpos.reshape(nt, 2, td).transpose(1, 0, 2).reshape(2, T)
    lpos_rows = jnp.zeros((8, T), I32).at[:2].set(lpos_k)
    return tab, lpos_rows, lpos_k.T, tile_e.astype(I32), n_used, pad_lo, pad_hi, n_rows


def _chunk_loops(tab_ref, fn):
    total = tab_ref[0, 0, 0]

    def one(c, priority):
        fn(pl.multiple_of(CHUNK * c, CHUNK), pl.multiple_of(tab_ref[0, 0, 1 + c], CHUNK), priority)

    def body(j, carry):
        one(2 * j, 0)
        one(2 * j + 1, 1)
        return carry
    lax.fori_loop(0, total // 2, body, 0)

    @pl.when(total % 2 == 1)
    def _():
        one(total - 1, 0)
    return total


def _dispatch_kernel(lo_ref, hi_ref, tab_ref, lp_ref, x_ref, xs_hbm, stage, zbuf, nwait, sem, zsem, *, ns):
    i = pl.program_id(0)
    slot = i % 2

    def zero_copy(dst):
        return pltpu.make_async_copy(zbuf, xs_hbm.at[pl.ds(pl.multiple_of(dst, CHUNK), CHUNK)], zsem)

    @pl.when(i == 0)
    def _():
        nwait[0] = 0
        nwait[1] = 0
        zbuf[...] = jnp.zeros(zbuf.shape, BF16)
        for e in range(N_EXPERTS + 1):
            nz = (hi_ref[e] - lo_ref[e]) // CHUNK
            lax.fori_loop(0, nz, lambda j, c, e=e: (zero_copy(lo_ref[e] + CHUNK * j).start(), c)[1], 0)
        for e in range(N_EXPERTS + 1):
            nz = (hi_ref[e] - lo_ref[e]) // CHUNK
            lax.fori_loop(0, nz, lambda j, c: (zero_copy(0).wait(), c)[1], 0)

    def piece(local_row, sorted_row, s):
        return pltpu.make_async_copy(stage.at[s, pl.ds(local_row, CHUNK)],
                                     xs_hbm.at[pl.ds(sorted_row, CHUNK)], sem.at[s])

    def drain(s):
        lax.fori_loop(0, nwait[s], lambda j, c: (piece(0, 0, s).wait(), c)[1], 0)

    drain(slot)
    lp = lp_ref[...]
    rows = lax.broadcasted_iota(I32, (ns, lp.shape[1]), 0)
    perm = jnp.where((rows == lp[0:1]) | (rows == lp[1:2]), 1.0, 0.0).astype(BF16)
    stage[slot] = jnp.dot(perm, x_ref[...], preferred_element_type=F32).astype(BF16)
    nwait[slot] = _chunk_loops(tab_ref, lambda lr, sr, pr: piece(lr, sr, slot).start(priority=pr))

    @pl.when(i == pl.num_programs(0) - 1)
    def _():
        drain(0)
        drain(1)


def _moe_dispatch(x1, tab, lpos_rows, pad_lo, pad_hi, n_rows, td):
    T = x1.shape[0]
    ns = _stage_rows(td)
    grid_spec = pltpu.PrefetchScalarGridSpec(
        num_scalar_prefetch=2,
        grid=(T // td,),
        in_specs=[
            pl.BlockSpec((1, 1, _tab_len(td)), lambda i, lo, hi: (i, 0, 0), memory_space=pltpu.SMEM),
            pl.BlockSpec((8, td), lambda i, lo, hi: (0, i)),
            pl.BlockSpec((td, D_MODEL), lambda i, lo, hi: (i, 0)),
        ],
        out_specs=pl.BlockSpec(memory_space=pl.ANY),
        scratch_shapes=[pltpu.VMEM((2, ns, D_MODEL), BF16), pltpu.VMEM((CHUNK, D_MODEL), BF16),
                        pltpu.SMEM((2,), I32), pltpu.SemaphoreType.DMA((2,)), pltpu.SemaphoreType.DMA(())],
    )
    return pl.pallas_call(
        functools.partial(_dispatch_kernel, ns=ns),
        grid_spec=grid_spec,
        out_shape=jax.ShapeDtypeStruct((n_rows, D_MODEL), BF16),
        compiler_params=_cparams(("arbitrary",)),
        name="moe_dispatch",
    )(pad_lo, pad_hi, tab, lpos_rows, x1)


def _moe_kernel(te_ref, nu_ref, x_ref, wg_ref, wu_ref, wd_ref, y_ref, wgb, wub, wdb):
    i = pl.program_id(0)

    @pl.when((i == 0) | (te_ref[i] != te_ref[jnp.maximum(i - 1, 0)]))
    def _():
        wgb[...] = wg_ref[0, 0].astype(BF16)
        wub[...] = wu_ref[0, 0].astype(BF16)
        wdb[...] = wd_ref[0, 0].astype(BF16)

    @pl.when(i < nu_ref[0])
    def _():
        xg = x_ref[...]
        hg = jnp.dot(xg, wgb[...], preferred_element_type=F32)
        hu = jnp.dot(xg, wub[...], preferred_element_type=F32)
        hdn = (hg / (1.0 + jnp.exp(-hg)) * hu).astype(BF16)
        y_ref[...] = jnp.dot(hdn, wdb[...], preferred_element_type=F32).astype(BF16)

    @pl.when(i >= nu_ref[0])
    def _():
        y_ref[...] = jnp.zeros(y_ref.shape, BF16)


def _moe_experts(xs, tile_e, n_used, wg, wu, wd, layer, tm):
    n_rows = xs.shape[0]
    grid_spec = pltpu.PrefetchScalarGridSpec(
        num_scalar_prefetch=2,
        grid=(n_rows // tm,),
        in_specs=[
            pl.BlockSpec((tm, D_MODEL), lambda i, te, nu: (jnp.minimum(i, jnp.maximum(nu[0] - 1, 0)), 0)),
            pl.BlockSpec((1, 1, D_MODEL, D_FF), lambda i, te, nu: (layer, te[i], 0, 0)),
            pl.BlockSpec((1, 1, D_MODEL, D_FF), lambda i, te, nu: (layer, te[i], 0, 0)),
            pl.BlockSpec((1, 1, D_FF, D_MODEL), lambda i, te, nu: (layer, te[i], 0, 0)),
        ],
        out_specs=pl.BlockSpec((tm, D_MODEL), lambda i, te, nu: (i, 0)),
        scratch_shapes=[pltpu.VMEM((D_MODEL, D_FF), BF16), pltpu.VMEM((D_MODEL, D_FF), BF16),
                        pltpu.VMEM((D_FF, D_MODEL), BF16)],
    )
    return pl.pallas_call(
        _moe_kernel,
        grid_spec=grid_spec,
        out_shape=jax.ShapeDtypeStruct((n_rows, D_MODEL), BF16),
        compiler_params=_cparams(("arbitrary",)),
        name="moe_experts",
    )(tile_e, n_used, xs, wg, wu, wd)


def _combine_kernel(tab_ref, tabn_ref, lp_ref, y_hbm, x_ref, wt_ref, g_ref, b_ref, *rest, ns, alpha, bounds):
    o_refs, (ystage, sem) = rest[:len(bounds)], rest[len(bounds):]
    i = pl.program_id(0)
    n = pl.num_programs(0)
    slot = i % 2

    def piece(local_row, sorted_row, s):
        return pltpu.make_async_copy(y_hbm.at[pl.ds(sorted_row, CHUNK)],
                                     ystage.at[s, pl.ds(local_row, CHUNK)], sem.at[s])

    @pl.when(i == 0)
    def _():
        ystage[...] = jnp.zeros(ystage.shape, BF16)
        _chunk_loops(tab_ref, lambda lr, sr, pr: piece(lr, sr, 0).start(priority=pr))

    @pl.when(i + 1 < n)
    def _():
        _chunk_loops(tabn_ref, lambda lr, sr, pr: piece(lr, sr, 1 - slot).start(priority=pr))

    lax.fori_loop(0, tab_ref[0, 0, 0], lambda j, c: (piece(0, 0, slot).wait(), c)[1], 0)

    ys = ystage[slot]
    lp = lp_ref[...]
    cols = lax.broadcasted_iota(I32, (lp.shape[0], ns), 1)
    w = wt_ref[...]
    tc = lp.shape[0]
    sel = jnp.concatenate([jnp.where(cols == lp[:, k:k + 1], 1.0, 0.0).astype(BF16) for k in range(2)], axis=0)
    picked = jnp.dot(sel, ys, preferred_element_type=F32)
    f = w[:, 0:1] * picked[:tc] + w[:, 1:2] * picked[tc:]
    o = _layer_norm(alpha * x_ref[...] + f, g_ref[...], b_ref[...])
    if len(bounds) == 1:
        o_refs[0][...] = o
    else:
        lo = 0
        for o_ref, hi in zip(o_refs, bounds):
            @pl.when((i >= lo) & (i < hi))
            def _(o_ref=o_ref):
                o_ref[...] = o
            lo = hi


def _moe_combine(y, tab, lpos_tk, wts, x1, ln_g, ln_b, alpha, tc, out_rows):
    T = x1.shape[0]
    n_tiles = T // tc
    ns = _stage_rows(tc)
    out_specs, out_shapes, bounds, lo = [], [], [], 0
    for rows in out_rows:
        n = rows // tc
        out_specs.append(pl.BlockSpec((tc, D_MODEL), lambda i, lo=lo, n=n: (jnp.clip(i - lo, 0, n - 1), 0)))
        out_shapes.append(jax.ShapeDtypeStruct((rows, D_MODEL), F32))
        lo += n
        bounds.append(lo)
    return pl.pallas_call(
        functools.partial(_combine_kernel, ns=ns, alpha=alpha, bounds=tuple(bounds)),
        grid=(n_tiles,),
        in_specs=[
            pl.BlockSpec((1, 1, _tab_len(tc)), lambda i: (i, 0, 0), memory_space=pltpu.SMEM),
            pl.BlockSpec((1, 1, _tab_len(tc)), lambda i: (jnp.minimum(i + 1, n_tiles - 1), 0, 0),
                         memory_space=pltpu.SMEM),
            pl.BlockSpec((tc, 2), lambda i: (i, 0)),
            pl.BlockSpec(memory_space=pl.ANY),
            pl.BlockSpec((tc, D_MODEL), lambda i: (i, 0)),
            pl.BlockSpec((tc, 2), lambda i: (i, 0)),
            pl.BlockSpec((1, D_MODEL), lambda i: (0, 0)),
            pl.BlockSpec((1, D_MODEL), lambda i: (0, 0)),
        ],
        out_specs=out_specs,
        out_shape=out_shapes,
        scratch_shapes=[pltpu.VMEM((2, ns, D_MODEL), BF16), pltpu.SemaphoreType.DMA((2,))],
        compiler_params=_cparams(("arbitrary",)),
        name="moe_combine",
    )(tab, tab, lpos_tk, y, x1, wts, ln_g[None, :], ln_b[None, :])


def _moe_layer(x1, xb, idx, wt, wg, wu, wd, layer, ln_g, ln_b, alpha, tm, tc, out_rows):
    tab, lpos_rows, lpos_tk, tile_e, n_used, pad_lo, pad_hi, n_rows = _moe_plan(idx[:2], tc, tm)
    xs = _moe_dispatch(xb, tab, lpos_rows, pad_lo, pad_hi, n_rows, tc)
    y = _moe_experts(xs, tile_e, n_used, wg, wu, wd, layer, tm)
    return _moe_combine(y, tab, lpos_tk, wt[:2].T, x1, ln_g, ln_b, alpha, tc, out_rows)


def _inproj_odd_kernel(x_ref, w_ref, vone_ref, bg_ref, u_ref, q_ref, k_ref, v_ref, nrm_ref):
    x = x_ref[...].astype(BF16)
    proj = jnp.dot(x, w_ref[...], preferred_element_type=F32)
    bg_ref[...] = proj[:, :512].astype(BF16)
    u_ref[...] = (proj[:, 512:1024] * proj[:, 1024:1536]).astype(BF16)
    qb = (proj[:, 1536:2048] * (HEAD_DIM ** -0.5 * LOG2E)).astype(BF16)
    kb = proj[:, 2048:2304].astype(BF16)
    q_ref[...] = qb
    k_ref[...] = kb
    v_ref[...] = (proj[:, 2304:2560] + vone_ref[...]).astype(BF16)
    qf = qb.astype(F32)
    kf = kb.astype(F32)
    qq = jnp.max(jnp.sum(qf * qf, axis=1, keepdims=True), axis=0, keepdims=True)
    kk = jnp.max(jnp.sum(kf * kf, axis=1, keepdims=True), axis=0, keepdims=True)
    rid = lax.broadcasted_iota(I32, (8, 128), 0)
    nrm_ref[0] = jnp.where(rid == 0, qq, jnp.where(rid == 1, kk, 0.0))


def _inproj_odd(x, w_in, ts):
    T = x.shape[0]
    c3 = 3 * D_CONV
    wk = w_in[:, c3 + 512:c3 + 640].reshape(D_MODEL, N_KV_D, HEAD_DIM)
    wv = w_in[:, c3 + 640:c3 + 768].reshape(D_MODEL, N_KV_D, HEAD_DIM)
    wkp = jnp.concatenate([wk, jnp.zeros_like(wk)], axis=2).reshape(D_MODEL, 256)
    wvp = jnp.concatenate([wv, jnp.zeros_like(wv)], axis=2).reshape(D_MODEL, 256)
    w = jnp.concatenate([w_in[:, :c3 + 512], wkp, wvp], axis=1).astype(BF16)
    vone_np = np.zeros((1, 256), np.float32)
    for h in range(N_KV_D):
        vone_np[0, 128 * h + 64:128 * (h + 1)] = 1.0
    n_w = w.shape[1]
    widths = [512, 512, 512, 256, 256]
    bg, u, q, kw, vw, nrm = pl.pallas_call(
        _inproj_odd_kernel,
        grid=(T // ts,),
        in_specs=[
            pl.BlockSpec((ts, D_MODEL), lambda i: (i, 0)),
            pl.BlockSpec((D_MODEL, n_w), lambda i: (0, 0)),
            pl.BlockSpec((1, 256), lambda i: (0, 0)),
        ],
        out_specs=[pl.BlockSpec((ts, n), lambda i: (i, 0)) for n in widths]
        + [pl.BlockSpec((1, 8, 128), lambda i: (i, 0, 0))],
        out_shape=[jax.ShapeDtypeStruct((T, n), BF16) for n in widths]
        + [jax.ShapeDtypeStruct((T // ts, 8, 128), F32)],
        compiler_params=_cparams(("parallel",)),
        name="inproj_odd",
    )(x, w, jnp.asarray(vone_np))
    score_bound = jnp.sqrt(jnp.max(nrm[:, 0, 0]) * jnp.max(nrm[:, 1, 0]))
    return bg, u, q, kw, vw, score_bound


def _wattn_kernel(bounded_ref, q_ref, kp_ref, kc_ref, kn_ref, vp_ref, vc_ref, vn_ref, bias_ref, sink_ref,
                  place_ref, o_ref, *, tq, seq_tiles):
    i = pl.program_id(0)
    first = i < 0
    last = i < 0
    for lo, hi, per in seq_tiles:
        inside = (i >= lo) & (i < hi)
        first = first | (inside & ((i - lo) % per == 0))
        last = last | (inside & ((i - lo) % per == per - 1))
    kfull = jnp.concatenate([kp_ref[...], kc_ref[...], kn_ref[...]], axis=0)[:, :HEAD_DIM]
    vfull = jnp.concatenate([vp_ref[...], vc_ref[...], vn_ref[...]], axis=0)
    q4 = q_ref[...]
    bias = bias_ref[0]
    sink = sink_ref[0]
    nb = tq // Q_BLOCK
    col = lax.broadcasted_iota(I32, (GRP_D * Q_BLOCK, 3 * Q_BLOCK), 1)

    def blocks(with_max):
        for n in range(nb):
            qs = jnp.concatenate([q4[Q_BLOCK * n:Q_BLOCK * (n + 1), HEAD_DIM * g:HEAD_DIM * (g + 1)]
                                  for g in range(GRP_D)], axis=0)
            keys = kfull[Q_BLOCK * n:Q_BLOCK * (n + 3)]
            vals = vfull[Q_BLOCK * n:Q_BLOCK * (n + 3)]
            s = lax.dot_general(qs, keys, (((1,), (1,)), ((), ())), preferred_element_type=F32) + bias
            if n == 0:
                s = jnp.where(first & (col < Q_BLOCK), NEG_INF, s)
            if n == nb - 1:
                s = jnp.where(last & (col >= 2 * Q_BLOCK), NEG_INF, s)
            if with_max:
                m = jnp.maximum(jnp.max(s, axis=1, keepdims=True), sink)
                s = s - m
                snk = sink - m
            else:
                snk = sink
            acc = jnp.dot(jnp.exp2(s).astype(BF16), vals, preferred_element_type=F32)
            den = acc[:, HEAD_DIM:HEAD_DIM + 1] + jnp.exp2(snk)
            o = (acc / den).astype(BF16)
            ocat = jnp.concatenate([o[Q_BLOCK * g:Q_BLOCK * (g + 1)] for g in range(GRP_D)], axis=1)
            out = jnp.dot(ocat, place_ref[...], preferred_element_type=F32)
            o_ref[Q_BLOCK * n:Q_BLOCK * (n + 1), :] = out.astype(BF16)

    @pl.when(bounded_ref[0] == 1)
    def _():
        blocks(False)

    @pl.when(bounded_ref[0] != 1)
    def _():
        blocks(True)


WINDOW_BOUND_MAX = 90.0


def _window_attention(q, kw, vw, sink_logits, score_bound, classes, tq):
    T = q.shape[0]
    sink_bound = jnp.max(jnp.abs(sink_logits.astype(F32))) * LOG2E
    bounded = ((1.02 * score_bound <= WINDOW_BOUND_MAX) & (sink_bound <= WINDOW_BOUND_MAX)).astype(I32)[None]
    n_tiles = T // tq
    hb = tq // Q_BLOCK
    n_hblk = T // Q_BLOCK
    r = jnp.arange(Q_BLOCK, dtype=I32)[:, None]
    j = jnp.arange(3 * Q_BLOCK, dtype=I32)[None, :]
    rel = jnp.abs(j - Q_BLOCK - r).astype(F32)
    slopes = jnp.asarray(np.array([2.0 ** (-8.0 * (h + 1) / N_HEADS_D) for h in range(N_HEADS_D)], np.float32))
    bias = jnp.where(rel[None] <= WINDOW, -slopes[:, None, None] * rel[None] * LOG2E, NEG_INF)
    bias = bias.reshape(N_KV_D, GRP_D * Q_BLOCK, 3 * Q_BLOCK)
    sink = jnp.repeat(sink_logits.astype(F32) * LOG2E, Q_BLOCK).reshape(N_KV_D, GRP_D * Q_BLOCK, 1)
    place = _lane_place(128, 256, [0, 1, 2, 3]).reshape(GRP_D * 128, 256)
    seq_tiles = []
    t0 = 0
    for nseq, S, _ in classes:
        cnt = nseq * S // tq
        seq_tiles.append((t0, t0 + cnt, S // tq))
        t0 += cnt
    prev_map = lambda i, h, fl: (jnp.maximum(i * hb - 1, 0), h)
    next_map = lambda i, h, fl: (jnp.minimum((i + 1) * hb, n_hblk - 1), h)
    grid_spec = pltpu.PrefetchScalarGridSpec(
        num_scalar_prefetch=1,
        grid=(n_tiles, N_KV_D),
        in_specs=[
            pl.BlockSpec((tq, 256), lambda i, h, fl: (i, h)),
            pl.BlockSpec((Q_BLOCK, 128), prev_map),
            pl.BlockSpec((tq, 128), lambda i, h, fl: (i, h)),
            pl.BlockSpec((Q_BLOCK, 128), next_map),
            pl.BlockSpec((Q_BLOCK, 128), prev_map),
            pl.BlockSpec((tq, 128), lambda i, h, fl: (i, h)),
            pl.BlockSpec((Q_BLOCK, 128), next_map),
            pl.BlockSpec((1, GRP_D * Q_BLOCK, 3 * Q_BLOCK), lambda i, h, fl: (h, 0, 0)),
            pl.BlockSpec((1, GRP_D * Q_BLOCK, 1), lambda i, h, fl: (h, 0, 0)),
            pl.BlockSpec((GRP_D * 128, 256), lambda i, h, fl: (0, 0)),
        ],
        out_specs=pl.BlockSpec((tq, 256), lambda i, h, fl: (i, h)),
    )
    return pl.pallas_call(
        functools.partial(_wattn_kernel, tq=tq, seq_tiles=tuple(seq_tiles)),
        grid_spec=grid_spec,
        out_shape=jax.ShapeDtypeStruct((T, 512), BF16),
        compiler_params=_cparams(("parallel", "parallel")),
        name="window_attention",
    )(bounded, q, kw, kw, kw, vw, vw, vw, bias, sink, place)


HALO = 16


def _outproj_odd_kernel(bg_ref, u_ref, up_ref, un_ref, a_ref, x_ref, cw_ref, cb_ref, wc_ref, wa_ref,
                        g_ref, b_ref, rw_ref, rb_ref, x1_ref, idx_ref, wt_ref, xb_ref,
                        *, alpha, ts, seq_tiles):
    i = pl.program_id(0)
    first = i < 0
    last = i < 0
    for lo, hi, per in seq_tiles:
        inside = (i >= lo) & (i < hi)
        first = first | (inside & ((i - lo) % per == 0))
        last = last | (inside & ((i - lo) % per == per - 1))
    u = u_ref[...].astype(F32)
    prev_row = jnp.where(first, 0.0, up_ref[HALO - 1:HALO, :].astype(F32))
    next_row = jnp.where(last, 0.0, un_ref[0:1, :].astype(F32))
    rid = lax.broadcasted_iota(I32, u.shape, 0)
    ud = jnp.where(rid == 0, prev_row, pltpu.roll(u, 1, 0))
    uu = jnp.where(rid == ts - 1, next_row, pltpu.roll(u, ts - 1, 0))
    cw = cw_ref[...]
    y = ud * cw[0:1] + u * cw[1:2] + uu * cw[2:3] + cb_ref[...]
    c = (bg_ref[...].astype(F32) * y).astype(BF16)
    nr = ts // ROW_SPLITS
    for h in range(ROW_SPLITS):
        r0 = h * nr
        m = (jnp.dot(c[r0:r0 + nr], wc_ref[...], preferred_element_type=F32)
             + jnp.dot(a_ref[r0:r0 + nr, :], wa_ref[...], preferred_element_type=F32))
        _norm_route_store(r0, nr, m, x_ref[r0:r0 + nr, :], g_ref, b_ref, rw_ref, rb_ref, x1_ref, idx_ref,
                          wt_ref, xb_ref, alpha)


def _outproj_odd(bg, u, a, x, conv_w, conv_b, w_out, ln_g, ln_b, router, alpha, classes, ts):
    T = x.shape[0]
    hb = ts // HALO
    n_h = T // HALO
    seq_tiles = []
    t0 = 0
    for nseq, S, _ in classes:
        cnt = nseq * S // ts
        seq_tiles.append((t0, t0 + cnt, S // ts))
        t0 += cnt
    out_specs, out_shapes = _mix_out_specs(ts, T)
    return pl.pallas_call(
        functools.partial(_outproj_odd_kernel, alpha=alpha, ts=ts, seq_tiles=tuple(seq_tiles)),
        grid=(T // ts,),
        in_specs=[
            pl.BlockSpec((ts, D_CONV), lambda i: (i, 0)),
            pl.BlockSpec((ts, D_CONV), lambda i: (i, 0)),
            pl.BlockSpec((HALO, D_CONV), lambda i: (jnp.maximum(i * hb - 1, 0), 0)),
            pl.BlockSpec((HALO, D_CONV), lambda i: (jnp.minimum((i + 1) * hb, n_h - 1), 0)),
            pl.BlockSpec((ts, 512), lambda i: (i, 0)),
            pl.BlockSpec((ts, D_MODEL), lambda i: (i, 0)),
            pl.BlockSpec((3, D_CONV), lambda i: (0, 0)),
            pl.BlockSpec((1, D_CONV), lambda i: (0, 0)),
            pl.BlockSpec((D_CONV, D_MODEL), lambda i: (0, 0)),
            pl.BlockSpec((512, D_MODEL), lambda i: (0, 0)),
            pl.BlockSpec((1, D_MODEL), lambda i: (0, 0)),
            pl.BlockSpec((1, D_MODEL), lambda i: (0, 0)),
        ] + _ROUTER_SPECS,
        out_specs=out_specs,
        out_shape=out_shapes,
        compiler_params=_cparams(("parallel",)),
        name="outproj_odd",
    )(bg, u, u, u, a, x, conv_w.astype(F32), conv_b.astype(F32)[None, :],
      w_out[:D_CONV].astype(BF16), w_out[D_CONV:].astype(BF16), ln_g[None, :], ln_b[None, :], *router)


TOKEN_TILE = 512
GLOBAL_Q_TILE = 1024
GLOBAL_KV_CHUNK = 4096
WINDOW_Q_TILE = 1024
EXPERT_ROW_TILE = 1024
MOE_TOKEN_TILE = 256


def _tile(n, cap):
    t = cap
    while n % t:
        t //= 2
    return t


def kernel(x_prompt, x_sample, w_in_even, fourier_norm_g, q_norm_g, k_norm_g, w_out_even, w_in_odd, conv_w,
           conv_b, sink_logits, w_out_odd, ln_mix_g, ln_mix_b, ln_ffn_g, ln_ffn_b, router_w, router_b,
           w_gate, w_up, w_down):
    depth = ln_mix_g.shape[0]
    alpha = float((2 * depth) ** 0.25)
    bp, sp, _ = x_prompt.shape
    bs, ss, _ = x_sample.shape
    classes = ((bp, sp, 0), (bs, ss, bp * sp))
    T = bp * sp + bs * ss
    min_s = min(sp, ss)
    ts = _tile(min_s, TOKEN_TILE)
    tq_w = _tile(min_s, WINDOW_Q_TILE)
    tm = EXPERT_ROW_TILE
    tc = _tile(min_s, MOE_TOKEN_TILE)
    x_parts = [x_prompt.reshape(bp * sp, D_MODEL), x_sample.reshape(bs * ss, D_MODEL)]
    router = _router_operands(router_w, router_b)
    for l in range(depth):
        i = l // 2
        if l % 2 == 0:
            uf, qk, v_aug = _inproj_even(x_parts, w_in_even[i], fourier_norm_g[i], q_norm_g[i], k_norm_g[i],
                                         classes, ts)
            bounded = _score_bounded(q_norm_g[i], k_norm_g[i])
            f_parts, a_parts = [], []
            for nseq, S, tok0 in classes:
                f_parts.append(_fourier_mix(uf, nseq, S, tok0))
                a_parts.append(_global_attention(qk, v_aug, bounded, nseq, S, tok0, _tile(S, GLOBAL_Q_TILE),
                                                 _tile(S, GLOBAL_KV_CHUNK)))
            x1, idx, wt, xb = _outproj_even(f_parts, a_parts, x_parts, w_out_even[i], ln_mix_g[l], ln_mix_b[l],
                                            router, alpha, ts)
        else:
            x = x_parts[0]
            bg, u, q, kw, vw, score_bound = _inproj_odd(x, w_in_odd[i], ts)
            a = _window_attention(q, kw, vw, sink_logits[i], score_bound, classes, tq_w)
            x1, idx, wt, xb = _outproj_odd(bg, u, a, x, conv_w[i], conv_b[i], w_out_odd[i], ln_mix_g[l],
                                           ln_mix_b[l], router, alpha, classes, ts)
        out_rows = [bp * sp, bs * ss] if l == depth - 1 else [T]
        x_parts = _moe_layer(x1, xb, idx, wt, w_gate, w_up, w_down, l, ln_ffn_g[l], ln_ffn_b[l], alpha,
                             tm, tc, out_rows)
    if len(x_parts) == 1:
        x_parts = [x_parts[0][:bp * sp], x_parts[0][bp * sp:]]
    return (x_parts[0].reshape(bp, sp, D_MODEL), x_parts[1].reshape(bs, ss, D_MODEL))
```
